```python
import math
import jax, jax.numpy as jnp
from jax import lax
import numpy as np

D_MODEL = 1024
BATCH = 16
SEQ = 2048
DEPTH = 2

HEAD_DIM = 64
HEADS_PER_GROUP = 4
ATTN_GROUPS = ((128, 1), (512, 4), (2048, 16))
N_ATTN_HEADS = HEADS_PER_GROUP * len(ATTN_GROUPS)
ATTN_WIDTH = N_ATTN_HEADS * HEAD_DIM
ATTN_OUT = HEADS_PER_GROUP * HEAD_DIM
QBLK = 64
POOL_WINDOWS = (2, 4, 8, 16)
POOL_GROUP = D_MODEL // 8
POOL_WIDTH = POOL_GROUP * len(POOL_WINDOWS)
HYENA_WIDTH = D_MODEL // 2
HYENA_ORDER = 2
SHORT_CONV = 3
FILTER_EMB = 33
FILTER_HIDDEN = 64
FILTER_INNER = 2
FILTER_OUT = HYENA_ORDER * 2 * HYENA_WIDTH
DECAY_TARGET = 1e-2
FAST_DECAY_PCT = 0.3
SLOW_DECAY_PCT = 1.5
N_BRANCH = 3
SPLIT_SIZES = (ATTN_WIDTH, ATTN_WIDTH, ATTN_WIDTH, POOL_WIDTH,
               HYENA_WIDTH, HYENA_WIDTH, HYENA_WIDTH, N_BRANCH * D_MODEL)
IN_WIDTH = sum(SPLIT_SIZES)
N_EXPERTS = 32
TOP_K = 4
D_FF = D_MODEL
SWIGLU_LIMIT = 7.0
SWIGLU_ALPHA = 1.702
MOE_BLK = 256
LN_EPS = 1e-5
ALPHA_RES = (2 * DEPTH) ** 0.25
BETA_INIT = (8 * DEPTH) ** -0.25

kernel_name = "hybrid_dilated_pool_hyena_moe_encoder"


def layer_norm(x, g, b):
    xf = x.astype(jnp.float32)
    mu = jnp.mean(xf, axis=-1, keepdims=True)
    var = jnp.mean(jnp.square(xf - mu), axis=-1, keepdims=True)
    return ((xf - mu) * lax.rsqrt(var + LN_EPS) * g + b).astype(x.dtype)


def alibi_slopes():
    h = np.arange(1, N_ATTN_HEADS + 1, dtype=np.float32)
    return jnp.asarray(2.0 ** (-8.0 * h / N_ATTN_HEADS), dtype=jnp.float32)


def dilated_window_attention(q, k, v, slopes, dilation, half):
    B, S, H, hd = q.shape
    L = S // dilation
    Q = math.gcd(QBLK, L)
    nb = L // Q
    kb_len = Q + 2 * half
    qs = q.reshape(B, nb, Q, dilation, H, hd).astype(jnp.float32)
    pad = ((0, 0), (half, half), (0, 0), (0, 0), (0, 0))
    kp = jnp.pad(k.reshape(B, L, dilation, H, hd), pad)
    vp = jnp.pad(v.reshape(B, L, dilation, H, hd), pad)
    idx = jnp.arange(nb)[:, None] * Q + jnp.arange(kb_len)[None, :]
    kb = kp[:, idx].astype(jnp.float32)
    vb = vp[:, idx].astype(jnp.float32)
    s = jnp.einsum('bnqrhd,bnkrhd->bnrhqk', qs, kb) / math.sqrt(hd)
    off = jnp.arange(kb_len)[None, :] - half - jnp.arange(Q)[:, None]
    key_m = idx - half
    valid = (jnp.abs(off) <= half)[None] & ((key_m >= 0) & (key_m < L))[:, None, :]
    bias = -slopes[:, None, None] * (dilation * jnp.abs(off)).astype(jnp.float32)[None]
    s = jnp.where(valid[None, :, None, None], s + bias, -jnp.inf)
    lse = jax.nn.logsumexp(s, axis=-1)
    p = jnp.exp(s - lse[..., None])
    o = jnp.einsum('bnrhqk,bnkrhd->bnqrhd', p, vb).reshape(B, S, H, hd)
    lse = lse.transpose(0, 1, 4, 2, 3).reshape(B, S, H)
    return o, lse


def mixer_attention(q, k, v):
    B, S = q.shape[:2]
    slopes = alibi_slopes()
    outs, lses = [], []
    for g, (window, dilation) in enumerate(ATTN_GROUPS):
        hs = slice(g * HEADS_PER_GROUP, (g + 1) * HEADS_PER_GROUP)
        o, lse = dilated_window_attention(q[:, :, hs], k[:, :, hs], v[:, :, hs], slopes[hs],
                                          dilation, window // (2 * dilation))
        outs.append(o)
        lses.append(lse)
    w = jax.nn.softmax(jnp.stack(lses, axis=0), axis=0)
    o = jnp.einsum('gbsh,gbshd->bshd', w, jnp.stack(outs, axis=0))
    return o.reshape(B, S, ATTN_OUT)


def mixer_pool(y, w_pool, pool_scale):
    B, S, C = y.shape
    yf = y.astype(jnp.float32)
    cs = jnp.concatenate([jnp.zeros((B, 1, C), jnp.float32), jnp.cumsum(yf, axis=1)], axis=1)
    t = jnp.arange(S)
    outs = []
    for g, w in enumerate(POOL_WINDOWS):
        sl = slice(g * POOL_GROUP, (g + 1) * POOL_GROUP)
        lo = jnp.clip(t - w // 2, 0, S)
        hi = jnp.clip(t + w // 2, 0, S)
        mean = (cs[:, hi, sl] - cs[:, lo, sl]) / (hi - lo).astype(jnp.float32)[None, :, None]
        outs.append(mean - yf[..., sl])
    pooled = jnp.stack(outs, axis=2)
    mixed = jnp.einsum('bsgc,gcd->bsgd', pooled, w_pool.astype(jnp.float32)).reshape(B, S, C)
    return (mixed * pool_scale).astype(y.dtype)


def hyena_filters(S, f_w1, f_b1, f_w_inner, f_b_inner, f_w_out, f_freq):
    t = jnp.linspace(0.0, 1.0, S, dtype=jnp.float32)[:, None]
    bands = (FILTER_EMB - 1) // 2
    w = 2.0 * math.pi * jnp.arange(S, dtype=jnp.float32) / S
    f = jnp.linspace(1e-4, bands - 1, bands, dtype=jnp.float32)
    ang = w[:, None] * f[None, :]
    z = jnp.concatenate([t, jnp.cos(ang), -jnp.sin(ang)], axis=-1)
    h = jnp.sin(f_freq * (z @ f_w1 + f_b1))
    for i in range(FILTER_INNER):
        h = jnp.sin(f_freq * (h @ f_w_inner[i] + f_b_inner[i]))
    h = (h @ f_w_out).astype(jnp.float32).reshape(S, HYENA_ORDER, 2, HYENA_WIDTH)
    min_decay = math.log(DECAY_TARGET) / SLOW_DECAY_PCT
    max_decay = math.log(DECAY_TARGET) / FAST_DECAY_PCT
    deltas = jnp.linspace(min_decay, max_decay, HYENA_WIDTH, dtype=jnp.float32)
    decay = jnp.exp(-t * jnp.abs(deltas)[None, :])
    return h * decay[:, None, None, :]


def two_sided_spectrum(h_fwd, h_bwd):
    S, C = h_fwd.shape
    k = jnp.concatenate([h_fwd, jnp.zeros((1, C), jnp.float32), h_bwd[1:][::-1]], axis=0)
    return jnp.fft.rfft(k, axis=0)


def long_conv(z, spec, bias):
    S = z.shape[1]
    zf = jnp.fft.rfft(z, n=2 * S, axis=1)
    y = jnp.fft.irfft(zf * spec[None], n=2 * S, axis=1)[:, :S]
    return y + z * bias


def mixer_hyena(hz, w_sconv, b_sconv, f_w1, f_b1, f_w_inner, f_b_inner, f_w_out, f_freq, hy_bias):
    B, S, C3 = hz.shape
    hz = lax.conv_general_dilated(hz, w_sconv[:, None, :], window_strides=(1,),
                                  padding=((SHORT_CONV // 2, SHORT_CONV // 2),),
                                  dimension_numbers=('NWC', 'WIO', 'NWC'),
                                  feature_group_count=C3) + b_sconv
    v, x1, x2 = jnp.split(hz, 3, axis=-1)
    filt = hyena_filters(S, f_w1, f_b1, f_w_inner, f_b_inner, f_w_out, f_freq)
    z = v.astype(jnp.float32)
    for o, gate in enumerate((x1, x2)):
        spec = two_sided_spectrum(filt[:, o, 0], filt[:, o, 1])
        z = gate.astype(jnp.float32) * long_conv(z, spec, hy_bias[o].astype(jnp.float32))
    return z.astype(hz.dtype)


def hybrid_mixer(u, w_in, w_sconv, b_sconv, w_pool, pool_scale, f_w1, f_b1, f_w_inner, f_b_inner,
                 f_w_out, f_freq, hy_bias, p_attn, p_pool, p_hyena, w_out):
    B, S, D = u.shape
    proj = u @ w_in
    cuts = np.cumsum(SPLIT_SIZES)[:-1].tolist()
    q, k, v, yp, hv, hx1, hx2, g_logit = jnp.split(proj, cuts, axis=-1)
    heads = lambda t: t.reshape(B, S, N_ATTN_HEADS, HEAD_DIM)
    ya = mixer_attention(heads(q), heads(k), heads(v)).astype(u.dtype)
    yb = mixer_pool(yp, w_pool, pool_scale)
    yc = mixer_hyena(jnp.concatenate([hv, hx1, hx2], axis=-1), w_sconv, b_sconv, f_w1, f_b1,
                     f_w_inner, f_b_inner, f_w_out, f_freq, hy_bias)
    g = jax.nn.sigmoid(g_logit.astype(jnp.float32)).reshape(B, S, N_BRANCH, D)
    merged = g[:, :, 0] * (ya @ p_attn) + g[:, :, 1] * (yb @ p_pool) + g[:, :, 2] * (yc @ p_hyena)
    return merged.astype(u.dtype) @ w_out


def moe_ffn(u, w_router, b_router, w1, b1, w2, b2):
    B, S, D = u.shape
    N = B * S
    NK = N * TOP_K
    xt = u.reshape(N, D)
    logits = (xt @ w_router + b_router).astype(jnp.float32)
    top_v, top_e = lax.top_k(logits, TOP_K)
    gates = jax.nn.softmax(top_v, axis=-1)
    flat_e = top_e.reshape(NK)
    order = jnp.argsort(flat_e)
    sorted_e = flat_e[order]
    counts = jnp.bincount(flat_e, length=N_EXPERTS)
    padded = (counts + MOE_BLK - 1) // MOE_BLK * MOE_BLK
    pend = jnp.cumsum(padded)
    cend = jnp.cumsum(counts)
    dest = (pend - padded)[sorted_e] + jnp.arange(NK) - (cend - counts)[sorted_e]
    n_blocks = -(-NK // MOE_BLK) + N_EXPERTS
    R = n_blocks * MOE_BLK
    row_tok = jnp.zeros((R,), jnp.int32).at[dest].set((order // TOP_K).astype(jnp.int32))
    row_gate = jnp.zeros((R,), jnp.float32).at[dest].set(gates.reshape(NK)[order])
    blk_e = jnp.minimum(jnp.searchsorted(pend, jnp.arange(n_blocks) * MOE_BLK, side='right'),
                        N_EXPERTS - 1)

    def expert_block(args):
        tok, gate, e = args
        h = xt[tok] @ w1[e] + b1[e]
        glu, lin = jnp.split(h, 2, axis=-1)
        glu = jnp.minimum(glu, SWIGLU_LIMIT)
        lin = jnp.clip(lin, -SWIGLU_LIMIT, SWIGLU_LIMIT)
        act = glu * jax.nn.sigmoid(SWIGLU_ALPHA * glu) * (lin + 1.0)
        y = act @ w2[e] + b2[e]
        return y * gate[:, None].astype(y.dtype)

    ys = lax.map(expert_block, (row_tok.reshape(n_blocks, MOE_BLK),
                                row_gate.reshape(n_blocks, MOE_BLK), blk_e))
    out = jax.ops.segment_sum(ys.reshape(R, D), row_tok, num_segments=N)
    return out.reshape(B, S, D).astype(u.dtype)


def setup_inputs(seed: int = 0) -> dict:
    key = jax.random.key(seed)
    ks = jax.random.split(key, 32)
    f32 = jnp.float32
    nrm = lambda k, shape, std: jax.random.normal(k, shape, f32) * std
    L, D, C = DEPTH, D_MODEL, HYENA_WIDTH
    return {
        "x": nrm(ks[0], (BATCH, SEQ, D), 1.0),
        "c": nrm(ks[1], (BATCH, D), 1.0),
        "w_ada": nrm(ks[2], (L, D, 6 * D), 0.2 * D ** -0.5),
        "b_ada": nrm(ks[3], (L, 6 * D), 0.01),
        "w_in": nrm(ks[4], (L, D, IN_WIDTH), D ** -0.5),
        "w_sconv": nrm(ks[5], (L, SHORT_CONV, 3 * C), SHORT_CONV ** -0.5),
        "b_sconv": nrm(ks[6], (L, 3 * C), 0.01),
        "w_pool": nrm(ks[7], (L, len(POOL_WINDOWS), POOL_GROUP, POOL_GROUP), POOL_GROUP ** -0.5),
        "pool_scale": 1.0 + nrm(ks[8], (L, POOL_WIDTH), 0.02),
        "f_w1": nrm(ks[9], (L, FILTER_EMB, FILTER_HIDDEN), FILTER_EMB ** -0.5),
        "f_b1": nrm(ks[10], (L, FILTER_HIDDEN), 0.02),
        "f_w_inner": nrm(ks[11], (L, FILTER_INNER, FILTER_HIDDEN, FILTER_HIDDEN), FILTER_HIDDEN ** -0.5),
        "f_b_inner": nrm(ks[12], (L, FILTER_INNER, FILTER_HIDDEN), 0.02),
        "f_w_out": nrm(ks[13], (L, FILTER_HIDDEN, FILTER_OUT), 0.01),
        "f_freq": 1.0 + nrm(ks[14], (L, FILTER_HIDDEN), 0.02),
        "hy_bias": nrm(ks[15], (L, HYENA_ORDER, C), 1.0),
        "p_attn": nrm(ks[16], (L, ATTN_OUT, D), ATTN_OUT ** -0.5),
        "p_pool": nrm(ks[17], (L, POOL_WIDTH, D), POOL_WIDTH ** -0.5),
        "p_hyena": nrm(ks[18], (L, C, D), C ** -0.5),
        "w_out": nrm(ks[19], (L, D, D), BETA_INIT * D ** -0.5),
        "ln1_g": 1.0 + nrm(ks[20], (L, D), 0.02),
        "ln1_b": nrm(ks[21], (L, D), 0.02),
        "w_router": nrm(ks[22], (L, D, N_EXPERTS), D ** -0.5),
        "b_router": nrm(ks[23], (L, N_EXPERTS), 0.01),
        "w1": nrm(ks[24], (L, N_EXPERTS, D, 2 * D_FF), D ** -0.5),
        "b1": nrm(ks[25], (L, N_EXPERTS, 2 * D_FF), 0.01),
        "w2": nrm(ks[26], (L, N_EXPERTS, D_FF, D), BETA_INIT * D_FF ** -0.5),
        "b2": nrm(ks[27], (L, N_EXPERTS, D), 0.01),
        "ln2_g": 1.0 + nrm(ks[28], (L, D), 0.02),
        "ln2_b": nrm(ks[29], (L, D), 0.02),
    }


def reference(x, c, w_ada, b_ada, w_in, w_sconv, b_sconv, w_pool, pool_scale, f_w1, f_b1, f_w_inner,
              f_b_inner, f_w_out, f_freq, hy_bias, p_attn, p_pool, p_hyena, w_out, ln1_g, ln1_b,
              w_router, b_router, w1, b1, w2, b2, ln2_g, ln2_b):
    B, S, D = x.shape
    cond = jax.nn.silu(c)
    for l in range(DEPTH):
        mod = (cond @ w_ada[l] + b_ada[l]).reshape(B, 6, 1, D)
        shift1, scale1, gate1 = mod[:, 0], mod[:, 1], mod[:, 2]
        shift2, scale2, gate2 = mod[:, 3], mod[:, 4], mod[:, 5]
        u = x * (1.0 + scale1) + shift1
        h = hybrid_mixer(u, w_in[l], w_sconv[l], b_sconv[l], w_pool[l], pool_scale[l], f_w1[l], f_b1[l],
                         f_w_inner[l], f_b_inner[l], f_w_out[l], f_freq[l], hy_bias[l],
                         p_attn[l], p_pool[l], p_hyena[l], w_out[l])
        x = layer_norm(ALPHA_RES * x + (1.0 + gate1) * h, ln1_g[l], ln1_b[l])
        u = x * (1.0 + scale2) + shift2
        h = moe_ffn(u, w_router[l], b_router[l], w1[l], b1[l], w2[l], b2[l])
        x = layer_norm(ALPHA_RES * x + (1.0 + gate2) * h, ln2_g[l], ln2_b[l])
    return x
```

```python
import functools
import math

import jax
import jax.numpy as jnp
import numpy as np
from jax import lax
from jax.experimental import pallas as pl
from jax.experimental.pallas import tpu as pltpu

F32 = jnp.float32
BF16 = jnp.bfloat16

HEAD_DIM = 64
HEADS_PER_GROUP = 4
ATTN_GROUPS = ((128, 1), (512, 4), (2048, 16))
N_ATTN_HEADS = HEADS_PER_GROUP * len(ATTN_GROUPS)
GROUP_WIDTH = HEADS_PER_GROUP * HEAD_DIM
HALF_WINDOW = 64
POOL_WINDOWS = (2, 4, 8, 16)
POOL_GROUP = 128
HYENA_WIDTH = 512
FILTER_EMB = 33
FILTER_HIDDEN = 64
N_EXPERTS = 32
TOP_K = 4
SWIGLU_LIMIT = 7.0
SWIGLU_ALPHA = 1.702
LN_EPS = 1e-5
DEPTH = 2
ALPHA_RES = (2 * DEPTH) ** 0.25
DECAY_TARGET = 1e-2
FAST_DECAY_PCT = 0.3
SLOW_DECAY_PCT = 1.5

QBLK = 128
FREQ_TILE = 256
ROW_CHUNK = 256
MERGE_ROWS = 512
TOPK_LANES = 1024
MOE_BLK = 256
MOE_TOK = 256
PAD_ROWS = 8
NEG_BIG = -1e30
LANES = 128
VMEM_LIMIT = 56 * 1024 * 1024


def _cparams(*sem):
    return pltpu.CompilerParams(dimension_semantics=sem, vmem_limit_bytes=VMEM_LIMIT)


def _dot(a, b):
    return jnp.dot(a, b, preferred_element_type=F32)


def _dot3(a, b):
    a_hi = a.astype(BF16)
    b_hi = b.astype(BF16)
    a_lo = (a - a_hi.astype(F32)).astype(BF16)
    b_lo = (b - b_hi.astype(F32)).astype(BF16)
    return _dot(a_hi, b_hi) + _dot(a_hi, b_lo) + _dot(a_lo, b_hi)


def _layer_norm(r, g, b):
    mu = jnp.mean(r, axis=-1, keepdims=True)
    c = r - mu
    var = jnp.mean(c * c, axis=-1, keepdims=True)
    return c * lax.rsqrt(var + LN_EPS) * g + b


def _store_token_tiles(ref, val):
    T, D = val.shape
    n = D // LANES
    for c in range(n):
        ref[pl.ds(c, T, stride=n), :] = val[:, c * LANES:(c + 1) * LANES]


def _load_token_tiles(ref, T, D, plane=None):
    n = D // LANES
    idx = (lambda c: (pl.ds(c, T, stride=n), slice(None))) if plane is None else (
        lambda c: (plane, pl.ds(c, T, stride=n), slice(None)))
    return [ref[idx(c)] for c in range(n)]


def _ada_kernel(c_ref, w_ref, b_ref, o_ref):
    c = c_ref[...]
    cond = c * (1.0 / (1.0 + jnp.exp(-c)))
    o_ref[...] = _dot3(cond, w_ref[...]) + b_ref[...]


def _ada(c, w_ada, b_ada, layer):
    B, D = c.shape
    n_out = w_ada.shape[-1]
    tn = 768
    return pl.pallas_call(
        _ada_kernel,
        grid=(n_out // tn,),
        in_specs=[
            pl.BlockSpec((B, D), lambda j: (0, 0)),
            pl.BlockSpec((None, D, tn), lambda j: (layer, 0, j)),
            pl.BlockSpec((None, 1, tn), lambda j: (layer, 0, j)),
        ],
        out_specs=pl.BlockSpec((B, tn), lambda j: (0, j)),
        out_shape=jax.ShapeDtypeStruct((B, n_out), F32),
        compiler_params=_cparams("arbitrary"),
        name="ada",
    )(c, w_ada, b_ada.reshape(b_ada.shape[0], 1, n_out))


def _inproj_nat_kernel(x_ref, mod_ref, w_ref, o_ref, u_ref):
    S = x_ref.shape[0]

    @pl.when(pl.program_id(1) == 0)
    def _():
        shift = mod_ref[0:1, :]
        scale1p = 1.0 + mod_ref[1:2, :]

        def body(i, carry):
            rows = pl.ds(pl.multiple_of(i * ROW_CHUNK, ROW_CHUNK), ROW_CHUNK)
            u_ref[rows, :] = (x_ref[rows, :] * scale1p + shift).astype(BF16)
            return carry

        lax.fori_loop(0, S // ROW_CHUNK, body, 0)

    o_ref[...] = _dot(u_ref[...], w_ref[...]).astype(BF16)


def _inproj_nat(x, mod, w_nat):
    B, S, D = x.shape
    n_out = w_nat.shape[1]
    tn = 512
    return pl.pallas_call(
        _inproj_nat_kernel,
        grid=(B, n_out // tn),
        in_specs=[
            pl.BlockSpec((None, S, D), lambda b, j: (b, 0, 0)),
            pl.BlockSpec((None, 6, D), lambda b, j: (b, 0, 0)),
            pl.BlockSpec((D, tn), lambda b, j: (0, j)),
        ],
        out_specs=[
            pl.BlockSpec((None, S, tn), lambda b, j: (b, 0, j)),
            pl.BlockSpec((None, S, D), lambda b, j: (b, 0, 0)),
        ],
        out_shape=[
            jax.ShapeDtypeStruct((B, S, n_out), BF16),
            jax.ShapeDtypeStruct((B, S, D), BF16),
        ],
        compiler_params=_cparams("arbitrary", "arbitrary"),
        name="inproj_nat",
    )(x, mod, w_nat)


def _inproj_dil_kernel(u_ref, w_ref, o0_ref, o1_ref, o2_ref, acc_ref):
    S = u_ref.shape[0]
    for g, o_ref in enumerate((o0_ref, o1_ref, o2_ref)):
        d = ATTN_GROUPS[g][1]
        L = S // d
        acc = _dot(u_ref[...], w_ref[g])
        if d == 1:
            o_ref[0] = acc.astype(BF16)
            continue
        for c in range(acc_ref.shape[0]):
            acc_ref[c] = acc[:, c * LANES:(c + 1) * LANES]
        for rho in range(d):
            for c in range(acc_ref.shape[0]):
                o_ref[rho, :, c * LANES:(c + 1) * LANES] = (
                    acc_ref[c, pl.ds(rho, L, stride=d), :].astype(BF16))


def _inproj_dil(u, w_qkv):
    B, S, D = u.shape
    W = w_qkv.shape[-1]
    dils = [d for _, d in ATTN_GROUPS]
    return pl.pallas_call(
        _inproj_dil_kernel,
        grid=(B,),
        in_specs=[
            pl.BlockSpec((None, S, D), lambda b: (b, 0, 0)),
            pl.BlockSpec((len(dils), D, W), lambda b: (0, 0, 0)),
        ],
        out_specs=[pl.BlockSpec((None, d, S // d, W), lambda b: (b, 0, 0, 0)) for d in dils],
        out_shape=[jax.ShapeDtypeStruct((B, d, S // d, W), BF16) for d in dils],
        scratch_shapes=[pltpu.VMEM((W // LANES, S, LANES), F32)],
        compiler_params=_cparams("arbitrary"),
        name="inproj_dil",
    )(u, w_qkv)


def _window_start(i, L, W):
    return min(max(QBLK * i - HALF_WINDOW, 0), L - W)


def _attn_bias_tables(S):
    h = np.arange(1, N_ATTN_HEADS + 1, dtype=np.float32)
    slopes = (2.0 ** (-8.0 * h / N_ATTN_HEADS)).astype(np.float32)
    tables = []
    for g, (_, d) in enumerate(ATTN_GROUPS):
        L = S // d
        W = min(2 * QBLK, L)
        nb = L // QBLK
        variants = sorted({_window_start(i, L, W) - QBLK * i for i in range(nb)}, reverse=True)
        r = np.arange(QBLK)[:, None]
        c = np.arange(W)[None, :]
        tab = np.empty((len(variants), HEADS_PER_GROUP * QBLK, W), np.float32)
        for vi, delta in enumerate(variants):
            off = np.abs(c - r + delta)
            for hh in range(HEADS_PER_GROUP):
                slope = np.float64(slopes[g * HEADS_PER_GROUP + hh])
                bias = np.where(off <= HALF_WINDOW, -slope * (d * off), NEG_BIG)
                tab[vi, hh * QBLK:(hh + 1) * QBLK] = bias.astype(np.float32)
        tables.append(tab)
    return tables


def _block_attention(q, kwin, vwin, bias, head_of_lane):
    zero = jnp.zeros_like(q)
    q4 = jnp.concatenate([jnp.where(head_of_lane == h, q, zero) for h in range(HEADS_PER_GROUP)], axis=0)
    s = lax.dot_general(q4, kwin, (((1,), (1,)), ((), ())), preferred_element_type=F32)
    s = s * (1.0 / math.sqrt(HEAD_DIM)) + bias
    m = jnp.max(s, axis=-1, keepdims=True)
    p = jnp.exp(s - m)
    l = jnp.sum(p, axis=-1, keepdims=True)
    o4 = _dot(p.astype(BF16), vwin) * (1.0 / l)
    lse4 = m + jnp.log(l)
    out = jnp.zeros((QBLK, GROUP_WIDTH), F32)
    lse = jnp.zeros((QBLK, GROUP_WIDTH), F32)
    for h in range(HEADS_PER_GROUP):
        rows = slice(h * QBLK, (h + 1) * QBLK)
        sel = head_of_lane == h
        out = jnp.where(sel, o4[rows], out)
        lse = jnp.where(sel, lse4[rows], lse)
    return out, lse


def _attn_kernel(q0_ref, q1_ref, q2_ref, b0_ref, b1_ref, b2_ref, o_ref, acc_ref, lse_ref):
    S = o_ref.shape[0]
    GW = GROUP_WIDTH
    head_of_lane = lax.broadcasted_iota(jnp.int32, (QBLK, GW), 1) // HEAD_DIM
    n_planes = GW // LANES

    def put(g, rows, out, lse):
        for c in range(n_planes):
            acc_ref[g, c, rows, :] = out[:, c * LANES:(c + 1) * LANES]
            lse_ref[g, c, rows, :] = lse[:, c * LANES:(c + 1) * LANES]

    L0 = S
    nb0 = L0 // QBLK

    def g0_body(i, carry):
        ws = pl.multiple_of(jnp.clip(QBLK * i - HALF_WINDOW, 0, L0 - 2 * QBLK), HALF_WINDOW)
        var = jnp.where(i == 0, 0, jnp.where(i == nb0 - 1, 2, 1))
        rows = pl.ds(pl.multiple_of(i * QBLK, QBLK), QBLK)
        out, lse = _block_attention(
            q0_ref[0, rows, 0:GW],
            q0_ref[0, pl.ds(ws, 2 * QBLK), GW:2 * GW],
            q0_ref[0, pl.ds(ws, 2 * QBLK), 2 * GW:3 * GW],
            b0_ref[var], head_of_lane)
        put(0, rows, out, lse)
        return carry

    lax.fori_loop(0, nb0, g0_body, 0)

    for g, (q_ref, b_ref) in ((1, (q1_ref, b1_ref)), (2, (q2_ref, b2_ref))):
        d = ATTN_GROUPS[g][1]
        L = S // d
        W = min(2 * QBLK, L)
        nb = L // QBLK
        deltas = sorted({_window_start(i, L, W) - QBLK * i for i in range(nb)}, reverse=True)

        def gd_body(rho, carry, q_ref=q_ref, b_ref=b_ref, d=d, L=L, W=W, nb=nb, deltas=deltas, g=g):
            for i in range(nb):
                ws = _window_start(i, L, W)
                var = deltas.index(ws - QBLK * i)
                out, lse = _block_attention(
                    q_ref[rho, i * QBLK:(i + 1) * QBLK, 0:GW],
                    q_ref[rho, ws:ws + W, GW:2 * GW],
                    q_ref[rho, ws:ws + W, 2 * GW:3 * GW],
                    b_ref[var], head_of_lane)
                dst = pl.ds(rho + i * QBLK * d, QBLK, stride=d)
                put(g, dst, out, lse)
            return carry

        lax.fori_loop(0, d, gd_body, 0)

    def merge_body(i, carry):
        rows = pl.ds(pl.multiple_of(i * ROW_CHUNK, ROW_CHUNK), ROW_CHUNK)
        for c in range(n_planes):
            l0, l1, l2 = lse_ref[0, c, rows, :], lse_ref[1, c, rows, :], lse_ref[2, c, rows, :]
            mx = jnp.maximum(jnp.maximum(l0, l1), l2)
            w0, w1, w2 = jnp.exp(l0 - mx), jnp.exp(l1 - mx), jnp.exp(l2 - mx)
            num = w0 * acc_ref[0, c, rows, :] + w1 * acc_ref[1, c, rows, :] + w2 * acc_ref[2, c, rows, :]
            o_ref[rows, c * LANES:(c + 1) * LANES] = (num * (1.0 / (w0 + w1 + w2))).astype(BF16)
        return carry

    lax.fori_loop(0, S // ROW_CHUNK, merge_body, 0)


def _attention(qkv, bias_tables):
    q0, q1, q2 = qkv
    B, _, S, W3 = q0.shape
    GW = GROUP_WIDTH
    in_specs = [pl.BlockSpec((None,) + q.shape[1:], lambda b: (b, 0, 0, 0)) for q in qkv]
    in_specs += [pl.BlockSpec(t.shape, lambda b: (0, 0, 0)) for t in bias_tables]
    return pl.pallas_call(
        _attn_kernel,
        grid=(B,),
        in_specs=in_specs,
        out_specs=pl.BlockSpec((None, S, GW), lambda b: (b, 0, 0)),
        out_shape=jax.ShapeDtypeStruct((B, S, GW), BF16),
        scratch_shapes=[pltpu.VMEM((3, GW // LANES, S, LANES), F32)] * 2,
        compiler_params=_cparams("arbitrary"),
        name="attention",
    )(q0, q1, q2, *bias_tables)


def _pool_kernel(y_ref, w_ref, sc_ref, o_ref, pad_ref):
    S = y_ref.shape[0]
    C = y_ref.shape[1]
    P = PAD_ROWS
    pad_ref[0:P, :] = jnp.zeros((P, C), F32)
    pad_ref[S + P:S + 2 * P, :] = jnp.zeros((P, C), F32)
    pad_ref[P:S + P, :] = y_ref[...].astype(F32)
    for c in range(S // ROW_CHUNK):
        r0 = c * ROW_CHUNK
        t = r0 + lax.broadcasted_iota(jnp.int32, (ROW_CHUNK, 1), 0)
        for g, w in enumerate(POOL_WINDOWS):
            cols = slice(g * POOL_GROUP, (g + 1) * POOL_GROUP)
            acc = pad_ref[P + r0 - w // 2:P + r0 - w // 2 + ROW_CHUNK, cols]
            for j in range(-w // 2 + 1, w // 2):
                acc = acc + pad_ref[P + r0 + j:P + r0 + j + ROW_CHUNK, cols]
            cnt = (jnp.minimum(t + w // 2, S) - jnp.maximum(t - w // 2, 0)).astype(F32)
            pooled = acc / cnt - pad_ref[P + r0:P + r0 + ROW_CHUNK, cols]
            mixed = _dot(pooled.astype(BF16), w_ref[g].astype(BF16)) * sc_ref[:, cols]
            o_ref[r0:r0 + ROW_CHUNK, cols] = mixed.astype(BF16)


def _pool(proj, col_block, w_pool, pool_scale, layer):
    B, S, _ = proj.shape
    C = POOL_GROUP * len(POOL_WINDOWS)
    return pl.pallas_call(
        _pool_kernel,
        grid=(B,),
        in_specs=[
            pl.BlockSpec((None, S, C), lambda b: (b, 0, col_block)),
            pl.BlockSpec((None,) + w_pool.shape[1:], lambda b: (layer, 0, 0, 0)),
            pl.BlockSpec((None, 1, C), lambda b: (layer, 0, 0)),
        ],
        out_specs=pl.BlockSpec((None, S, C), lambda b: (b, 0, 0)),
        out_shape=jax.ShapeDtypeStruct((B, S, C), BF16),
        scratch_shapes=[pltpu.VMEM((S + 2 * PAD_ROWS, C), F32)],
        compiler_params=_cparams("arbitrary"),
        name="pool",
    )(proj, w_pool, pool_scale.reshape(pool_scale.shape[0], 1, C))


def _filter_kernel(z_ref, w1_ref, b1_ref, wi_ref, bi_ref, wo_ref, fr_ref, dec_ref, o_ref):
    S = z_ref.shape[0]
    C = HYENA_WIDTH
    freq = fr_ref[...]
    h = jnp.sin(freq * (_dot3(z_ref[...], w1_ref[...]) + b1_ref[...]))
    for i in range(wi_ref.shape[0]):
        h = jnp.sin(freq * (_dot3(h, wi_ref[i]) + bi_ref[i]))
    not_first = lax.broadcasted_iota(jnp.int32, (S, 1), 0) > 0
    for blk in range(wo_ref.shape[1] // C):
        cols = slice(blk * C, (blk + 1) * C)
        f = _dot3(h, wo_ref[:, cols]) * dec_ref[...]
        if blk % 2 == 1:
            f = jnp.where(not_first, f, 0.0)
        o_ref[:, cols] = f.astype(BF16)


def _hyena_filters(zfeat, decay, f_w1, f_b1, f_w_inner, f_b_inner, f_w_out, f_freq, layer):
    S, E = zfeat.shape
    H = FILTER_HIDDEN
    n_out = f_w_out.shape[-1]
    n_inner = f_w_inner.shape[1]
    w1 = jnp.zeros((f_w1.shape[0], E, H), F32).at[:, :f_w1.shape[1]].set(f_w1)
    full = lambda shape: pl.BlockSpec(shape, lambda i: (0,) * len(shape))
    lay = lambda shape: pl.BlockSpec((None,) + shape, lambda i: (layer,) + (0,) * len(shape))
    return pl.pallas_call(
        _filter_kernel,
        grid=(1,),
        in_specs=[
            full((S, E)), lay((E, H)), lay((1, H)), lay((n_inner, H, H)), lay((n_inner, 1, H)),
            lay((H, n_out)), lay((1, H)), full((S, HYENA_WIDTH)),
        ],
        out_specs=full((S, n_out)),
        out_shape=jax.ShapeDtypeStruct((S, n_out), BF16),
        compiler_params=_cparams("arbitrary"),
        name="hyena_filter",
    )(zfeat, w1, f_b1.reshape(-1, 1, H), f_w_inner, f_b_inner.reshape(-1, n_inner, 1, H), f_w_out,
      f_freq.reshape(-1, 1, H), decay)


def _spec_kernel(f_ref, h_ref, a_ref, b_ref, c_ref):
    TK = f_ref.shape[0] // 2
    C = HYENA_WIDTH
    n_fft = 2 * h_ref.shape[0]
    x = _dot(f_ref[...], h_ref[...])
    xc, xs = x[:TK], x[TK:]
    row = pl.program_id(0) * TK + lax.broadcasted_iota(jnp.int32, (TK, 1), 0)
    is0 = row == 0
    s = jnp.where(is0, 1.0 / n_fft, 2.0 / n_fft)
    for o in range(a_ref.shape[0]):
        hf = slice(2 * o * C, (2 * o + 1) * C)
        hb = slice((2 * o + 1) * C, (2 * o + 2) * C)
        a = (xc[:, hf] + xc[:, hb]) * s
        a_ref[o] = a
        b_ref[o] = jnp.where(is0, 0.0, (xs[:, hf] - xs[:, hb]) * s)
        c_ref[o] = jnp.where(is0, (xs[:, hf] + xs[:, hb]) * (1.0 / n_fft), a)


def _filter_spectra(fmat, filt):
    nkt, tk2, S = fmat.shape
    TK = tk2 // 2
    C = HYENA_WIDTH
    n_ord = filt.shape[1] // (2 * C)
    out = jax.ShapeDtypeStruct((n_ord, S, C), F32)
    ospec = pl.BlockSpec((n_ord, TK, C), lambda k: (0, k, 0))
    return pl.pallas_call(
        _spec_kernel,
        grid=(nkt,),
        in_specs=[
            pl.BlockSpec((None, tk2, S), lambda k: (k, 0, 0)),
            pl.BlockSpec(filt.shape, lambda k: (0, 0)),
        ],
        out_specs=[ospec, ospec, ospec],
        out_shape=[out, out, out],
        compiler_params=_cparams("arbitrary"),
        name="hyena_spectra",
    )(fmat, filt)


def _hyena_kernel(v_ref, x1_ref, x2_ref, f_ref, g_ref, a_ref, b_ref, c_ref, wsc_ref, bsc_ref, hb_ref,
                  o_ref, pad_ref, z_ref, zb_ref, y_ref):
    S, C = z_ref.shape
    TK = f_ref.shape[0] // 2
    P = PAD_ROWS
    o = pl.program_id(1)
    kt = pl.program_id(2)
    last_kt = pl.num_programs(2) - 1

    def short_conv(in_ref, blk, consume):
        cols = slice(blk * C, (blk + 1) * C)
        pad_ref[0:P, :] = jnp.zeros((P, C), F32)
        pad_ref[S + P:S + 2 * P, :] = jnp.zeros((P, C), F32)
        pad_ref[P:S + P, :] = in_ref[...].astype(F32)
        w0, w1, w2 = wsc_ref[0:1, cols], wsc_ref[1:2, cols], wsc_ref[2:3, cols]
        bias = bsc_ref[:, cols]
        for c in range(S // ROW_CHUNK):
            r0 = c * ROW_CHUNK
            conv = (w0 * pad_ref[P + r0 - 1:P + r0 - 1 + ROW_CHUNK, :]
                    + w1 * pad_ref[P + r0:P + r0 + ROW_CHUNK, :]
                    + w2 * pad_ref[P + r0 + 1:P + r0 + 1 + ROW_CHUNK, :] + bias)
            consume(r0, conv)

    @pl.when((o == 0) & (kt == 0))
    def _():
        def store_z(r0, conv):
            z_ref[r0:r0 + ROW_CHUNK, :] = conv
            zb_ref[r0:r0 + ROW_CHUNK, :] = conv.astype(BF16)
        short_conv(v_ref, 0, store_z)

    x = _dot(f_ref[...], zb_ref[...])
    xre, xim = x[:TK], x[TK:]
    a, b, c = a_ref[...], b_ref[...], c_ref[...]
    ycat = jnp.concatenate([xre * a - xim * b, xre * b + xim * c], axis=0).astype(BF16)
    contrib = _dot(g_ref[...], ycat)

    @pl.when(kt == 0)
    def _():
        y_ref[...] = contrib

    @pl.when(kt > 0)
    def _():
        y_ref[...] += contrib

    def finish(gate_ref, blk, order, is_last):
        hbias = hb_ref[order:order + 1, :]

        def consume(r0, gate):
            rows = slice(r0, r0 + ROW_CHUNK)
            znew = gate * (y_ref[rows, :] + z_ref[rows, :] * hbias)
            if is_last:
                o_ref[rows, :] = znew.astype(BF16)
            else:
                z_ref[rows, :] = znew
                zb_ref[rows, :] = znew.astype(BF16)
        short_conv(gate_ref, blk, consume)

    @pl.when((kt == last_kt) & (o == 0))
    def _():
        finish(x1_ref, 1, 0, False)

    @pl.when((kt == last_kt) & (o == 1))
    def _():
        finish(x2_ref, 2, 1, True)


def _hyena(proj, first_col_block, fmat, gmat, spectra, w_sconv, b_sconv, hy_bias, layer):
    B, S, _ = proj.shape
    C = HYENA_WIDTH
    nkt, tk2, _ = fmat.shape
    TK = tk2 // 2
    a, b, c = spectra
    n_ord = a.shape[0]
    inp = lambda blk: pl.BlockSpec((None, S, C), lambda bb, o, k: (bb, 0, first_col_block + blk))
    spec = pl.BlockSpec((None, TK, C), lambda bb, o, k: (o, k, 0))
    return pl.pallas_call(
        _hyena_kernel,
        grid=(B, n_ord, nkt),
        in_specs=[
            inp(0), inp(1), inp(2),
            pl.BlockSpec((None, tk2, S), lambda bb, o, k: (k, 0, 0)),
            pl.BlockSpec((None, S, tk2), lambda bb, o, k: (k, 0, 0)),
            spec, spec, spec,
            pl.BlockSpec((None, 3, 3 * C), lambda bb, o, k: (layer, 0, 0)),
            pl.BlockSpec((None, 1, 3 * C), lambda bb, o, k: (layer, 0, 0)),
            pl.BlockSpec((None, n_ord, C), lambda bb, o, k: (layer, 0, 0)),
        ],
        out_specs=pl.BlockSpec((None, S, C), lambda bb, o, k: (bb, 0, 0)),
        out_shape=jax.ShapeDtypeStruct((B, S, C), BF16),
        scratch_shapes=[
            pltpu.VMEM((S + 2 * PAD_ROWS, C), F32),
            pltpu.VMEM((S, C), F32),
            pltpu.VMEM((S, C), BF16),
            pltpu.VMEM((S, C), F32),
        ],
        compiler_params=_cparams("arbitrary", "arbitrary", "arbitrary"),
        name="hyena_conv",
    )(proj, proj, proj, fmat, gmat, a, b, c, w_sconv, b_sconv.reshape(b_sconv.shape[0], 1, 3 * C), hy_bias)


def _dft_matrices(S, TK):
    n_fft = 2 * S
    k = lax.broadcasted_iota(jnp.int32, (S, S), 0)
    n = lax.broadcasted_iota(jnp.int32, (S, S), 1)
    theta = ((k * n) & (n_fft - 1)).astype(F32) * (2.0 * math.pi / n_fft)
    fc = jnp.cos(theta)
    fs = jnp.where(k == 0, (1 - 2 * (n & 1)).astype(F32), -jnp.sin(theta))
    nkt = S // TK
    fmat = jnp.concatenate([fc.reshape(nkt, TK, S), fs.reshape(nkt, TK, S)], axis=1).astype(BF16)
    return fmat, jnp.swapaxes(fmat, 1, 2)


def _filter_features(S):
    t = jnp.linspace(0.0, 1.0, S, dtype=F32)[:, None]
    bands = (FILTER_EMB - 1) // 2
    w = 2.0 * math.pi * jnp.arange(S, dtype=F32) / S
    f = jnp.linspace(1e-4, bands - 1, bands, dtype=F32)
    ang = w[:, None] * f[None, :]
    z = jnp.concatenate([t, jnp.cos(ang), -jnp.sin(ang)], axis=-1)
    z = jnp.pad(z, ((0, 0), (0, 128 - FILTER_EMB)))
    min_decay = math.log(DECAY_TARGET) / SLOW_DECAY_PCT
    max_decay = math.log(DECAY_TARGET) / FAST_DECAY_PCT
    deltas = jnp.linspace(min_decay, max_decay, HYENA_WIDTH, dtype=F32)
    decay = jnp.exp(-t * jnp.abs(deltas)[None, :])
    return z, decay


def _merge_kernel(ya_ref, yb_ref, yc_ref, gl_ref, x_ref, mod_ref, pa_ref, pp_ref, ph_ref, wo_ref,
                  g_ref, b_ref, wr_ref, br_ref, x1_ref, u2_ref, lt_ref):
    D = x_ref.shape[1]
    sig = lambda v: 1.0 / (1.0 + jnp.exp(-v))
    merged = sig(gl_ref[:, 0:D].astype(F32)) * _dot(ya_ref[...], pa_ref[...])
    merged += sig(gl_ref[:, D:2 * D].astype(F32)) * _dot(yb_ref[...], pp_ref[...])
    merged += sig(gl_ref[:, 2 * D:3 * D].astype(F32)) * _dot(yc_ref[...], ph_ref[...])
    h = _dot(merged.astype(BF16), wo_ref[...])
    gate1, shift2, scale2 = mod_ref[2:3, :], mod_ref[3:4, :], mod_ref[4:5, :]
    x1 = _layer_norm(ALPHA_RES * x_ref[...] + (1.0 + gate1) * h, g_ref[...], b_ref[...])
    x1_ref[...] = x1
    u2 = x1 * (1.0 + scale2) + shift2
    _store_token_tiles(u2_ref, u2)
    logits = _dot3(u2, wr_ref[...]) + br_ref[...]
    lt_ref[...] = logits.T[0:N_EXPERTS, :]


def _merge(ya, yb, yc, proj, x, mod, p_attn, p_pool, p_hyena, w_out, ln_g, ln_b, w_router, b_router, layer):
    B, S, D = x.shape
    N = B * S
    TM = MERGE_ROWS
    tiles_per_seq = S // TM
    EP = 128
    flat = lambda t: t.reshape(N, t.shape[-1])
    wr = jnp.zeros((D, EP), F32).at[:, :N_EXPERTS].set(w_router[layer])
    br = jnp.zeros((1, EP), F32).at[:, :N_EXPERTS].set(b_router[layer][None])
    tok = lambda w: pl.BlockSpec((TM, w), lambda i: (i, 0))
    lay = lambda t: pl.BlockSpec((None,) + t.shape[1:], lambda i: (layer,) + (0,) * (t.ndim - 1))
    ln_g3, ln_b3 = ln_g.reshape(-1, 1, D), ln_b.reshape(-1, 1, D)
    return pl.pallas_call(
        _merge_kernel,
        grid=(N // TM,),
        in_specs=[
            tok(ya.shape[-1]), tok(yb.shape[-1]), tok(yc.shape[-1]), tok(3 * D), tok(D),
            pl.BlockSpec((None, 6, D), lambda i: (i // tiles_per_seq, 0, 0)),
            lay(p_attn), lay(p_pool), lay(p_hyena), lay(w_out), lay(ln_g3), lay(ln_b3),
            pl.BlockSpec((D, EP), lambda i: (0, 0)),
            pl.BlockSpec((1, EP), lambda i: (0, 0)),
        ],
        out_specs=[tok(D), pl.BlockSpec((TM * D // LANES, LANES), lambda i: (i, 0)),
                   pl.BlockSpec((N_EXPERTS, TM), lambda i: (0, i))],
        out_shape=[
            jax.ShapeDtypeStruct((N, D), F32),
            jax.ShapeDtypeStruct((N * D // LANES, LANES), F32),
            jax.ShapeDtypeStruct((N_EXPERTS, N), F32),
        ],
        compiler_params=_cparams("arbitrary"),
        name="merge_ln1_router",
    )(flat(ya), flat(yb), flat(yc), flat(proj), flat(x), mod, p_attn, p_pool, p_hyena, w_out,
      ln_g3, ln_b3, wr, br)


def _topk_kernel(l_ref, e_ref, g_ref, r_ref, cnt_ref, carry_ref):
    E, TL = l_ref.shape

    @pl.when(pl.program_id(0) == 0)
    def _():
        carry_ref[...] = jnp.zeros_like(carry_ref)

    l = l_ref[...]
    eio = lax.broadcasted_iota(jnp.int32, (E, TL), 0).astype(F32)
    vals, hots = [], []
    for k in range(TOP_K):
        m = jnp.max(l, axis=0, keepdims=True)
        idx = jnp.min(jnp.where(l == m, eio, float(E)), axis=0, keepdims=True)
        hot = eio == idx
        l = jnp.where(hot, -jnp.inf, l)
        vals.append(m)
        hots.append(hot)
        e_ref[k:k + 1, :] = idx.astype(jnp.int32)
    exps = [jnp.exp(v - vals[0]) for v in vals]
    inv = 1.0 / (exps[0] + exps[1] + exps[2] + exps[3])
    for k in range(TOP_K):
        g_ref[k:k + 1, :] = exps[k] * inv

    memb = jnp.zeros((E, TL), F32)
    for hot in hots:
        memb = memb + hot.astype(F32)
    upper = (lax.broadcasted_iota(jnp.int32, (TL, TL), 0)
             <= lax.broadcasted_iota(jnp.int32, (TL, TL), 1)).astype(BF16)
    incl = _dot(memb.astype(BF16), upper)
    excl = incl - memb + carry_ref[:, 0:1]
    for k in range(TOP_K):
        rank = jnp.sum(jnp.where(hots[k], excl, 0.0), axis=0, keepdims=True)
        r_ref[k:k + 1, :] = rank.astype(jnp.int32)
    carry_ref[...] = carry_ref[...] + jnp.sum(memb, axis=1, keepdims=True)
    cnt_ref[...] = carry_ref[...]


def _topk(logits_t):
    E, N = logits_t.shape
    TL = TOPK_LANES
    tok = pl.BlockSpec((TOP_K, TL), lambda i: (0, i))
    return pl.pallas_call(
        _topk_kernel,
        grid=(N // TL,),
        in_specs=[pl.BlockSpec((E, TL), lambda i: (0, i))],
        out_specs=[tok, tok, tok, pl.BlockSpec((E, 128), lambda i: (0, 0))],
        out_shape=[
            jax.ShapeDtypeStruct((TOP_K, N), jnp.int32),
            jax.ShapeDtypeStruct((TOP_K, N), F32),
            jax.ShapeDtypeStruct((TOP_K, N), jnp.int32),
            jax.ShapeDtypeStruct((E, 128), F32),
        ],
        scratch_shapes=[pltpu.VMEM((E, 128), F32)],
        compiler_params=_cparams("arbitrary"),
        name="topk_rank",
    )(logits_t)


def _dest_kernel(offs_ref, e_ref, r_ref, d_ref):
    e = e_ref[...]
    acc = r_ref[...]
    for ex in range(N_EXPERTS):
        acc = acc + jnp.where(e == ex, offs_ref[ex], 0)
    d_ref[...] = acc


def _dest_rows(offs, top_e, rank):
    K, N = top_e.shape
    TL = TOPK_LANES
    tok = pl.BlockSpec((K, TL), lambda i, offs: (0, i))
    return pl.pallas_call(
        _dest_kernel,
        grid_spec=pltpu.PrefetchScalarGridSpec(
            num_scalar_prefetch=1, grid=(N // TL,), in_specs=[tok, tok], out_specs=tok),
        out_shape=jax.ShapeDtypeStruct((K, N), jnp.int32),
        compiler_params=_cparams("arbitrary"),
        name="dest_rows",
    )(offs, top_e, rank)


def _dispatch_kernel(tail_ref, n_used_ref, dest_ref, u_ref, xs_ref, zero_ref, sem):
    TT = dest_ref.shape[1]
    RT = u_ref.shape[0] // TT
    blk_rows = zero_ref.shape[0]
    n_blocks = xs_ref.shape[0] // blk_rows

    def zero_copy(first_row):
        start = pl.multiple_of(first_row * RT, RT)
        return pltpu.make_async_copy(zero_ref, xs_ref.at[pl.ds(start, blk_rows), :], sem)

    @pl.when(pl.program_id(0) == 0)
    def _():
        zero_ref[...] = jnp.zeros_like(zero_ref)

        def tails(fn):
            def body(e, carry):
                @pl.when(tail_ref[e] >= 0)
                def _():
                    fn(zero_copy(jnp.maximum(tail_ref[e], 0)))
                return carry
            lax.fori_loop(0, N_EXPERTS, body, 0)

        def unused(fn):
            def body(i, carry):
                fn(zero_copy(i * (blk_rows // RT)))
                return carry
            lax.fori_loop(n_used_ref[0], n_blocks, body, 0)

        tails(lambda cp: cp.start())
        unused(lambda cp: cp.start())
        tails(lambda cp: cp.wait())
        unused(lambda cp: cp.wait())

    def row_copy(j, k):
        src = pl.multiple_of(j * RT, RT)
        dst = pl.multiple_of(dest_ref[k, j] * RT, RT)
        return pltpu.make_async_copy(u_ref.at[pl.ds(src, RT), :], xs_ref.at[pl.ds(dst, RT), :], sem)

    def start(j, carry):
        for k in range(TOP_K):
            row_copy(j, k).start()
        return carry

    def wait(j, carry):
        for k in range(TOP_K):
            row_copy(j, k).wait()
        return carry

    lax.fori_loop(0, TT, start, 0)
    lax.fori_loop(0, TT, wait, 0)


def _dispatch(tail_start, n_used, dest, u2_tiles, n_rows, D):
    RT = D // LANES
    N = u2_tiles.shape[0] // RT
    TT = MOE_TOK
    return pl.pallas_call(
        _dispatch_kernel,
        grid_spec=pltpu.PrefetchScalarGridSpec(
            num_scalar_prefetch=2,
            grid=(N // TT,),
            in_specs=[
                pl.BlockSpec((TOP_K, TT), lambda i, tail, nu: (0, i), memory_space=pltpu.SMEM),
                pl.BlockSpec((TT * RT, LANES), lambda i, tail, nu: (i, 0)),
            ],
            out_specs=pl.BlockSpec(memory_space=pl.ANY),
            scratch_shapes=[pltpu.VMEM((MOE_BLK * RT, LANES), F32), pltpu.SemaphoreType.DMA(())],
        ),
        out_shape=jax.ShapeDtypeStruct((n_rows * RT, LANES), F32),
        compiler_params=_cparams("arbitrary"),
        name="moe_dispatch",
    )(tail_start, n_used, dest, u2_tiles)


def _ffn_kernel(blk_e_ref, n_used_ref, xs_ref, w1_ref, b1_ref, w2_ref, b2_ref, ys_ref):
    F, D = w2_ref.shape
    BLK = xs_ref.shape[0] * LANES // D

    @pl.when(pl.program_id(0) < n_used_ref[0])
    def _():
        x = jnp.concatenate(_load_token_tiles(xs_ref, BLK, D), axis=1).astype(BF16)
        h = _dot(x, w1_ref[...]) + b1_ref[...]
        glu = jnp.minimum(h[:, :F], SWIGLU_LIMIT)
        lin = jnp.clip(h[:, F:], -SWIGLU_LIMIT, SWIGLU_LIMIT)
        act = glu * (1.0 / (1.0 + jnp.exp(-SWIGLU_ALPHA * glu))) * (lin + 1.0)
        _store_token_tiles(ys_ref, _dot(act.astype(BF16), w2_ref[...]) + b2_ref[...])

    @pl.when(pl.program_id(0) >= n_used_ref[0])
    def _():
        ys_ref[...] = jnp.zeros_like(ys_ref)


def _expert_ffn(blk_e, n_used, xs_tiles, w1, b1, w2, b2):
    E, D, F2 = w1.shape
    F = F2 // 2
    RT = D // LANES
    BLK = MOE_BLK
    rows = pl.BlockSpec((BLK * RT, LANES), lambda i, be, nu: (i, 0))
    return pl.pallas_call(
        _ffn_kernel,
        grid_spec=pltpu.PrefetchScalarGridSpec(
            num_scalar_prefetch=2,
            grid=(xs_tiles.shape[0] // (BLK * RT),),
            in_specs=[
                rows,
                pl.BlockSpec((None, D, F2), lambda i, be, nu: (be[i], 0, 0)),
                pl.BlockSpec((None, 1, F2), lambda i, be, nu: (be[i], 0, 0)),
                pl.BlockSpec((None, F, D), lambda i, be, nu: (be[i], 0, 0)),
                pl.BlockSpec((None, 1, D), lambda i, be, nu: (be[i], 0, 0)),
            ],
            out_specs=rows,
        ),
        out_shape=jax.ShapeDtypeStruct(xs_tiles.shape, F32),
        compiler_params=_cparams("arbitrary"),
        name="moe_ffn",
    )(blk_e, n_used, xs_tiles, w1, b1.reshape(E, 1, F2), w2, b2.reshape(E, 1, D))


def _combine_kernel(dest_ref, gt_ref, x1_ref, mod_ref, g_ref, b_ref, ys_ref, o_ref, buf_ref, sem):
    TT, D = x1_ref.shape
    RT = D // LANES

    def row_copy(j, k):
        src = pl.multiple_of(dest_ref[k, j] * RT, RT)
        dst = pl.multiple_of(j * RT, RT)
        return pltpu.make_async_copy(ys_ref.at[pl.ds(src, RT), :], buf_ref.at[k, pl.ds(dst, RT), :], sem)

    def start(j, carry):
        for k in range(TOP_K):
            row_copy(j, k).start()
        return carry

    def wait(j, carry):
        for k in range(TOP_K):
            row_copy(j, k).wait()
        return carry

    lax.fori_loop(0, TT, start, 0)
    lax.fori_loop(0, TT, wait, 0)

    chunks = None
    for k in range(TOP_K):
        gate = gt_ref[:, k:k + 1]
        rows = [gate * t for t in _load_token_tiles(buf_ref, TT, D, plane=k)]
        chunks = rows if chunks is None else [a + b for a, b in zip(chunks, rows)]
    h = jnp.concatenate(chunks, axis=1)
    gate2 = mod_ref[5:6, :]
    o_ref[...] = _layer_norm(ALPHA_RES * x1_ref[...] + (1.0 + gate2) * h, g_ref[...], b_ref[...])


def _combine(dest, gates_t, x1, mod, ln_g, ln_b, ys, seq_len, layer):
    N, D = x1.shape
    TT = MOE_TOK
    tiles_per_seq = seq_len // TT
    ln_g3, ln_b3 = ln_g.reshape(-1, 1, D), ln_b.reshape(-1, 1, D)
    lay = lambda t: pl.BlockSpec((None,) + t.shape[1:], lambda i: (layer,) + (0,) * (t.ndim - 1))
    return pl.pallas_call(
        _combine_kernel,
        grid=(N // TT,),
        in_specs=[
            pl.BlockSpec((TOP_K, TT), lambda i: (0, i), memory_space=pltpu.SMEM),
            pl.BlockSpec((TT, TOP_K), lambda i: (i, 0)),
            pl.BlockSpec((TT, D), lambda i: (i, 0)),
            pl.BlockSpec((None, 6, D), lambda i: (i // tiles_per_seq, 0, 0)),
            lay(ln_g3), lay(ln_b3),
            pl.BlockSpec(memory_space=pl.ANY),
        ],
        out_specs=pl.BlockSpec((TT, D), lambda i: (i, 0)),
        out_shape=jax.ShapeDtypeStruct((N, D), F32),
        scratch_shapes=[pltpu.VMEM((TOP_K, TT * D // LANES, LANES), F32), pltpu.SemaphoreType.DMA(())],
        compiler_params=_cparams("arbitrary"),
        name="moe_combine_ln2",
    )(dest, gates_t, x1, mod, ln_g3, ln_b3, ys)


def _moe(u2_tiles, logits_t, x1, mod, w1, b1, w2, b2, ln_g, ln_b, seq_len, layer):
    N, D = x1.shape
    BLK = MOE_BLK
    top_e, gates, rank, counts = _topk(logits_t)
    counts = counts[:, 0].astype(jnp.int32)
    padded = (counts + BLK - 1) // BLK * BLK
    pend = jnp.cumsum(padded)
    offs = pend - padded
    n_blocks = -(-(N * TOP_K) // BLK) + N_EXPERTS
    blk_e = jnp.minimum(jnp.searchsorted(pend, jnp.arange(n_blocks) * BLK, side='right'),
                        N_EXPERTS - 1).astype(jnp.int32)
    n_used = (pend[-1:] // BLK).astype(jnp.int32)
    tail_start = jnp.where(padded > 0, pend - BLK, -1).astype(jnp.int32)
    dest = _dest_rows(offs.astype(jnp.int32), top_e, rank)
    xs = _dispatch(tail_start, n_used, dest, u2_tiles, n_blocks * BLK, D)
    ys = _expert_ffn(blk_e, n_used, xs, w1, b1, w2, b2)
    return _combine(dest, gates.T, x1, mod, ln_g, ln_b, ys, seq_len, layer)


def _split_w_in(w_in):
    AW = N_ATTN_HEADS * HEAD_DIM
    GW = GROUP_WIDTH
    w = w_in.astype(BF16)
    w_nat = jnp.concatenate([w[:, 3 * AW + 4 * HYENA_WIDTH:], w[:, 3 * AW:3 * AW + 4 * HYENA_WIDTH]], axis=1)
    w_qkv = jnp.stack([
        jnp.concatenate([w[:, t * AW + g * GW:t * AW + (g + 1) * GW] for t in range(3)], axis=1)
        for g in range(len(ATTN_GROUPS))])
    return w_nat, w_qkv


def kernel(x, c, w_ada, b_ada, w_in, w_sconv, b_sconv, w_pool, pool_scale, f_w1, f_b1, f_w_inner, f_b_inner, f_w_out, f_freq, hy_bias, p_attn, p_pool, p_hyena, w_out, ln1_g, ln1_b, w_router, b_router, w1, b1, w2, b2, ln2_g, ln2_b):
    B, S, D = x.shape
    depth = w_in.shape[0]
    bias_tables = [jnp.asarray(t) for t in _attn_bias_tables(S)]
    fmat, gmat = _dft_matrices(S, FREQ_TILE)
    zfeat, decay = _filter_features(S)
    gate_cols = 3 * D
    pool_block = gate_cols // HYENA_WIDTH
    bf = lambda t: t.astype(BF16)
    p_attn_b, p_pool_b, p_hyena_b, w_out_b, w1_b, w2_b = map(bf, (p_attn, p_pool, p_hyena, w_out, w1, w2))

    for l in range(depth):
        mod = _ada(c, w_ada, b_ada, l).reshape(B, 6, D)
        w_nat, w_qkv = _split_w_in(w_in[l])
        proj, u = _inproj_nat(x, mod, w_nat)
        qkv = _inproj_dil(u, w_qkv)
        ya = _attention(qkv, bias_tables)
        yb = _pool(proj, pool_block, w_pool, pool_scale, l)
        filt = _hyena_filters(zfeat, decay, f_w1, f_b1, f_w_inner, f_b_inner, f_w_out, f_freq, l)
        spectra = _filter_spectra(fmat, filt)
        yc = _hyena(proj, pool_block + 1, fmat, gmat, spectra, w_sconv, b_sconv, hy_bias, l)
        x1, u2, logits_t = _merge(ya, yb, yc, proj, x, mod, p_attn_b, p_pool_b, p_hyena_b, w_out_b,
                                  ln1_g, ln1_b, w_router, b_router, l)
        x = _moe(u2, logits_t, x1, mod, w1_b[l], b1[l], w2_b[l], b2[l], ln2_g, ln2_b, S, l).reshape(B, S, D)
    return x
```

```python
import functools
import math

import jax
import jax.numpy as jnp
import numpy as np
from jax import lax
from jax.experimental import pallas as pl
from jax.experimental.pallas import tpu as pltpu

F32 = jnp.float32
BF16 = jnp.bfloat16

HEAD_DIM = 64
HEADS_PER_GROUP = 4
ATTN_GROUPS = ((128, 1), (512, 4), (2048, 16))
N_ATTN_HEADS = HEADS_PER_GROUP * len(ATTN_GROUPS)
GROUP_WIDTH = HEADS_PER_GROUP * HEAD_DIM
HALF_WINDOW = 64
POOL_WINDOWS = (2, 4, 8, 16)
POOL_GROUP = 128
HYENA_WIDTH = 512
FILTER_EMB = 33
FILTER_HIDDEN = 64
N_EXPERTS = 32
TOP_K = 4
SWIGLU_LIMIT = 7.0
SWIGLU_ALPHA = 1.702
LN_EPS = 1e-5
DEPTH = 2
ALPHA_RES = (2 * DEPTH) ** 0.25
DECAY_TARGET = 1e-2
FAST_DECAY_PCT = 0.3
SLOW_DECAY_PCT = 1.5

QBLK = 128
FREQ_TILE = 256
ROW_CHUNK = 256
MERGE_ROWS = 512
TOPK_LANES = 1024
MOE_BLK = 512
FFN_CHUNK = 512
WEIGHT_CAST_ROWS = 128
DMA_PRIORITIES = 2
DMA_ISSUE_UNROLL = 4
MOE_TOK = 256
PAD_ROWS = 8
NEG_BIG = -1e30
LANES = 128
VMEM_LIMIT = 56 * 1024 * 1024


def _cparams(*sem):
    return pltpu.CompilerParams(dimension_semantics=sem, vmem_limit_bytes=VMEM_LIMIT)


def _dot(a, b):
    return jnp.dot(a, b, preferred_element_type=F32)


def _dot3(a, b):
    a_hi = a.astype(BF16)
    b_hi = b.astype(BF16)
    a_lo = (a - a_hi.astype(F32)).astype(BF16)
    b_lo = (b - b_hi.astype(F32)).astype(BF16)
    return _dot(a_hi, b_hi) + _dot(a_hi, b_lo) + _dot(a_lo, b_hi)


def _layer_norm(r, g, b):
    mu = jnp.mean(r, axis=-1, keepdims=True)
    c = r - mu
    var = jnp.mean(c * c, axis=-1, keepdims=True)
    return c * lax.rsqrt(var + LN_EPS) * g + b


def _store_token_tiles(ref, val):
    T, D = val.shape
    n = D // LANES
    for c in range(n):
        ref[pl.ds(c, T, stride=n), :] = val[:, c * LANES:(c + 1) * LANES]


def _load_token_tiles(ref, T, D, plane=None):
    n = D // LANES
    lead = () if plane is None else tuple(plane)
    idx = lambda c: lead + (pl.ds(c, T, stride=n), slice(None))
    return [ref[idx(c)] for c in range(n)]


def _ada_kernel(c_ref, w_ref, b_ref, o_ref):
    c = c_ref[...]
    cond = c * (1.0 / (1.0 + jnp.exp(-c)))
    o_ref[...] = _dot3(cond, w_ref[...]) + b_ref[...]


def _ada(c, w_ada, b_ada, layer):
    B, D = c.shape
    n_out = w_ada.shape[-1]
    tn = 768
    return pl.pallas_call(
        _ada_kernel,
        grid=(n_out // tn,),
        in_specs=[
            pl.BlockSpec((B, D), lambda j: (0, 0)),
            pl.BlockSpec((None, D, tn), lambda j: (layer, 0, j)),
            pl.BlockSpec((None, 1, tn), lambda j: (layer, 0, j)),
        ],
        out_specs=pl.BlockSpec((B, tn), lambda j: (0, j)),
        out_shape=jax.ShapeDtypeStruct((B, n_out), F32),
        compiler_params=_cparams("arbitrary"),
        name="ada",
    )(c, w_ada, b_ada.reshape(b_ada.shape[0], 1, n_out))


def _inproj_nat_kernel(x_ref, mod_ref, w_ref, o_ref, u_ref):
    S = x_ref.shape[0]

    @pl.when(pl.program_id(1) == 0)
    def _():
        shift = mod_ref[0:1, :]
        scale1p = 1.0 + mod_ref[1:2, :]

        def body(i, carry):
            rows = pl.ds(pl.multiple_of(i * ROW_CHUNK, ROW_CHUNK), ROW_CHUNK)
            u_ref[rows, :] = (x_ref[rows, :] * scale1p + shift).astype(BF16)
            return carry

        lax.fori_loop(0, S // ROW_CHUNK, body, 0)

    o_ref[...] = _dot(u_ref[...], w_ref[...]).astype(BF16)


def _inproj_nat(x, mod, w_nat):
    B, S, D = x.shape
    n_out = w_nat.shape[1]
    tn = 512
    return pl.pallas_call(
        _inproj_nat_kernel,
        grid=(B, n_out // tn),
        in_specs=[
            pl.BlockSpec((None, S, D), lambda b, j: (b, 0, 0)),
            pl.BlockSpec((None, 6, D), lambda b, j: (b, 0, 0)),
            pl.BlockSpec((D, tn), lambda b, j: (0, j)),
        ],
        out_specs=[
            pl.BlockSpec((None, S, tn), lambda b, j: (b, 0, j)),
            pl.BlockSpec((None, S, D), lambda b, j: (b, 0, 0)),
        ],
        out_shape=[
            jax.ShapeDtypeStruct((B, S, n_out), BF16),
            jax.ShapeDtypeStruct((B, S, D), BF16),
        ],
        compiler_params=_cparams("arbitrary", "arbitrary"),
        name="inproj_nat",
    )(x, mod, w_nat)


def _inproj_dil_kernel(u_ref, w_ref, o0_ref, o1_ref, o2_ref, acc_ref):
    S = u_ref.shape[0]
    for g, o_ref in enumerate((o0_ref, o1_ref, o2_ref)):
        d = ATTN_GROUPS[g][1]
        L = S // d
        acc = _dot(u_ref[...], w_ref[g])
        if d == 1:
            o_ref[0] = acc.astype(BF16)
            continue
        for c in range(acc_ref.shape[0]):
            acc_ref[c] = acc[:, c * LANES:(c + 1) * LANES]
        for rho in range(d):
            for c in range(acc_ref.shape[0]):
                o_ref[rho, :, c * LANES:(c + 1) * LANES] = (
                    acc_ref[c, pl.ds(rho, L, stride=d), :].astype(BF16))


def _inproj_dil(u, w_qkv):
    B, S, D = u.shape
    W = w_qkv.shape[-1]
    dils = [d for _, d in ATTN_GROUPS]
    return pl.pallas_call(
        _inproj_dil_kernel,
        grid=(B,),
        in_specs=[
            pl.BlockSpec((None, S, D), lambda b: (b, 0, 0)),
            pl.BlockSpec((len(dils), D, W), lambda b: (0, 0, 0)),
        ],
        out_specs=[pl.BlockSpec((None, d, S // d, W), lambda b: (b, 0, 0, 0)) for d in dils],
        out_shape=[jax.ShapeDtypeStruct((B, d, S // d, W), BF16) for d in dils],
        scratch_shapes=[pltpu.VMEM((W // LANES, S, LANES), F32)],
        compiler_params=_cparams("arbitrary"),
        name="inproj_dil",
    )(u, w_qkv)


def _window_start(i, L, W):
    return min(max(QBLK * i - HALF_WINDOW, 0), L - W)


def _attn_bias_tables(S):
    h = np.arange(1, N_ATTN_HEADS + 1, dtype=np.float32)
    slopes = (2.0 ** (-8.0 * h / N_ATTN_HEADS)).astype(np.float32)
    tables = []
    for g, (_, d) in enumerate(ATTN_GROUPS):
        L = S // d
        W = min(2 * QBLK, L)
        nb = L // QBLK
        variants = sorted({_window_start(i, L, W) - QBLK * i for i in range(nb)}, reverse=True)
        r = np.arange(QBLK)[:, None]
        c = np.arange(W)[None, :]
        tab = np.empty((len(variants), HEADS_PER_GROUP * QBLK, W), np.float32)
        for vi, delta in enumerate(variants):
            off = np.abs(c - r + delta)
            for hh in range(HEADS_PER_GROUP):
                slope = np.float64(slopes[g * HEADS_PER_GROUP + hh])
                bias = np.where(off <= HALF_WINDOW, -slope * (d * off), NEG_BIG)
                tab[vi, hh * QBLK:(hh + 1) * QBLK] = bias.astype(np.float32)
        tables.append(tab)
    return tables


def _block_attention(q, kwin, vwin, bias, head_of_lane):
    zero = jnp.zeros_like(q)
    q4 = jnp.concatenate([jnp.where(head_of_lane == h, q, zero) for h in range(HEADS_PER_GROUP)], axis=0)
    s = lax.dot_general(q4, kwin, (((1,), (1,)), ((), ())), preferred_element_type=F32)
    s = s * (1.0 / math.sqrt(HEAD_DIM)) + bias
    m = jnp.max(s, axis=-1, keepdims=True)
    p = jnp.exp(s - m)
    l = jnp.sum(p, axis=-1, keepdims=True)
    o4 = _dot(p.astype(BF16), vwin) * (1.0 / l)
    lse4 = m + jnp.log(l)
    out = jnp.zeros((QBLK, GROUP_WIDTH), F32)
    lse = jnp.zeros((QBLK, GROUP_WIDTH), F32)
    for h in range(HEADS_PER_GROUP):
        rows = slice(h * QBLK, (h + 1) * QBLK)
        sel = head_of_lane == h
        out = jnp.where(sel, o4[rows], out)
        lse = jnp.where(sel, lse4[rows], lse)
    return out, lse


def _attn_kernel(q0_ref, q1_ref, q2_ref, b0_ref, b1_ref, b2_ref, o_ref, acc_ref, lse_ref):
    S = o_ref.shape[0]
    GW = GROUP_WIDTH
    head_of_lane = lax.broadcasted_iota(jnp.int32, (QBLK, GW), 1) // HEAD_DIM
    n_planes = GW // LANES

    def put(g, rows, out, lse):
        for c in range(n_planes):
            acc_ref[g, c, rows, :] = out[:, c * LANES:(c + 1) * LANES]
            lse_ref[g, c, rows, :] = lse[:, c * LANES:(c + 1) * LANES]

    L0 = S
    nb0 = L0 // QBLK

    def g0_body(i, carry):
        ws = pl.multiple_of(jnp.clip(QBLK * i - HALF_WINDOW, 0, L0 - 2 * QBLK), HALF_WINDOW)
        var = jnp.where(i == 0, 0, jnp.where(i == nb0 - 1, 2, 1))
        rows = pl.ds(pl.multiple_of(i * QBLK, QBLK), QBLK)
        out, lse = _block_attention(
            q0_ref[0, rows, 0:GW],
            q0_ref[0, pl.ds(ws, 2 * QBLK), GW:2 * GW],
            q0_ref[0, pl.ds(ws, 2 * QBLK), 2 * GW:3 * GW],
            b0_ref[var], head_of_lane)
        put(0, rows, out, lse)
        return carry

    lax.fori_loop(0, nb0, g0_body, 0)

    for g, (q_ref, b_ref) in ((1, (q1_ref, b1_ref)), (2, (q2_ref, b2_ref))):
        d = ATTN_GROUPS[g][1]
        L = S // d
        W = min(2 * QBLK, L)
        nb = L // QBLK
        deltas = sorted({_window_start(i, L, W) - QBLK * i for i in range(nb)}, reverse=True)

        def gd_body(rho, carry, q_ref=q_ref, b_ref=b_ref, d=d, L=L, W=W, nb=nb, deltas=deltas, g=g):
            for i in range(nb):
                ws = _window_start(i, L, W)
                var = deltas.index(ws - QBLK * i)
                out, lse = _block_attention(
                    q_ref[rho, i * QBLK:(i + 1) * QBLK, 0:GW],
                    q_ref[rho, ws:ws + W, GW:2 * GW],
                    q_ref[rho, ws:ws + W, 2 * GW:3 * GW],
                    b_ref[var], head_of_lane)
                dst = pl.ds(rho + i * QBLK * d, QBLK, stride=d)
                put(g, dst, out, lse)
            return carry

        lax.fori_loop(0, d, gd_body, 0)

    def merge_body(i, carry):
        rows = pl.ds(pl.multiple_of(i * ROW_CHUNK, ROW_CHUNK), ROW_CHUNK)
        for c in range(n_planes):
            l0, l1, l2 = lse_ref[0, c, rows, :], lse_ref[1, c, rows, :], lse_ref[2, c, rows, :]
            mx = jnp.maximum(jnp.maximum(l0, l1), l2)
            w0, w1, w2 = jnp.exp(l0 - mx), jnp.exp(l1 - mx), jnp.exp(l2 - mx)
            num = w0 * acc_ref[0, c, rows, :] + w1 * acc_ref[1, c, rows, :] + w2 * acc_ref[2, c, rows, :]
            o_ref[rows, c * LANES:(c + 1) * LANES] = (num * (1.0 / (w0 + w1 + w2))).astype(BF16)
        return carry

    lax.fori_loop(0, S // ROW_CHUNK, merge_body, 0)


def _attention(qkv, bias_tables):
    q0, q1, q2 = qkv
    B, _, S, W3 = q0.shape
    GW = GROUP_WIDTH
    in_specs = [pl.BlockSpec((None,) + q.shape[1:], lambda b: (b, 0, 0, 0)) for q in qkv]
    in_specs += [pl.BlockSpec(t.shape, lambda b: (0, 0, 0)) for t in bias_tables]
    return pl.pallas_call(
        _attn_kernel,
        grid=(B,),
        in_specs=in_specs,
        out_specs=pl.BlockSpec((None, S, GW), lambda b: (b, 0, 0)),
        out_shape=jax.ShapeDtypeStruct((B, S, GW), BF16),
        scratch_shapes=[pltpu.VMEM((3, GW // LANES, S, LANES), F32)] * 2,
        compiler_params=_cparams("arbitrary"),
        name="attention",
    )(q0, q1, q2, *bias_tables)


def _pool_kernel(y_ref, w_ref, sc_ref, o_ref, pad_ref):
    S = y_ref.shape[0]
    C = y_ref.shape[1]
    P = PAD_ROWS
    pad_ref[0:P, :] = jnp.zeros((P, C), F32)
    pad_ref[S + P:S + 2 * P, :] = jnp.zeros((P, C), F32)
    pad_ref[P:S + P, :] = y_ref[...].astype(F32)
    for c in range(S // ROW_CHUNK):
        r0 = c * ROW_CHUNK
        t = r0 + lax.broadcasted_iota(jnp.int32, (ROW_CHUNK, 1), 0)
        for g, w in enumerate(POOL_WINDOWS):
            cols = slice(g * POOL_GROUP, (g + 1) * POOL_GROUP)
            acc = pad_ref[P + r0 - w // 2:P + r0 - w // 2 + ROW_CHUNK, cols]
            for j in range(-w // 2 + 1, w // 2):
                acc = acc + pad_ref[P + r0 + j:P + r0 + j + ROW_CHUNK, cols]
            cnt = (jnp.minimum(t + w // 2, S) - jnp.maximum(t - w // 2, 0)).astype(F32)
            pooled = acc / cnt - pad_ref[P + r0:P + r0 + ROW_CHUNK, cols]
            mixed = _dot(pooled.astype(BF16), w_ref[g].astype(BF16)) * sc_ref[:, cols]
            o_ref[r0:r0 + ROW_CHUNK, cols] = mixed.astype(BF16)


def _pool(proj, col_block, w_pool, pool_scale, layer):
    B, S, _ = proj.shape
    C = POOL_GROUP * len(POOL_WINDOWS)
    return pl.pallas_call(
        _pool_kernel,
        grid=(B,),
        in_specs=[
            pl.BlockSpec((None, S, C), lambda b: (b, 0, col_block)),
            pl.BlockSpec((None,) + w_pool.shape[1:], lambda b: (layer, 0, 0, 0)),
            pl.BlockSpec((None, 1, C), lambda b: (layer, 0, 0)),
        ],
        out_specs=pl.BlockSpec((None, S, C), lambda b: (b, 0, 0)),
        out_shape=jax.ShapeDtypeStruct((B, S, C), BF16),
        scratch_shapes=[pltpu.VMEM((S + 2 * PAD_ROWS, C), F32)],
        compiler_params=_cparams("arbitrary"),
        name="pool",
    )(proj, w_pool, pool_scale.reshape(pool_scale.shape[0], 1, C))


def _filter_kernel(z_ref, w1_ref, b1_ref, wi_ref, bi_ref, wo_ref, fr_ref, dec_ref, o_ref):
    S = z_ref.shape[0]
    C = HYENA_WIDTH
    freq = fr_ref[...]
    h = jnp.sin(freq * (_dot3(z_ref[...], w1_ref[...]) + b1_ref[...]))
    for i in range(wi_ref.shape[0]):
        h = jnp.sin(freq * (_dot3(h, wi_ref[i]) + bi_ref[i]))
    not_first = lax.broadcasted_iota(jnp.int32, (S, 1), 0) > 0
    for blk in range(wo_ref.shape[1] // C):
        cols = slice(blk * C, (blk + 1) * C)
        f = _dot3(h, wo_ref[:, cols]) * dec_ref[...]
        if blk % 2 == 1:
            f = jnp.where(not_first, f, 0.0)
        o_ref[:, cols] = f.astype(BF16)


def _hyena_filters(zfeat, decay, f_w1, f_b1, f_w_inner, f_b_inner, f_w_out, f_freq, layer):
    S, E = zfeat.shape
    H = FILTER_HIDDEN
    n_out = f_w_out.shape[-1]
    n_inner = f_w_inner.shape[1]
    w1 = jnp.zeros((f_w1.shape[0], E, H), F32).at[:, :f_w1.shape[1]].set(f_w1)
    full = lambda shape: pl.BlockSpec(shape, lambda i: (0,) * len(shape))
    lay = lambda shape: pl.BlockSpec((None,) + shape, lambda i: (layer,) + (0,) * len(shape))
    return pl.pallas_call(
        _filter_kernel,
        grid=(1,),
        in_specs=[
            full((S, E)), lay((E, H)), lay((1, H)), lay((n_inner, H, H)), lay((n_inner, 1, H)),
            lay((H, n_out)), lay((1, H)), full((S, HYENA_WIDTH)),
        ],
        out_specs=full((S, n_out)),
        out_shape=jax.ShapeDtypeStruct((S, n_out), BF16),
        compiler_params=_cparams("arbitrary"),
        name="hyena_filter",
    )(zfeat, w1, f_b1.reshape(-1, 1, H), f_w_inner, f_b_inner.reshape(-1, n_inner, 1, H), f_w_out,
      f_freq.reshape(-1, 1, H), decay)


def _spec_kernel(f_ref, h_ref, a_ref, b_ref, c_ref):
    TK = f_ref.shape[0] // 2
    C = HYENA_WIDTH
    n_fft = 2 * h_ref.shape[0]
    x = _dot(f_ref[...], h_ref[...])
    xc, xs = x[:TK], x[TK:]
    row = pl.program_id(0) * TK + lax.broadcasted_iota(jnp.int32, (TK, 1), 0)
    is0 = row == 0
    s = jnp.where(is0, 1.0 / n_fft, 2.0 / n_fft)
    for o in range(a_ref.shape[0]):
        hf = slice(2 * o * C, (2 * o + 1) * C)
        hb = slice((2 * o + 1) * C, (2 * o + 2) * C)
        a = (xc[:, hf] + xc[:, hb]) * s
        a_ref[o] = a
        b_ref[o] = jnp.where(is0, 0.0, (xs[:, hf] - xs[:, hb]) * s)
        c_ref[o] = jnp.where(is0, (xs[:, hf] + xs[:, hb]) * (1.0 / n_fft), a)


def _filter_spectra(fmat, filt):
    nkt, tk2, S = fmat.shape
    TK = tk2 // 2
    C = HYENA_WIDTH
    n_ord = filt.shape[1] // (2 * C)
    out = jax.ShapeDtypeStruct((n_ord, S, C), F32)
    ospec = pl.BlockSpec((n_ord, TK, C), lambda k: (0, k, 0))
    return pl.pallas_call(
        _spec_kernel,
        grid=(nkt,),
        in_specs=[
            pl.BlockSpec((None, tk2, S), lambda k: (k, 0, 0)),
            pl.BlockSpec(filt.shape, lambda k: (0, 0)),
        ],
        out_specs=[ospec, ospec, ospec],
        out_shape=[out, out, out],
        compiler_params=_cparams("arbitrary"),
        name="hyena_spectra",
    )(fmat, filt)


def _hyena_kernel(v_ref, x1_ref, x2_ref, f_ref, g_ref, a_ref, b_ref, c_ref, wsc_ref, bsc_ref, hb_ref,
                  o_ref, pad_ref, z_ref, zb_ref, gate_ref, ycat_ref):
    S, C = z_ref.shape
    TK = f_ref.shape[0] // 2
    TT = g_ref.shape[0]
    nkt = S // TK
    P = PAD_ROWS
    o = pl.program_id(1)
    s = pl.program_id(2)

    def short_conv(in_ref, blk, consume):
        cols = slice(blk * C, (blk + 1) * C)
        pad_ref[0:P, :] = jnp.zeros((P, C), F32)
        pad_ref[S + P:S + 2 * P, :] = jnp.zeros((P, C), F32)
        pad_ref[P:S + P, :] = in_ref[...].astype(F32)
        w0, w1, w2 = wsc_ref[0:1, cols], wsc_ref[1:2, cols], wsc_ref[2:3, cols]
        bias = bsc_ref[:, cols]
        for c in range(S // ROW_CHUNK):
            r0 = c * ROW_CHUNK
            conv = (w0 * pad_ref[P + r0 - 1:P + r0 - 1 + ROW_CHUNK, :]
                    + w1 * pad_ref[P + r0:P + r0 + ROW_CHUNK, :]
                    + w2 * pad_ref[P + r0 + 1:P + r0 + 1 + ROW_CHUNK, :] + bias)
            consume(r0, conv)

    def store_gate(r0, conv):
        gate_ref[r0:r0 + ROW_CHUNK, :] = conv

    @pl.when((o == 0) & (s == 0))
    def _():
        def store_z(r0, conv):
            z_ref[r0:r0 + ROW_CHUNK, :] = conv
            zb_ref[r0:r0 + ROW_CHUNK, :] = conv.astype(BF16)
        short_conv(v_ref, 0, store_z)

    @pl.when(s < nkt)
    def _():
        x = _dot(f_ref[...], zb_ref[...])
        xre, xim = x[:TK], x[TK:]
        a, b, c = a_ref[...], b_ref[...], c_ref[...]
        base = pl.multiple_of(s * 2 * TK, 2 * TK)
        ycat_ref[pl.ds(base, TK), :] = (xre * a - xim * b).astype(BF16)
        ycat_ref[pl.ds(base + TK, TK), :] = (xre * b + xim * c).astype(BF16)

    @pl.when((s == nkt) & (o == 0))
    def _():
        short_conv(x1_ref, 1, store_gate)

    @pl.when((s == nkt) & (o == 1))
    def _():
        short_conv(x2_ref, 2, store_gate)

    @pl.when(s >= nkt)
    def _():
        rows = pl.ds(pl.multiple_of((s - nkt) * TT, TT), TT)
        y = _dot(g_ref[...], ycat_ref[...])
        hbias = jnp.where(o == 0, hb_ref[0:1, :], hb_ref[1:2, :])
        znew = gate_ref[rows, :] * (y + z_ref[rows, :] * hbias)
        z_ref[rows, :] = znew
        zb_ref[rows, :] = znew.astype(BF16)
        o_ref[rows, :] = znew.astype(BF16)


def _hyena(proj, first_col_block, fmat, gmat, spectra, w_sconv, b_sconv, hy_bias, layer):
    B, S, _ = proj.shape
    C = HYENA_WIDTH
    nkt, tk2, _ = fmat.shape
    TK = tk2 // 2
    TT = S // nkt
    a, b, c = spectra
    n_ord = a.shape[0]
    assert n_ord == 2
    fwd = lambda s: jnp.minimum(s, nkt - 1)
    inv = lambda s: jnp.clip(s - nkt, 0, nkt - 1)
    inp = lambda blk: pl.BlockSpec((None, S, C), lambda bb, o, s: (bb, 0, first_col_block + blk))
    spec = pl.BlockSpec((None, TK, C), lambda bb, o, s: (o, fwd(s), 0))
    return pl.pallas_call(
        _hyena_kernel,
        grid=(B, n_ord, 2 * nkt),
        in_specs=[
            inp(0), inp(1), inp(2),
            pl.BlockSpec((None, tk2, S), lambda bb, o, s: (fwd(s), 0, 0)),
            pl.BlockSpec((TT, 2 * S), lambda bb, o, s: (inv(s), 0)),
            spec, spec, spec,
            pl.BlockSpec((None, 3, 3 * C), lambda bb, o, s: (layer, 0, 0)),
            pl.BlockSpec((None, 1, 3 * C), lambda bb, o, s: (layer, 0, 0)),
            pl.BlockSpec((None, n_ord, C), lambda bb, o, s: (layer, 0, 0)),
        ],
        out_specs=pl.BlockSpec((None, S, C), lambda bb, o, s: (bb, 0, 0)),
        out_shape=jax.ShapeDtypeStruct((B, S, C), BF16),
        scratch_shapes=[
            pltpu.VMEM((S + 2 * PAD_ROWS, C), F32),
            pltpu.VMEM((S, C), F32),
            pltpu.VMEM((S, C), BF16),
            pltpu.VMEM((S, C), F32),
            pltpu.VMEM((2 * S, C), BF16),
        ],
        compiler_params=_cparams("arbitrary", "arbitrary", "arbitrary"),
        name="hyena_conv",
    )(proj, proj, proj, fmat, gmat, a, b, c, w_sconv, b_sconv.reshape(b_sconv.shape[0], 1, 3 * C), hy_bias)


def _dft_matrices(S, TK):
    n_fft = 2 * S
    k = lax.broadcasted_iota(jnp.int32, (S, S), 0)
    n = lax.broadcasted_iota(jnp.int32, (S, S), 1)
    theta = ((k * n) & (n_fft - 1)).astype(F32) * (2.0 * math.pi / n_fft)
    fc = jnp.cos(theta)
    fs = jnp.where(k == 0, (1 - 2 * (n & 1)).astype(F32), -jnp.sin(theta))
    nkt = S // TK
    fmat = jnp.concatenate([fc.reshape(nkt, TK, S), fs.reshape(nkt, TK, S)], axis=1).astype(BF16)
    return fmat, fmat.reshape(n_fft, S).T


def _filter_features(S):
    t = jnp.linspace(0.0, 1.0, S, dtype=F32)[:, None]
    bands = (FILTER_EMB - 1) // 2
    w = 2.0 * math.pi * jnp.arange(S, dtype=F32) / S
    f = jnp.linspace(1e-4, bands - 1, bands, dtype=F32)
    ang = w[:, None] * f[None, :]
    z = jnp.concatenate([t, jnp.cos(ang), -jnp.sin(ang)], axis=-1)
    z = jnp.pad(z, ((0, 0), (0, 128 - FILTER_EMB)))
    min_decay = math.log(DECAY_TARGET) / SLOW_DECAY_PCT
    max_decay = math.log(DECAY_TARGET) / FAST_DECAY_PCT
    deltas = jnp.linspace(min_decay, max_decay, HYENA_WIDTH, dtype=F32)
    decay = jnp.exp(-t * jnp.abs(deltas)[None, :])
    return z, decay


def _merge_kernel(ya_ref, yb_ref, yc_ref, gl_ref, x_ref, mod_ref, pa_ref, pp_ref, ph_ref, wo_ref,
                  g_ref, b_ref, wr_ref, br_ref, x1_ref, u2_ref, lt_ref):
    D = x_ref.shape[1]
    sig = lambda v: 1.0 / (1.0 + jnp.exp(-v))
    merged = sig(gl_ref[:, 0:D].astype(F32)) * _dot(ya_ref[...], pa_ref[...])
    merged += sig(gl_ref[:, D:2 * D].astype(F32)) * _dot(yb_ref[...], pp_ref[...])
    merged += sig(gl_ref[:, 2 * D:3 * D].astype(F32)) * _dot(yc_ref[...], ph_ref[...])
    h = _dot(merged.astype(BF16), wo_ref[...])
    gate1, shift2, scale2 = mod_ref[2:3, :], mod_ref[3:4, :], mod_ref[4:5, :]
    x1 = _layer_norm(ALPHA_RES * x_ref[...] + (1.0 + gate1) * h, g_ref[...], b_ref[...])
    x1_ref[...] = x1
    u2 = x1 * (1.0 + scale2) + shift2
    _store_token_tiles(u2_ref, u2)
    logits = _dot3(u2, wr_ref[...]) + br_ref[...]
    lt_ref[...] = logits.T[0:N_EXPERTS, :]


def _merge(ya, yb, yc, proj, x, mod, p_attn, p_pool, p_hyena, w_out, ln_g, ln_b, w_router, b_router, layer):
    B, S, D = x.shape
    N = B * S
    TM = MERGE_ROWS
    tiles_per_seq = S // TM
    EP = 128
    flat = lambda t: t.reshape(N, t.shape[-1])
    wr = jnp.zeros((D, EP), F32).at[:, :N_EXPERTS].set(w_router[layer])
    br = jnp.zeros((1, EP), F32).at[:, :N_EXPERTS].set(b_router[layer][None])
    tok = lambda w: pl.BlockSpec((TM, w), lambda i: (i, 0))
    lay = lambda t: pl.BlockSpec((None,) + t.shape[1:], lambda i: (layer,) + (0,) * (t.ndim - 1))
    ln_g3, ln_b3 = ln_g.reshape(-1, 1, D), ln_b.reshape(-1, 1, D)
    return pl.pallas_call(
        _merge_kernel,
        grid=(N // TM,),
        in_specs=[
            tok(ya.shape[-1]), tok(yb.shape[-1]), tok(yc.shape[-1]), tok(3 * D), tok(D),
            pl.BlockSpec((None, 6, D), lambda i: (i // tiles_per_seq, 0, 0)),
            lay(p_attn), lay(p_pool), lay(p_hyena), lay(w_out), lay(ln_g3), lay(ln_b3),
            pl.BlockSpec((D, EP), lambda i: (0, 0)),
            pl.BlockSpec((1, EP), lambda i: (0, 0)),
        ],
        out_specs=[tok(D), pl.BlockSpec((TM * D // LANES, LANES), lambda i: (i, 0)),
                   pl.BlockSpec((N_EXPERTS, TM), lambda i: (0, i))],
        out_shape=[
            jax.ShapeDtypeStruct((N, D), F32),
            jax.ShapeDtypeStruct((N * D // LANES, LANES), F32),
            jax.ShapeDtypeStruct((N_EXPERTS, N), F32),
        ],
        compiler_params=_cparams("arbitrary"),
        name="merge_ln1_router",
    )(flat(ya), flat(yb), flat(yc), flat(proj), flat(x), mod, p_attn, p_pool, p_hyena, w_out,
      ln_g3, ln_b3, wr, br)


def _topk_kernel(l_ref, e_ref, g_ref, r_ref, cnt_ref, carry_ref):
    E, TL = l_ref.shape

    @pl.when(pl.program_id(0) == 0)
    def _():
        carry_ref[...] = jnp.zeros_like(carry_ref)

    l = l_ref[...]
    eio = lax.broadcasted_iota(jnp.int32, (E, TL), 0).astype(F32)
    vals, hots = [], []
    for k in range(TOP_K):
        m = jnp.max(l, axis=0, keepdims=True)
        idx = jnp.min(jnp.where(l == m, eio, float(E)), axis=0, keepdims=True)
        hot = eio == idx
        l = jnp.where(hot, -jnp.inf, l)
        vals.append(m)
        hots.append(hot)
        e_ref[k:k + 1, :] = idx.astype(jnp.int32)
    exps = [jnp.exp(v - vals[0]) for v in vals]
    inv = 1.0 / (exps[0] + exps[1] + exps[2] + exps[3])
    for k in range(TOP_K):
        g_ref[k:k + 1, :] = exps[k] * inv

    memb = jnp.zeros((E, TL), F32)
    for hot in hots:
        memb = memb + hot.astype(F32)
    upper = (lax.broadcasted_iota(jnp.int32, (TL, TL), 0)
             <= lax.broadcasted_iota(jnp.int32, (TL, TL), 1)).astype(BF16)
    incl = _dot(memb.astype(BF16), upper)
    excl = incl - memb + carry_ref[:, 0:1]
    for k in range(TOP_K):
        rank = jnp.sum(jnp.where(hots[k], excl, 0.0), axis=0, keepdims=True)
        r_ref[k:k + 1, :] = rank.astype(jnp.int32)
    carry_ref[...] = carry_ref[...] + jnp.sum(memb, axis=1, keepdims=True)
    cnt_ref[...] = carry_ref[...]


def _topk(logits_t):
    E, N = logits_t.shape
    TL = TOPK_LANES
    tok = pl.BlockSpec((TOP_K, TL), lambda i: (0, i))
    return pl.pallas_call(
        _topk_kernel,
        grid=(N // TL,),
        in_specs=[pl.BlockSpec((E, TL), lambda i: (0, i))],
        out_specs=[tok, tok, tok, pl.BlockSpec((E, 128), lambda i: (0, 0))],
        out_shape=[
            jax.ShapeDtypeStruct((TOP_K, N), jnp.int32),
            jax.ShapeDtypeStruct((TOP_K, N), F32),
            jax.ShapeDtypeStruct((TOP_K, N), jnp.int32),
            jax.ShapeDtypeStruct((E, 128), F32),
        ],
        scratch_shapes=[pltpu.VMEM((E, 128), F32)],
        compiler_params=_cparams("arbitrary"),
        name="topk_rank",
    )(logits_t)


def _dest_kernel(offs_ref, e_ref, r_ref, d_ref):
    e = e_ref[...]
    acc = r_ref[...]
    for ex in range(N_EXPERTS):
        acc = acc + jnp.where(e == ex, offs_ref[ex], 0)
    d_ref[...] = acc


def _dest_rows(offs, top_e, rank):
    K, N = top_e.shape
    TL = TOPK_LANES
    tok = pl.BlockSpec((K, TL), lambda i, offs: (0, i))
    return pl.pallas_call(
        _dest_kernel,
        grid_spec=pltpu.PrefetchScalarGridSpec(
            num_scalar_prefetch=1, grid=(N // TL,), in_specs=[tok, tok], out_specs=tok),
        out_shape=jax.ShapeDtypeStruct((K, N), jnp.int32),
        compiler_params=_cparams("arbitrary"),
        name="dest_rows",
    )(offs, top_e, rank)


def _dispatch_kernel(tail_ref, n_used_ref, dest_ref, u_ref, xs_ref, zero_ref, sem):
    TT = dest_ref.shape[1]
    RT = u_ref.shape[0] // TT
    blk_rows = zero_ref.shape[0]
    n_blocks = xs_ref.shape[0] // blk_rows

    def zero_copy(first_row):
        start = pl.multiple_of(first_row * RT, RT)
        return pltpu.make_async_copy(zero_ref, xs_ref.at[pl.ds(start, blk_rows), :], sem)

    @pl.when(pl.program_id(0) == 0)
    def _():
        zero_ref[...] = jnp.zeros_like(zero_ref)

        def tails(fn):
            def body(e, carry):
                @pl.when(tail_ref[e] >= 0)
                def _():
                    fn(zero_copy(jnp.maximum(tail_ref[e], 0)))
                return carry
            lax.fori_loop(0, N_EXPERTS, body, 0)

        def unused(fn):
            def body(i, carry):
                fn(zero_copy(i * (blk_rows // RT)))
                return carry
            lax.fori_loop(n_used_ref[0], n_blocks, body, 0)

        tails(lambda cp: cp.start())
        unused(lambda cp: cp.start())
        tails(lambda cp: cp.wait())
        unused(lambda cp: cp.wait())

    def row_copy(j, k):
        src = pl.multiple_of(j * RT, RT)
        dst = pl.multiple_of(dest_ref[k, j] * RT, RT)
        return pltpu.make_async_copy(u_ref.at[pl.ds(src, RT), :], xs_ref.at[pl.ds(dst, RT), :], sem)

    def start(j, carry):
        for k in range(TOP_K):
            row_copy(j, k).start(priority=k % DMA_PRIORITIES)
        return carry

    lax.fori_loop(0, TT, start, 0, unroll=DMA_ISSUE_UNROLL)
    for k in range(TOP_K):
        pltpu.make_async_copy(u_ref, xs_ref.at[pl.ds(0, TT * RT), :], sem).wait()


def _dispatch(tail_start, n_used, dest, u2_tiles, n_rows, D):
    RT = D // LANES
    N = u2_tiles.shape[0] // RT
    TT = MOE_TOK
    return pl.pallas_call(
        _dispatch_kernel,
        grid_spec=pltpu.PrefetchScalarGridSpec(
            num_scalar_prefetch=2,
            grid=(N // TT,),
            in_specs=[
                pl.BlockSpec((TOP_K, TT), lambda i, tail, nu: (0, i), memory_space=pltpu.SMEM),
                pl.BlockSpec((TT * RT, LANES), lambda i, tail, nu: (i, 0)),
            ],
            out_specs=pl.BlockSpec(memory_space=pl.ANY),
            scratch_shapes=[pltpu.VMEM((MOE_BLK * RT, LANES), F32), pltpu.SemaphoreType.DMA(())],
        ),
        out_shape=jax.ShapeDtypeStruct((n_rows * RT, LANES), F32),
        compiler_params=_cparams("arbitrary"),
        name="moe_dispatch",
    )(tail_start, n_used, dest, u2_tiles)


def _ffn_kernel(blk_e_ref, n_used_ref, xs_ref, w1_ref, b1_ref, w2_ref, b2_ref, ys_ref,
                w1b_ref, w2b_ref, act_ref):
    F, D = w2_ref.shape
    BLK = xs_ref.shape[0] * LANES // D
    i = pl.program_id(0)
    used = i < n_used_ref[0]
    new_expert = (i == 0) | (blk_e_ref[i] != blk_e_ref[jnp.maximum(i - 1, 0)])

    @pl.when(used & new_expert)
    def _():
        for r in range(0, D, WEIGHT_CAST_ROWS):
            w1b_ref[r:r + WEIGHT_CAST_ROWS, :] = w1_ref[r:r + WEIGHT_CAST_ROWS, :].astype(BF16)
        for r in range(0, F, WEIGHT_CAST_ROWS):
            w2b_ref[r:r + WEIGHT_CAST_ROWS, :] = w2_ref[r:r + WEIGHT_CAST_ROWS, :].astype(BF16)

    @pl.when(used)
    def _():
        x = jnp.concatenate(_load_token_tiles(xs_ref, BLK, D), axis=1).astype(BF16)
        for c in range(0, F, FFN_CHUNK):
            cg = slice(c, c + FFN_CHUNK)
            cl = slice(F + c, F + c + FFN_CHUNK)
            glu = jnp.minimum(_dot(x, w1b_ref[:, cg]) + b1_ref[:, cg], SWIGLU_LIMIT)
            lin = jnp.clip(_dot(x, w1b_ref[:, cl]) + b1_ref[:, cl], -SWIGLU_LIMIT, SWIGLU_LIMIT)
            act = glu * (1.0 / (1.0 + jnp.exp(-SWIGLU_ALPHA * glu))) * (lin + 1.0)
            act_ref[:, cg] = act.astype(BF16)
        _store_token_tiles(ys_ref, _dot(act_ref[...], w2b_ref[...]) + b2_ref[...])

    @pl.when(jnp.logical_not(used))
    def _():
        ys_ref[...] = jnp.zeros_like(ys_ref)


def _expert_ffn(blk_e, n_used, xs_tiles, w1, b1, w2, b2, layer):
    _, E, D, F2 = w1.shape
    F = F2 // 2
    RT = D // LANES
    BLK = MOE_BLK
    rows = pl.BlockSpec((BLK * RT, LANES), lambda i, be, nu: (i, 0))
    per_expert = lambda r, c: pl.BlockSpec((None, None, r, c), lambda i, be, nu: (layer, be[i], 0, 0))
    return pl.pallas_call(
        _ffn_kernel,
        grid_spec=pltpu.PrefetchScalarGridSpec(
            num_scalar_prefetch=2,
            grid=(xs_tiles.shape[0] // (BLK * RT),),
            in_specs=[rows, per_expert(D, F2), per_expert(1, F2), per_expert(F, D), per_expert(1, D)],
            out_specs=rows,
            scratch_shapes=[pltpu.VMEM((D, F2), BF16), pltpu.VMEM((F, D), BF16), pltpu.VMEM((BLK, F), BF16)],
        ),
        out_shape=jax.ShapeDtypeStruct(xs_tiles.shape, F32),
        compiler_params=_cparams("arbitrary"),
        name="moe_ffn",
    )(blk_e, n_used, xs_tiles, w1, b1.reshape(-1, E, 1, F2), w2, b2.reshape(-1, E, 1, D))


def _combine_kernel(dest_ref, dest_next_ref, gt_ref, x1_ref, mod_ref, g_ref, b_ref, ys_ref, o_ref, buf_ref, sem):
    TT, D = x1_ref.shape
    RT = D // LANES
    i = pl.program_id(0)
    slot = i % 2

    def issue(d_ref, s):
        def body(j, carry):
            for k in range(TOP_K):
                src = pl.multiple_of(d_ref[k, j] * RT, RT)
                dst = pl.multiple_of(j * RT, RT)
                pltpu.make_async_copy(ys_ref.at[pl.ds(src, RT), :], buf_ref.at[s, k, pl.ds(dst, RT), :],
                                      sem.at[s]).start(priority=k % DMA_PRIORITIES)
            return carry
        lax.fori_loop(0, TT, body, 0, unroll=DMA_ISSUE_UNROLL)

    @pl.when(i == 0)
    def _():
        issue(dest_ref, 0)

    @pl.when(i + 1 < pl.num_programs(0))
    def _():
        issue(dest_next_ref, 1 - slot)

    for k in range(TOP_K):
        pltpu.make_async_copy(ys_ref.at[pl.ds(0, TT * RT), :], buf_ref.at[slot, k], sem.at[slot]).wait()

    chunks = None
    for k in range(TOP_K):
        gate = gt_ref[:, k:k + 1]
        rows = [gate * t for t in _load_token_tiles(buf_ref, TT, D, plane=(slot, k))]
        chunks = rows if chunks is None else [a + b for a, b in zip(chunks, rows)]
    h = jnp.concatenate(chunks, axis=1)
    gate2 = mod_ref[5:6, :]
    o_ref[...] = _layer_norm(ALPHA_RES * x1_ref[...] + (1.0 + gate2) * h, g_ref[...], b_ref[...])


def _combine(dest, gates_t, x1, mod, ln_g, ln_b, ys, seq_len, layer):
    N, D = x1.shape
    TT = MOE_TOK
    tiles_per_seq = seq_len // TT
    ln_g3, ln_b3 = ln_g.reshape(-1, 1, D), ln_b.reshape(-1, 1, D)
    lay = lambda t: pl.BlockSpec((None,) + t.shape[1:], lambda i: (layer,) + (0,) * (t.ndim - 1))
    return pl.pallas_call(
        _combine_kernel,
        grid=(N // TT,),
        in_specs=[
            pl.BlockSpec((TOP_K, TT), lambda i: (0, i), memory_space=pltpu.SMEM),
            pl.BlockSpec((TOP_K, TT), lambda i: (0, jnp.minimum(i + 1, N // TT - 1)), memory_space=pltpu.SMEM),
            pl.BlockSpec((TT, TOP_K), lambda i: (i, 0)),
            pl.BlockSpec((TT, D), lambda i: (i, 0)),
            pl.BlockSpec((None, 6, D), lambda i: (i // tiles_per_seq, 0, 0)),
            lay(ln_g3), lay(ln_b3),
            pl.BlockSpec(memory_space=pl.ANY),
        ],
        out_specs=pl.BlockSpec((TT, D), lambda i: (i, 0)),
        out_shape=jax.ShapeDtypeStruct((N, D), F32),
        scratch_shapes=[pltpu.VMEM((2, TOP_K, TT * D // LANES, LANES), F32), pltpu.SemaphoreType.DMA((2,))],
        compiler_params=_cparams("arbitrary"),
        name="moe_combine_ln2",
    )(dest, dest, gates_t, x1, mod, ln_g3, ln_b3, ys)


def _moe(u2_tiles, logits_t, x1, mod, w1, b1, w2, b2, ln_g, ln_b, seq_len, layer):
    N, D = x1.shape
    BLK = MOE_BLK
    top_e, gates, rank, counts = _topk(logits_t)
    counts = counts[:, 0].astype(jnp.int32)
    padded = (counts + BLK - 1) // BLK * BLK
    pend = jnp.cumsum(padded)
    offs = pend - padded
    n_blocks = -(-(N * TOP_K) // BLK) + N_EXPERTS
    blk_start = jnp.arange(n_blocks, dtype=jnp.int32) * BLK
    blk_e = jnp.sum((pend[None, :] <= blk_start[:, None]).astype(jnp.int32), axis=1)
    blk_e = jnp.minimum(blk_e, N_EXPERTS - 1).astype(jnp.int32)
    n_used = (pend[-1:] // BLK).astype(jnp.int32)
    tail_start = jnp.where(padded > 0, pend - BLK, -1).astype(jnp.int32)
    dest = _dest_rows(offs.astype(jnp.int32), top_e, rank)
    xs = _dispatch(tail_start, n_used, dest, u2_tiles, n_blocks * BLK, D)
    ys = _expert_ffn(blk_e, n_used, xs, w1, b1, w2, b2, layer)
    return _combine(dest, gates.T, x1, mod, ln_g, ln_b, ys, seq_len, layer)


def _split_w_in(w_in):
    AW = N_ATTN_HEADS * HEAD_DIM
    GW = GROUP_WIDTH
    w = w_in.astype(BF16)
    w_nat = jnp.concatenate([w[:, 3 * AW + 4 * HYENA_WIDTH:], w[:, 3 * AW:3 * AW + 4 * HYENA_WIDTH]], axis=1)
    w_qkv = jnp.stack([
        jnp.concatenate([w[:, t * AW + g * GW:t * AW + (g + 1) * GW] for t in range(3)], axis=1)
        for g in range(len(ATTN_GROUPS))])
    return w_nat, w_qkv


def kernel(x, c, w_ada, b_ada, w_in, w_sconv, b_sconv, w_pool, pool_scale, f_w1, f_b1, f_w_inner, f_b_inner, f_w_out, f_freq, hy_bias, p_attn, p_pool, p_hyena, w_out, ln1_g, ln1_b, w_router, b_router, w1, b1, w2, b2, ln2_g, ln2_b):
    B, S, D = x.shape
    depth = w_in.shape[0]
    bias_tables = [jnp.asarray(t) for t in _attn_bias_tables(S)]
    fmat, gmat = _dft_matrices(S, FREQ_TILE)
    zfeat, decay = _filter_features(S)
    gate_cols = 3 * D
    pool_block = gate_cols // HYENA_WIDTH
    bf = lambda t: t.astype(BF16)
    p_attn_b, p_pool_b, p_hyena_b, w_out_b = map(bf, (p_attn, p_pool, p_hyena, w_out))

    for l in range(depth):
        mod = _ada(c, w_ada, b_ada, l).reshape(B, 6, D)
        w_nat, w_qkv = _split_w_in(w_in[l])
        proj, u = _inproj_nat(x, mod, w_nat)
        qkv = _inproj_dil(u, w_qkv)
        ya = _attention(qkv, bias_tables)
        yb = _pool(proj, pool_block, w_pool, pool_scale, l)
        filt = _hyena_filters(zfeat, decay, f_w1, f_b1, f_w_inner, f_b_inner, f_w_out, f_freq, l)
        spectra = _filter_spectra(fmat, filt)
        yc = _hyena(proj, pool_block + 1, fmat, gmat, spectra, w_sconv, b_sconv, hy_bias, l)
        x1, u2, logits_t = _merge(ya, yb, yc, proj, x, mod, p_attn_b, p_pool_b, p_hyena_b, w_out_b,
                                  ln1_g, ln1_b, w_router, b_router, l)
        x = _moe(u2, logits_t, x1, mod, w1, b1, w2, b2, ln2_g, ln2_b, S, l).reshape(B, S, D)
    return x
```

```python
import functools
import math

import jax
import jax.numpy as jnp
import numpy as np
from jax import lax
from jax.experimental import pallas as pl
from jax.experimental.pallas import tpu as pltpu

F32 = jnp.float32
BF16 = jnp.bfloat16

HEAD_DIM = 64
HEADS_PER_GROUP = 4
ATTN_GROUPS = ((128, 1), (512, 4), (2048, 16))
N_ATTN_HEADS = HEADS_PER_GROUP * len(ATTN_GROUPS)
GROUP_WIDTH = HEADS_PER_GROUP * HEAD_DIM
HALF_WINDOW = 64
POOL_WINDOWS = (2, 4, 8, 16)
POOL_GROUP = 128
HYENA_WIDTH = 512
FILTER_EMB = 33
FILTER_HIDDEN = 64
N_EXPERTS = 32
TOP_K = 4
SWIGLU_LIMIT = 7.0
SWIGLU_ALPHA = 1.702
LN_EPS = 1e-5
DEPTH = 2
ALPHA_RES = (2 * DEPTH) ** 0.25
DECAY_TARGET = 1e-2
FAST_DECAY_PCT = 0.3
SLOW_DECAY_PCT = 1.5

QBLK = 128
FREQ_TILE = 512
INPROJ_COLS = 1024
ATTN_UNROLL = 2
ROW_CHUNK = 256
MERGE_ROWS = 512
TOPK_LANES = 1024
MOE_BLK = 512
FFN_CHUNK = 512
WEIGHT_CAST_ROWS = 128
DMA_PRIORITIES = 2
DMA_ISSUE_UNROLL = 4
MOE_TOK = 256
PAD_ROWS = 8
NEG_BIG = -1e30
LANES = 128
VMEM_LIMIT = 56 * 1024 * 1024


def _cparams(*sem):
    return pltpu.CompilerParams(dimension_semantics=sem, vmem_limit_bytes=VMEM_LIMIT)


def _dot(a, b):
    return jnp.dot(a, b, preferred_element_type=F32)


def _dot3(a, b):
    a_hi = a.astype(BF16)
    b_hi = b.astype(BF16)
    a_lo = (a - a_hi.astype(F32)).astype(BF16)
    b_lo = (b - b_hi.astype(F32)).astype(BF16)
    return _dot(a_hi, b_hi) + _dot(a_hi, b_lo) + _dot(a_lo, b_hi)


def _layer_norm(r, g, b):
    mu = jnp.mean(r, axis=-1, keepdims=True)
    c = r - mu
    var = jnp.mean(c * c, axis=-1, keepdims=True)
    return c * lax.rsqrt(var + LN_EPS) * g + b


def _store_token_tiles(ref, val):
    T, D = val.shape
    n = D // LANES
    for c in range(n):
        ref[pl.ds(c, T, stride=n), :] = val[:, c * LANES:(c + 1) * LANES]


def _load_token_tiles(ref, T, D, plane=None):
    n = D // LANES
    lead = () if plane is None else tuple(plane)
    idx = lambda c: lead + (pl.ds(c, T, stride=n), slice(None))
    return [ref[idx(c)] for c in range(n)]


def _ada_kernel(c_ref, w_ref, b_ref, o_ref):
    c = c_ref[...]
    cond = c * (1.0 / (1.0 + jnp.exp(-c)))
    o_ref[...] = _dot3(cond, w_ref[...]) + b_ref[...]


def _ada(c, w_ada, b_ada, layer):
    B, D = c.shape
    n_out = w_ada.shape[-1]
    tn = 768
    return pl.pallas_call(
        _ada_kernel,
        grid=(n_out // tn,),
        in_specs=[
            pl.BlockSpec((B, D), lambda j: (0, 0)),
            pl.BlockSpec((None, D, tn), lambda j: (layer, 0, j)),
            pl.BlockSpec((None, 1, tn), lambda j: (layer, 0, j)),
        ],
        out_specs=pl.BlockSpec((B, tn), lambda j: (0, j)),
        out_shape=jax.ShapeDtypeStruct((B, n_out), F32),
        compiler_params=_cparams("arbitrary"),
        name="ada",
    )(c, w_ada, b_ada.reshape(b_ada.shape[0], 1, n_out))


def _inproj_nat_kernel(x_ref, mod_ref, w_ref, o_ref, u_ref):
    S = x_ref.shape[0]

    @pl.when(pl.program_id(1) == 0)
    def _():
        shift = mod_ref[0:1, :]
        scale1p = 1.0 + mod_ref[1:2, :]

        def body(i, carry):
            rows = pl.ds(pl.multiple_of(i * ROW_CHUNK, ROW_CHUNK), ROW_CHUNK)
            u_ref[rows, :] = (x_ref[rows, :] * scale1p + shift).astype(BF16)
            return carry

        lax.fori_loop(0, S // ROW_CHUNK, body, 0)

    o_ref[...] = _dot(u_ref[...], w_ref[...]).astype(BF16)


def _inproj_nat(x, mod, w_nat):
    B, S, D = x.shape
    n_out = w_nat.shape[1]
    tn = INPROJ_COLS
    return pl.pallas_call(
        _inproj_nat_kernel,
        grid=(B, n_out // tn),
        in_specs=[
            pl.BlockSpec((None, S, D), lambda b, j: (b, 0, 0)),
            pl.BlockSpec((None, 6, D), lambda b, j: (b, 0, 0)),
            pl.BlockSpec((D, tn), lambda b, j: (0, j)),
        ],
        out_specs=[
            pl.BlockSpec((None, S, tn), lambda b, j: (b, 0, j)),
            pl.BlockSpec((None, S, D), lambda b, j: (b, 0, 0)),
        ],
        out_shape=[
            jax.ShapeDtypeStruct((B, S, n_out), BF16),
            jax.ShapeDtypeStruct((B, S, D), BF16),
        ],
        compiler_params=_cparams("arbitrary", "arbitrary"),
        name="inproj_nat",
    )(x, mod, w_nat)


def _inproj_dil_kernel(u_ref, w_ref, o0_ref, o1_ref, o2_ref, acc_ref):
    S = u_ref.shape[0]
    for g, o_ref in enumerate((o0_ref, o1_ref, o2_ref)):
        d = ATTN_GROUPS[g][1]
        L = S // d
        acc = _dot(u_ref[...], w_ref[g])
        if d == 1:
            o_ref[0] = acc.astype(BF16)
            continue
        for c in range(acc_ref.shape[0]):
            acc_ref[c] = acc[:, c * LANES:(c + 1) * LANES]
        for rho in range(d):
            for c in range(acc_ref.shape[0]):
                o_ref[rho, :, c * LANES:(c + 1) * LANES] = (
                    acc_ref[c, pl.ds(rho, L, stride=d), :].astype(BF16))


def _inproj_dil(u, w_qkv):
    B, S, D = u.shape
    W = w_qkv.shape[-1]
    dils = [d for _, d in ATTN_GROUPS]
    return pl.pallas_call(
        _inproj_dil_kernel,
        grid=(B,),
        in_specs=[
            pl.BlockSpec((None, S, D), lambda b: (b, 0, 0)),
            pl.BlockSpec((len(dils), D, W), lambda b: (0, 0, 0)),
        ],
        out_specs=[pl.BlockSpec((None, d, S // d, W), lambda b: (b, 0, 0, 0)) for d in dils],
        out_shape=[jax.ShapeDtypeStruct((B, d, S // d, W), BF16) for d in dils],
        scratch_shapes=[pltpu.VMEM((W // LANES, S, LANES), F32)],
        compiler_params=_cparams("arbitrary"),
        name="inproj_dil",
    )(u, w_qkv)


def _window_start(i, L, W):
    return min(max(QBLK * i - HALF_WINDOW, 0), L - W)


def _attn_bias_tables(S):
    h = np.arange(1, N_ATTN_HEADS + 1, dtype=np.float32)
    slopes = (2.0 ** (-8.0 * h / N_ATTN_HEADS)).astype(np.float32)
    tables = []
    for g, (_, d) in enumerate(ATTN_GROUPS):
        L = S // d
        W = min(2 * QBLK, L)
        nb = L // QBLK
        variants = sorted({_window_start(i, L, W) - QBLK * i for i in range(nb)}, reverse=True)
        r = np.arange(QBLK)[:, None]
        c = np.arange(W)[None, :]
        tab = np.empty((len(variants), HEADS_PER_GROUP * QBLK, W), np.float32)
        for vi, delta in enumerate(variants):
            off = np.abs(c - r + delta)
            for hh in range(HEADS_PER_GROUP):
                slope = np.float64(slopes[g * HEADS_PER_GROUP + hh])
                bias = np.where(off <= HALF_WINDOW, -slope * (d * off), NEG_BIG)
                tab[vi, hh * QBLK:(hh + 1) * QBLK] = bias.astype(np.float32)
        tables.append(tab)
    return tables


def _block_attention(q, kwin, vwin, bias, head_of_lane):
    zero = jnp.zeros_like(q)
    q4 = jnp.concatenate([jnp.where(head_of_lane == h, q, zero) for h in range(HEADS_PER_GROUP)], axis=0)
    s = lax.dot_general(q4, kwin, (((1,), (1,)), ((), ())), preferred_element_type=F32)
    s = s * (1.0 / math.sqrt(HEAD_DIM)) + bias
    m = jnp.max(s, axis=-1, keepdims=True)
    p = jnp.exp(s - m)
    l = jnp.sum(p, axis=-1, keepdims=True)
    o4 = _dot(p.astype(BF16), vwin) * (1.0 / l)
    lse4 = m + jnp.log(l)
    out = jnp.zeros((QBLK, GROUP_WIDTH), F32)
    lse = jnp.zeros((QBLK, GROUP_WIDTH), F32)
    for h in range(HEADS_PER_GROUP):
        rows = slice(h * QBLK, (h + 1) * QBLK)
        sel = head_of_lane == h
        out = jnp.where(sel, o4[rows], out)
        lse = jnp.where(sel, lse4[rows], lse)
    return out, lse


def _attn_kernel(q0_ref, q1_ref, q2_ref, b0_ref, b1_ref, b2_ref, o_ref, acc_ref, lse_ref):
    S = o_ref.shape[0]
    GW = GROUP_WIDTH
    head_of_lane = lax.broadcasted_iota(jnp.int32, (QBLK, GW), 1) // HEAD_DIM
    n_planes = GW // LANES

    def put(g, rows, out, lse):
        for c in range(n_planes):
            acc_ref[g, c, rows, :] = out[:, c * LANES:(c + 1) * LANES]
            lse_ref[g, c, rows, :] = lse[:, c * LANES:(c + 1) * LANES]

    L0 = S
    nb0 = L0 // QBLK

    def g0_body(i, carry):
        ws = pl.multiple_of(jnp.clip(QBLK * i - HALF_WINDOW, 0, L0 - 2 * QBLK), HALF_WINDOW)
        var = jnp.where(i == 0, 0, jnp.where(i == nb0 - 1, 2, 1))
        rows = pl.ds(pl.multiple_of(i * QBLK, QBLK), QBLK)
        out, lse = _block_attention(
            q0_ref[0, rows, 0:GW],
            q0_ref[0, pl.ds(ws, 2 * QBLK), GW:2 * GW],
            q0_ref[0, pl.ds(ws, 2 * QBLK), 2 * GW:3 * GW],
            b0_ref[var], head_of_lane)
        put(0, rows, out, lse)
        return carry

    lax.fori_loop(0, nb0, g0_body, 0, unroll=ATTN_UNROLL)

    for g, (q_ref, b_ref) in ((1, (q1_ref, b1_ref)), (2, (q2_ref, b2_ref))):
        d = ATTN_GROUPS[g][1]
        L = S // d
        W = min(2 * QBLK, L)
        nb = L // QBLK
        deltas = sorted({_window_start(i, L, W) - QBLK * i for i in range(nb)}, reverse=True)

        def gd_body(rho, carry, q_ref=q_ref, b_ref=b_ref, d=d, L=L, W=W, nb=nb, deltas=deltas, g=g):
            for i in range(nb):
                ws = _window_start(i, L, W)
                var = deltas.index(ws - QBLK * i)
                out, lse = _block_attention(
                    q_ref[rho, i * QBLK:(i + 1) * QBLK, 0:GW],
                    q_ref[rho, ws:ws + W, GW:2 * GW],
                    q_ref[rho, ws:ws + W, 2 * GW:3 * GW],
                    b_ref[var], head_of_lane)
                dst = pl.ds(rho + i * QBLK * d, QBLK, stride=d)
                put(g, dst, out, lse)
            return carry

        lax.fori_loop(0, d, gd_body, 0, unroll=max(1, ATTN_UNROLL // nb))

    def merge_body(i, carry):
        rows = pl.ds(pl.multiple_of(i * ROW_CHUNK, ROW_CHUNK), ROW_CHUNK)
        for c in range(n_planes):
            l0, l1, l2 = lse_ref[0, c, rows, :], lse_ref[1, c, rows, :], lse_ref[2, c, rows, :]
            mx = jnp.maximum(jnp.maximum(l0, l1), l2)
            w0, w1, w2 = jnp.exp(l0 - mx), jnp.exp(l1 - mx), jnp.exp(l2 - mx)
            num = w0 * acc_ref[0, c, rows, :] + w1 * acc_ref[1, c, rows, :] + w2 * acc_ref[2, c, rows, :]
            o_ref[rows, c * LANES:(c + 1) * LANES] = (num * (1.0 / (w0 + w1 + w2))).astype(BF16)
        return carry

    lax.fori_loop(0, S // ROW_CHUNK, merge_body, 0)


def _attention(qkv, bias_tables):
    q0, q1, q2 = qkv
    B, _, S, W3 = q0.shape
    GW = GROUP_WIDTH
    in_specs = [pl.BlockSpec((None,) + q.shape[1:], lambda b: (b, 0, 0, 0)) for q in qkv]
    in_specs += [pl.BlockSpec(t.shape, lambda b: (0, 0, 0)) for t in bias_tables]
    return pl.pallas_call(
        _attn_kernel,
        grid=(B,),
        in_specs=in_specs,
        out_specs=pl.BlockSpec((None, S, GW), lambda b: (b, 0, 0)),
        out_shape=jax.ShapeDtypeStruct((B, S, GW), BF16),
        scratch_shapes=[pltpu.VMEM((3, GW // LANES, S, LANES), F32)] * 2,
        compiler_params=_cparams("arbitrary"),
        name="attention",
    )(q0, q1, q2, *bias_tables)


def _pool_kernel(y_ref, w_ref, sc_ref, o_ref, pad_ref):
    S = y_ref.shape[0]
    C = y_ref.shape[1]
    P = PAD_ROWS
    pad_ref[0:P, :] = jnp.zeros((P, C), F32)
    pad_ref[S + P:S + 2 * P, :] = jnp.zeros((P, C), F32)
    pad_ref[P:S + P, :] = y_ref[...].astype(F32)
    for c in range(S // ROW_CHUNK):
        r0 = c * ROW_CHUNK
        t = r0 + lax.broadcasted_iota(jnp.int32, (ROW_CHUNK, 1), 0)
        for g, w in enumerate(POOL_WINDOWS):
            cols = slice(g * POOL_GROUP, (g + 1) * POOL_GROUP)
            acc = pad_ref[P + r0 - w // 2:P + r0 - w // 2 + ROW_CHUNK, cols]
            for j in range(-w // 2 + 1, w // 2):
                acc = acc + pad_ref[P + r0 + j:P + r0 + j + ROW_CHUNK, cols]
            cnt = (jnp.minimum(t + w // 2, S) - jnp.maximum(t - w // 2, 0)).astype(F32)
            pooled = acc / cnt - pad_ref[P + r0:P + r0 + ROW_CHUNK, cols]
            mixed = _dot(pooled.astype(BF16), w_ref[g].astype(BF16)) * sc_ref[:, cols]
            o_ref[r0:r0 + ROW_CHUNK, cols] = mixed.astype(BF16)


def _pool(proj, col_block, w_pool, pool_scale, layer):
    B, S, _ = proj.shape
    C = POOL_GROUP * len(POOL_WINDOWS)
    return pl.pallas_call(
        _pool_kernel,
        grid=(B,),
        in_specs=[
            pl.BlockSpec((None, S, C), lambda b: (b, 0, col_block)),
            pl.BlockSpec((None,) + w_pool.shape[1:], lambda b: (layer, 0, 0, 0)),
            pl.BlockSpec((None, 1, C), lambda b: (layer, 0, 0)),
        ],
        out_specs=pl.BlockSpec((None, S, C), lambda b: (b, 0, 0)),
        out_shape=jax.ShapeDtypeStruct((B, S, C), BF16),
        scratch_shapes=[pltpu.VMEM((S + 2 * PAD_ROWS, C), F32)],
        compiler_params=_cparams("arbitrary"),
        name="pool",
    )(proj, w_pool, pool_scale.reshape(pool_scale.shape[0], 1, C))


def _filter_kernel(z_ref, w1_ref, b1_ref, wi_ref, bi_ref, wo_ref, fr_ref, dec_ref, o_ref):
    S = z_ref.shape[0]
    C = HYENA_WIDTH
    freq = fr_ref[...]
    h = jnp.sin(freq * (_dot3(z_ref[...], w1_ref[...]) + b1_ref[...]))
    for i in range(wi_ref.shape[0]):
        h = jnp.sin(freq * (_dot3(h, wi_ref[i]) + bi_ref[i]))
    not_first = lax.broadcasted_iota(jnp.int32, (S, 1), 0) > 0
    for blk in range(wo_ref.shape[1] // C):
        cols = slice(blk * C, (blk + 1) * C)
        f = _dot3(h, wo_ref[:, cols]) * dec_ref[...]
        if blk % 2 == 1:
            f = jnp.where(not_first, f, 0.0)
        o_ref[:, cols] = f.astype(BF16)


def _hyena_filters(zfeat, decay, f_w1, f_b1, f_w_inner, f_b_inner, f_w_out, f_freq, layer):
    S, E = zfeat.shape
    H = FILTER_HIDDEN
    n_out = f_w_out.shape[-1]
    n_inner = f_w_inner.shape[1]
    w1 = jnp.zeros((f_w1.shape[0], E, H), F32).at[:, :f_w1.shape[1]].set(f_w1)
    full = lambda shape: pl.BlockSpec(shape, lambda i: (0,) * len(shape))
    lay = lambda shape: pl.BlockSpec((None,) + shape, lambda i: (layer,) + (0,) * len(shape))
    return pl.pallas_call(
        _filter_kernel,
        grid=(1,),
        in_specs=[
            full((S, E)), lay((E, H)), lay((1, H)), lay((n_inner, H, H)), lay((n_inner, 1, H)),
            lay((H, n_out)), lay((1, H)), full((S, HYENA_WIDTH)),
        ],
        out_specs=full((S, n_out)),
        out_shape=jax.ShapeDtypeStruct((S, n_out), BF16),
        compiler_params=_cparams("arbitrary"),
        name="hyena_filter",
    )(zfeat, w1, f_b1.reshape(-1, 1, H), f_w_inner, f_b_inner.reshape(-1, n_inner, 1, H), f_w_out,
      f_freq.reshape(-1, 1, H), decay)


def _spec_kernel(f_ref, h_ref, a_ref, b_ref, c_ref):
    TK = f_ref.shape[0] // 2
    C = HYENA_WIDTH
    n_fft = 2 * h_ref.shape[0]
    x = _dot(f_ref[...], h_ref[...])
    xc, xs = x[:TK], x[TK:]
    row = pl.program_id(0) * TK + lax.broadcasted_iota(jnp.int32, (TK, 1), 0)
    is0 = row == 0
    s = jnp.where(is0, 1.0 / n_fft, 2.0 / n_fft)
    for o in range(a_ref.shape[0]):
        hf = slice(2 * o * C, (2 * o + 1) * C)
        hb = slice((2 * o + 1) * C, (2 * o + 2) * C)
        a = (xc[:, hf] + xc[:, hb]) * s
        a_ref[o] = a
        b_ref[o] = jnp.where(is0, 0.0, (xs[:, hf] - xs[:, hb]) * s)
        c_ref[o] = jnp.where(is0, (xs[:, hf] + xs[:, hb]) * (1.0 / n_fft), a)


def _filter_spectra(fmat, filt):
    nkt, tk2, S = fmat.shape
    TK = tk2 // 2
    C = HYENA_WIDTH
    n_ord = filt.shape[1] // (2 * C)
    out = jax.ShapeDtypeStruct((n_ord, S, C), F32)
    ospec = pl.BlockSpec((n_ord, TK, C), lambda k: (0, k, 0))
    return pl.pallas_call(
        _spec_kernel,
        grid=(nkt,),
        in_specs=[
            pl.BlockSpec((None, tk2, S), lambda k: (k, 0, 0)),
            pl.BlockSpec(filt.shape, lambda k: (0, 0)),
        ],
        out_specs=[ospec, ospec, ospec],
        out_shape=[out, out, out],
        compiler_params=_cparams("arbitrary"),
        name="hyena_spectra",
    )(fmat, filt)


def _hyena_kernel(v_ref, x1_ref, x2_ref, f_ref, g_ref, a_ref, b_ref, c_ref, wsc_ref, bsc_ref, hb_ref,
                  o_ref, pad_ref, z_ref, zb_ref, gate_ref, ycat_ref):
    S, C = z_ref.shape
    TK = f_ref.shape[0] // 2
    TT = g_ref.shape[0]
    nkt = S // TK
    P = PAD_ROWS
    o = pl.program_id(1)
    s = pl.program_id(2)

    def short_conv(in_ref, blk, consume):
        cols = slice(blk * C, (blk + 1) * C)
        pad_ref[0:P, :] = jnp.zeros((P, C), F32)
        pad_ref[S + P:S + 2 * P, :] = jnp.zeros((P, C), F32)
        pad_ref[P:S + P, :] = in_ref[...].astype(F32)
        w0, w1, w2 = wsc_ref[0:1, cols], wsc_ref[1:2, cols], wsc_ref[2:3, cols]
        bias = bsc_ref[:, cols]
        for c in range(S // ROW_CHUNK):
            r0 = c * ROW_CHUNK
            conv = (w0 * pad_ref[P + r0 - 1:P + r0 - 1 + ROW_CHUNK, :]
                    + w1 * pad_ref[P + r0:P + r0 + ROW_CHUNK, :]
                    + w2 * pad_ref[P + r0 + 1:P + r0 + 1 + ROW_CHUNK, :] + bias)
            consume(r0, conv)

    def store_gate(r0, conv):
        gate_ref[r0:r0 + ROW_CHUNK, :] = conv

    @pl.when((o == 0) & (s == 0))
    def _():
        def store_z(r0, conv):
            z_ref[r0:r0 + ROW_CHUNK, :] = conv
            zb_ref[r0:r0 + ROW_CHUNK, :] = conv.astype(BF16)
        short_conv(v_ref, 0, store_z)

    @pl.when(s < nkt)
    def _():
        x = _dot(f_ref[...], zb_ref[...])
        xre, xim = x[:TK], x[TK:]
        a, b, c = a_ref[...], b_ref[...], c_ref[...]
        base = pl.multiple_of(s * 2 * TK, 2 * TK)
        ycat_ref[pl.ds(base, TK), :] = (xre * a - xim * b).astype(BF16)
        ycat_ref[pl.ds(base + TK, TK), :] = (xre * b + xim * c).astype(BF16)

    @pl.when((s == nkt) & (o == 0))
    def _():
        short_conv(x1_ref, 1, store_gate)

    @pl.when((s == nkt) & (o == 1))
    def _():
        short_conv(x2_ref, 2, store_gate)

    @pl.when(s >= nkt)
    def _():
        rows = pl.ds(pl.multiple_of((s - nkt) * TT, TT), TT)
        y = _dot(g_ref[...], ycat_ref[...])
        hbias = jnp.where(o == 0, hb_ref[0:1, :], hb_ref[1:2, :])
        znew = gate_ref[rows, :] * (y + z_ref[rows, :] * hbias)
        z_ref[rows, :] = znew
        zb_ref[rows, :] = znew.astype(BF16)
        o_ref[rows, :] = znew.astype(BF16)


def _hyena(proj, first_col_block, fmat, gmat, spectra, w_sconv, b_sconv, hy_bias, layer):
    B, S, _ = proj.shape
    C = HYENA_WIDTH
    nkt, tk2, _ = fmat.shape
    TK = tk2 // 2
    TT = S // nkt
    a, b, c = spectra
    n_ord = a.shape[0]
    assert n_ord == 2
    fwd = lambda s: jnp.minimum(s, nkt - 1)
    inv = lambda s: jnp.clip(s - nkt, 0, nkt - 1)
    inp = lambda blk: pl.BlockSpec((None, S, C), lambda bb, o, s: (bb, 0, first_col_block + blk),
                                   pipeline_mode=pl.Buffered(1))
    spec = pl.BlockSpec((None, TK, C), lambda bb, o, s: (o, fwd(s), 0))
    return pl.pallas_call(
        _hyena_kernel,
        grid=(B, n_ord, 2 * nkt),
        in_specs=[
            inp(0), inp(1), inp(2),
            pl.BlockSpec((None, tk2, S), lambda bb, o, s: (fwd(s), 0, 0)),
            pl.BlockSpec((TT, 2 * S), lambda bb, o, s: (inv(s), 0)),
            spec, spec, spec,
            pl.BlockSpec((None, 3, 3 * C), lambda bb, o, s: (layer, 0, 0)),
            pl.BlockSpec((None, 1, 3 * C), lambda bb, o, s: (layer, 0, 0)),
            pl.BlockSpec((None, n_ord, C), lambda bb, o, s: (layer, 0, 0)),
        ],
        out_specs=pl.BlockSpec((None, S, C), lambda bb, o, s: (bb, 0, 0)),
        out_shape=jax.ShapeDtypeStruct((B, S, C), BF16),
        scratch_shapes=[
            pltpu.VMEM((S + 2 * PAD_ROWS, C), F32),
            pltpu.VMEM((S, C), F32),
            pltpu.VMEM((S, C), BF16),
            pltpu.VMEM((S, C), F32),
            pltpu.VMEM((2 * S, C), BF16),
        ],
        compiler_params=_cparams("arbitrary", "arbitrary", "arbitrary"),
        name="hyena_conv",
    )(proj, proj, proj, fmat, gmat, a, b, c, w_sconv, b_sconv.reshape(b_sconv.shape[0], 1, 3 * C), hy_bias)


def _dft_matrices(S, TK):
    n_fft = 2 * S
    k = lax.broadcasted_iota(jnp.int32, (S, S), 0)
    n = lax.broadcasted_iota(jnp.int32, (S, S), 1)
    theta = ((k * n) & (n_fft - 1)).astype(F32) * (2.0 * math.pi / n_fft)
    fc = jnp.cos(theta)
    fs = jnp.where(k == 0, (1 - 2 * (n & 1)).astype(F32), -jnp.sin(theta))
    nkt = S // TK
    fmat = jnp.concatenate([fc.reshape(nkt, TK, S), fs.reshape(nkt, TK, S)], axis=1).astype(BF16)
    return fmat, fmat.reshape(n_fft, S).T


def _filter_features(S):
    t = jnp.linspace(0.0, 1.0, S, dtype=F32)[:, None]
    bands = (FILTER_EMB - 1) // 2
    w = 2.0 * math.pi * jnp.arange(S, dtype=F32) / S
    f = jnp.linspace(1e-4, bands - 1, bands, dtype=F32)
    ang = w[:, None] * f[None, :]
    z = jnp.concatenate([t, jnp.cos(ang), -jnp.sin(ang)], axis=-1)
    z = jnp.pad(z, ((0, 0), (0, 128 - FILTER_EMB)))
    min_decay = math.log(DECAY_TARGET) / SLOW_DECAY_PCT
    max_decay = math.log(DECAY_TARGET) / FAST_DECAY_PCT
    deltas = jnp.linspace(min_decay, max_decay, HYENA_WIDTH, dtype=F32)
    decay = jnp.exp(-t * jnp.abs(deltas)[None, :])
    return z, decay


def _merge_kernel(ya_ref, yb_ref, yc_ref, gl_ref, x_ref, mod_ref, pa_ref, pp_ref, ph_ref, wo_ref,
                  g_ref, b_ref, wr_ref, br_ref, x1_ref, u2_ref, lt_ref):
    D = x_ref.shape[1]
    sig = lambda v: 1.0 / (1.0 + jnp.exp(-v))
    merged = sig(gl_ref[:, 0:D].astype(F32)) * _dot(ya_ref[...], pa_ref[...])
    merged += sig(gl_ref[:, D:2 * D].astype(F32)) * _dot(yb_ref[...], pp_ref[...])
    merged += sig(gl_ref[:, 2 * D:3 * D].astype(F32)) * _dot(yc_ref[...], ph_ref[...])
    h = _dot(merged.astype(BF16), wo_ref[...])
    gate1, shift2, scale2 = mod_ref[2:3, :], mod_ref[3:4, :], mod_ref[4:5, :]
    x1 = _layer_norm(ALPHA_RES * x_ref[...] + (1.0 + gate1) * h, g_ref[...], b_ref[...])
    x1_ref[...] = x1
    u2 = x1 * (1.0 + scale2) + shift2
    _store_token_tiles(u2_ref, u2)
    logits = _dot3(u2, wr_ref[...]) + br_ref[...]
    lt_ref[...] = logits.T[0:N_EXPERTS, :]


def _merge(ya, yb, yc, proj, x, mod, p_attn, p_pool, p_hyena, w_out, ln_g, ln_b, w_router, b_router, layer):
    B, S, D = x.shape
    N = B * S
    TM = MERGE_ROWS
    tiles_per_seq = S // TM
    EP = 128
    flat = lambda t: t.reshape(N, t.shape[-1])
    wr = jnp.zeros((D, EP), F32).at[:, :N_EXPERTS].set(w_router[layer])
    br = jnp.zeros((1, EP), F32).at[:, :N_EXPERTS].set(b_router[layer][None])
    tok = lambda w: pl.BlockSpec((TM, w), lambda i: (i, 0))
    lay = lambda t: pl.BlockSpec((None,) + t.shape[1:], lambda i: (layer,) + (0,) * (t.ndim - 1))
    ln_g3, ln_b3 = ln_g.reshape(-1, 1, D), ln_b.reshape(-1, 1, D)
    return pl.pallas_call(
        _merge_kernel,
        grid=(N // TM,),
        in_specs=[
            tok(ya.shape[-1]), tok(yb.shape[-1]), tok(yc.shape[-1]), tok(3 * D), tok(D),
            pl.BlockSpec((None, 6, D), lambda i: (i // tiles_per_seq, 0, 0)),
            lay(p_attn), lay(p_pool), lay(p_hyena), lay(w_out), lay(ln_g3), lay(ln_b3),
            pl.BlockSpec((D, EP), lambda i: (0, 0)),
            pl.BlockSpec((1, EP), lambda i: (0, 0)),
        ],
        out_specs=[tok(D), pl.BlockSpec((TM * D // LANES, LANES), lambda i: (i, 0)),
                   pl.BlockSpec((N_EXPERTS, TM), lambda i: (0, i))],
        out_shape=[
            jax.ShapeDtypeStruct((N, D), F32),
            jax.ShapeDtypeStruct((N * D // LANES, LANES), F32),
            jax.ShapeDtypeStruct((N_EXPERTS, N), F32),
        ],
        compiler_params=_cparams("arbitrary"),
        name="merge_ln1_router",
    )(flat(ya), flat(yb), flat(yc), flat(proj), flat(x), mod, p_attn, p_pool, p_hyena, w_out,
      ln_g3, ln_b3, wr, br)


def _topk_kernel(l_ref, e_ref, g_ref, r_ref, cnt_ref, carry_ref):
    E, TL = l_ref.shape

    @pl.when(pl.program_id(0) == 0)
    def _():
        carry_ref[...] = jnp.zeros_like(carry_ref)

    l = l_ref[...]
    eio = lax.broadcasted_iota(jnp.int32, (E, TL), 0).astype(F32)
    vals, hots = [], []
    for k in range(TOP_K):
        m = jnp.max(l, axis=0, keepdims=True)
        idx = jnp.min(jnp.where(l == m, eio, float(E)), axis=0, keepdims=True)
        hot = eio == idx
        l = jnp.where(hot, -jnp.inf, l)
        vals.append(m)
        hots.append(hot)
        e_ref[k:k + 1, :] = idx.astype(jnp.int32)
    exps = [jnp.exp(v - vals[0]) for v in vals]
    inv = 1.0 / (exps[0] + exps[1] + exps[2] + exps[3])
    for k in range(TOP_K):
        g_ref[k:k + 1, :] = exps[k] * inv

    memb = jnp.zeros((E, TL), F32)
    for hot in hots:
        memb = memb + hot.astype(F32)
    upper = (lax.broadcasted_iota(jnp.int32, (TL, TL), 0)
             <= lax.broadcasted_iota(jnp.int32, (TL, TL), 1)).astype(BF16)
    incl = _dot(memb.astype(BF16), upper)
    excl = incl - memb + carry_ref[:, 0:1]
    for k in range(TOP_K):
        rank = jnp.sum(jnp.where(hots[k], excl, 0.0), axis=0, keepdims=True)
        r_ref[k:k + 1, :] = rank.astype(jnp.int32)
    carry_ref[...] = carry_ref[...] + jnp.sum(memb, axis=1, keepdims=True)
    cnt_ref[...] = carry_ref[...]


def _topk(logits_t):
    E, N = logits_t.shape
    TL = TOPK_LANES
    tok = pl.BlockSpec((TOP_K, TL), lambda i: (0, i))
    return pl.pallas_call(
        _topk_kernel,
        grid=(N // TL,),
        in_specs=[pl.BlockSpec((E, TL), lambda i: (0, i))],
        out_specs=[tok, tok, tok, pl.BlockSpec((E, 128), lambda i: (0, 0))],
        out_shape=[
            jax.ShapeDtypeStruct((TOP_K, N), jnp.int32),
            jax.ShapeDtypeStruct((TOP_K, N), F32),
            jax.ShapeDtypeStruct((TOP_K, N), jnp.int32),
            jax.ShapeDtypeStruct((E, 128), F32),
        ],
        scratch_shapes=[pltpu.VMEM((E, 128), F32)],
        compiler_params=_cparams("arbitrary"),
        name="topk_rank",
    )(logits_t)


def _dest_kernel(offs_ref, e_ref, r_ref, d_ref):
    e = e_ref[...]
    acc = r_ref[...]
    for ex in range(N_EXPERTS):
        acc = acc + jnp.where(e == ex, offs_ref[ex], 0)
    d_ref[...] = acc


def _dest_rows(offs, top_e, rank):
    K, N = top_e.shape
    TL = TOPK_LANES
    tok = pl.BlockSpec((K, TL), lambda i, offs: (0, i))
    return pl.pallas_call(
        _dest_kernel,
        grid_spec=pltpu.PrefetchScalarGridSpec(
            num_scalar_prefetch=1, grid=(N // TL,), in_specs=[tok, tok], out_specs=tok),
        out_shape=jax.ShapeDtypeStruct((K, N), jnp.int32),
        compiler_params=_cparams("arbitrary"),
        name="dest_rows",
    )(offs, top_e, rank)


def _dispatch_kernel(tail_ref, n_used_ref, dest_ref, u_ref, xs_ref, zero_ref, sem):
    TT = dest_ref.shape[1]
    RT = u_ref.shape[0] // TT
    blk_rows = zero_ref.shape[0]
    n_blocks = xs_ref.shape[0] // blk_rows

    def zero_copy(first_row):
        start = pl.multiple_of(first_row * RT, RT)
        return pltpu.make_async_copy(zero_ref, xs_ref.at[pl.ds(start, blk_rows), :], sem)

    @pl.when(pl.program_id(0) == 0)
    def _():
        zero_ref[...] = jnp.zeros_like(zero_ref)

        def tails(fn):
            def body(e, carry):
                @pl.when(tail_ref[e] >= 0)
                def _():
                    fn(zero_copy(jnp.maximum(tail_ref[e], 0)))
                return carry
            lax.fori_loop(0, N_EXPERTS, body, 0)

        def unused(fn):
            def body(i, carry):
                fn(zero_copy(i * (blk_rows // RT)))
                return carry
            lax.fori_loop(n_used_ref[0], n_blocks, body, 0)

        tails(lambda cp: cp.start())
        unused(lambda cp: cp.start())
        tails(lambda cp: cp.wait())
        unused(lambda cp: cp.wait())

    def row_copy(j, k):
        src = pl.multiple_of(j * RT, RT)
        dst = pl.multiple_of(dest_ref[k, j] * RT, RT)
        return pltpu.make_async_copy(u_ref.at[pl.ds(src, RT), :], xs_ref.at[pl.ds(dst, RT), :], sem)

    def start(j, carry):
        for k in range(TOP_K):
            row_copy(j, k).start(priority=k % DMA_PRIORITIES)
        return carry

    lax.fori_loop(0, TT, start, 0, unroll=DMA_ISSUE_UNROLL)
    for k in range(TOP_K):
        pltpu.make_async_copy(u_ref, xs_ref.at[pl.ds(0, TT * RT), :], sem).wait()


def _dispatch(tail_start, n_used, dest, u2_tiles, n_rows, D):
    RT = D // LANES
    N = u2_tiles.shape[0] // RT
    TT = MOE_TOK
    return pl.pallas_call(
        _dispatch_kernel,
        grid_spec=pltpu.PrefetchScalarGridSpec(
            num_scalar_prefetch=2,
            grid=(N // TT,),
            in_specs=[
                pl.BlockSpec((TOP_K, TT), lambda i, tail, nu: (0, i), memory_space=pltpu.SMEM),
                pl.BlockSpec((TT * RT, LANES), lambda i, tail, nu: (i, 0)),
            ],
            out_specs=pl.BlockSpec(memory_space=pl.ANY),
            scratch_shapes=[pltpu.VMEM((MOE_BLK * RT, LANES), F32), pltpu.SemaphoreType.DMA(())],
        ),
        out_shape=jax.ShapeDtypeStruct((n_rows * RT, LANES), F32),
        compiler_params=_cparams("arbitrary"),
        name="moe_dispatch",
    )(tail_start, n_used, dest, u2_tiles)


def _ffn_kernel(blk_e_ref, n_used_ref, xs_ref, w1_ref, b1_ref, w2_ref, b2_ref, ys_ref,
                w1b_ref, w2b_ref, act_ref):
    F, D = w2_ref.shape
    BLK = xs_ref.shape[0] * LANES // D
    i = pl.program_id(0)
    used = i < n_used_ref[0]
    new_expert = (i == 0) | (blk_e_ref[i] != blk_e_ref[jnp.maximum(i - 1, 0)])

    @pl.when(used & new_expert)
    def _():
        for r in range(0, D, WEIGHT_CAST_ROWS):
            w1b_ref[r:r + WEIGHT_CAST_ROWS, :] = w1_ref[r:r + WEIGHT_CAST_ROWS, :].astype(BF16)
        for r in range(0, F, WEIGHT_CAST_ROWS):
            w2b_ref[r:r + WEIGHT_CAST_ROWS, :] = w2_ref[r:r + WEIGHT_CAST_ROWS, :].astype(BF16)

    @pl.when(used)
    def _():
        x = jnp.concatenate(_load_token_tiles(xs_ref, BLK, D), axis=1).astype(BF16)
        for c in range(0, F, FFN_CHUNK):
            cg = slice(c, c + FFN_CHUNK)
            cl = slice(F + c, F + c + FFN_CHUNK)
            glu = jnp.minimum(_dot(x, w1b_ref[:, cg]) + b1_ref[:, cg], SWIGLU_LIMIT)
            lin = jnp.clip(_dot(x, w1b_ref[:, cl]) + b1_ref[:, cl], -SWIGLU_LIMIT, SWIGLU_LIMIT)
            act = glu * (1.0 / (1.0 + jnp.exp(-SWIGLU_ALPHA * glu))) * (lin + 1.0)
            act_ref[:, cg] = act.astype(BF16)
        _store_token_tiles(ys_ref, _dot(act_ref[...], w2b_ref[...]) + b2_ref[...])

    @pl.when(jnp.logical_not(used))
    def _():
        ys_ref[...] = jnp.zeros_like(ys_ref)


def _expert_ffn(blk_e, n_used, xs_tiles, w1, b1, w2, b2, layer):
    _, E, D, F2 = w1.shape
    F = F2 // 2
    RT = D // LANES
    BLK = MOE_BLK
    rows = pl.BlockSpec((BLK * RT, LANES), lambda i, be, nu: (i, 0))
    per_expert = lambda r, c: pl.BlockSpec((None, None, r, c), lambda i, be, nu: (layer, be[i], 0, 0))
    return pl.pallas_call(
        _ffn_kernel,
        grid_spec=pltpu.PrefetchScalarGridSpec(
            num_scalar_prefetch=2,
            grid=(xs_tiles.shape[0] // (BLK * RT),),
            in_specs=[rows, per_expert(D, F2), per_expert(1, F2), per_expert(F, D), per_expert(1, D)],
            out_specs=rows,
            scratch_shapes=[pltpu.VMEM((D, F2), BF16), pltpu.VMEM((F, D), BF16), pltpu.VMEM((BLK, F), BF16)],
        ),
        out_shape=jax.ShapeDtypeStruct(xs_tiles.shape, F32),
        compiler_params=_cparams("arbitrary"),
        name="moe_ffn",
    )(blk_e, n_used, xs_tiles, w1, b1.reshape(-1, E, 1, F2), w2, b2.reshape(-1, E, 1, D))


def _combine_kernel(dest_ref, dest_next_ref, gt_ref, x1_ref, mod_ref, g_ref, b_ref, ys_ref, o_ref, buf_ref, sem):
    TT, D = x1_ref.shape
    RT = D // LANES
    i = pl.program_id(0)
    slot = i % 2

    def issue(d_ref, s):
        def body(j, carry):
            for k in range(TOP_K):
                src = pl.multiple_of(d_ref[k, j] * RT, RT)
                dst = pl.multiple_of(j * RT, RT)
                pltpu.make_async_copy(ys_ref.at[pl.ds(src, RT), :], buf_ref.at[s, k, pl.ds(dst, RT), :],
                                      sem.at[s]).start(priority=k % DMA_PRIORITIES)
            return carry
        lax.fori_loop(0, TT, body, 0, unroll=DMA_ISSUE_UNROLL)

    @pl.when(i == 0)
    def _():
        issue(dest_ref, 0)

    @pl.when(i + 1 < pl.num_programs(0))
    def _():
        issue(dest_next_ref, 1 - slot)

    for k in range(TOP_K):
        pltpu.make_async_copy(ys_ref.at[pl.ds(0, TT * RT), :], buf_ref.at[slot, k], sem.at[slot]).wait()

    chunks = None
    for k in range(TOP_K):
        gate = gt_ref[:, k:k + 1]
        rows = [gate * t for t in _load_token_tiles(buf_ref, TT, D, plane=(slot, k))]
        chunks = rows if chunks is None else [a + b for a, b in zip(chunks, rows)]
    h = jnp.concatenate(chunks, axis=1)
    gate2 = mod_ref[5:6, :]
    o_ref[...] = _layer_norm(ALPHA_RES * x1_ref[...] + (1.0 + gate2) * h, g_ref[...], b_ref[...])


def _combine(dest, gates_t, x1, mod, ln_g, ln_b, ys, seq_len, layer):
    N, D = x1.shape
    TT = MOE_TOK
    tiles_per_seq = seq_len // TT
    ln_g3, ln_b3 = ln_g.reshape(-1, 1, D), ln_b.reshape(-1, 1, D)
    lay = lambda t: pl.BlockSpec((None,) + t.shape[1:], lambda i: (layer,) + (0,) * (t.ndim - 1))
    return pl.pallas_call(
        _combine_kernel,
        grid=(N // TT,),
        in_specs=[
            pl.BlockSpec((TOP_K, TT), lambda i: (0, i), memory_space=pltpu.SMEM),
            pl.BlockSpec((TOP_K, TT), lambda i: (0, jnp.minimum(i + 1, N // TT - 1)), memory_space=pltpu.SMEM),
            pl.BlockSpec((TT, TOP_K), lambda i: (i, 0)),
            pl.BlockSpec((TT, D), lambda i: (i, 0)),
            pl.BlockSpec((None, 6, D), lambda i: (i // tiles_per_seq, 0, 0)),
            lay(ln_g3), lay(ln_b3),
            pl.BlockSpec(memory_space=pl.ANY),
        ],
        out_specs=pl.BlockSpec((TT, D), lambda i: (i, 0)),
        out_shape=jax.ShapeDtypeStruct((N, D), F32),
        scratch_shapes=[pltpu.VMEM((2, TOP_K, TT * D // LANES, LANES), F32), pltpu.SemaphoreType.DMA((2,))],
        compiler_params=_cparams("arbitrary"),
        name="moe_combine_ln2",
    )(dest, dest, gates_t, x1, mod, ln_g3, ln_b3, ys)


def _moe(u2_tiles, logits_t, x1, mod, w1, b1, w2, b2, ln_g, ln_b, seq_len, layer):
    N, D = x1.shape
    BLK = MOE_BLK
    top_e, gates, rank, counts = _topk(logits_t)
    counts = counts[:, 0].astype(jnp.int32)
    padded = (counts + BLK - 1) // BLK * BLK
    pend = jnp.cumsum(padded)
    offs = pend - padded
    n_blocks = -(-(N * TOP_K) // BLK) + N_EXPERTS
    blk_start = jnp.arange(n_blocks, dtype=jnp.int32) * BLK
    blk_e = jnp.sum((pend[None, :] <= blk_start[:, None]).astype(jnp.int32), axis=1)
    blk_e = jnp.minimum(blk_e, N_EXPERTS - 1).astype(jnp.int32)
    n_used = (pend[-1:] // BLK).astype(jnp.int32)
    tail_start = jnp.where(padded > 0, pend - BLK, -1).astype(jnp.int32)
    dest = _dest_rows(offs.astype(jnp.int32), top_e, rank)
    xs = _dispatch(tail_start, n_used, dest, u2_tiles, n_blocks * BLK, D)
    ys = _expert_ffn(blk_e, n_used, xs, w1, b1, w2, b2, layer)
    return _combine(dest, gates.T, x1, mod, ln_g, ln_b, ys, seq_len, layer)


def _split_w_in(w_in):
    AW = N_ATTN_HEADS * HEAD_DIM
    GW = GROUP_WIDTH
    w = w_in.astype(BF16)
    w_nat = jnp.concatenate([w[:, 3 * AW + 4 * HYENA_WIDTH:], w[:, 3 * AW:3 * AW + 4 * HYENA_WIDTH]], axis=1)
    w_qkv = jnp.stack([
        jnp.concatenate([w[:, t * AW + g * GW:t * AW + (g + 1) * GW] for t in range(3)], axis=1)
        for g in range(len(ATTN_GROUPS))])
    return w_nat, w_qkv


def kernel(x, c, w_ada, b_ada, w_in, w_sconv, b_sconv, w_pool, pool_scale, f_w1, f_b1, f_w_inner, f_b_inner, f_w_out, f_freq, hy_bias, p_attn, p_pool, p_hyena, w_out, ln1_g, ln1_b, w_router, b_router, w1, b1, w2, b2, ln2_g, ln2_b):
    B, S, D = x.shape
    depth = w_in.shape[0]
    bias_tables = [jnp.asarray(t) for t in _attn_bias_tables(S)]
    fmat, gmat = _dft_matrices(S, FREQ_TILE)
    zfeat, decay = _filter_features(S)
    gate_cols = 3 * D
    pool_block = gate_cols // HYENA_WIDTH
    bf = lambda t: t.astype(BF16)
    p_attn_b, p_pool_b, p_hyena_b, w_out_b = map(bf, (p_attn, p_pool, p_hyena, w_out))

    for l in range(depth):
        mod = _ada(c, w_ada, b_ada, l).reshape(B, 6, D)
        w_nat, w_qkv = _split_w_in(w_in[l])
        proj, u = _inproj_nat(x, mod, w_nat)
        qkv = _inproj_dil(u, w_qkv)
        ya = _attention(qkv, bias_tables)
        yb = _pool(proj, pool_block, w_pool, pool_scale, l)
        filt = _hyena_filters(zfeat, decay, f_w1, f_b1, f_w_inner, f_b_inner, f_w_out, f_freq, l)
        spectra = _filter_spectra(fmat, filt)
        yc = _hyena(proj, pool_block + 1, fmat, gmat, spectra, w_sconv, b_sconv, hy_bias, l)
        x1, u2, logits_t = _merge(ya, yb, yc, proj, x, mod, p_attn_b, p_pool_b, p_hyena_b, w_out_b,
                                  ln1_g, ln1_b, w_router, b_router, l)
        x = _moe(u2, logits_t, x1, mod, w1, b1, w2, b2, ln2_g, ln2_b, S, l).reshape(B, S, D)
    return x
```

```python
import functools
import math

import jax
import jax.numpy as jnp
import numpy as np
from jax import lax
from jax.experimental import pallas as pl
from jax.experimental.pallas import tpu as pltpu

F32 = jnp.float32
BF16 = jnp.bfloat16

HEAD_DIM = 64
HEADS_PER_GROUP = 4
ATTN_GROUPS = ((128, 1), (512, 4), (2048, 16))
N_ATTN_HEADS = HEADS_PER_GROUP * len(ATTN_GROUPS)
GROUP_WIDTH = HEADS_PER_GROUP * HEAD_DIM
HALF_WINDOW = 64
POOL_WINDOWS = (2, 4, 8, 16)
POOL_GROUP = 128
HYENA_WIDTH = 512
FILTER_EMB = 33
FILTER_HIDDEN = 64
N_EXPERTS = 32
TOP_K = 4
SWIGLU_LIMIT = 7.0
SWIGLU_ALPHA = 1.702
LN_EPS = 1e-5
DEPTH = 2
ALPHA_RES = (2 * DEPTH) ** 0.25
DECAY_TARGET = 1e-2
FAST_DECAY_PCT = 0.3
SLOW_DECAY_PCT = 1.5

QBLK = 128
HYENA_FREQ_TILE = 256
HYENA_TIME_TILE = 512
SPECTRA_TILE = 256
INPROJ_COLS = 896
ATTN_UNROLL = 2
ROW_CHUNK = 256
MERGE_ROWS = 512
TOPK_LANES = 1024
MOE_BLK = 512
FFN_CHUNK = 512
WEIGHT_CAST_ROWS = 128
DMA_PRIORITIES = 2
DMA_ISSUE_UNROLL = 4
MOE_TOK = 256
PAD_ROWS = 8
NEG_BIG = -1e30
LANES = 128
VMEM_LIMIT = 56 * 1024 * 1024


def _cparams(*sem):
    return pltpu.CompilerParams(dimension_semantics=sem, vmem_limit_bytes=VMEM_LIMIT)


def _dot(a, b):
    return jnp.dot(a, b, preferred_element_type=F32)


def _dot3(a, b):
    a_hi = a.astype(BF16)
    b_hi = b.astype(BF16)
    a_lo = (a - a_hi.astype(F32)).astype(BF16)
    b_lo = (b - b_hi.astype(F32)).astype(BF16)
    return _dot(a_hi, b_hi) + _dot(a_hi, b_lo) + _dot(a_lo, b_hi)


def _layer_norm(r, g, b):
    mu = jnp.mean(r, axis=-1, keepdims=True)
    c = r - mu
    var = jnp.mean(c * c, axis=-1, keepdims=True)
    return c * lax.rsqrt(var + LN_EPS) * g + b


def _store_token_tiles(ref, val):
    T, D = val.shape
    n = D // LANES
    for c in range(n):
        ref[pl.ds(c, T, stride=n), :] = val[:, c * LANES:(c + 1) * LANES]


def _load_token_tiles(ref, T, D, plane=None):
    n = D // LANES
    lead = () if plane is None else tuple(plane)
    idx = lambda c: lead + (pl.ds(c, T, stride=n), slice(None))
    return [ref[idx(c)] for c in range(n)]


def _ada_kernel(c_ref, w_ref, b_ref, o_ref):
    c = c_ref[...]
    cond = c * (1.0 / (1.0 + jnp.exp(-c)))
    o_ref[...] = _dot3(cond, w_ref[...]) + b_ref[...]


def _ada(c, w_ada, b_ada, layer):
    B, D = c.shape
    n_out = w_ada.shape[-1]
    tn = 768
    return pl.pallas_call(
        _ada_kernel,
        grid=(n_out // tn,),
        in_specs=[
            pl.BlockSpec((B, D), lambda j: (0, 0)),
            pl.BlockSpec((None, D, tn), lambda j: (layer, 0, j)),
            pl.BlockSpec((None, 1, tn), lambda j: (layer, 0, j)),
        ],
        out_specs=pl.BlockSpec((B, tn), lambda j: (0, j)),
        out_shape=jax.ShapeDtypeStruct((B, n_out), F32),
        compiler_params=_cparams("arbitrary"),
        name="ada",
    )(c, w_ada, b_ada.reshape(b_ada.shape[0], 1, n_out))


def _inproj_nat_kernel(x_ref, mod_ref, w_ref, o_ref, u_ref):
    S = x_ref.shape[0]

    @pl.when(pl.program_id(1) == 0)
    def _():
        shift = mod_ref[0:1, :]
        scale1p = 1.0 + mod_ref[1:2, :]

        def body(i, carry):
            rows = pl.ds(pl.multiple_of(i * ROW_CHUNK, ROW_CHUNK), ROW_CHUNK)
            u_ref[rows, :] = (x_ref[rows, :] * scale1p + shift).astype(BF16)
            return carry

        lax.fori_loop(0, S // ROW_CHUNK, body, 0)

    o_ref[...] = _dot(u_ref[...], w_ref[...]).astype(BF16)


def _inproj_nat(x, mod, w_nat):
    B, S, D = x.shape
    n_out = w_nat.shape[1]
    tn = INPROJ_COLS
    return pl.pallas_call(
        _inproj_nat_kernel,
        grid=(B, n_out // tn),
        in_specs=[
            pl.BlockSpec((None, S, D), lambda b, j: (b, 0, 0)),
            pl.BlockSpec((None, 6, D), lambda b, j: (b, 0, 0)),
            pl.BlockSpec((D, tn), lambda b, j: (0, j)),
        ],
        out_specs=[
            pl.BlockSpec((None, S, tn), lambda b, j: (b, 0, j)),
            pl.BlockSpec((None, S, D), lambda b, j: (b, 0, 0)),
        ],
        out_shape=[
            jax.ShapeDtypeStruct((B, S, n_out), BF16),
            jax.ShapeDtypeStruct((B, S, D), BF16),
        ],
        compiler_params=_cparams("arbitrary", "arbitrary"),
        name="inproj_nat",
    )(x, mod, w_nat)


def _inproj_dil_kernel(u_ref, w_ref, o0_ref, o1_ref, o2_ref, acc_ref, *, dils):
    S = u_ref.shape[0]
    for g, o_ref in enumerate((o0_ref, o1_ref, o2_ref)):
        d = dils[g]
        L = S // d
        acc = _dot(u_ref[...], w_ref[g])
        if d == 1:
            o_ref[0] = acc.astype(BF16)
            continue
        for c in range(acc_ref.shape[0]):
            acc_ref[c] = acc[:, c * LANES:(c + 1) * LANES]
        for rho in range(d):
            for c in range(acc_ref.shape[0]):
                o_ref[rho, :, c * LANES:(c + 1) * LANES] = (
                    acc_ref[c, pl.ds(rho, L, stride=d), :].astype(BF16))


def _inproj_dil(u, w_groups, dils):
    B, S, D = u.shape
    W = w_groups.shape[-1]
    assert len(dils) == w_groups.shape[0] == 3
    return pl.pallas_call(
        functools.partial(_inproj_dil_kernel, dils=tuple(dils)),
        grid=(B,),
        in_specs=[
            pl.BlockSpec((None, S, D), lambda b: (b, 0, 0)),
            pl.BlockSpec((len(dils), D, W), lambda b: (0, 0, 0)),
        ],
        out_specs=[pl.BlockSpec((None, d, S // d, W), lambda b: (b, 0, 0, 0)) for d in dils],
        out_shape=[jax.ShapeDtypeStruct((B, d, S // d, W), BF16) for d in dils],
        scratch_shapes=[pltpu.VMEM((W // LANES, S, LANES), F32)],
        compiler_params=_cparams("arbitrary"),
        name="inproj_dil",
    )(u, w_groups)


def _window_start(i, L, W):
    return min(max(QBLK * i - HALF_WINDOW, 0), L - W)


def _attn_bias_tables(S):
    h = np.arange(1, N_ATTN_HEADS + 1, dtype=np.float32)
    slopes = (2.0 ** (-8.0 * h / N_ATTN_HEADS)).astype(np.float32)
    tables = []
    for g, (_, d) in enumerate(ATTN_GROUPS):
        L = S // d
        W = min(2 * QBLK, L)
        nb = L // QBLK
        variants = sorted({_window_start(i, L, W) - QBLK * i for i in range(nb)}, reverse=True)
        r = np.arange(QBLK)[:, None]
        c = np.arange(W)[None, :]
        tab = np.empty((len(variants), HEADS_PER_GROUP * QBLK, W), np.float32)
        for vi, delta in enumerate(variants):
            off = np.abs(c - r + delta)
            for hh in range(HEADS_PER_GROUP):
                slope = np.float64(slopes[g * HEADS_PER_GROUP + hh])
                bias = np.where(off <= HALF_WINDOW, -slope * (d * off), NEG_BIG)
                tab[vi, hh * QBLK:(hh + 1) * QBLK] = bias.astype(np.float32)
        tables.append(tab)
    return tables


def _block_attention(q, kwin, vwin, bias, head_of_lane):
    zero = jnp.zeros_like(q)
    q4 = jnp.concatenate([jnp.where(head_of_lane == h, q, zero) for h in range(HEADS_PER_GROUP)], axis=0)
    s = lax.dot_general(q4, kwin, (((1,), (1,)), ((), ())), preferred_element_type=F32)
    s = s * (1.0 / math.sqrt(HEAD_DIM)) + bias
    m = jnp.max(s, axis=-1, keepdims=True)
    p = jnp.exp(s - m)
    l = jnp.sum(p, axis=-1, keepdims=True)
    o4 = _dot(p.astype(BF16), vwin) * (1.0 / l)
    lse4 = m + jnp.log(l)
    out = jnp.zeros((QBLK, GROUP_WIDTH), F32)
    lse = jnp.zeros((QBLK, GROUP_WIDTH), F32)
    for h in range(HEADS_PER_GROUP):
        rows = slice(h * QBLK, (h + 1) * QBLK)
        sel = head_of_lane == h
        out = jnp.where(sel, o4[rows], out)
        lse = jnp.where(sel, lse4[rows], lse)
    return out, lse


def _attn_kernel(q0_ref, q1_ref, q2_ref, b0_ref, b1_ref, b2_ref, o_ref, acc_ref, lse_ref):
    S = o_ref.shape[0]
    GW = GROUP_WIDTH
    head_of_lane = lax.broadcasted_iota(jnp.int32, (QBLK, GW), 1) // HEAD_DIM
    n_planes = GW // LANES

    def put(g, rows, out, lse):
        for c in range(n_planes):
            acc_ref[g, c, rows, :] = out[:, c * LANES:(c + 1) * LANES]
            lse_ref[g, c, rows, :] = lse[:, c * LANES:(c + 1) * LANES]

    L0 = S
    nb0 = L0 // QBLK

    def g0_body(i, carry):
        ws = pl.multiple_of(jnp.clip(QBLK * i - HALF_WINDOW, 0, L0 - 2 * QBLK), HALF_WINDOW)
        var = jnp.where(i == 0, 0, jnp.where(i == nb0 - 1, 2, 1))
        rows = pl.ds(pl.multiple_of(i * QBLK, QBLK), QBLK)
        out, lse = _block_attention(
            q0_ref[0, rows, 0:GW],
            q0_ref[0, pl.ds(ws, 2 * QBLK), GW:2 * GW],
            q0_ref[0, pl.ds(ws, 2 * QBLK), 2 * GW:3 * GW],
            b0_ref[var], head_of_lane)
        put(0, rows, out, lse)
        return carry

    lax.fori_loop(0, nb0, g0_body, 0, unroll=ATTN_UNROLL)

    for g, (q_ref, b_ref) in ((1, (q1_ref, b1_ref)), (2, (q2_ref, b2_ref))):
        d = ATTN_GROUPS[g][1]
        L = S // d
        W = min(2 * QBLK, L)
        nb = L // QBLK
        deltas = sorted({_window_start(i, L, W) - QBLK * i for i in range(nb)}, reverse=True)

        def gd_body(rho, carry, q_ref=q_ref, b_ref=b_ref, d=d, L=L, W=W, nb=nb, deltas=deltas, g=g):
            for i in range(nb):
                ws = _window_start(i, L, W)
                var = deltas.index(ws - QBLK * i)
                out, lse = _block_attention(
                    q_ref[rho, i * QBLK:(i + 1) * QBLK, 0:GW],
                    q_ref[rho, ws:ws + W, GW:2 * GW],
                    q_ref[rho, ws:ws + W, 2 * GW:3 * GW],
                    b_ref[var], head_of_lane)
                dst = pl.ds(rho + i * QBLK * d, QBLK, stride=d)
                put(g, dst, out, lse)
            return carry

        lax.fori_loop(0, d, gd_body, 0, unroll=max(1, ATTN_UNROLL // nb))

    def merge_body(i, carry):
        rows = pl.ds(pl.multiple_of(i * ROW_CHUNK, ROW_CHUNK), ROW_CHUNK)
        for c in range(n_planes):
            l0, l1, l2 = lse_ref[0, c, rows, :], lse_ref[1, c, rows, :], lse_ref[2, c, rows, :]
            mx = jnp.maximum(jnp.maximum(l0, l1), l2)
            w0, w1, w2 = jnp.exp(l0 - mx), jnp.exp(l1 - mx), jnp.exp(l2 - mx)
            num = w0 * acc_ref[0, c, rows, :] + w1 * acc_ref[1, c, rows, :] + w2 * acc_ref[2, c, rows, :]
            o_ref[rows, c * LANES:(c + 1) * LANES] = (num * (1.0 / (w0 + w1 + w2))).astype(BF16)
        return carry

    lax.fori_loop(0, S // ROW_CHUNK, merge_body, 0)


def _attention(qkv, bias_tables):
    q0, q1, q2 = qkv
    B, _, S, W3 = q0.shape
    GW = GROUP_WIDTH
    in_specs = [pl.BlockSpec((None,) + q.shape[1:], lambda b: (b, 0, 0, 0)) for q in qkv]
    in_specs += [pl.BlockSpec(t.shape, lambda b: (0, 0, 0)) for t in bias_tables]
    return pl.pallas_call(
        _attn_kernel,
        grid=(B,),
        in_specs=in_specs,
        out_specs=pl.BlockSpec((None, S, GW), lambda b: (b, 0, 0)),
        out_shape=jax.ShapeDtypeStruct((B, S, GW), BF16),
        scratch_shapes=[pltpu.VMEM((3, GW // LANES, S, LANES), F32)] * 2,
        compiler_params=_cparams("arbitrary"),
        name="attention",
    )(q0, q1, q2, *bias_tables)


def _pool_kernel(y_ref, w_ref, sc_ref, o_ref, pad_ref):
    S = y_ref.shape[0]
    C = y_ref.shape[1]
    P = PAD_ROWS
    pad_ref[0:P, :] = jnp.zeros((P, C), F32)
    pad_ref[S + P:S + 2 * P, :] = jnp.zeros((P, C), F32)
    pad_ref[P:S + P, :] = y_ref[...].astype(F32)
    for c in range(S // ROW_CHUNK):
        r0 = c * ROW_CHUNK
        t = r0 + lax.broadcasted_iota(jnp.int32, (ROW_CHUNK, 1), 0)
        for g, w in enumerate(POOL_WINDOWS):
            cols = slice(g * POOL_GROUP, (g + 1) * POOL_GROUP)
            acc = pad_ref[P + r0 - w // 2:P + r0 - w // 2 + ROW_CHUNK, cols]
            for j in range(-w // 2 + 1, w // 2):
                acc = acc + pad_ref[P + r0 + j:P + r0 + j + ROW_CHUNK, cols]
            cnt = (jnp.minimum(t + w // 2, S) - jnp.maximum(t - w // 2, 0)).astype(F32)
            pooled = acc / cnt - pad_ref[P + r0:P + r0 + ROW_CHUNK, cols]
            mixed = _dot(pooled.astype(BF16), w_ref[g].astype(BF16)) * sc_ref[:, cols]
            o_ref[r0:r0 + ROW_CHUNK, cols] = mixed.astype(BF16)


def _pool(proj, col_block, w_pool, pool_scale, layer):
    B, S, _ = proj.shape
    C = POOL_GROUP * len(POOL_WINDOWS)
    return pl.pallas_call(
        _pool_kernel,
        grid=(B,),
        in_specs=[
            pl.BlockSpec((None, S, C), lambda b: (b, 0, col_block)),
            pl.BlockSpec((None,) + w_pool.shape[1:], lambda b: (layer, 0, 0, 0)),
            pl.BlockSpec((None, 1, C), lambda b: (layer, 0, 0)),
        ],
        out_specs=pl.BlockSpec((None, S, C), lambda b: (b, 0, 0)),
        out_shape=jax.ShapeDtypeStruct((B, S, C), BF16),
        scratch_shapes=[pltpu.VMEM((S + 2 * PAD_ROWS, C), F32)],
        compiler_params=_cparams("arbitrary"),
        name="pool",
    )(proj, w_pool, pool_scale.reshape(pool_scale.shape[0], 1, C))


def _filter_kernel(z_ref, w1_ref, b1_ref, wi_ref, bi_ref, wo_ref, fr_ref, dec_ref, o_ref):
    S = z_ref.shape[0]
    C = HYENA_WIDTH
    freq = fr_ref[...]
    h = jnp.sin(freq * (_dot3(z_ref[...], w1_ref[...]) + b1_ref[...]))
    for i in range(wi_ref.shape[0]):
        h = jnp.sin(freq * (_dot3(h, wi_ref[i]) + bi_ref[i]))
    not_first = lax.broadcasted_iota(jnp.int32, (S, 1), 0) > 0
    for blk in range(wo_ref.shape[1] // C):
        cols = slice(blk * C, (blk + 1) * C)
        f = _dot3(h, wo_ref[:, cols]) * dec_ref[...]
        if blk % 2 == 1:
            f = jnp.where(not_first, f, 0.0)
        o_ref[:, cols] = f.astype(BF16)


def _hyena_filters(zfeat, decay, f_w1, f_b1, f_w_inner, f_b_inner, f_w_out, f_freq, layer):
    S, E = zfeat.shape
    H = FILTER_HIDDEN
    n_out = f_w_out.shape[-1]
    n_inner = f_w_inner.shape[1]
    w1 = jnp.zeros((f_w1.shape[0], E, H), F32).at[:, :f_w1.shape[1]].set(f_w1)
    full = lambda shape: pl.BlockSpec(shape, lambda i: (0,) * len(shape))
    lay = lambda shape: pl.BlockSpec((None,) + shape, lambda i: (layer,) + (0,) * len(shape))
    return pl.pallas_call(
        _filter_kernel,
        grid=(1,),
        in_specs=[
            full((S, E)), lay((E, H)), lay((1, H)), lay((n_inner, H, H)), lay((n_inner, 1, H)),
            lay((H, n_out)), lay((1, H)), full((S, HYENA_WIDTH)),
        ],
        out_specs=full((S, n_out)),
        out_shape=jax.ShapeDtypeStruct((S, n_out), BF16),
        compiler_params=_cparams("arbitrary"),
        name="hyena_filter",
    )(zfeat, w1, f_b1.reshape(-1, 1, H), f_w_inner, f_b_inner.reshape(-1, n_inner, 1, H), f_w_out,
      f_freq.reshape(-1, 1, H), decay)


def _spec_kernel(fa_ref, fb_ref, h_ref, pre_ref, pim_ref, q1re_ref, q1im_ref, q2re_ref, q2im_ref, sp0_ref):
    TK = fa_ref.shape[0] // 2
    C = HYENA_WIDTH
    n_fft = 2 * h_ref.shape[0]
    xa = _dot(fa_ref[...], h_ref[...])
    xb = _dot(fb_ref[...], h_ref[...])
    row = pl.program_id(0) * TK + lax.broadcasted_iota(jnp.int32, (TK, 1), 0)
    theta = row.astype(F32) * (2.0 * math.pi / n_fft)
    cj, sj = jnp.cos(theta), jnp.sin(theta)
    sc = 2.0 / n_fft
    for o in range(pre_ref.shape[0]):
        hf = slice(2 * o * C, (2 * o + 1) * C)
        hb = slice((2 * o + 1) * C, (2 * o + 2) * C)
        ha_re = xa[:TK, hf] + xa[:TK, hb]
        ha_im = xa[TK:, hf] - xa[TK:, hb]
        hm_re = xb[:TK, hf] + xb[:TK, hb]
        hm_im = xb[TK:, hf] - xb[TK:, hb]
        p_re, p_im = (ha_re + hm_re) * sc, (ha_im - hm_im) * sc
        q_re, q_im = (ha_re - hm_re) * sc, (ha_im + hm_im) * sc
        pre_ref[o] = p_re
        pim_ref[o] = p_im
        q1re_ref[o] = q_re * cj + q_im * sj
        q1im_ref[o] = q_im * cj - q_re * sj
        q2re_ref[o] = q_re * cj - q_im * sj
        q2im_ref[o] = q_im * cj + q_re * sj

        @pl.when(pl.program_id(0) == 0)
        def _():
            h0, hs = ha_re[0:1], hm_re[0:1]
            hh_re = xa[TK:TK + 1, hf] + xa[TK:TK + 1, hb]
            hh_im = xb[TK:TK + 1, hf] - xb[TK:TK + 1, hb]
            sp0_ref[o] = jnp.concatenate(
                [(h0 + hs) * (0.5 * sc), (h0 - hs) * (0.5 * sc), hh_re * sc, hh_im * sc,
                 jnp.zeros((4, C), F32)], axis=0)


def _filter_spectra(fa, fb, filt):
    nkt, tk2, S = fa.shape
    TK = tk2 // 2
    C = HYENA_WIDTH
    n_ord = filt.shape[1] // (2 * C)
    out = jax.ShapeDtypeStruct((n_ord, nkt * TK, C), F32)
    ospec = pl.BlockSpec((n_ord, TK, C), lambda k: (0, k, 0))
    fspec = pl.BlockSpec((None, tk2, S), lambda k: (k, 0, 0))
    return pl.pallas_call(
        _spec_kernel,
        grid=(nkt,),
        in_specs=[fspec, fspec, pl.BlockSpec(filt.shape, lambda k: (0, 0))],
        out_specs=[ospec] * 6 + [pl.BlockSpec((n_ord, 8, C), lambda k: (0, 0, 0))],
        out_shape=[out] * 6 + [jax.ShapeDtypeStruct((n_ord, 8, C), F32)],
        compiler_params=_cparams("arbitrary"),
        name="hyena_spectra",
    )(fa, fb, filt)


def _hyena_kernel(v_ref, x1_ref, x2_ref, f_ref, g_ref, pre_ref, pim_ref, q1re_ref, q1im_ref, q2re_ref, q2im_ref,
                  sp0_ref, wsc_ref, bsc_ref, hb_ref, o_ref, pad_ref, z_ref, zb_ref, gate_ref, zcat_ref, il_ref):
    H, C2 = z_ref.shape
    C = C2 // 2
    TK = f_ref.shape[0] // 2
    TT = g_ref.shape[0]
    nf = H // TK
    P = PAD_ROWS
    o = pl.program_id(1)
    s = pl.program_id(2)
    even, odd = slice(0, C), slice(C, C2)

    def short_conv(in_ref, blk, consume):
        cols = slice(blk * C, (blk + 1) * C)
        pad_ref[0, 0:P, :] = jnp.zeros((P, C), F32)
        pad_ref[1, 0:P, :] = jnp.zeros((P, C), F32)
        pad_ref[0, H + P:H + 2 * P, :] = jnp.zeros((P, C), F32)
        pad_ref[1, H + P:H + 2 * P, :] = jnp.zeros((P, C), F32)
        pad_ref[0, P:H + P, :] = in_ref[0].astype(F32)
        pad_ref[1, P:H + P, :] = in_ref[1].astype(F32)
        w0, w1, w2 = wsc_ref[0:1, cols], wsc_ref[1:2, cols], wsc_ref[2:3, cols]
        bias = bsc_ref[:, cols]
        for c in range(H // ROW_CHUNK):
            r0 = P + c * ROW_CHUNK
            ev = pad_ref[0, r0:r0 + ROW_CHUNK, :]
            od = pad_ref[1, r0:r0 + ROW_CHUNK, :]
            od_prev = pad_ref[1, r0 - 1:r0 - 1 + ROW_CHUNK, :]
            ev_next = pad_ref[0, r0 + 1:r0 + 1 + ROW_CHUNK, :]
            consume(c * ROW_CHUNK, w0 * od_prev + w1 * ev + w2 * od + bias, w0 * ev + w1 * od + w2 * ev_next + bias)

    def store_gate(r0, conv_e, conv_o):
        gate_ref[r0:r0 + ROW_CHUNK, even] = conv_e
        gate_ref[r0:r0 + ROW_CHUNK, odd] = conv_o

    @pl.when((o == 0) & (s == 0))
    def _():
        def store_z(r0, conv_e, conv_o):
            rows = slice(r0, r0 + ROW_CHUNK)
            z_ref[rows, even] = conv_e
            z_ref[rows, odd] = conv_o
            zb_ref[rows, even] = conv_e.astype(BF16)
            zb_ref[rows, odd] = conv_o.astype(BF16)
        short_conv(v_ref, 0, store_z)

    @pl.when(s < nf)
    def _():
        x = _dot(f_ref[...], zb_ref[...])
        e_re, o_re, e_im, o_im = x[:TK, even], x[:TK, odd], x[TK:, even], x[TK:, odd]
        p_re, p_im = pre_ref[...], pim_ref[...]
        q1_re, q1_im, q2_re, q2_im = q1re_ref[...], q1im_ref[...], q2re_ref[...], q2im_ref[...]
        ze_re = e_re * p_re - e_im * p_im + o_re * q1_re - o_im * q1_im
        ze_im = e_re * p_im + e_im * p_re + o_re * q1_im + o_im * q1_re
        zo_re = e_re * q2_re - e_im * q2_im + o_re * p_re - o_im * p_im
        zo_im = e_re * q2_im + e_im * q2_re + o_re * p_im + o_im * p_re
        first = (s * TK + lax.broadcasted_iota(jnp.int32, (TK, 1), 0)) == 0
        a, b, c, d = sp0_ref[0:1, :], sp0_ref[1:2, :], sp0_ref[2:3, :], sp0_ref[3:4, :]
        ze_re = jnp.where(first, e_re[0:1] * a + o_re[0:1] * b, ze_re)
        zo_re = jnp.where(first, e_re[0:1] * b + o_re[0:1] * a, zo_re)
        ze_im = jnp.where(first, e_im[0:1] * c + o_im[0:1] * d, ze_im)
        zo_im = jnp.where(first, o_im[0:1] * c - e_im[0:1] * d, zo_im)
        base = pl.multiple_of(s * 2 * TK, 2 * TK)
        zcat_ref[pl.ds(base, TK), even] = ze_re.astype(BF16)
        zcat_ref[pl.ds(base, TK), odd] = zo_re.astype(BF16)
        zcat_ref[pl.ds(base + TK, TK), even] = ze_im.astype(BF16)
        zcat_ref[pl.ds(base + TK, TK), odd] = zo_im.astype(BF16)

    @pl.when((s == nf) & (o == 0))
    def _():
        short_conv(x1_ref, 1, store_gate)

    @pl.when((s == nf) & (o == 1))
    def _():
        short_conv(x2_ref, 2, store_gate)

    @pl.when(s >= nf)
    def _():
        tt = s - nf
        rows = pl.ds(pl.multiple_of(tt * TT, TT), TT)
        y = _dot(g_ref[...], zcat_ref[...])
        hrow = jnp.where(o == 0, hb_ref[0:1, :], hb_ref[1:2, :])
        hbias = jnp.concatenate([hrow, hrow], axis=1)
        znew = gate_ref[rows, :] * (y + z_ref[rows, :] * hbias)
        z_ref[rows, :] = znew
        zb_ref[rows, :] = znew.astype(BF16)

        @pl.when(o == pl.num_programs(1) - 1)
        def _():
            for p in range(C // LANES):
                il_ref[p, pl.ds(0, TT, stride=2), :] = znew[:, p * LANES:(p + 1) * LANES]
                il_ref[p, pl.ds(1, TT, stride=2), :] = znew[:, C + p * LANES:C + (p + 1) * LANES]
            out_rows = pl.ds(pl.multiple_of(tt * 2 * TT, 2 * TT), 2 * TT)
            for p in range(C // LANES):
                o_ref[out_rows, p * LANES:(p + 1) * LANES] = il_ref[p].astype(BF16)


def _hyena(hv, hx1, hx2, fhalf, ghalf, coeffs, w_sconv, b_sconv, hy_bias, layer):
    B, _, H, C = hv.shape
    S = 2 * H
    nf, tk2, _ = fhalf.shape
    TK = tk2 // 2
    TT = HYENA_TIME_TILE
    ni = H // TT
    sp0 = coeffs[-1]
    n_ord = sp0.shape[0]
    fwd = lambda s: jnp.minimum(s, nf - 1)
    inv = lambda s: jnp.clip(s - nf, 0, ni - 1)
    inp = pl.BlockSpec((None, 2, H, C), lambda bb, o, s: (bb, 0, 0, 0), pipeline_mode=pl.Buffered(1))
    coef = pl.BlockSpec((None, TK, C), lambda bb, o, s: (o, fwd(s), 0))
    return pl.pallas_call(
        _hyena_kernel,
        grid=(B, n_ord, nf + ni),
        in_specs=[
            inp, inp, inp,
            pl.BlockSpec((None, tk2, H), lambda bb, o, s: (fwd(s), 0, 0)),
            pl.BlockSpec((TT, 2 * H), lambda bb, o, s: (inv(s), 0)),
            coef, coef, coef, coef, coef, coef,
            pl.BlockSpec((None, 8, C), lambda bb, o, s: (o, 0, 0)),
            pl.BlockSpec((None, 3, 3 * C), lambda bb, o, s: (layer, 0, 0)),
            pl.BlockSpec((None, 1, 3 * C), lambda bb, o, s: (layer, 0, 0)),
            pl.BlockSpec((None, n_ord, C), lambda bb, o, s: (layer, 0, 0)),
        ],
        out_specs=pl.BlockSpec((None, S, C), lambda bb, o, s: (bb, 0, 0)),
        out_shape=jax.ShapeDtypeStruct((B, S, C), BF16),
        scratch_shapes=[
            pltpu.VMEM((2, H + 2 * PAD_ROWS, C), F32),
            pltpu.VMEM((H, 2 * C), F32),
            pltpu.VMEM((H, 2 * C), BF16),
            pltpu.VMEM((H, 2 * C), F32),
            pltpu.VMEM((2 * H, 2 * C), BF16),
            pltpu.VMEM((C // LANES, 2 * TT, LANES), F32),
        ],
        compiler_params=_cparams("arbitrary", "arbitrary", "arbitrary"),
        name="hyena_conv",
    )(hv, hx1, hx2, fhalf, ghalf, *coeffs, w_sconv, b_sconv.reshape(b_sconv.shape[0], 1, 3 * C), hy_bias)


def _packed_dft(bins, n_samples, n_fft, tile, sin_row0):
    k = bins[:, None]
    n = lax.broadcasted_iota(jnp.int32, (bins.shape[0], n_samples), 1)
    theta = ((k * n) & (n_fft - 1)).astype(F32) * (2.0 * math.pi / n_fft)
    first = lax.broadcasted_iota(jnp.int32, theta.shape, 0) == 0
    fc = jnp.cos(theta)
    fs = jnp.where(first, sin_row0[None, :], -jnp.sin(theta))
    nt = bins.shape[0] // tile
    return jnp.concatenate([fc.reshape(nt, tile, n_samples), fs.reshape(nt, tile, n_samples)], axis=1).astype(BF16)


def _dft_matrices(S):
    H, N = S // 2, 2 * S
    j = jnp.arange(H, dtype=jnp.int32)
    m = jnp.arange(H, dtype=jnp.int32)
    n = jnp.arange(S, dtype=jnp.int32)
    quarter = jnp.asarray([1.0, 0.0, -1.0, 0.0], F32)
    fhalf = _packed_dft(j, H, S, HYENA_FREQ_TILE, (1 - 2 * (m & 1)).astype(F32))
    ghalf = fhalf.reshape(2 * H, H).T
    fa = _packed_dft(j, S, N, SPECTRA_TILE, quarter[n & 3])
    fb = _packed_dft(S - j, S, N, SPECTRA_TILE, -quarter[(n + 3) & 3])
    return fhalf, ghalf, fa, fb


def _filter_features(S):
    t = jnp.linspace(0.0, 1.0, S, dtype=F32)[:, None]
    bands = (FILTER_EMB - 1) // 2
    w = 2.0 * math.pi * jnp.arange(S, dtype=F32) / S
    f = jnp.linspace(1e-4, bands - 1, bands, dtype=F32)
    ang = w[:, None] * f[None, :]
    z = jnp.concatenate([t, jnp.cos(ang), -jnp.sin(ang)], axis=-1)
    z = jnp.pad(z, ((0, 0), (0, 128 - FILTER_EMB)))
    min_decay = math.log(DECAY_TARGET) / SLOW_DECAY_PCT
    max_decay = math.log(DECAY_TARGET) / FAST_DECAY_PCT
    deltas = jnp.linspace(min_decay, max_decay, HYENA_WIDTH, dtype=F32)
    decay = jnp.exp(-t * jnp.abs(deltas)[None, :])
    return z, decay


def _merge_kernel(ya_ref, yb_ref, yc_ref, gl_ref, x_ref, mod_ref, pa_ref, pp_ref, ph_ref, wo_ref,
                  g_ref, b_ref, wr_ref, br_ref, x1_ref, u2_ref, lt_ref):
    D = x_ref.shape[1]
    sig = lambda v: 1.0 / (1.0 + jnp.exp(-v))
    merged = sig(gl_ref[:, 0:D].astype(F32)) * _dot(ya_ref[...], pa_ref[...])
    merged += sig(gl_ref[:, D:2 * D].astype(F32)) * _dot(yb_ref[...], pp_ref[...])
    merged += sig(gl_ref[:, 2 * D:3 * D].astype(F32)) * _dot(yc_ref[...], ph_ref[...])
    h = _dot(merged.astype(BF16), wo_ref[...])
    gate1, shift2, scale2 = mod_ref[2:3, :], mod_ref[3:4, :], mod_ref[4:5, :]
    x1 = _layer_norm(ALPHA_RES * x_ref[...] + (1.0 + gate1) * h, g_ref[...], b_ref[...])
    x1_ref[...] = x1
    u2 = x1 * (1.0 + scale2) + shift2
    _store_token_tiles(u2_ref, u2)
    logits = _dot3(u2, wr_ref[...]) + br_ref[...]
    lt_ref[...] = logits.T[0:N_EXPERTS, :]


def _merge(ya, yb, yc, proj, x, mod, p_attn, p_pool, p_hyena, w_out, ln_g, ln_b, w_router, b_router, layer):
    B, S, D = x.shape
    N = B * S
    TM = MERGE_ROWS
    tiles_per_seq = S // TM
    EP = 128
    flat = lambda t: t.reshape(N, t.shape[-1])
    wr = jnp.zeros((D, EP), F32).at[:, :N_EXPERTS].set(w_router[layer])
    br = jnp.zeros((1, EP), F32).at[:, :N_EXPERTS].set(b_router[layer][None])
    tok = lambda w: pl.BlockSpec((TM, w), lambda i: (i, 0))
    lay = lambda t: pl.BlockSpec((None,) + t.shape[1:], lambda i: (layer,) + (0,) * (t.ndim - 1))
    ln_g3, ln_b3 = ln_g.reshape(-1, 1, D), ln_b.reshape(-1, 1, D)
    return pl.pallas_call(
        _merge_kernel,
        grid=(N // TM,),
        in_specs=[
            tok(ya.shape[-1]), tok(yb.shape[-1]), tok(yc.shape[-1]), tok(3 * D), tok(D),
            pl.BlockSpec((None, 6, D), lambda i: (i // tiles_per_seq, 0, 0)),
            lay(p_attn), lay(p_pool), lay(p_hyena), lay(w_out), lay(ln_g3), lay(ln_b3),
            pl.BlockSpec((D, EP), lambda i: (0, 0)),
            pl.BlockSpec((1, EP), lambda i: (0, 0)),
        ],
        out_specs=[tok(D), pl.BlockSpec((TM * D // LANES, LANES), lambda i: (i, 0)),
                   pl.BlockSpec((N_EXPERTS, TM), lambda i: (0, i))],
        out_shape=[
            jax.ShapeDtypeStruct((N, D), F32),
            jax.ShapeDtypeStruct((N * D // LANES, LANES), F32),
            jax.ShapeDtypeStruct((N_EXPERTS, N), F32),
        ],
        compiler_params=_cparams("arbitrary"),
        name="merge_ln1_router",
    )(flat(ya), flat(yb), flat(yc), flat(proj), flat(x), mod, p_attn, p_pool, p_hyena, w_out,
      ln_g3, ln_b3, wr, br)


def _topk_kernel(l_ref, e_ref, g_ref, r_ref, cnt_ref, carry_ref):
    E, TL = l_ref.shape

    @pl.when(pl.program_id(0) == 0)
    def _():
        carry_ref[...] = jnp.zeros_like(carry_ref)

    l = l_ref[...]
    eio = lax.broadcasted_iota(jnp.int32, (E, TL), 0).astype(F32)
    vals, hots = [], []
    for k in range(TOP_K):
        m = jnp.max(l, axis=0, keepdims=True)
        idx = jnp.min(jnp.where(l == m, eio, float(E)), axis=0, keepdims=True)
        hot = eio == idx
        l = jnp.where(hot, -jnp.inf, l)
        vals.append(m)
        hots.append(hot)
        e_ref[k:k + 1, :] = idx.astype(jnp.int32)
    exps = [jnp.exp(v - vals[0]) for v in vals]
    inv = 1.0 / (exps[0] + exps[1] + exps[2] + exps[3])
    for k in range(TOP_K):
        g_ref[k:k + 1, :] = exps[k] * inv

    memb = jnp.zeros((E, TL), F32)
    for hot in hots:
        memb = memb + hot.astype(F32)
    upper = (lax.broadcasted_iota(jnp.int32, (TL, TL), 0)
             <= lax.broadcasted_iota(jnp.int32, (TL, TL), 1)).astype(BF16)
    incl = _dot(memb.astype(BF16), upper)
    excl = incl - memb + carry_ref[:, 0:1]
    for k in range(TOP_K):
        rank = jnp.sum(jnp.where(hots[k], excl, 0.0), axis=0, keepdims=True)
        r_ref[k:k + 1, :] = rank.astype(jnp.int32)
    carry_ref[...] = carry_ref[...] + jnp.sum(memb, axis=1, keepdims=True)
    cnt_ref[...] = carry_ref[...]


def _topk(logits_t):
    E, N = logits_t.shape
    TL = TOPK_LANES
    tok = pl.BlockSpec((TOP_K, TL), lambda i: (0, i))
    return pl.pallas_call(
        _topk_kernel,
        grid=(N // TL,),
        in_specs=[pl.BlockSpec((E, TL), lambda i: (0, i))],
        out_specs=[tok, tok, tok, pl.BlockSpec((E, 128), lambda i: (0, 0))],
        out_shape=[
            jax.ShapeDtypeStruct((TOP_K, N), jnp.int32),
            jax.ShapeDtypeStruct((TOP_K, N), F32),
            jax.ShapeDtypeStruct((TOP_K, N), jnp.int32),
            jax.ShapeDtypeStruct((E, 128), F32),
        ],
        scratch_shapes=[pltpu.VMEM((E, 128), F32)],
        compiler_params=_cparams("arbitrary"),
        name="topk_rank",
    )(logits_t)


def _dest_kernel(offs_ref, e_ref, r_ref, d_ref):
    e = e_ref[...]
    acc = r_ref[...]
    for ex in range(N_EXPERTS):
        acc = acc + jnp.where(e == ex, offs_ref[ex], 0)
    d_ref[...] = acc


def _dest_rows(offs, top_e, rank):
    K, N = top_e.shape
    TL = TOPK_LANES
    tok = pl.BlockSpec((K, TL), lambda i, offs: (0, i))
    return pl.pallas_call(
        _dest_kernel,
        grid_spec=pltpu.PrefetchScalarGridSpec(
            num_scalar_prefetch=1, grid=(N // TL,), in_specs=[tok, tok], out_specs=tok),
        out_shape=jax.ShapeDtypeStruct((K, N), jnp.int32),
        compiler_params=_cparams("arbitrary"),
        name="dest_rows",
    )(offs, top_e, rank)


def _dispatch_kernel(tail_ref, n_used_ref, dest_ref, u_ref, xs_ref, zero_ref, sem):
    TT = dest_ref.shape[1]
    RT = u_ref.shape[0] // TT
    blk_rows = zero_ref.shape[0]
    n_blocks = xs_ref.shape[0] // blk_rows

    def zero_copy(first_row):
        start = pl.multiple_of(first_row * RT, RT)
        return pltpu.make_async_copy(zero_ref, xs_ref.at[pl.ds(start, blk_rows), :], sem)

    @pl.when(pl.program_id(0) == 0)
    def _():
        zero_ref[...] = jnp.zeros_like(zero_ref)

        def tails(fn):
            def body(e, carry):
                @pl.when(tail_ref[e] >= 0)
                def _():
                    fn(zero_copy(jnp.maximum(tail_ref[e], 0)))
                return carry
            lax.fori_loop(0, N_EXPERTS, body, 0)

        def unused(fn):
            def body(i, carry):
                fn(zero_copy(i * (blk_rows // RT)))
                return carry
            lax.fori_loop(n_used_ref[0], n_blocks, body, 0)

        tails(lambda cp: cp.start())
        unused(lambda cp: cp.start())
        tails(lambda cp: cp.wait())
        unused(lambda cp: cp.wait())

    def row_copy(j, k):
        src = pl.multiple_of(j * RT, RT)
        dst = pl.multiple_of(dest_ref[k, j] * RT, RT)
        return pltpu.make_async_copy(u_ref.at[pl.ds(src, RT), :], xs_ref.at[pl.ds(dst, RT), :], sem)

    def start(j, carry):
        for k in range(TOP_K):
            row_copy(j, k).start(priority=k % DMA_PRIORITIES)
        return carry

    lax.fori_loop(0, TT, start, 0, unroll=DMA_ISSUE_UNROLL)
    for k in range(TOP_K):
        pltpu.make_async_copy(u_ref, xs_ref.at[pl.ds(0, TT * RT), :], sem).wait()


def _dispatch(tail_start, n_used, dest, u2_tiles, n_rows, D):
    RT = D // LANES
    N = u2_tiles.shape[0] // RT
    TT = MOE_TOK
    return pl.pallas_call(
        _dispatch_kernel,
        grid_spec=pltpu.PrefetchScalarGridSpec(
            num_scalar_prefetch=2,
            grid=(N // TT,),
            in_specs=[
                pl.BlockSpec((TOP_K, TT), lambda i, tail, nu: (0, i), memory_space=pltpu.SMEM),
                pl.BlockSpec((TT * RT, LANES), lambda i, tail, nu: (i, 0)),
            ],
            out_specs=pl.BlockSpec(memory_space=pl.ANY),
            scratch_shapes=[pltpu.VMEM((MOE_BLK * RT, LANES), F32), pltpu.SemaphoreType.DMA(())],
        ),
        out_shape=jax.ShapeDtypeStruct((n_rows * RT, LANES), F32),
        compiler_params=_cparams("arbitrary"),
        name="moe_dispatch",
    )(tail_start, n_used, dest, u2_tiles)


def _ffn_kernel(blk_e_ref, n_used_ref, xs_ref, w1_ref, b1_ref, w2_ref, b2_ref, ys_ref,
                w1b_ref, w2b_ref, act_ref):
    F, D = w2_ref.shape
    BLK = xs_ref.shape[0] * LANES // D
    i = pl.program_id(0)
    used = i < n_used_ref[0]
    new_expert = (i == 0) | (blk_e_ref[i] != blk_e_ref[jnp.maximum(i - 1, 0)])

    @pl.when(used & new_expert)
    def _():
        for r in range(0, D, WEIGHT_CAST_ROWS):
            w1b_ref[r:r + WEIGHT_CAST_ROWS, :] = w1_ref[r:r + WEIGHT_CAST_ROWS, :].astype(BF16)
        for r in range(0, F, WEIGHT_CAST_ROWS):
            w2b_ref[r:r + WEIGHT_CAST_ROWS, :] = w2_ref[r:r + WEIGHT_CAST_ROWS, :].astype(BF16)

    @pl.when(used)
    def _():
        x = jnp.concatenate(_load_token_tiles(xs_ref, BLK, D), axis=1).astype(BF16)
        for c in range(0, F, FFN_CHUNK):
            cg = slice(c, c + FFN_CHUNK)
            cl = slice(F + c, F + c + FFN_CHUNK)
            glu = jnp.minimum(_dot(x, w1b_ref[:, cg]) + b1_ref[:, cg], SWIGLU_LIMIT)
            lin = jnp.clip(_dot(x, w1b_ref[:, cl]) + b1_ref[:, cl], -SWIGLU_LIMIT, SWIGLU_LIMIT)
            act = glu * (1.0 / (1.0 + jnp.exp(-SWIGLU_ALPHA * glu))) * (lin + 1.0)
            act_ref[:, cg] = act.astype(BF16)
        _store_token_tiles(ys_ref, _dot(act_ref[...], w2b_ref[...]) + b2_ref[...])

    @pl.when(jnp.logical_not(used))
    def _():
        ys_ref[...] = jnp.zeros_like(ys_ref)


def _expert_ffn(blk_e, n_used, xs_tiles, w1, b1, w2, b2, layer):
    _, E, D, F2 = w1.shape
    F = F2 // 2
    RT = D // LANES
    BLK = MOE_BLK
    rows = pl.BlockSpec((BLK * RT, LANES), lambda i, be, nu: (i, 0))
    per_expert = lambda r, c: pl.BlockSpec((None, None, r, c), lambda i, be, nu: (layer, be[i], 0, 0))
    return pl.pallas_call(
        _ffn_kernel,
        grid_spec=pltpu.PrefetchScalarGridSpec(
            num_scalar_prefetch=2,
            grid=(xs_tiles.shape[0] // (BLK * RT),),
            in_specs=[rows, per_expert(D, F2), per_expert(1, F2), per_expert(F, D), per_expert(1, D)],
            out_specs=rows,
            scratch_shapes=[pltpu.VMEM((D, F2), BF16), pltpu.VMEM((F, D), BF16), pltpu.VMEM((BLK, F), BF16)],
        ),
        out_shape=jax.ShapeDtypeStruct(xs_tiles.shape, F32),
        compiler_params=_cparams("arbitrary"),
        name="moe_ffn",
    )(blk_e, n_used, xs_tiles, w1, b1.reshape(-1, E, 1, F2), w2, b2.reshape(-1, E, 1, D))


def _combine_kernel(dest_ref, dest_next_ref, gt_ref, x1_ref, mod_ref, g_ref, b_ref, ys_ref, o_ref, buf_ref, sem):
    TT, D = x1_ref.shape
    RT = D // LANES
    i = pl.program_id(0)
    slot = i % 2

    def issue(d_ref, s):
        def body(j, carry):
            for k in range(TOP_K):
                src = pl.multiple_of(d_ref[k, j] * RT, RT)
                dst = pl.multiple_of(j * RT, RT)
                pltpu.make_async_copy(ys_ref.at[pl.ds(src, RT), :], buf_ref.at[s, k, pl.ds(dst, RT), :],
                                      sem.at[s]).start(priority=k % DMA_PRIORITIES)
            return carry
        lax.fori_loop(0, TT, body, 0, unroll=DMA_ISSUE_UNROLL)

    @pl.when(i == 0)
    def _():
        issue(dest_ref, 0)

    @pl.when(i + 1 < pl.num_programs(0))
    def _():
        issue(dest_next_ref, 1 - slot)

    for k in range(TOP_K):
        pltpu.make_async_copy(ys_ref.at[pl.ds(0, TT * RT), :], buf_ref.at[slot, k], sem.at[slot]).wait()

    chunks = None
    for k in range(TOP_K):
        gate = gt_ref[:, k:k + 1]
        rows = [gate * t for t in _load_token_tiles(buf_ref, TT, D, plane=(slot, k))]
        chunks = rows if chunks is None else [a + b for a, b in zip(chunks, rows)]
    h = jnp.concatenate(chunks, axis=1)
    gate2 = mod_ref[5:6, :]
    o_ref[...] = _layer_norm(ALPHA_RES * x1_ref[...] + (1.0 + gate2) * h, g_ref[...], b_ref[...])


def _combine(dest, gates_t, x1, mod, ln_g, ln_b, ys, seq_len, layer):
    N, D = x1.shape
    TT = MOE_TOK
    tiles_per_seq = seq_len // TT
    ln_g3, ln_b3 = ln_g.reshape(-1, 1, D), ln_b.reshape(-1, 1, D)
    lay = lambda t: pl.BlockSpec((None,) + t.shape[1:], lambda i: (layer,) + (0,) * (t.ndim - 1))
    return pl.pallas_call(
        _combine_kernel,
        grid=(N // TT,),
        in_specs=[
            pl.BlockSpec((TOP_K, TT), lambda i: (0, i), memory_space=pltpu.SMEM),
            pl.BlockSpec((TOP_K, TT), lambda i: (0, jnp.minimum(i + 1, N // TT - 1)), memory_space=pltpu.SMEM),
            pl.BlockSpec((TT, TOP_K), lambda i: (i, 0)),
            pl.BlockSpec((TT, D), lambda i: (i, 0)),
            pl.BlockSpec((None, 6, D), lambda i: (i // tiles_per_seq, 0, 0)),
            lay(ln_g3), lay(ln_b3),
            pl.BlockSpec(memory_space=pl.ANY),
        ],
        out_specs=pl.BlockSpec((TT, D), lambda i: (i, 0)),
        out_shape=jax.ShapeDtypeStruct((N, D), F32),
        scratch_shapes=[pltpu.VMEM((2, TOP_K, TT * D // LANES, LANES), F32), pltpu.SemaphoreType.DMA((2,))],
        compiler_params=_cparams("arbitrary"),
        name="moe_combine_ln2",
    )(dest, dest, gates_t, x1, mod, ln_g3, ln_b3, ys)


def _moe(u2_tiles, logits_t, x1, mod, w1, b1, w2, b2, ln_g, ln_b, seq_len, layer):
    N, D = x1.shape
    BLK = MOE_BLK
    top_e, gates, rank, counts = _topk(logits_t)
    counts = counts[:, 0].astype(jnp.int32)
    padded = (counts + BLK - 1) // BLK * BLK
    pend = jnp.cumsum(padded)
    offs = pend - padded
    n_blocks = -(-(N * TOP_K) // BLK) + N_EXPERTS
    blk_start = jnp.arange(n_blocks, dtype=jnp.int32) * BLK
    blk_e = jnp.sum((pend[None, :] <= blk_start[:, None]).astype(jnp.int32), axis=1)
    blk_e = jnp.minimum(blk_e, N_EXPERTS - 1).astype(jnp.int32)
    n_used = (pend[-1:] // BLK).astype(jnp.int32)
    tail_start = jnp.where(padded > 0, pend - BLK, -1).astype(jnp.int32)
    dest = _dest_rows(offs.astype(jnp.int32), top_e, rank)
    xs = _dispatch(tail_start, n_used, dest, u2_tiles, n_blocks * BLK, D)
    ys = _expert_ffn(blk_e, n_used, xs, w1, b1, w2, b2, layer)
    return _combine(dest, gates.T, x1, mod, ln_g, ln_b, ys, seq_len, layer)


def _split_w_in(w_in):
    AW = N_ATTN_HEADS * HEAD_DIM
    GW = GROUP_WIDTH
    C = HYENA_WIDTH
    w = w_in.astype(BF16)
    w_nat = jnp.concatenate([w[:, 3 * AW + 4 * C:], w[:, 3 * AW:3 * AW + C]], axis=1)
    w_qkv = jnp.stack([
        jnp.concatenate([w[:, t * AW + g * GW:t * AW + (g + 1) * GW] for t in range(3)], axis=1)
        for g in range(len(ATTN_GROUPS))])
    w_hy = jnp.stack([w[:, 3 * AW + (1 + t) * C:3 * AW + (2 + t) * C] for t in range(3)])
    return w_nat, w_qkv, w_hy


def kernel(x, c, w_ada, b_ada, w_in, w_sconv, b_sconv, w_pool, pool_scale, f_w1, f_b1, f_w_inner, f_b_inner, f_w_out, f_freq, hy_bias, p_attn, p_pool, p_hyena, w_out, ln1_g, ln1_b, w_router, b_router, w1, b1, w2, b2, ln2_g, ln2_b):
    B, S, D = x.shape
    depth = w_in.shape[0]
    bias_tables = [jnp.asarray(t) for t in _attn_bias_tables(S)]
    fhalf, ghalf, fa, fb = _dft_matrices(S)
    zfeat, decay = _filter_features(S)
    gate_cols = 3 * D
    pool_block = gate_cols // HYENA_WIDTH
    bf = lambda t: t.astype(BF16)
    p_attn_b, p_pool_b, p_hyena_b, w_out_b = map(bf, (p_attn, p_pool, p_hyena, w_out))

    for l in range(depth):
        mod = _ada(c, w_ada, b_ada, l).reshape(B, 6, D)
        w_nat, w_qkv, w_hy = _split_w_in(w_in[l])
        proj, u = _inproj_nat(x, mod, w_nat)
        qkv = _inproj_dil(u, w_qkv, [d for _, d in ATTN_GROUPS])
        hv, hx1, hx2 = _inproj_dil(u, w_hy, [2, 2, 2])
        ya = _attention(qkv, bias_tables)
        yb = _pool(proj, pool_block, w_pool, pool_scale, l)
        filt = _hyena_filters(zfeat, decay, f_w1, f_b1, f_w_inner, f_b_inner, f_w_out, f_freq, l)
        coeffs = _filter_spectra(fa, fb, filt)
        yc = _hyena(hv, hx1, hx2, fhalf, ghalf, coeffs, w_sconv, b_sconv, hy_bias, l)
        x1, u2, logits_t = _merge(ya, yb, yc, proj, x, mod, p_attn_b, p_pool_b, p_hyena_b, w_out_b,
                                  ln1_g, ln1_b, w_router, b_router, l)
        x = _moe(u2, logits_t, x1, mod, w1, b1, w2, b2, ln2_g, ln2_b, S, l).reshape(B, S, D)
    return x
```

```python
import functools
import math

import jax
import jax.numpy as jnp
import numpy as np
from jax import lax
from jax.experimental import pallas as pl
from jax.experimental.pallas import tpu as pltpu

F32 = jnp.float32
BF16 = jnp.bfloat16

HEAD_DIM = 64
HEADS_PER_GROUP = 4
ATTN_GROUPS = ((128, 1), (512, 4), (2048, 16))
N_ATTN_HEADS = HEADS_PER_GROUP * len(ATTN_GROUPS)
GROUP_WIDTH = HEADS_PER_GROUP * HEAD_DIM
HALF_WINDOW = 64
POOL_WINDOWS = (2, 4, 8, 16)
POOL_GROUP = 128
HYENA_WIDTH = 512
FILTER_EMB = 33
FILTER_HIDDEN = 64
N_EXPERTS = 32
TOP_K = 4
SWIGLU_LIMIT = 7.0
SWIGLU_ALPHA = 1.702
LN_EPS = 1e-5
DEPTH = 2
ALPHA_RES = (2 * DEPTH) ** 0.25
DECAY_TARGET = 1e-2
FAST_DECAY_PCT = 0.3
SLOW_DECAY_PCT = 1.5

QBLK = 128
HYENA_FREQ_TILE = 256
HYENA_FREQ_TILES_PER_STEP = 2
HYENA_TIME_TILE = 512
SPECTRA_TILE = 256
INPROJ_COLS = 896
ATTN_UNROLL = 2
ROW_CHUNK = 256
MERGE_ROWS = 512
TOPK_LANES = 1024
MOE_BLK = 512
FFN_CHUNK = 512
WEIGHT_CAST_ROWS = 128
DMA_PRIORITIES = 2
DMA_ISSUE_UNROLL = 8
MOE_TOK = 256
PAD_ROWS = 8
NEG_BIG = -1e30
LANES = 128
VMEM_LIMIT = 56 * 1024 * 1024


def _cparams(*sem):
    return pltpu.CompilerParams(dimension_semantics=sem, vmem_limit_bytes=VMEM_LIMIT)


def _dot(a, b):
    return jnp.dot(a, b, preferred_element_type=F32)


def _dot3(a, b):
    a_hi = a.astype(BF16)
    b_hi = b.astype(BF16)
    a_lo = (a - a_hi.astype(F32)).astype(BF16)
    b_lo = (b - b_hi.astype(F32)).astype(BF16)
    return _dot(a_hi, b_hi) + _dot(a_hi, b_lo) + _dot(a_lo, b_hi)


def _layer_norm(r, g, b):
    mu = jnp.mean(r, axis=-1, keepdims=True)
    c = r - mu
    var = jnp.mean(c * c, axis=-1, keepdims=True)
    return c * lax.rsqrt(var + LN_EPS) * g + b


def _store_token_tiles(ref, val):
    T, D = val.shape
    n = D // LANES
    for c in range(n):
        ref[pl.ds(c, T, stride=n), :] = val[:, c * LANES:(c + 1) * LANES]


def _load_token_tiles(ref, T, D, plane=None):
    n = D // LANES
    lead = () if plane is None else tuple(plane)
    idx = lambda c: lead + (pl.ds(c, T, stride=n), slice(None))
    return [ref[idx(c)] for c in range(n)]


def _ada_kernel(c_ref, w_ref, b_ref, o_ref):
    c = c_ref[...]
    cond = c * (1.0 / (1.0 + jnp.exp(-c)))
    o_ref[...] = _dot3(cond, w_ref[...]) + b_ref[...]


def _ada(c, w_ada, b_ada, layer):
    B, D = c.shape
    n_out = w_ada.shape[-1]
    tn = 768
    return pl.pallas_call(
        _ada_kernel,
        grid=(n_out // tn,),
        in_specs=[
            pl.BlockSpec((B, D), lambda j: (0, 0)),
            pl.BlockSpec((None, D, tn), lambda j: (layer, 0, j)),
            pl.BlockSpec((None, 1, tn), lambda j: (layer, 0, j)),
        ],
        out_specs=pl.BlockSpec((B, tn), lambda j: (0, j)),
        out_shape=jax.ShapeDtypeStruct((B, n_out), F32),
        compiler_params=_cparams("arbitrary"),
        name="ada",
    )(c, w_ada, b_ada.reshape(b_ada.shape[0], 1, n_out))


def _inproj_nat_kernel(x_ref, mod_ref, w_ref, o_ref, u_ref):
    S = x_ref.shape[0]

    @pl.when(pl.program_id(1) == 0)
    def _():
        shift = mod_ref[0:1, :]
        scale1p = 1.0 + mod_ref[1:2, :]

        def body(i, carry):
            rows = pl.ds(pl.multiple_of(i * ROW_CHUNK, ROW_CHUNK), ROW_CHUNK)
            u_ref[rows, :] = (x_ref[rows, :] * scale1p + shift).astype(BF16)
            return carry

        lax.fori_loop(0, S // ROW_CHUNK, body, 0)

    o_ref[...] = _dot(u_ref[...], w_ref[...]).astype(BF16)


def _inproj_nat(x, mod, w_nat):
    B, S, D = x.shape
    n_out = w_nat.shape[1]
    tn = INPROJ_COLS
    return pl.pallas_call(
        _inproj_nat_kernel,
        grid=(B, n_out // tn),
        in_specs=[
            pl.BlockSpec((None, S, D), lambda b, j: (b, 0, 0)),
            pl.BlockSpec((None, 6, D), lambda b, j: (b, 0, 0)),
            pl.BlockSpec((D, tn), lambda b, j: (0, j)),
        ],
        out_specs=[
            pl.BlockSpec((None, S, tn), lambda b, j: (b, 0, j)),
            pl.BlockSpec((None, S, D), lambda b, j: (b, 0, 0)),
        ],
        out_shape=[
            jax.ShapeDtypeStruct((B, S, n_out), BF16),
            jax.ShapeDtypeStruct((B, S, D), BF16),
        ],
        compiler_params=_cparams("arbitrary", "arbitrary"),
        name="inproj_nat",
    )(x, mod, w_nat)


def _inproj_dil_kernel(u_ref, w_ref, o0_ref, o1_ref, o2_ref, acc_ref, *, dils):
    S = u_ref.shape[0]
    for g, o_ref in enumerate((o0_ref, o1_ref, o2_ref)):
        d = dils[g]
        L = S // d
        acc = _dot(u_ref[...], w_ref[g])
        if d == 1:
            o_ref[0] = acc.astype(BF16)
            continue
        for c in range(acc_ref.shape[0]):
            acc_ref[c] = acc[:, c * LANES:(c + 1) * LANES]
        for rho in range(d):
            for c in range(acc_ref.shape[0]):
                o_ref[rho, :, c * LANES:(c + 1) * LANES] = (
                    acc_ref[c, pl.ds(rho, L, stride=d), :].astype(BF16))


def _inproj_dil(u, w_groups, dils):
    B, S, D = u.shape
    W = w_groups.shape[-1]
    assert len(dils) == w_groups.shape[0] == 3
    return pl.pallas_call(
        functools.partial(_inproj_dil_kernel, dils=tuple(dils)),
        grid=(B,),
        in_specs=[
            pl.BlockSpec((None, S, D), lambda b: (b, 0, 0)),
            pl.BlockSpec((len(dils), D, W), lambda b: (0, 0, 0)),
        ],
        out_specs=[pl.BlockSpec((None, d, S // d, W), lambda b: (b, 0, 0, 0)) for d in dils],
        out_shape=[jax.ShapeDtypeStruct((B, d, S // d, W), BF16) for d in dils],
        scratch_shapes=[pltpu.VMEM((W // LANES, S, LANES), F32)],
        compiler_params=_cparams("arbitrary"),
        name="inproj_dil",
    )(u, w_groups)


def _window_start(i, L, W):
    return min(max(QBLK * i - HALF_WINDOW, 0), L - W)


def _attn_bias_tables(S):
    h = np.arange(1, N_ATTN_HEADS + 1, dtype=np.float32)
    slopes = (2.0 ** (-8.0 * h / N_ATTN_HEADS)).astype(np.float32)
    tables = []
    for g, (_, d) in enumerate(ATTN_GROUPS):
        L = S // d
        W = min(2 * QBLK, L)
        nb = L // QBLK
        variants = sorted({_window_start(i, L, W) - QBLK * i for i in range(nb)}, reverse=True)
        r = np.arange(QBLK)[:, None]
        c = np.arange(W)[None, :]
        tab = np.empty((len(variants), HEADS_PER_GROUP * QBLK, W), np.float32)
        for vi, delta in enumerate(variants):
            off = np.abs(c - r + delta)
            for hh in range(HEADS_PER_GROUP):
                slope = np.float64(slopes[g * HEADS_PER_GROUP + hh])
                bias = np.where(off <= HALF_WINDOW, -slope * (d * off), NEG_BIG)
                tab[vi, hh * QBLK:(hh + 1) * QBLK] = bias.astype(np.float32)
        tables.append(tab)
    return tables


def _block_attention(q, kwin, vwin, bias, head_of_lane):
    zero = jnp.zeros_like(q)
    q4 = jnp.concatenate([jnp.where(head_of_lane == h, q, zero) for h in range(HEADS_PER_GROUP)], axis=0)
    s = lax.dot_general(q4, kwin, (((1,), (1,)), ((), ())), preferred_element_type=F32)
    s = s * (1.0 / math.sqrt(HEAD_DIM)) + bias
    m = jnp.max(s, axis=-1, keepdims=True)
    p = jnp.exp(s - m)
    l = jnp.sum(p, axis=-1, keepdims=True)
    o4 = _dot(p.astype(BF16), vwin) * (1.0 / l)
    lse4 = m + jnp.log(l)
    out = jnp.zeros((QBLK, GROUP_WIDTH), F32)
    lse = jnp.zeros((QBLK, GROUP_WIDTH), F32)
    for h in range(HEADS_PER_GROUP):
        rows = slice(h * QBLK, (h + 1) * QBLK)
        sel = head_of_lane == h
        out = jnp.where(sel, o4[rows], out)
        lse = jnp.where(sel, lse4[rows], lse)
    return out, lse


def _attn_kernel(q0_ref, q1_ref, q2_ref, b0_ref, b1_ref, b2_ref, o_ref, acc_ref, lse_ref):
    S = o_ref.shape[0]
    GW = GROUP_WIDTH
    head_of_lane = lax.broadcasted_iota(jnp.int32, (QBLK, GW), 1) // HEAD_DIM
    n_planes = GW // LANES

    def put(g, rows, out, lse):
        for c in range(n_planes):
            acc_ref[g, c, rows, :] = out[:, c * LANES:(c + 1) * LANES]
            lse_ref[g, c, rows, :] = lse[:, c * LANES:(c + 1) * LANES]

    L0 = S
    nb0 = L0 // QBLK

    def g0_body(i, carry):
        ws = pl.multiple_of(jnp.clip(QBLK * i - HALF_WINDOW, 0, L0 - 2 * QBLK), HALF_WINDOW)
        var = jnp.where(i == 0, 0, jnp.where(i == nb0 - 1, 2, 1))
        rows = pl.ds(pl.multiple_of(i * QBLK, QBLK), QBLK)
        out, lse = _block_attention(
            q0_ref[0, rows, 0:GW],
            q0_ref[0, pl.ds(ws, 2 * QBLK), GW:2 * GW],
            q0_ref[0, pl.ds(ws, 2 * QBLK), 2 * GW:3 * GW],
            b0_ref[var], head_of_lane)
        put(0, rows, out, lse)
        return carry

    lax.fori_loop(0, nb0, g0_body, 0, unroll=ATTN_UNROLL)

    for g, (q_ref, b_ref) in ((1, (q1_ref, b1_ref)), (2, (q2_ref, b2_ref))):
        d = ATTN_GROUPS[g][1]
        L = S // d
        W = min(2 * QBLK, L)
        nb = L // QBLK
        deltas = sorted({_window_start(i, L, W) - QBLK * i for i in range(nb)}, reverse=True)

        def gd_body(rho, carry, q_ref=q_ref, b_ref=b_ref, d=d, L=L, W=W, nb=nb, deltas=deltas, g=g):
            for i in range(nb):
                ws = _window_start(i, L, W)
                var = deltas.index(ws - QBLK * i)
                out, lse = _block_attention(
                    q_ref[rho, i * QBLK:(i + 1) * QBLK, 0:GW],
                    q_ref[rho, ws:ws + W, GW:2 * GW],
                    q_ref[rho, ws:ws + W, 2 * GW:3 * GW],
                    b_ref[var], head_of_lane)
                dst = pl.ds(rho + i * QBLK * d, QBLK, stride=d)
                put(g, dst, out, lse)
            return carry

        lax.fori_loop(0, d, gd_body, 0, unroll=max(1, ATTN_UNROLL // nb))

    def merge_body(i, carry):
        rows = pl.ds(pl.multiple_of(i * ROW_CHUNK, ROW_CHUNK), ROW_CHUNK)
        for c in range(n_planes):
            l0, l1, l2 = lse_ref[0, c, rows, :], lse_ref[1, c, rows, :], lse_ref[2, c, rows, :]
            mx = jnp.maximum(jnp.maximum(l0, l1), l2)
            w0, w1, w2 = jnp.exp(l0 - mx), jnp.exp(l1 - mx), jnp.exp(l2 - mx)
            num = w0 * acc_ref[0, c, rows, :] + w1 * acc_ref[1, c, rows, :] + w2 * acc_ref[2, c, rows, :]
            o_ref[rows, c * LANES:(c + 1) * LANES] = (num * (1.0 / (w0 + w1 + w2))).astype(BF16)
        return carry

    lax.fori_loop(0, S // ROW_CHUNK, merge_body, 0)


def _attention(qkv, bias_tables):
    q0, q1, q2 = qkv
    B, _, S, W3 = q0.shape
    GW = GROUP_WIDTH
    in_specs = [pl.BlockSpec((None,) + q.shape[1:], lambda b: (b, 0, 0, 0)) for q in qkv]
    in_specs += [pl.BlockSpec(t.shape, lambda b: (0, 0, 0)) for t in bias_tables]
    return pl.pallas_call(
        _attn_kernel,
        grid=(B,),
        in_specs=in_specs,
        out_specs=pl.BlockSpec((None, S, GW), lambda b: (b, 0, 0)),
        out_shape=jax.ShapeDtypeStruct((B, S, GW), BF16),
        scratch_shapes=[pltpu.VMEM((3, GW // LANES, S, LANES), F32)] * 2,
        compiler_params=_cparams("arbitrary"),
        name="attention",
    )(q0, q1, q2, *bias_tables)


def _pool_kernel(y_ref, w_ref, sc_ref, o_ref, pad_ref):
    S = y_ref.shape[0]
    C = y_ref.shape[1]
    P = PAD_ROWS
    pad_ref[0:P, :] = jnp.zeros((P, C), F32)
    pad_ref[S + P:S + 2 * P, :] = jnp.zeros((P, C), F32)
    pad_ref[P:S + P, :] = y_ref[...].astype(F32)
    for c in range(S // ROW_CHUNK):
        r0 = c * ROW_CHUNK
        t = r0 + lax.broadcasted_iota(jnp.int32, (ROW_CHUNK, 1), 0)
        for g, w in enumerate(POOL_WINDOWS):
            cols = slice(g * POOL_GROUP, (g + 1) * POOL_GROUP)
            acc = pad_ref[P + r0 - w // 2:P + r0 - w // 2 + ROW_CHUNK, cols]
            for j in range(-w // 2 + 1, w // 2):
                acc = acc + pad_ref[P + r0 + j:P + r0 + j + ROW_CHUNK, cols]
            cnt = (jnp.minimum(t + w // 2, S) - jnp.maximum(t - w // 2, 0)).astype(F32)
            pooled = acc / cnt - pad_ref[P + r0:P + r0 + ROW_CHUNK, cols]
            mixed = _dot(pooled.astype(BF16), w_ref[g].astype(BF16)) * sc_ref[:, cols]
            o_ref[r0:r0 + ROW_CHUNK, cols] = mixed.astype(BF16)


def _pool(proj, col_block, w_pool, pool_scale, layer):
    B, S, _ = proj.shape
    C = POOL_GROUP * len(POOL_WINDOWS)
    return pl.pallas_call(
        _pool_kernel,
        grid=(B,),
        in_specs=[
            pl.BlockSpec((None, S, C), lambda b: (b, 0, col_block)),
            pl.BlockSpec((None,) + w_pool.shape[1:], lambda b: (layer, 0, 0, 0)),
            pl.BlockSpec((None, 1, C), lambda b: (layer, 0, 0)),
        ],
        out_specs=pl.BlockSpec((None, S, C), lambda b: (b, 0, 0)),
        out_shape=jax.ShapeDtypeStruct((B, S, C), BF16),
        scratch_shapes=[pltpu.VMEM((S + 2 * PAD_ROWS, C), F32)],
        compiler_params=_cparams("arbitrary"),
        name="pool",
    )(proj, w_pool, pool_scale.reshape(pool_scale.shape[0], 1, C))


def _filter_kernel(z_ref, w1_ref, b1_ref, wi_ref, bi_ref, wo_ref, fr_ref, dec_ref, o_ref):
    S = z_ref.shape[0]
    C = HYENA_WIDTH
    freq = fr_ref[...]
    h = jnp.sin(freq * (_dot3(z_ref[...], w1_ref[...]) + b1_ref[...]))
    for i in range(wi_ref.shape[0]):
        h = jnp.sin(freq * (_dot3(h, wi_ref[i]) + bi_ref[i]))
    not_first = lax.broadcasted_iota(jnp.int32, (S, 1), 0) > 0
    for blk in range(wo_ref.shape[1] // C):
        cols = slice(blk * C, (blk + 1) * C)
        f = _dot3(h, wo_ref[:, cols]) * dec_ref[...]
        if blk % 2 == 1:
            f = jnp.where(not_first, f, 0.0)
        o_ref[:, cols] = f.astype(BF16)


def _hyena_filters(zfeat, decay, f_w1, f_b1, f_w_inner, f_b_inner, f_w_out, f_freq, layer):
    S, E = zfeat.shape
    H = FILTER_HIDDEN
    n_out = f_w_out.shape[-1]
    n_inner = f_w_inner.shape[1]
    w1 = jnp.zeros((f_w1.shape[0], E, H), F32).at[:, :f_w1.shape[1]].set(f_w1)
    full = lambda shape: pl.BlockSpec(shape, lambda i: (0,) * len(shape))
    lay = lambda shape: pl.BlockSpec((None,) + shape, lambda i: (layer,) + (0,) * len(shape))
    return pl.pallas_call(
        _filter_kernel,
        grid=(1,),
        in_specs=[
            full((S, E)), lay((E, H)), lay((1, H)), lay((n_inner, H, H)), lay((n_inner, 1, H)),
            lay((H, n_out)), lay((1, H)), full((S, HYENA_WIDTH)),
        ],
        out_specs=full((S, n_out)),
        out_shape=jax.ShapeDtypeStruct((S, n_out), BF16),
        compiler_params=_cparams("arbitrary"),
        name="hyena_filter",
    )(zfeat, w1, f_b1.reshape(-1, 1, H), f_w_inner, f_b_inner.reshape(-1, n_inner, 1, H), f_w_out,
      f_freq.reshape(-1, 1, H), decay)


def _spec_kernel(fa_ref, fb_ref, h_ref, pre_ref, pim_ref, q1re_ref, q1im_ref, q2re_ref, q2im_ref, sp0_ref):
    TK = fa_ref.shape[0] // 2
    C = HYENA_WIDTH
    n_fft = 2 * h_ref.shape[0]
    xa = _dot(fa_ref[...], h_ref[...])
    xb = _dot(fb_ref[...], h_ref[...])
    row = pl.program_id(0) * TK + lax.broadcasted_iota(jnp.int32, (TK, 1), 0)
    theta = row.astype(F32) * (2.0 * math.pi / n_fft)
    cj, sj = jnp.cos(theta), jnp.sin(theta)
    sc = 2.0 / n_fft
    for o in range(pre_ref.shape[0]):
        hf = slice(2 * o * C, (2 * o + 1) * C)
        hb = slice((2 * o + 1) * C, (2 * o + 2) * C)
        ha_re = xa[:TK, hf] + xa[:TK, hb]
        ha_im = xa[TK:, hf] - xa[TK:, hb]
        hm_re = xb[:TK, hf] + xb[:TK, hb]
        hm_im = xb[TK:, hf] - xb[TK:, hb]
        p_re, p_im = (ha_re + hm_re) * sc, (ha_im - hm_im) * sc
        q_re, q_im = (ha_re - hm_re) * sc, (ha_im + hm_im) * sc
        pre_ref[o] = p_re
        pim_ref[o] = p_im
        q1re_ref[o] = q_re * cj + q_im * sj
        q1im_ref[o] = q_im * cj - q_re * sj
        q2re_ref[o] = q_re * cj - q_im * sj
        q2im_ref[o] = q_im * cj + q_re * sj

        @pl.when(pl.program_id(0) == 0)
        def _():
            h0, hs = ha_re[0:1], hm_re[0:1]
            hh_re = xa[TK:TK + 1, hf] + xa[TK:TK + 1, hb]
            hh_im = xb[TK:TK + 1, hf] - xb[TK:TK + 1, hb]
            sp0_ref[o] = jnp.concatenate(
                [(h0 + hs) * (0.5 * sc), (h0 - hs) * (0.5 * sc), hh_re * sc, hh_im * sc,
                 jnp.zeros((4, C), F32)], axis=0)


def _filter_spectra(fa, fb, filt):
    nkt, tk2, S = fa.shape
    TK = tk2 // 2
    C = HYENA_WIDTH
    n_ord = filt.shape[1] // (2 * C)
    out = jax.ShapeDtypeStruct((n_ord, nkt * TK, C), F32)
    ospec = pl.BlockSpec((n_ord, TK, C), lambda k: (0, k, 0))
    fspec = pl.BlockSpec((None, tk2, S), lambda k: (k, 0, 0))
    return pl.pallas_call(
        _spec_kernel,
        grid=(nkt,),
        in_specs=[fspec, fspec, pl.BlockSpec(filt.shape, lambda k: (0, 0))],
        out_specs=[ospec] * 6 + [pl.BlockSpec((n_ord, 8, C), lambda k: (0, 0, 0))],
        out_shape=[out] * 6 + [jax.ShapeDtypeStruct((n_ord, 8, C), F32)],
        compiler_params=_cparams("arbitrary"),
        name="hyena_spectra",
    )(fa, fb, filt)


def _hyena_kernel(v_ref, x1_ref, x2_ref, f_ref, g_ref, pre_ref, pim_ref, q1re_ref, q1im_ref, q2re_ref, q2im_ref,
                  sp0_ref, wsc_ref, bsc_ref, hb_ref, o_ref, pad_ref, z_ref, zb_ref, gate_ref, zcat_ref, il_ref):
    H, C2 = z_ref.shape
    C = C2 // 2
    n_sub, tk2, _ = f_ref.shape
    TK = tk2 // 2
    TT = g_ref.shape[0]
    nf = H // (n_sub * TK)
    R = ROW_CHUNK
    P = PAD_ROWS
    o = pl.program_id(1)
    s = pl.program_id(2)
    even, odd = slice(0, C), slice(C, C2)

    def fill_pad(in_ref):
        for par in range(2):
            pad_ref[par, 0:P, :] = jnp.zeros((P, C), F32)
            pad_ref[par, H + P:H + 2 * P, :] = jnp.zeros((P, C), F32)
            pad_ref[par, P:H + P, :] = in_ref[par].astype(F32)

    def conv_chunk(r0, w, bias):
        ext_e = pad_ref[0, pl.ds(r0, R + 2 * P), :]
        ext_o = pad_ref[1, pl.ds(r0, R + 2 * P), :]
        ev, od = ext_e[P:P + R], ext_o[P:P + R]
        od_prev = ext_o[P - 1:P - 1 + R]
        ev_next = ext_e[P + 1:P + 1 + R]
        w0, w1, w2 = w[0:1], w[1:2], w[2:3]
        return w0 * od_prev + w1 * ev + w2 * od + bias, w0 * ev + w1 * od + w2 * ev_next + bias

    @pl.when((o == 0) & (s == 0))
    def _():
        fill_pad(v_ref)
        for c in range(H // R):
            rows = slice(c * R, (c + 1) * R)
            conv_e, conv_o = conv_chunk(c * R, wsc_ref[:, 0:C], bsc_ref[:, 0:C])
            z_ref[rows, even] = conv_e
            z_ref[rows, odd] = conv_o
            zb_ref[rows, even] = conv_e.astype(BF16)
            zb_ref[rows, odd] = conv_o.astype(BF16)

    @pl.when((s == 0) & (o == 0))
    def _():
        fill_pad(x1_ref)

    @pl.when((s == 0) & (o == 1))
    def _():
        fill_pad(x2_ref)

    @pl.when(s < nf)
    def _():
        a, b, c, d = sp0_ref[0:1, :], sp0_ref[1:2, :], sp0_ref[2:3, :], sp0_ref[3:4, :]
        for sub in range(n_sub):
            tile = s * n_sub + sub
            cs = slice(sub * TK, (sub + 1) * TK)
            x = _dot(f_ref[sub], zb_ref[...])
            e_re, o_re, e_im, o_im = x[:TK, even], x[:TK, odd], x[TK:, even], x[TK:, odd]
            p_re, p_im = pre_ref[cs, :], pim_ref[cs, :]
            q1_re, q1_im, q2_re, q2_im = q1re_ref[cs, :], q1im_ref[cs, :], q2re_ref[cs, :], q2im_ref[cs, :]
            ze_re = e_re * p_re - e_im * p_im + o_re * q1_re - o_im * q1_im
            ze_im = e_re * p_im + e_im * p_re + o_re * q1_im + o_im * q1_re
            zo_re = e_re * q2_re - e_im * q2_im + o_re * p_re - o_im * p_im
            zo_im = e_re * q2_im + e_im * q2_re + o_re * p_im + o_im * p_re
            first = (tile * TK + lax.broadcasted_iota(jnp.int32, (TK, 1), 0)) == 0
            ze_re = jnp.where(first, e_re[0:1] * a + o_re[0:1] * b, ze_re)
            zo_re = jnp.where(first, e_re[0:1] * b + o_re[0:1] * a, zo_re)
            ze_im = jnp.where(first, e_im[0:1] * c + o_im[0:1] * d, ze_im)
            zo_im = jnp.where(first, o_im[0:1] * c - e_im[0:1] * d, zo_im)
            base = pl.multiple_of(tile * 2 * TK, 2 * TK)
            zcat_ref[pl.ds(base, TK), even] = ze_re.astype(BF16)
            zcat_ref[pl.ds(base, TK), odd] = zo_re.astype(BF16)
            zcat_ref[pl.ds(base + TK, TK), even] = ze_im.astype(BF16)
            zcat_ref[pl.ds(base + TK, TK), odd] = zo_im.astype(BF16)

        w = jnp.where(o == 0, wsc_ref[:, C:2 * C], wsc_ref[:, 2 * C:3 * C])
        bias = jnp.where(o == 0, bsc_ref[:, C:2 * C], bsc_ref[:, 2 * C:3 * C])
        chunks_per_step = (H // R) // nf
        for cc in range(chunks_per_step):
            r0 = pl.multiple_of((s * chunks_per_step + cc) * R, R)
            conv_e, conv_o = conv_chunk(r0, w, bias)
            gate_ref[pl.ds(r0, R), even] = conv_e
            gate_ref[pl.ds(r0, R), odd] = conv_o

    @pl.when(s >= nf)
    def _():
        tt = s - nf
        rows = pl.ds(pl.multiple_of(tt * TT, TT), TT)
        y = _dot(g_ref[...], zcat_ref[...])
        hrow = jnp.where(o == 0, hb_ref[0:1, :], hb_ref[1:2, :])
        hbias = jnp.concatenate([hrow, hrow], axis=1)
        znew = gate_ref[rows, :] * (y + z_ref[rows, :] * hbias)
        z_ref[rows, :] = znew
        zb_ref[rows, :] = znew.astype(BF16)

        @pl.when(o == pl.num_programs(1) - 1)
        def _():
            for p in range(C // LANES):
                il_ref[p, pl.ds(0, TT, stride=2), :] = znew[:, p * LANES:(p + 1) * LANES]
                il_ref[p, pl.ds(1, TT, stride=2), :] = znew[:, C + p * LANES:C + (p + 1) * LANES]
            out_rows = pl.ds(pl.multiple_of(tt * 2 * TT, 2 * TT), 2 * TT)
            for p in range(C // LANES):
                o_ref[out_rows, p * LANES:(p + 1) * LANES] = il_ref[p].astype(BF16)


def _hyena(hv, hx1, hx2, fhalf, ghalf, coeffs, w_sconv, b_sconv, hy_bias, layer):
    B, _, H, C = hv.shape
    S = 2 * H
    n_tiles, tk2, _ = fhalf.shape
    TK = tk2 // 2
    n_sub = HYENA_FREQ_TILES_PER_STEP
    nf = n_tiles // n_sub
    TT = HYENA_TIME_TILE
    ni = H // TT
    sp0 = coeffs[-1]
    n_ord = sp0.shape[0]
    fwd = lambda s: jnp.minimum(s, nf - 1)
    inv = lambda s: jnp.clip(s - nf, 0, ni - 1)
    inp = pl.BlockSpec((None, 2, H, C), lambda bb, o, s: (bb, 0, 0, 0), pipeline_mode=pl.Buffered(1))
    coef = pl.BlockSpec((None, n_sub * TK, C), lambda bb, o, s: (o, fwd(s), 0))
    return pl.pallas_call(
        _hyena_kernel,
        grid=(B, n_ord, nf + ni),
        in_specs=[
            inp, inp, inp,
            pl.BlockSpec((n_sub, tk2, H), lambda bb, o, s: (fwd(s), 0, 0)),
            pl.BlockSpec((TT, 2 * H), lambda bb, o, s: (inv(s), 0)),
            coef, coef, coef, coef, coef, coef,
            pl.BlockSpec((None, 8, C), lambda bb, o, s: (o, 0, 0)),
            pl.BlockSpec((None, 3, 3 * C), lambda bb, o, s: (layer, 0, 0)),
            pl.BlockSpec((None, 1, 3 * C), lambda bb, o, s: (layer, 0, 0)),
            pl.BlockSpec((None, n_ord, C), lambda bb, o, s: (layer, 0, 0)),
        ],
        out_specs=pl.BlockSpec((None, S, C), lambda bb, o, s: (bb, 0, 0)),
        out_shape=jax.ShapeDtypeStruct((B, S, C), BF16),
        scratch_shapes=[
            pltpu.VMEM((2, H + 2 * PAD_ROWS, C), F32),
            pltpu.VMEM((H, 2 * C), F32),
            pltpu.VMEM((H, 2 * C), BF16),
            pltpu.VMEM((H, 2 * C), F32),
            pltpu.VMEM((2 * H, 2 * C), BF16),
            pltpu.VMEM((C // LANES, 2 * TT, LANES), F32),
        ],
        compiler_params=_cparams("arbitrary", "arbitrary", "arbitrary"),
        name="hyena_conv",
    )(hv, hx1, hx2, fhalf, ghalf, *coeffs, w_sconv, b_sconv.reshape(b_sconv.shape[0], 1, 3 * C), hy_bias)


def _packed_dft(bins, n_samples, n_fft, tile, sin_row0):
    theta = ((bins[:, None] * np.arange(n_samples)[None, :]) % n_fft) * (2.0 * math.pi / n_fft)
    fc = np.cos(theta)
    fs = -np.sin(theta)
    fs[0] = sin_row0
    nt = bins.shape[0] // tile
    packed = np.concatenate([fc.reshape(nt, tile, n_samples), fs.reshape(nt, tile, n_samples)], axis=1)
    return packed.astype(np.float32).astype(BF16)


@functools.lru_cache(maxsize=None)
def _dft_matrices(S):
    H, N = S // 2, 2 * S
    j = np.arange(H)
    m = np.arange(H)
    n = np.arange(S)
    quarter = np.asarray([1.0, 0.0, -1.0, 0.0])
    fhalf = _packed_dft(j, H, S, HYENA_FREQ_TILE, 1.0 - 2.0 * (m & 1))
    ghalf = np.ascontiguousarray(fhalf.reshape(2 * H, H).T)
    fa = _packed_dft(j, S, N, SPECTRA_TILE, quarter[n & 3])
    fb = _packed_dft(S - j, S, N, SPECTRA_TILE, -quarter[(n + 3) & 3])
    return fhalf, ghalf, fa, fb


def _filter_features(S):
    t = jnp.linspace(0.0, 1.0, S, dtype=F32)[:, None]
    bands = (FILTER_EMB - 1) // 2
    w = 2.0 * math.pi * jnp.arange(S, dtype=F32) / S
    f = jnp.linspace(1e-4, bands - 1, bands, dtype=F32)
    ang = w[:, None] * f[None, :]
    z = jnp.concatenate([t, jnp.cos(ang), -jnp.sin(ang)], axis=-1)
    z = jnp.pad(z, ((0, 0), (0, 128 - FILTER_EMB)))
    min_decay = math.log(DECAY_TARGET) / SLOW_DECAY_PCT
    max_decay = math.log(DECAY_TARGET) / FAST_DECAY_PCT
    deltas = jnp.linspace(min_decay, max_decay, HYENA_WIDTH, dtype=F32)
    decay = jnp.exp(-t * jnp.abs(deltas)[None, :])
    return z, decay


def _merge_kernel(ya_ref, yb_ref, yc_ref, gl_ref, x_ref, mod_ref, pa_ref, pp_ref, ph_ref, wo_ref,
                  g_ref, b_ref, wr_ref, br_ref, x1_ref, u2_ref, lt_ref):
    D = x_ref.shape[1]
    sig = lambda v: 1.0 / (1.0 + jnp.exp(-v))
    merged = sig(gl_ref[:, 0:D].astype(F32)) * _dot(ya_ref[...], pa_ref[...])
    merged += sig(gl_ref[:, D:2 * D].astype(F32)) * _dot(yb_ref[...], pp_ref[...])
    merged += sig(gl_ref[:, 2 * D:3 * D].astype(F32)) * _dot(yc_ref[...], ph_ref[...])
    h = _dot(merged.astype(BF16), wo_ref[...])
    gate1, shift2, scale2 = mod_ref[2:3, :], mod_ref[3:4, :], mod_ref[4:5, :]
    x1 = _layer_norm(ALPHA_RES * x_ref[...] + (1.0 + gate1) * h, g_ref[...], b_ref[...])
    x1_ref[...] = x1
    u2 = x1 * (1.0 + scale2) + shift2
    _store_token_tiles(u2_ref, u2)
    logits = _dot3(u2, wr_ref[...]) + br_ref[...]
    lt_ref[...] = logits.T[0:N_EXPERTS, :]


def _merge(ya, yb, yc, proj, x, mod, p_attn, p_pool, p_hyena, w_out, ln_g, ln_b, w_router, b_router, layer):
    B, S, D = x.shape
    N = B * S
    TM = MERGE_ROWS
    tiles_per_seq = S // TM
    EP = 128
    flat = lambda t: t.reshape(N, t.shape[-1])
    wr = jnp.zeros((D, EP), F32).at[:, :N_EXPERTS].set(w_router[layer])
    br = jnp.zeros((1, EP), F32).at[:, :N_EXPERTS].set(b_router[layer][None])
    tok = lambda w: pl.BlockSpec((TM, w), lambda i: (i, 0))
    lay = lambda t: pl.BlockSpec((None,) + t.shape[1:], lambda i: (layer,) + (0,) * (t.ndim - 1))
    ln_g3, ln_b3 = ln_g.reshape(-1, 1, D), ln_b.reshape(-1, 1, D)
    return pl.pallas_call(
        _merge_kernel,
        grid=(N // TM,),
        in_specs=[
            tok(ya.shape[-1]), tok(yb.shape[-1]), tok(yc.shape[-1]), tok(3 * D), tok(D),
            pl.BlockSpec((None, 6, D), lambda i: (i // tiles_per_seq, 0, 0)),
            lay(p_attn), lay(p_pool), lay(p_hyena), lay(w_out), lay(ln_g3), lay(ln_b3),
            pl.BlockSpec((D, EP), lambda i: (0, 0)),
            pl.BlockSpec((1, EP), lambda i: (0, 0)),
        ],
        out_specs=[tok(D), pl.BlockSpec((TM * D // LANES, LANES), lambda i: (i, 0)),
                   pl.BlockSpec((N_EXPERTS, TM), lambda i: (0, i))],
        out_shape=[
            jax.ShapeDtypeStruct((N, D), F32),
            jax.ShapeDtypeStruct((N * D // LANES, LANES), F32),
            jax.ShapeDtypeStruct((N_EXPERTS, N), F32),
        ],
        compiler_params=_cparams("arbitrary"),
        name="merge_ln1_router",
    )(flat(ya), flat(yb), flat(yc), flat(proj), flat(x), mod, p_attn, p_pool, p_hyena, w_out,
      ln_g3, ln_b3, wr, br)


def _topk_kernel(l_ref, e_ref, g_ref, r_ref, cnt_ref, carry_ref):
    E, TL = l_ref.shape

    @pl.when(pl.program_id(0) == 0)
    def _():
        carry_ref[...] = jnp.zeros_like(carry_ref)

    l = l_ref[...]
    eio = lax.broadcasted_iota(jnp.int32, (E, TL), 0).astype(F32)
    vals, hots = [], []
    for k in range(TOP_K):
        m = jnp.max(l, axis=0, keepdims=True)
        idx = jnp.min(jnp.where(l == m, eio, float(E)), axis=0, keepdims=True)
        hot = eio == idx
        l = jnp.where(hot, -jnp.inf, l)
        vals.append(m)
        hots.append(hot)
        e_ref[k:k + 1, :] = idx.astype(jnp.int32)
    exps = [jnp.exp(v - vals[0]) for v in vals]
    inv = 1.0 / (exps[0] + exps[1] + exps[2] + exps[3])
    for k in range(TOP_K):
        g_ref[k:k + 1, :] = exps[k] * inv

    memb = jnp.zeros((E, TL), F32)
    for hot in hots:
        memb = memb + hot.astype(F32)
    upper = (lax.broadcasted_iota(jnp.int32, (TL, TL), 0)
             <= lax.broadcasted_iota(jnp.int32, (TL, TL), 1)).astype(BF16)
    incl = _dot(memb.astype(BF16), upper)
    excl = incl - memb + carry_ref[:, 0:1]
    for k in range(TOP_K):
        rank = jnp.sum(jnp.where(hots[k], excl, 0.0), axis=0, keepdims=True)
        r_ref[k:k + 1, :] = rank.astype(jnp.int32)
    carry_ref[...] = carry_ref[...] + jnp.sum(memb, axis=1, keepdims=True)
    cnt_ref[...] = carry_ref[...]


def _topk(logits_t):
    E, N = logits_t.shape
    TL = TOPK_LANES
    tok = pl.BlockSpec((TOP_K, TL), lambda i: (0, i))
    return pl.pallas_call(
        _topk_kernel,
        grid=(N // TL,),
        in_specs=[pl.BlockSpec((E, TL), lambda i: (0, i))],
        out_specs=[tok, tok, tok, pl.BlockSpec((E, 128), lambda i: (0, 0))],
        out_shape=[
            jax.ShapeDtypeStruct((TOP_K, N), jnp.int32),
            jax.ShapeDtypeStruct((TOP_K, N), F32),
            jax.ShapeDtypeStruct((TOP_K, N), jnp.int32),
            jax.ShapeDtypeStruct((E, 128), F32),
        ],
        scratch_shapes=[pltpu.VMEM((E, 128), F32)],
        compiler_params=_cparams("arbitrary"),
        name="topk_rank",
    )(logits_t)


def _dest_kernel(offs_ref, e_ref, r_ref, d_ref):
    e = e_ref[...]
    acc = r_ref[...]
    for ex in range(N_EXPERTS):
        acc = acc + jnp.where(e == ex, offs_ref[ex], 0)
    d_ref[...] = acc


def _dest_rows(offs, top_e, rank):
    K, N = top_e.shape
    TL = TOPK_LANES
    tok = pl.BlockSpec((K, TL), lambda i, offs: (0, i))
    return pl.pallas_call(
        _dest_kernel,
        grid_spec=pltpu.PrefetchScalarGridSpec(
            num_scalar_prefetch=1, grid=(N // TL,), in_specs=[tok, tok], out_specs=tok),
        out_shape=jax.ShapeDtypeStruct((K, N), jnp.int32),
        compiler_params=_cparams("arbitrary"),
        name="dest_rows",
    )(offs, top_e, rank)


def _dispatch_kernel(tail_ref, n_used_ref, dest_ref, u_ref, xs_ref, zero_ref, sem):
    TT = dest_ref.shape[0] // TOP_K
    RT = u_ref.shape[0] // TT
    blk_rows = zero_ref.shape[0]
    n_blocks = xs_ref.shape[0] // blk_rows

    def zero_copy(first_row):
        start = pl.multiple_of(first_row * RT, RT)
        return pltpu.make_async_copy(zero_ref, xs_ref.at[pl.ds(start, blk_rows), :], sem)

    @pl.when(pl.program_id(0) == 0)
    def _():
        zero_ref[...] = jnp.zeros_like(zero_ref)

        def tails(fn):
            def body(e, carry):
                @pl.when(tail_ref[e] >= 0)
                def _():
                    fn(zero_copy(jnp.maximum(tail_ref[e], 0)))
                return carry
            lax.fori_loop(0, N_EXPERTS, body, 0)

        def unused(fn):
            def body(i, carry):
                fn(zero_copy(i * (blk_rows // RT)))
                return carry
            lax.fori_loop(n_used_ref[0], n_blocks, body, 0)

        tails(lambda cp: cp.start())
        unused(lambda cp: cp.start())
        tails(lambda cp: cp.wait())
        unused(lambda cp: cp.wait())

    def row_copy(j, k):
        src = pl.multiple_of(j * RT, RT)
        dst = pl.multiple_of(dest_ref[j * TOP_K + k] * RT, RT)
        return pltpu.make_async_copy(u_ref.at[pl.ds(src, RT), :], xs_ref.at[pl.ds(dst, RT), :], sem)

    def start(j, carry):
        for k in range(TOP_K):
            row_copy(j, k).start(priority=k % DMA_PRIORITIES)
        return carry

    lax.fori_loop(0, TT, start, 0, unroll=DMA_ISSUE_UNROLL)
    for k in range(TOP_K):
        pltpu.make_async_copy(u_ref, xs_ref.at[pl.ds(0, TT * RT), :], sem).wait()


def _dispatch(tail_start, n_used, dest, u2_tiles, n_rows, D):
    RT = D // LANES
    N = u2_tiles.shape[0] // RT
    TT = MOE_TOK
    return pl.pallas_call(
        _dispatch_kernel,
        grid_spec=pltpu.PrefetchScalarGridSpec(
            num_scalar_prefetch=2,
            grid=(N // TT,),
            in_specs=[
                pl.BlockSpec((TOP_K * TT,), lambda i, tail, nu: (i,), memory_space=pltpu.SMEM),
                pl.BlockSpec((TT * RT, LANES), lambda i, tail, nu: (i, 0)),
            ],
            out_specs=pl.BlockSpec(memory_space=pl.ANY),
            scratch_shapes=[pltpu.VMEM((MOE_BLK * RT, LANES), F32), pltpu.SemaphoreType.DMA(())],
        ),
        out_shape=jax.ShapeDtypeStruct((n_rows * RT, LANES), F32),
        compiler_params=_cparams("arbitrary"),
        name="moe_dispatch",
    )(tail_start, n_used, dest, u2_tiles)


def _ffn_kernel(blk_e_ref, n_used_ref, xs_ref, w1_ref, b1_ref, w2_ref, b2_ref, ys_ref,
                w1b_ref, w2b_ref, act_ref):
    F, D = w2_ref.shape
    BLK = xs_ref.shape[0] * LANES // D
    i = pl.program_id(0)
    used = i < n_used_ref[0]
    new_expert = (i == 0) | (blk_e_ref[i] != blk_e_ref[jnp.maximum(i - 1, 0)])

    @pl.when(used & new_expert)
    def _():
        for r in range(0, D, WEIGHT_CAST_ROWS):
            w1b_ref[r:r + WEIGHT_CAST_ROWS, :] = w1_ref[r:r + WEIGHT_CAST_ROWS, :].astype(BF16)
        for r in range(0, F, WEIGHT_CAST_ROWS):
            w2b_ref[r:r + WEIGHT_CAST_ROWS, :] = w2_ref[r:r + WEIGHT_CAST_ROWS, :].astype(BF16)

    @pl.when(used)
    def _():
        x = jnp.concatenate(_load_token_tiles(xs_ref, BLK, D), axis=1).astype(BF16)
        for c in range(0, F, FFN_CHUNK):
            cg = slice(c, c + FFN_CHUNK)
            cl = slice(F + c, F + c + FFN_CHUNK)
            glu = jnp.minimum(_dot(x, w1b_ref[:, cg]) + b1_ref[:, cg], SWIGLU_LIMIT)
            lin = jnp.clip(_dot(x, w1b_ref[:, cl]) + b1_ref[:, cl], -SWIGLU_LIMIT, SWIGLU_LIMIT)
            act = glu * (1.0 / (1.0 + jnp.exp(-SWIGLU_ALPHA * glu))) * (lin + 1.0)
            act_ref[:, cg] = act.astype(BF16)
        _store_token_tiles(ys_ref, _dot(act_ref[...], w2b_ref[...]) + b2_ref[...])

    @pl.when(jnp.logical_not(used))
    def _():
        ys_ref[...] = jnp.zeros_like(ys_ref)


def _expert_ffn(blk_e, n_used, xs_tiles, w1, b1, w2, b2, layer):
    _, E, D, F2 = w1.shape
    F = F2 // 2
    RT = D // LANES
    BLK = MOE_BLK
    rows = pl.BlockSpec((BLK * RT, LANES), lambda i, be, nu: (i, 0))
    per_expert = lambda r, c: pl.BlockSpec((None, None, r, c), lambda i, be, nu: (layer, be[i], 0, 0))
    return pl.pallas_call(
        _ffn_kernel,
        grid_spec=pltpu.PrefetchScalarGridSpec(
            num_scalar_prefetch=2,
            grid=(xs_tiles.shape[0] // (BLK * RT),),
            in_specs=[rows, per_expert(D, F2), per_expert(1, F2), per_expert(F, D), per_expert(1, D)],
            out_specs=rows,
            scratch_shapes=[pltpu.VMEM((D, F2), BF16), pltpu.VMEM((F, D), BF16), pltpu.VMEM((BLK, F), BF16)],
        ),
        out_shape=jax.ShapeDtypeStruct(xs_tiles.shape, F32),
        compiler_params=_cparams("arbitrary"),
        name="moe_ffn",
    )(blk_e, n_used, xs_tiles, w1, b1.reshape(-1, E, 1, F2), w2, b2.reshape(-1, E, 1, D))


def _combine_kernel(dest_ref, dest_next_ref, gt_ref, x1_ref, mod_ref, g_ref, b_ref, ys_ref, o_ref, buf_ref, sem):
    TT, D = x1_ref.shape
    RT = D // LANES
    i = pl.program_id(0)
    slot = i % 2

    def issue(d_ref, s):
        def body(j, carry):
            for k in range(TOP_K):
                src = pl.multiple_of(d_ref[j * TOP_K + k] * RT, RT)
                dst = pl.multiple_of(j * RT, RT)
                pltpu.make_async_copy(ys_ref.at[pl.ds(src, RT), :], buf_ref.at[s, k, pl.ds(dst, RT), :],
                                      sem.at[s]).start(priority=k % DMA_PRIORITIES)
            return carry
        lax.fori_loop(0, TT, body, 0, unroll=DMA_ISSUE_UNROLL)

    @pl.when(i == 0)
    def _():
        issue(dest_ref, 0)

    @pl.when(i + 1 < pl.num_programs(0))
    def _():
        issue(dest_next_ref, 1 - slot)

    for k in range(TOP_K):
        pltpu.make_async_copy(ys_ref.at[pl.ds(0, TT * RT), :], buf_ref.at[slot, k], sem.at[slot]).wait()

    chunks = None
    for k in range(TOP_K):
        gate = gt_ref[:, k:k + 1]
        rows = [gate * t for t in _load_token_tiles(buf_ref, TT, D, plane=(slot, k))]
        chunks = rows if chunks is None else [a + b for a, b in zip(chunks, rows)]
    h = jnp.concatenate(chunks, axis=1)
    gate2 = mod_ref[5:6, :]
    o_ref[...] = _layer_norm(ALPHA_RES * x1_ref[...] + (1.0 + gate2) * h, g_ref[...], b_ref[...])


def _combine(dest, gates_t, x1, mod, ln_g, ln_b, ys, seq_len, layer):
    N, D = x1.shape
    TT = MOE_TOK
    tiles_per_seq = seq_len // TT
    ln_g3, ln_b3 = ln_g.reshape(-1, 1, D), ln_b.reshape(-1, 1, D)
    lay = lambda t: pl.BlockSpec((None,) + t.shape[1:], lambda i: (layer,) + (0,) * (t.ndim - 1))
    return pl.pallas_call(
        _combine_kernel,
        grid=(N // TT,),
        in_specs=[
            pl.BlockSpec((TOP_K * TT,), lambda i: (i,), memory_space=pltpu.SMEM),
            pl.BlockSpec((TOP_K * TT,), lambda i: (jnp.minimum(i + 1, N // TT - 1),), memory_space=pltpu.SMEM),
            pl.BlockSpec((TT, TOP_K), lambda i: (i, 0)),
            pl.BlockSpec((TT, D), lambda i: (i, 0)),
            pl.BlockSpec((None, 6, D), lambda i: (i // tiles_per_seq, 0, 0)),
            lay(ln_g3), lay(ln_b3),
            pl.BlockSpec(memory_space=pl.ANY),
        ],
        out_specs=pl.BlockSpec((TT, D), lambda i: (i, 0)),
        out_shape=jax.ShapeDtypeStruct((N, D), F32),
        scratch_shapes=[pltpu.VMEM((2, TOP_K, TT * D // LANES, LANES), F32), pltpu.SemaphoreType.DMA((2,))],
        compiler_params=_cparams("arbitrary"),
        name="moe_combine_ln2",
    )(dest, dest, gates_t, x1, mod, ln_g3, ln_b3, ys)


def _moe(u2_tiles, logits_t, x1, mod, w1, b1, w2, b2, ln_g, ln_b, seq_len, layer):
    N, D = x1.shape
    BLK = MOE_BLK
    top_e, gates, rank, counts = _topk(logits_t)
    counts = counts[:, 0].astype(jnp.int32)
    padded = (counts + BLK - 1) // BLK * BLK
    pend = jnp.cumsum(padded)
    offs = pend - padded
    n_blocks = -(-(N * TOP_K) // BLK) + N_EXPERTS
    blk_start = jnp.arange(n_blocks, dtype=jnp.int32) * BLK
    blk_e = jnp.sum((pend[None, :] <= blk_start[:, None]).astype(jnp.int32), axis=1)
    blk_e = jnp.minimum(blk_e, N_EXPERTS - 1).astype(jnp.int32)
    n_used = (pend[-1:] // BLK).astype(jnp.int32)
    tail_start = jnp.where(padded > 0, pend - BLK, -1).astype(jnp.int32)
    dest = _dest_rows(offs.astype(jnp.int32), top_e, rank).T.reshape(-1)
    xs = _dispatch(tail_start, n_used, dest, u2_tiles, n_blocks * BLK, D)
    ys = _expert_ffn(blk_e, n_used, xs, w1, b1, w2, b2, layer)
    return _combine(dest, gates.T, x1, mod, ln_g, ln_b, ys, seq_len, layer)


def _split_w_in(w_in):
    AW = N_ATTN_HEADS * HEAD_DIM
    GW = GROUP_WIDTH
    C = HYENA_WIDTH
    w = w_in.astype(BF16)
    w_nat = jnp.concatenate([w[:, 3 * AW + 4 * C:], w[:, 3 * AW:3 * AW + C]], axis=1)
    w_qkv = jnp.stack([
        jnp.concatenate([w[:, t * AW + g * GW:t * AW + (g + 1) * GW] for t in range(3)], axis=1)
        for g in range(len(ATTN_GROUPS))])
    w_hy = jnp.stack([w[:, 3 * AW + (1 + t) * C:3 * AW + (2 + t) * C] for t in range(3)])
    return w_nat, w_qkv, w_hy


def kernel(x, c, w_ada, b_ada, w_in, w_sconv, b_sconv, w_pool, pool_scale, f_w1, f_b1, f_w_inner, f_b_inner, f_w_out, f_freq, hy_bias, p_attn, p_pool, p_hyena, w_out, ln1_g, ln1_b, w_router, b_router, w1, b1, w2, b2, ln2_g, ln2_b):
    B, S, D = x.shape
    depth = w_in.shape[0]
    bias_tables = [jnp.asarray(t) for t in _attn_bias_tables(S)]
    fhalf, ghalf, fa, fb = (jnp.asarray(m) for m in _dft_matrices(S))
    zfeat, decay = _filter_features(S)
    gate_cols = 3 * D
    pool_block = gate_cols // HYENA_WIDTH
    bf = lambda t: t.astype(BF16)
    p_attn_b, p_pool_b, p_hyena_b, w_out_b = map(bf, (p_attn, p_pool, p_hyena, w_out))

    for l in range(depth):
        mod = _ada(c, w_ada, b_ada, l).reshape(B, 6, D)
        w_nat, w_qkv, w_hy = _split_w_in(w_in[l])
        proj, u = _inproj_nat(x, mod, w_nat)
        qkv = _inproj_dil(u, w_qkv, [d for _, d in ATTN_GROUPS])
        hv, hx1, hx2 = _inproj_dil(u, w_hy, [2, 2, 2])
        ya = _attention(qkv, bias_tables)
        yb = _pool(proj, pool_block, w_pool, pool_scale, l)
        filt = _hyena_filters(zfeat, decay, f_w1, f_b1, f_w_inner, f_b_inner, f_w_out, f_freq, l)
        coeffs = _filter_spectra(fa, fb, filt)
        yc = _hyena(hv, hx1, hx2, fhalf, ghalf, coeffs, w_sconv, b_sconv, hy_bias, l)
        x1, u2, logits_t = _merge(ya, yb, yc, proj, x, mod, p_attn_b, p_pool_b, p_hyena_b, w_out_b,
                                  ln1_g, ln1_b, w_router, b_router, l)
        x = _moe(u2, logits_t, x1, mod, w1, b1, w2, b2, ln2_g, ln2_b, S, l).reshape(B, S, D)
    return x
```

```python
import functools
import math

import jax
import jax.numpy as jnp
import numpy as np
from jax import lax
from jax.experimental import pallas as pl
from jax.experimental.pallas import tpu as pltpu

F32 = jnp.float32
BF16 = jnp.bfloat16

HEAD_DIM = 64
HEADS_PER_GROUP = 4
ATTN_GROUPS = ((128, 1), (512, 4), (2048, 16))
N_ATTN_HEADS = HEADS_PER_GROUP * len(ATTN_GROUPS)
GROUP_WIDTH = HEADS_PER_GROUP * HEAD_DIM
HALF_WINDOW = 64
POOL_WINDOWS = (2, 4, 8, 16)
POOL_GROUP = 128
HYENA_WIDTH = 512
FILTER_EMB = 33
FILTER_HIDDEN = 64
N_EXPERTS = 32
TOP_K = 4
SWIGLU_LIMIT = 7.0
SWIGLU_ALPHA = 1.702
LN_EPS = 1e-5
DEPTH = 2
ALPHA_RES = (2 * DEPTH) ** 0.25
DECAY_TARGET = 1e-2
FAST_DECAY_PCT = 0.3
SLOW_DECAY_PCT = 1.5

QBLK = 128
HYENA_FREQ_TILE = 256
HYENA_FREQ_TILES_PER_STEP = 1
HYENA_TIME_TILE = 512
SPECTRA_TILE = 256
INPROJ_COLS = 896
ATTN_UNROLL = 8
ROW_CHUNK = 256
MERGE_ROWS = 512
TOPK_LANES = 1024
MOE_BLK = 512
FFN_CHUNK = 512
WEIGHT_CAST_ROWS = 128
DMA_PRIORITIES = 2
DMA_ISSUE_UNROLL = 8
COMBINE_CHUNKS = 8
MOE_TOK = 256
PAD_ROWS = 8
NEG_BIG = -1e30
LANES = 128
VMEM_LIMIT = 56 * 1024 * 1024


def _cparams(*sem):
    return pltpu.CompilerParams(dimension_semantics=sem, vmem_limit_bytes=VMEM_LIMIT)


def _dot(a, b):
    return jnp.dot(a, b, preferred_element_type=F32)


def _dot3(a, b):
    a_hi = a.astype(BF16)
    b_hi = b.astype(BF16)
    a_lo = (a - a_hi.astype(F32)).astype(BF16)
    b_lo = (b - b_hi.astype(F32)).astype(BF16)
    return _dot(a_hi, b_hi) + _dot(a_hi, b_lo) + _dot(a_lo, b_hi)


def _layer_norm(r, g, b):
    mu = jnp.mean(r, axis=-1, keepdims=True)
    c = r - mu
    var = jnp.mean(c * c, axis=-1, keepdims=True)
    return c * lax.rsqrt(var + LN_EPS) * g + b


def _store_token_tiles(ref, val):
    T, D = val.shape
    n = D // LANES
    for c in range(n):
        ref[pl.ds(c, T, stride=n), :] = val[:, c * LANES:(c + 1) * LANES]


def _load_token_tiles(ref, T, D, plane=None):
    n = D // LANES
    lead = () if plane is None else tuple(plane)
    idx = lambda c: lead + (pl.ds(c, T, stride=n), slice(None))
    return [ref[idx(c)] for c in range(n)]


def _ada_kernel(c_ref, w_ref, b_ref, o_ref):
    c = c_ref[...]
    cond = c * (1.0 / (1.0 + jnp.exp(-c)))
    o_ref[...] = _dot3(cond, w_ref[...]) + b_ref[...]


def _ada(c, w_ada, b_ada, layer):
    B, D = c.shape
    n_out = w_ada.shape[-1]
    tn = 768
    return pl.pallas_call(
        _ada_kernel,
        grid=(n_out // tn,),
        in_specs=[
            pl.BlockSpec((B, D), lambda j: (0, 0)),
            pl.BlockSpec((None, D, tn), lambda j: (layer, 0, j)),
            pl.BlockSpec((None, 1, tn), lambda j: (layer, 0, j)),
        ],
        out_specs=pl.BlockSpec((B, tn), lambda j: (0, j)),
        out_shape=jax.ShapeDtypeStruct((B, n_out), F32),
        compiler_params=_cparams("arbitrary"),
        name="ada",
    )(c, w_ada, b_ada.reshape(b_ada.shape[0], 1, n_out))


def _inproj_nat_kernel(x_ref, mod_ref, w_ref, o_ref, u_ref):
    S = x_ref.shape[0]

    @pl.when(pl.program_id(1) == 0)
    def _():
        shift = mod_ref[0:1, :]
        scale1p = 1.0 + mod_ref[1:2, :]

        def body(i, carry):
            rows = pl.ds(pl.multiple_of(i * ROW_CHUNK, ROW_CHUNK), ROW_CHUNK)
            u_ref[rows, :] = (x_ref[rows, :] * scale1p + shift).astype(BF16)
            return carry

        lax.fori_loop(0, S // ROW_CHUNK, body, 0)

    o_ref[...] = _dot(u_ref[...], w_ref[...]).astype(BF16)


def _inproj_nat(x, mod, w_nat):
    B, S, D = x.shape
    n_out = w_nat.shape[1]
    tn = INPROJ_COLS
    return pl.pallas_call(
        _inproj_nat_kernel,
        grid=(B, n_out // tn),
        in_specs=[
            pl.BlockSpec((None, S, D), lambda b, j: (b, 0, 0)),
            pl.BlockSpec((None, 6, D), lambda b, j: (b, 0, 0)),
            pl.BlockSpec((D, tn), lambda b, j: (0, j)),
        ],
        out_specs=[
            pl.BlockSpec((None, S, tn), lambda b, j: (b, 0, j)),
            pl.BlockSpec((None, S, D), lambda b, j: (b, 0, 0)),
        ],
        out_shape=[
            jax.ShapeDtypeStruct((B, S, n_out), BF16),
            jax.ShapeDtypeStruct((B, S, D), BF16),
        ],
        compiler_params=_cparams("arbitrary", "arbitrary"),
        name="inproj_nat",
    )(x, mod, w_nat)


def _inproj_dil_kernel(u_ref, w_ref, o0_ref, o1_ref, o2_ref, acc_ref, *, dils):
    S = u_ref.shape[0]
    for g, o_ref in enumerate((o0_ref, o1_ref, o2_ref)):
        d = dils[g]
        L = S // d
        acc = _dot(u_ref[...], w_ref[g])
        if d == 1:
            o_ref[0] = acc.astype(BF16)
            continue
        for c in range(acc_ref.shape[0]):
            acc_ref[c] = acc[:, c * LANES:(c + 1) * LANES]
        for rho in range(d):
            for c in range(acc_ref.shape[0]):
                o_ref[rho, :, c * LANES:(c + 1) * LANES] = (
                    acc_ref[c, pl.ds(rho, L, stride=d), :].astype(BF16))


def _inproj_dil(u, w_groups, dils):
    B, S, D = u.shape
    W = w_groups.shape[-1]
    assert len(dils) == w_groups.shape[0] == 3
    return pl.pallas_call(
        functools.partial(_inproj_dil_kernel, dils=tuple(dils)),
        grid=(B,),
        in_specs=[
            pl.BlockSpec((None, S, D), lambda b: (b, 0, 0)),
            pl.BlockSpec((len(dils), D, W), lambda b: (0, 0, 0)),
        ],
        out_specs=[pl.BlockSpec((None, d, S // d, W), lambda b: (b, 0, 0, 0)) for d in dils],
        out_shape=[jax.ShapeDtypeStruct((B, d, S // d, W), BF16) for d in dils],
        scratch_shapes=[pltpu.VMEM((W // LANES, S, LANES), F32)],
        compiler_params=_cparams("arbitrary"),
        name="inproj_dil",
    )(u, w_groups)


def _window_start(i, L, W):
    return min(max(QBLK * i - HALF_WINDOW, 0), L - W)


def _attn_bias_tables(S):
    h = np.arange(1, N_ATTN_HEADS + 1, dtype=np.float32)
    slopes = (2.0 ** (-8.0 * h / N_ATTN_HEADS)).astype(np.float32)
    tables = []
    for g, (_, d) in enumerate(ATTN_GROUPS):
        L = S // d
        W = min(2 * QBLK, L)
        nb = L // QBLK
        variants = sorted({_window_start(i, L, W) - QBLK * i for i in range(nb)}, reverse=True)
        r = np.arange(QBLK)[:, None]
        c = np.arange(W)[None, :]
        tab = np.empty((len(variants), HEADS_PER_GROUP * QBLK, W), np.float32)
        for vi, delta in enumerate(variants):
            off = np.abs(c - r + delta)
            for hh in range(HEADS_PER_GROUP):
                slope = np.float64(slopes[g * HEADS_PER_GROUP + hh])
                bias = np.where(off <= HALF_WINDOW, -slope * (d * off), NEG_BIG)
                tab[vi, hh * QBLK:(hh + 1) * QBLK] = bias.astype(np.float32)
        tables.append(tab)
    return tables


def _block_attention(q, kwin, vwin, bias, head_of_lane):
    zero = jnp.zeros_like(q)
    q4 = jnp.concatenate([jnp.where(head_of_lane == h, q, zero) for h in range(HEADS_PER_GROUP)], axis=0)
    s = lax.dot_general(q4, kwin, (((1,), (1,)), ((), ())), preferred_element_type=F32)
    s = s * (1.0 / math.sqrt(HEAD_DIM)) + bias
    m = jnp.max(s, axis=-1, keepdims=True)
    p = jnp.exp(s - m)
    l = jnp.sum(p, axis=-1, keepdims=True)
    o4 = _dot(p.astype(BF16), vwin) * (1.0 / l)
    lse4 = m + jnp.log(l)
    out = jnp.zeros((QBLK, GROUP_WIDTH), F32)
    lse = jnp.zeros((QBLK, GROUP_WIDTH), F32)
    for h in range(HEADS_PER_GROUP):
        rows = slice(h * QBLK, (h + 1) * QBLK)
        sel = head_of_lane == h
        out = jnp.where(sel, o4[rows], out)
        lse = jnp.where(sel, lse4[rows], lse)
    return out, lse


def _attn_kernel(q0_ref, q1_ref, q2_ref, b0_ref, b1_ref, b2_ref, o_ref, acc_ref, lse_ref):
    S = o_ref.shape[0]
    GW = GROUP_WIDTH
    head_of_lane = lax.broadcasted_iota(jnp.int32, (QBLK, GW), 1) // HEAD_DIM
    n_planes = GW // LANES

    def put(g, rows, out, lse):
        for c in range(n_planes):
            acc_ref[g, c, rows, :] = out[:, c * LANES:(c + 1) * LANES]
            lse_ref[g, c, rows, :] = lse[:, c * LANES:(c + 1) * LANES]

    L0 = S
    nb0 = L0 // QBLK

    def g0_body(i, carry):
        ws = pl.multiple_of(jnp.clip(QBLK * i - HALF_WINDOW, 0, L0 - 2 * QBLK), HALF_WINDOW)
        var = jnp.where(i == 0, 0, jnp.where(i == nb0 - 1, 2, 1))
        rows = pl.ds(pl.multiple_of(i * QBLK, QBLK), QBLK)
        out, lse = _block_attention(
            q0_ref[0, rows, 0:GW],
            q0_ref[0, pl.ds(ws, 2 * QBLK), GW:2 * GW],
            q0_ref[0, pl.ds(ws, 2 * QBLK), 2 * GW:3 * GW],
            b0_ref[var], head_of_lane)
        put(0, rows, out, lse)
        return carry

    lax.fori_loop(0, nb0, g0_body, 0, unroll=ATTN_UNROLL)

    for g, (q_ref, b_ref) in ((1, (q1_ref, b1_ref)), (2, (q2_ref, b2_ref))):
        d = ATTN_GROUPS[g][1]
        L = S // d
        W = min(2 * QBLK, L)
        nb = L // QBLK
        deltas = sorted({_window_start(i, L, W) - QBLK * i for i in range(nb)}, reverse=True)

        def gd_body(rho, carry, q_ref=q_ref, b_ref=b_ref, d=d, L=L, W=W, nb=nb, deltas=deltas, g=g):
            for i in range(nb):
                ws = _window_start(i, L, W)
                var = deltas.index(ws - QBLK * i)
                out, lse = _block_attention(
                    q_ref[rho, i * QBLK:(i + 1) * QBLK, 0:GW],
                    q_ref[rho, ws:ws + W, GW:2 * GW],
                    q_ref[rho, ws:ws + W, 2 * GW:3 * GW],
                    b_ref[var], head_of_lane)
                dst = pl.ds(rho + i * QBLK * d, QBLK, stride=d)
                put(g, dst, out, lse)
            return carry

        lax.fori_loop(0, d, gd_body, 0, unroll=max(1, ATTN_UNROLL // nb))

    def merge_body(i, carry):
        rows = pl.ds(pl.multiple_of(i * ROW_CHUNK, ROW_CHUNK), ROW_CHUNK)
        for c in range(n_planes):
            l0, l1, l2 = lse_ref[0, c, rows, :], lse_ref[1, c, rows, :], lse_ref[2, c, rows, :]
            mx = jnp.maximum(jnp.maximum(l0, l1), l2)
            w0, w1, w2 = jnp.exp(l0 - mx), jnp.exp(l1 - mx), jnp.exp(l2 - mx)
            num = w0 * acc_ref[0, c, rows, :] + w1 * acc_ref[1, c, rows, :] + w2 * acc_ref[2, c, rows, :]
            o_ref[rows, c * LANES:(c + 1) * LANES] = (num * (1.0 / (w0 + w1 + w2))).astype(BF16)
        return carry

    lax.fori_loop(0, S // ROW_CHUNK, merge_body, 0)


def _attention(qkv, bias_tables):
    q0, q1, q2 = qkv
    B, _, S, W3 = q0.shape
    GW = GROUP_WIDTH
    in_specs = [pl.BlockSpec((None,) + q.shape[1:], lambda b: (b, 0, 0, 0)) for q in qkv]
    in_specs += [pl.BlockSpec(t.shape, lambda b: (0, 0, 0)) for t in bias_tables]
    return pl.pallas_call(
        _attn_kernel,
        grid=(B,),
        in_specs=in_specs,
        out_specs=pl.BlockSpec((None, S, GW), lambda b: (b, 0, 0)),
        out_shape=jax.ShapeDtypeStruct((B, S, GW), BF16),
        scratch_shapes=[pltpu.VMEM((3, GW // LANES, S, LANES), F32)] * 2,
        compiler_params=_cparams("arbitrary"),
        name="attention",
    )(q0, q1, q2, *bias_tables)


def _pool_kernel(y_ref, w_ref, sc_ref, o_ref, pad_ref):
    S = y_ref.shape[0]
    C = y_ref.shape[1]
    P = PAD_ROWS
    pad_ref[0:P, :] = jnp.zeros((P, C), F32)
    pad_ref[S + P:S + 2 * P, :] = jnp.zeros((P, C), F32)
    pad_ref[P:S + P, :] = y_ref[...].astype(F32)
    for c in range(S // ROW_CHUNK):
        r0 = c * ROW_CHUNK
        t = r0 + lax.broadcasted_iota(jnp.int32, (ROW_CHUNK, 1), 0)
        for g, w in enumerate(POOL_WINDOWS):
            cols = slice(g * POOL_GROUP, (g + 1) * POOL_GROUP)
            acc = pad_ref[P + r0 - w // 2:P + r0 - w // 2 + ROW_CHUNK, cols]
            for j in range(-w // 2 + 1, w // 2):
                acc = acc + pad_ref[P + r0 + j:P + r0 + j + ROW_CHUNK, cols]
            cnt = (jnp.minimum(t + w // 2, S) - jnp.maximum(t - w // 2, 0)).astype(F32)
            pooled = acc / cnt - pad_ref[P + r0:P + r0 + ROW_CHUNK, cols]
            mixed = _dot(pooled.astype(BF16), w_ref[g].astype(BF16)) * sc_ref[:, cols]
            o_ref[r0:r0 + ROW_CHUNK, cols] = mixed.astype(BF16)


def _pool(proj, col_block, w_pool, pool_scale, layer):
    B, S, _ = proj.shape
    C = POOL_GROUP * len(POOL_WINDOWS)
    return pl.pallas_call(
        _pool_kernel,
        grid=(B,),
        in_specs=[
            pl.BlockSpec((None, S, C), lambda b: (b, 0, col_block)),
            pl.BlockSpec((None,) + w_pool.shape[1:], lambda b: (layer, 0, 0, 0)),
            pl.BlockSpec((None, 1, C), lambda b: (layer, 0, 0)),
        ],
        out_specs=pl.BlockSpec((None, S, C), lambda b: (b, 0, 0)),
        out_shape=jax.ShapeDtypeStruct((B, S, C), BF16),
        scratch_shapes=[pltpu.VMEM((S + 2 * PAD_ROWS, C), F32)],
        compiler_params=_cparams("arbitrary"),
        name="pool",
    )(proj, w_pool, pool_scale.reshape(pool_scale.shape[0], 1, C))


def _filter_kernel(z_ref, w1_ref, b1_ref, wi_ref, bi_ref, wo_ref, fr_ref, dec_ref, o_ref):
    S = z_ref.shape[0]
    C = HYENA_WIDTH
    freq = fr_ref[...]
    h = jnp.sin(freq * (_dot3(z_ref[...], w1_ref[...]) + b1_ref[...]))
    for i in range(wi_ref.shape[0]):
        h = jnp.sin(freq * (_dot3(h, wi_ref[i]) + bi_ref[i]))
    not_first = lax.broadcasted_iota(jnp.int32, (S, 1), 0) > 0
    for blk in range(wo_ref.shape[1] // C):
        cols = slice(blk * C, (blk + 1) * C)
        f = _dot3(h, wo_ref[:, cols]) * dec_ref[...]
        if blk % 2 == 1:
            f = jnp.where(not_first, f, 0.0)
        o_ref[:, cols] = f.astype(BF16)


def _hyena_filters(zfeat, decay, f_w1, f_b1, f_w_inner, f_b_inner, f_w_out, f_freq, layer):
    S, E = zfeat.shape
    H = FILTER_HIDDEN
    n_out = f_w_out.shape[-1]
    n_inner = f_w_inner.shape[1]
    w1 = jnp.zeros((f_w1.shape[0], E, H), F32).at[:, :f_w1.shape[1]].set(f_w1)
    full = lambda shape: pl.BlockSpec(shape, lambda i: (0,) * len(shape))
    lay = lambda shape: pl.BlockSpec((None,) + shape, lambda i: (layer,) + (0,) * len(shape))
    return pl.pallas_call(
        _filter_kernel,
        grid=(1,),
        in_specs=[
            full((S, E)), lay((E, H)), lay((1, H)), lay((n_inner, H, H)), lay((n_inner, 1, H)),
            lay((H, n_out)), lay((1, H)), full((S, HYENA_WIDTH)),
        ],
        out_specs=full((S, n_out)),
        out_shape=jax.ShapeDtypeStruct((S, n_out), BF16),
        compiler_params=_cparams("arbitrary"),
        name="hyena_filter",
    )(zfeat, w1, f_b1.reshape(-1, 1, H), f_w_inner, f_b_inner.reshape(-1, n_inner, 1, H), f_w_out,
      f_freq.reshape(-1, 1, H), decay)


def _spec_kernel(fa_ref, fb_ref, h_ref, pre_ref, pim_ref, q1re_ref, q1im_ref, q2re_ref, q2im_ref, sp0_ref):
    TK = fa_ref.shape[0] // 2
    C = HYENA_WIDTH
    n_fft = 2 * h_ref.shape[0]
    xa = _dot(fa_ref[...], h_ref[...])
    xb = _dot(fb_ref[...], h_ref[...])
    row = pl.program_id(0) * TK + lax.broadcasted_iota(jnp.int32, (TK, 1), 0)
    theta = row.astype(F32) * (2.0 * math.pi / n_fft)
    cj, sj = jnp.cos(theta), jnp.sin(theta)
    sc = 2.0 / n_fft
    for o in range(pre_ref.shape[0]):
        hf = slice(2 * o * C, (2 * o + 1) * C)
        hb = slice((2 * o + 1) * C, (2 * o + 2) * C)
        ha_re = xa[:TK, hf] + xa[:TK, hb]
        ha_im = xa[TK:, hf] - xa[TK:, hb]
        hm_re = xb[:TK, hf] + xb[:TK, hb]
        hm_im = xb[TK:, hf] - xb[TK:, hb]
        p_re, p_im = (ha_re + hm_re) * sc, (ha_im - hm_im) * sc
        q_re, q_im = (ha_re - hm_re) * sc, (ha_im + hm_im) * sc
        pre_ref[o] = p_re
        pim_ref[o] = p_im
        q1re_ref[o] = q_re * cj + q_im * sj
        q1im_ref[o] = q_im * cj - q_re * sj
        q2re_ref[o] = q_re * cj - q_im * sj
        q2im_ref[o] = q_im * cj + q_re * sj

        @pl.when(pl.program_id(0) == 0)
        def _():
            h0, hs = ha_re[0:1], hm_re[0:1]
            hh_re = xa[TK:TK + 1, hf] + xa[TK:TK + 1, hb]
            hh_im = xb[TK:TK + 1, hf] - xb[TK:TK + 1, hb]
            sp0_ref[o] = jnp.concatenate(
                [(h0 + hs) * (0.5 * sc), (h0 - hs) * (0.5 * sc), hh_re * sc, hh_im * sc,
                 jnp.zeros((4, C), F32)], axis=0)


def _filter_spectra(fa, fb, filt):
    nkt, tk2, S = fa.shape
    TK = tk2 // 2
    C = HYENA_WIDTH
    n_ord = filt.shape[1] // (2 * C)
    out = jax.ShapeDtypeStruct((n_ord, nkt * TK, C), F32)
    ospec = pl.BlockSpec((n_ord, TK, C), lambda k: (0, k, 0))
    fspec = pl.BlockSpec((None, tk2, S), lambda k: (k, 0, 0))
    return pl.pallas_call(
        _spec_kernel,
        grid=(nkt,),
        in_specs=[fspec, fspec, pl.BlockSpec(filt.shape, lambda k: (0, 0))],
        out_specs=[ospec] * 6 + [pl.BlockSpec((n_ord, 8, C), lambda k: (0, 0, 0))],
        out_shape=[out] * 6 + [jax.ShapeDtypeStruct((n_ord, 8, C), F32)],
        compiler_params=_cparams("arbitrary"),
        name="hyena_spectra",
    )(fa, fb, filt)


def _hyena_kernel(v_ref, x1_ref, x2_ref, f_ref, g_ref, pre_ref, pim_ref, q1re_ref, q1im_ref, q2re_ref, q2im_ref,
                  sp0_ref, wsc_ref, bsc_ref, hb_ref, o_ref, pad_ref, z_ref, zb_ref, gate_ref, zcat_ref, il_ref):
    H, C2 = z_ref.shape
    C = C2 // 2
    n_sub, tk2, _ = f_ref.shape
    TK = tk2 // 2
    TT = g_ref.shape[0]
    nf = H // (n_sub * TK)
    R = ROW_CHUNK
    P = PAD_ROWS
    o = pl.program_id(1)
    s = pl.program_id(2)
    even, odd = slice(0, C), slice(C, C2)

    def fill_pad(in_ref):
        for par in range(2):
            pad_ref[par, 0:P, :] = jnp.zeros((P, C), F32)
            pad_ref[par, H + P:H + 2 * P, :] = jnp.zeros((P, C), F32)
            pad_ref[par, P:H + P, :] = in_ref[par].astype(F32)

    def conv_chunk(r0, w, bias):
        ev = pad_ref[0, P + r0:P + r0 + R, :]
        od = pad_ref[1, P + r0:P + r0 + R, :]
        od_prev = pad_ref[1, P + r0 - 1:P + r0 - 1 + R, :]
        ev_next = pad_ref[0, P + r0 + 1:P + r0 + 1 + R, :]
        w0, w1, w2 = w[0:1], w[1:2], w[2:3]
        return w0 * od_prev + w1 * ev + w2 * od + bias, w0 * ev + w1 * od + w2 * ev_next + bias

    @pl.when((o == 0) & (s == 0))
    def _():
        fill_pad(v_ref)
        for c in range(H // R):
            rows = slice(c * R, (c + 1) * R)
            conv_e, conv_o = conv_chunk(c * R, wsc_ref[:, 0:C], bsc_ref[:, 0:C])
            z_ref[rows, even] = conv_e
            z_ref[rows, odd] = conv_o
            zb_ref[rows, even] = conv_e.astype(BF16)
            zb_ref[rows, odd] = conv_o.astype(BF16)

    @pl.when((s == 0) & (o == 0))
    def _():
        fill_pad(x1_ref)

    @pl.when((s == 0) & (o == 1))
    def _():
        fill_pad(x2_ref)

    @pl.when(s < nf)
    def _():
        a, b, c, d = sp0_ref[0:1, :], sp0_ref[1:2, :], sp0_ref[2:3, :], sp0_ref[3:4, :]
        for sub in range(n_sub):
            tile = s * n_sub + sub
            cs = slice(sub * TK, (sub + 1) * TK)
            x = _dot(f_ref[sub], zb_ref[...])
            e_re, o_re, e_im, o_im = x[:TK, even], x[:TK, odd], x[TK:, even], x[TK:, odd]
            p_re, p_im = pre_ref[cs, :], pim_ref[cs, :]
            q1_re, q1_im, q2_re, q2_im = q1re_ref[cs, :], q1im_ref[cs, :], q2re_ref[cs, :], q2im_ref[cs, :]
            ze_re = e_re * p_re - e_im * p_im + o_re * q1_re - o_im * q1_im
            ze_im = e_re * p_im + e_im * p_re + o_re * q1_im + o_im * q1_re
            zo_re = e_re * q2_re - e_im * q2_im + o_re * p_re - o_im * p_im
            zo_im = e_re * q2_im + e_im * q2_re + o_re * p_im + o_im * p_re
            first = (tile * TK + lax.broadcasted_iota(jnp.int32, (TK, 1), 0)) == 0
            ze_re = jnp.where(first, e_re[0:1] * a + o_re[0:1] * b, ze_re)
            zo_re = jnp.where(first, e_re[0:1] * b + o_re[0:1] * a, zo_re)
            ze_im = jnp.where(first, e_im[0:1] * c + o_im[0:1] * d, ze_im)
            zo_im = jnp.where(first, o_im[0:1] * c - e_im[0:1] * d, zo_im)
            base = pl.multiple_of(tile * 2 * TK, 2 * TK)
            zcat_ref[pl.ds(base, TK), even] = ze_re.astype(BF16)
            zcat_ref[pl.ds(base, TK), odd] = zo_re.astype(BF16)
            zcat_ref[pl.ds(base + TK, TK), even] = ze_im.astype(BF16)
            zcat_ref[pl.ds(base + TK, TK), odd] = zo_im.astype(BF16)

    @pl.when(s == nf)
    def _():
        w = jnp.where(o == 0, wsc_ref[:, C:2 * C], wsc_ref[:, 2 * C:3 * C])
        bias = jnp.where(o == 0, bsc_ref[:, C:2 * C], bsc_ref[:, 2 * C:3 * C])
        for c in range(H // R):
            conv_e, conv_o = conv_chunk(c * R, w, bias)
            gate_ref[c * R:(c + 1) * R, even] = conv_e
            gate_ref[c * R:(c + 1) * R, odd] = conv_o

    @pl.when(s >= nf)
    def _():
        tt = s - nf
        rows = pl.ds(pl.multiple_of(tt * TT, TT), TT)
        y = _dot(g_ref[...], zcat_ref[...])
        hrow = jnp.where(o == 0, hb_ref[0:1, :], hb_ref[1:2, :])
        hbias = jnp.concatenate([hrow, hrow], axis=1)
        znew = gate_ref[rows, :] * (y + z_ref[rows, :] * hbias)
        z_ref[rows, :] = znew
        zb_ref[rows, :] = znew.astype(BF16)

        @pl.when(o == pl.num_programs(1) - 1)
        def _():
            for p in range(C // LANES):
                il_ref[p, pl.ds(0, TT, stride=2), :] = znew[:, p * LANES:(p + 1) * LANES]
                il_ref[p, pl.ds(1, TT, stride=2), :] = znew[:, C + p * LANES:C + (p + 1) * LANES]
            out_rows = pl.ds(pl.multiple_of(tt * 2 * TT, 2 * TT), 2 * TT)
            for p in range(C // LANES):
                o_ref[out_rows, p * LANES:(p + 1) * LANES] = il_ref[p].astype(BF16)


def _hyena(hv, hx1, hx2, fhalf, ghalf, coeffs, w_sconv, b_sconv, hy_bias, layer):
    B, _, H, C = hv.shape
    S = 2 * H
    n_tiles, tk2, _ = fhalf.shape
    TK = tk2 // 2
    n_sub = HYENA_FREQ_TILES_PER_STEP
    nf = n_tiles // n_sub
    TT = HYENA_TIME_TILE
    ni = H // TT
    sp0 = coeffs[-1]
    n_ord = sp0.shape[0]
    fwd = lambda s: jnp.minimum(s, nf - 1)
    inv = lambda s: jnp.clip(s - nf, 0, ni - 1)
    inp = pl.BlockSpec((None, 2, H, C), lambda bb, o, s: (bb, 0, 0, 0), pipeline_mode=pl.Buffered(1))
    coef = pl.BlockSpec((None, n_sub * TK, C), lambda bb, o, s: (o, fwd(s), 0))
    return pl.pallas_call(
        _hyena_kernel,
        grid=(B, n_ord, nf + ni),
        in_specs=[
            inp, inp, inp,
            pl.BlockSpec((n_sub, tk2, H), lambda bb, o, s: (fwd(s), 0, 0)),
            pl.BlockSpec((TT, 2 * H), lambda bb, o, s: (inv(s), 0)),
            coef, coef, coef, coef, coef, coef,
            pl.BlockSpec((None, 8, C), lambda bb, o, s: (o, 0, 0)),
            pl.BlockSpec((None, 3, 3 * C), lambda bb, o, s: (layer, 0, 0)),
            pl.BlockSpec((None, 1, 3 * C), lambda bb, o, s: (layer, 0, 0)),
            pl.BlockSpec((None, n_ord, C), lambda bb, o, s: (layer, 0, 0)),
        ],
        out_specs=pl.BlockSpec((None, S, C), lambda bb, o, s: (bb, 0, 0)),
        out_shape=jax.ShapeDtypeStruct((B, S, C), BF16),
        scratch_shapes=[
            pltpu.VMEM((2, H + 2 * PAD_ROWS, C), F32),
            pltpu.VMEM((H, 2 * C), F32),
            pltpu.VMEM((H, 2 * C), BF16),
            pltpu.VMEM((H, 2 * C), F32),
            pltpu.VMEM((2 * H, 2 * C), BF16),
            pltpu.VMEM((C // LANES, 2 * TT, LANES), F32),
        ],
        compiler_params=_cparams("arbitrary", "arbitrary", "arbitrary"),
        name="hyena_conv",
    )(hv, hx1, hx2, fhalf, ghalf, *coeffs, w_sconv, b_sconv.reshape(b_sconv.shape[0], 1, 3 * C), hy_bias)


def _packed_dft(bins, n_samples, n_fft, tile, sin_row0):
    theta = ((bins[:, None] * np.arange(n_samples)[None, :]) % n_fft) * (2.0 * math.pi / n_fft)
    fc = np.cos(theta)
    fs = -np.sin(theta)
    fs[0] = sin_row0
    nt = bins.shape[0] // tile
    packed = np.concatenate([fc.reshape(nt, tile, n_samples), fs.reshape(nt, tile, n_samples)], axis=1)
    return packed.astype(np.float32).astype(BF16)


@functools.lru_cache(maxsize=None)
def _dft_matrices(S):
    H, N = S // 2, 2 * S
    j = np.arange(H)
    m = np.arange(H)
    n = np.arange(S)
    quarter = np.asarray([1.0, 0.0, -1.0, 0.0])
    fhalf = _packed_dft(j, H, S, HYENA_FREQ_TILE, 1.0 - 2.0 * (m & 1))
    ghalf = np.ascontiguousarray(fhalf.reshape(2 * H, H).T)
    fa = _packed_dft(j, S, N, SPECTRA_TILE, quarter[n & 3])
    fb = _packed_dft(S - j, S, N, SPECTRA_TILE, -quarter[(n + 3) & 3])
    return fhalf, ghalf, fa, fb


def _filter_features(S):
    t = jnp.linspace(0.0, 1.0, S, dtype=F32)[:, None]
    bands = (FILTER_EMB - 1) // 2
    w = 2.0 * math.pi * jnp.arange(S, dtype=F32) / S
    f = jnp.linspace(1e-4, bands - 1, bands, dtype=F32)
    ang = w[:, None] * f[None, :]
    z = jnp.concatenate([t, jnp.cos(ang), -jnp.sin(ang)], axis=-1)
    z = jnp.pad(z, ((0, 0), (0, 128 - FILTER_EMB)))
    min_decay = math.log(DECAY_TARGET) / SLOW_DECAY_PCT
    max_decay = math.log(DECAY_TARGET) / FAST_DECAY_PCT
    deltas = jnp.linspace(min_decay, max_decay, HYENA_WIDTH, dtype=F32)
    decay = jnp.exp(-t * jnp.abs(deltas)[None, :])
    return z, decay


def _merge_kernel(ya_ref, yb_ref, yc_ref, gl_ref, x_ref, mod_ref, pa_ref, pp_ref, ph_ref, wo_ref,
                  g_ref, b_ref, wr_ref, br_ref, x1_ref, u2_ref, lt_ref):
    D = x_ref.shape[1]
    sig = lambda v: 1.0 / (1.0 + jnp.exp(-v))
    merged = sig(gl_ref[:, 0:D].astype(F32)) * _dot(ya_ref[...], pa_ref[...])
    merged += sig(gl_ref[:, D:2 * D].astype(F32)) * _dot(yb_ref[...], pp_ref[...])
    merged += sig(gl_ref[:, 2 * D:3 * D].astype(F32)) * _dot(yc_ref[...], ph_ref[...])
    h = _dot(merged.astype(BF16), wo_ref[...])
    gate1, shift2, scale2 = mod_ref[2:3, :], mod_ref[3:4, :], mod_ref[4:5, :]
    x1 = _layer_norm(ALPHA_RES * x_ref[...] + (1.0 + gate1) * h, g_ref[...], b_ref[...])
    x1_ref[...] = x1
    u2 = x1 * (1.0 + scale2) + shift2
    _store_token_tiles(u2_ref, u2)
    logits = _dot3(u2, wr_ref[...]) + br_ref[...]
    lt_ref[...] = logits.T[0:N_EXPERTS, :]


def _merge(ya, yb, yc, proj, x, mod, p_attn, p_pool, p_hyena, w_out, ln_g, ln_b, w_router, b_router, layer):
    B, S, D = x.shape
    N = B * S
    TM = MERGE_ROWS
    tiles_per_seq = S // TM
    EP = 128
    flat = lambda t: t.reshape(N, t.shape[-1])
    wr = jnp.zeros((D, EP), F32).at[:, :N_EXPERTS].set(w_router[layer])
    br = jnp.zeros((1, EP), F32).at[:, :N_EXPERTS].set(b_router[layer][None])
    tok = lambda w: pl.BlockSpec((TM, w), lambda i: (i, 0))
    lay = lambda t: pl.BlockSpec((None,) + t.shape[1:], lambda i: (layer,) + (0,) * (t.ndim - 1))
    ln_g3, ln_b3 = ln_g.reshape(-1, 1, D), ln_b.reshape(-1, 1, D)
    return pl.pallas_call(
        _merge_kernel,
        grid=(N // TM,),
        in_specs=[
            tok(ya.shape[-1]), tok(yb.shape[-1]), tok(yc.shape[-1]), tok(3 * D), tok(D),
            pl.BlockSpec((None, 6, D), lambda i: (i // tiles_per_seq, 0, 0)),
            lay(p_attn), lay(p_pool), lay(p_hyena), lay(w_out), lay(ln_g3), lay(ln_b3),
            pl.BlockSpec((D, EP), lambda i: (0, 0)),
            pl.BlockSpec((1, EP), lambda i: (0, 0)),
        ],
        out_specs=[tok(D), pl.BlockSpec((TM * D // LANES, LANES), lambda i: (i, 0)),
                   pl.BlockSpec((N_EXPERTS, TM), lambda i: (0, i))],
        out_shape=[
            jax.ShapeDtypeStruct((N, D), F32),
            jax.ShapeDtypeStruct((N * D // LANES, LANES), F32),
            jax.ShapeDtypeStruct((N_EXPERTS, N), F32),
        ],
        compiler_params=_cparams("arbitrary"),
        name="merge_ln1_router",
    )(flat(ya), flat(yb), flat(yc), flat(proj), flat(x), mod, p_attn, p_pool, p_hyena, w_out,
      ln_g3, ln_b3, wr, br)


def _topk_kernel(l_ref, e_ref, g_ref, r_ref, cnt_ref, carry_ref):
    E, TL = l_ref.shape

    @pl.when(pl.program_id(0) == 0)
    def _():
        carry_ref[...] = jnp.zeros_like(carry_ref)

    l = l_ref[...]
    eio = lax.broadcasted_iota(jnp.int32, (E, TL), 0).astype(F32)
    vals, hots = [], []
    for k in range(TOP_K):
        m = jnp.max(l, axis=0, keepdims=True)
        idx = jnp.min(jnp.where(l == m, eio, float(E)), axis=0, keepdims=True)
        hot = eio == idx
        l = jnp.where(hot, -jnp.inf, l)
        vals.append(m)
        hots.append(hot)
        e_ref[k:k + 1, :] = idx.astype(jnp.int32)
    exps = [jnp.exp(v - vals[0]) for v in vals]
    inv = 1.0 / (exps[0] + exps[1] + exps[2] + exps[3])
    for k in range(TOP_K):
        g_ref[k:k + 1, :] = exps[k] * inv

    memb = jnp.zeros((E, TL), F32)
    for hot in hots:
        memb = memb + hot.astype(F32)
    upper = (lax.broadcasted_iota(jnp.int32, (TL, TL), 0)
             <= lax.broadcasted_iota(jnp.int32, (TL, TL), 1)).astype(BF16)
    incl = _dot(memb.astype(BF16), upper)
    excl = incl - memb + carry_ref[:, 0:1]
    for k in range(TOP_K):
        rank = jnp.sum(jnp.where(hots[k], excl, 0.0), axis=0, keepdims=True)
        r_ref[k:k + 1, :] = rank.astype(jnp.int32)
    carry_ref[...] = carry_ref[...] + jnp.sum(memb, axis=1, keepdims=True)
    cnt_ref[...] = carry_ref[...]


def _topk(logits_t):
    E, N = logits_t.shape
    TL = TOPK_LANES
    tok = pl.BlockSpec((TOP_K, TL), lambda i: (0, i))
    return pl.pallas_call(
        _topk_kernel,
        grid=(N // TL,),
        in_specs=[pl.BlockSpec((E, TL), lambda i: (0, i))],
        out_specs=[tok, tok, tok, pl.BlockSpec((E, 128), lambda i: (0, 0))],
        out_shape=[
            jax.ShapeDtypeStruct((TOP_K, N), jnp.int32),
            jax.ShapeDtypeStruct((TOP_K, N), F32),
            jax.ShapeDtypeStruct((TOP_K, N), jnp.int32),
            jax.ShapeDtypeStruct((E, 128), F32),
        ],
        scratch_shapes=[pltpu.VMEM((E, 128), F32)],
        compiler_params=_cparams("arbitrary"),
        name="topk_rank",
    )(logits_t)


def _dest_kernel(offs_ref, e_ref, r_ref, d_ref):
    e = e_ref[...]
    acc = r_ref[...]
    for ex in range(N_EXPERTS):
        acc = acc + jnp.where(e == ex, offs_ref[ex], 0)
    d_ref[...] = acc


def _dest_rows(offs, top_e, rank):
    K, N = top_e.shape
    TL = TOPK_LANES
    tok = pl.BlockSpec((K, TL), lambda i, offs: (0, i))
    return pl.pallas_call(
        _dest_kernel,
        grid_spec=pltpu.PrefetchScalarGridSpec(
            num_scalar_prefetch=1, grid=(N // TL,), in_specs=[tok, tok], out_specs=tok),
        out_shape=jax.ShapeDtypeStruct((K, N), jnp.int32),
        compiler_params=_cparams("arbitrary"),
        name="dest_rows",
    )(offs, top_e, rank)


def _dispatch_kernel(tail_ref, n_used_ref, dest_ref, u_ref, xs_ref, zero_ref, sem):
    TT = dest_ref.shape[0] // TOP_K
    RT = u_ref.shape[0] // TT
    blk_rows = zero_ref.shape[0]
    n_blocks = xs_ref.shape[0] // blk_rows

    def zero_copy(first_row):
        start = pl.multiple_of(first_row * RT, RT)
        return pltpu.make_async_copy(zero_ref, xs_ref.at[pl.ds(start, blk_rows), :], sem)

    @pl.when(pl.program_id(0) == 0)
    def _():
        zero_ref[...] = jnp.zeros_like(zero_ref)

        def tails(fn):
            def body(e, carry):
                @pl.when(tail_ref[e] >= 0)
                def _():
                    fn(zero_copy(jnp.maximum(tail_ref[e], 0)))
                return carry
            lax.fori_loop(0, N_EXPERTS, body, 0)

        def unused(fn):
            def body(i, carry):
                fn(zero_copy(i * (blk_rows // RT)))
                return carry
            lax.fori_loop(n_used_ref[0], n_blocks, body, 0)

        tails(lambda cp: cp.start())
        unused(lambda cp: cp.start())
        tails(lambda cp: cp.wait())
        unused(lambda cp: cp.wait())

    def row_copy(j, k):
        src = pl.multiple_of(j * RT, RT)
        dst = pl.multiple_of(dest_ref[j * TOP_K + k] * RT, RT)
        return pltpu.make_async_copy(u_ref.at[pl.ds(src, RT), :], xs_ref.at[pl.ds(dst, RT), :], sem)

    def start(j, carry):
        for k in range(TOP_K):
            row_copy(j, k).start(priority=k % DMA_PRIORITIES)
        return carry

    lax.fori_loop(0, TT, start, 0, unroll=DMA_ISSUE_UNROLL)
    for k in range(TOP_K):
        pltpu.make_async_copy(u_ref, xs_ref.at[pl.ds(0, TT * RT), :], sem).wait()


def _dispatch(tail_start, n_used, dest, u2_tiles, n_rows, D):
    RT = D // LANES
    N = u2_tiles.shape[0] // RT
    TT = MOE_TOK
    return pl.pallas_call(
        _dispatch_kernel,
        grid_spec=pltpu.PrefetchScalarGridSpec(
            num_scalar_prefetch=2,
            grid=(N // TT,),
            in_specs=[
                pl.BlockSpec((TOP_K * TT,), lambda i, tail, nu: (i,), memory_space=pltpu.SMEM),
                pl.BlockSpec((TT * RT, LANES), lambda i, tail, nu: (i, 0)),
            ],
            out_specs=pl.BlockSpec(memory_space=pl.ANY),
            scratch_shapes=[pltpu.VMEM((MOE_BLK * RT, LANES), F32), pltpu.SemaphoreType.DMA(())],
        ),
        out_shape=jax.ShapeDtypeStruct((n_rows * RT, LANES), F32),
        compiler_params=_cparams("arbitrary"),
        name="moe_dispatch",
    )(tail_start, n_used, dest, u2_tiles)


def _ffn_kernel(blk_e_ref, n_used_ref, xs_ref, w1_ref, b1_ref, w2_ref, b2_ref, ys_ref,
                w1b_ref, w2b_ref, act_ref):
    F, D = w2_ref.shape
    BLK = xs_ref.shape[0] * LANES // D
    i = pl.program_id(0)
    used = i < n_used_ref[0]
    new_expert = (i == 0) | (blk_e_ref[i] != blk_e_ref[jnp.maximum(i - 1, 0)])

    @pl.when(used & new_expert)
    def _():
        for r in range(0, D, WEIGHT_CAST_ROWS):
            w1b_ref[r:r + WEIGHT_CAST_ROWS, :] = w1_ref[r:r + WEIGHT_CAST_ROWS, :].astype(BF16)
        for r in range(0, F, WEIGHT_CAST_ROWS):
            w2b_ref[r:r + WEIGHT_CAST_ROWS, :] = w2_ref[r:r + WEIGHT_CAST_ROWS, :].astype(BF16)

    @pl.when(used)
    def _():
        x = jnp.concatenate(_load_token_tiles(xs_ref, BLK, D), axis=1).astype(BF16)
        for c in range(0, F, FFN_CHUNK):
            cg = slice(c, c + FFN_CHUNK)
            cl = slice(F + c, F + c + FFN_CHUNK)
            glu = jnp.minimum(_dot(x, w1b_ref[:, cg]) + b1_ref[:, cg], SWIGLU_LIMIT)
            lin = jnp.clip(_dot(x, w1b_ref[:, cl]) + b1_ref[:, cl], -SWIGLU_LIMIT, SWIGLU_LIMIT)
            act = glu * (1.0 / (1.0 + jnp.exp(-SWIGLU_ALPHA * glu))) * (lin + 1.0)
            act_ref[:, cg] = act.astype(BF16)
        _store_token_tiles(ys_ref, _dot(act_ref[...], w2b_ref[...]) + b2_ref[...])

    @pl.when(jnp.logical_not(used))
    def _():
        ys_ref[...] = jnp.zeros_like(ys_ref)


def _expert_ffn(blk_e, n_used, xs_tiles, w1, b1, w2, b2, layer):
    _, E, D, F2 = w1.shape
    F = F2 // 2
    RT = D // LANES
    BLK = MOE_BLK
    rows = pl.BlockSpec((BLK * RT, LANES), lambda i, be, nu: (i, 0))
    per_expert = lambda r, c: pl.BlockSpec((None, None, r, c), lambda i, be, nu: (layer, be[i], 0, 0))
    return pl.pallas_call(
        _ffn_kernel,
        grid_spec=pltpu.PrefetchScalarGridSpec(
            num_scalar_prefetch=2,
            grid=(xs_tiles.shape[0] // (BLK * RT),),
            in_specs=[rows, per_expert(D, F2), per_expert(1, F2), per_expert(F, D), per_expert(1, D)],
            out_specs=rows,
            scratch_shapes=[pltpu.VMEM((D, F2), BF16), pltpu.VMEM((F, D), BF16), pltpu.VMEM((BLK, F), BF16)],
        ),
        out_shape=jax.ShapeDtypeStruct(xs_tiles.shape, F32),
        compiler_params=_cparams("arbitrary"),
        name="moe_ffn",
    )(blk_e, n_used, xs_tiles, w1, b1.reshape(-1, E, 1, F2), w2, b2.reshape(-1, E, 1, D))


def _combine_kernel(dest_ref, dest_next_ref, gt_ref, x1_ref, mod_ref, g_ref, b_ref, ys_ref, o_ref,
                    buf_a, buf_b, sem):
    TT, D = x1_ref.shape
    RT = D // LANES
    RC = TT // COMBINE_CHUNKS
    i = pl.program_id(0)
    last = pl.num_programs(0) - 1

    def row_copy(d_ref, buf, s, j, k):
        src = pl.multiple_of(d_ref[j * TOP_K + k] * RT, RT)
        dst = pl.multiple_of(j * RT, RT)
        return pltpu.make_async_copy(ys_ref.at[pl.ds(src, RT), :], buf.at[k, pl.ds(dst, RT), :], sem.at[s])

    def wait_tile(buf, s):
        for k in range(TOP_K):
            pltpu.make_async_copy(ys_ref.at[pl.ds(0, TT * RT), :], buf.at[k], sem.at[s]).wait()

    def reduce_rows(buf, r0):
        rows = slice(r0, r0 + RC)
        chunks = None
        for k in range(TOP_K):
            gate = gt_ref[rows, k:k + 1]
            tiles = [gate * buf[k, pl.ds(r0 * RT + c, RC, stride=RT), :] for c in range(RT)]
            chunks = tiles if chunks is None else [a + b for a, b in zip(chunks, tiles)]
        h = jnp.concatenate(chunks, axis=1)
        gate2 = mod_ref[5:6, :]
        o_ref[rows, :] = _layer_norm(ALPHA_RES * x1_ref[rows, :] + (1.0 + gate2) * h, g_ref[...], b_ref[...])

    def run(cur, s_cur, nxt, s_nxt):
        wait_tile(cur, s_cur)
        for ch in range(COMBINE_CHUNKS):
            reduce_rows(cur, ch * RC)
            for j in range(ch * RC, (ch + 1) * RC):
                for k in range(TOP_K):
                    row_copy(dest_next_ref, nxt, s_nxt, j, k).start(priority=k % DMA_PRIORITIES)

        @pl.when(i == last)
        def _():
            wait_tile(nxt, s_nxt)

    @pl.when(i == 0)
    def _():
        def body(j, carry):
            for k in range(TOP_K):
                row_copy(dest_ref, buf_a, 0, j, k).start(priority=k % DMA_PRIORITIES)
            return carry
        lax.fori_loop(0, TT, body, 0, unroll=DMA_ISSUE_UNROLL)

    @pl.when(i % 2 == 0)
    def _():
        run(buf_a, 0, buf_b, 1)

    @pl.when(i % 2 == 1)
    def _():
        run(buf_b, 1, buf_a, 0)


def _combine(dest, gates_t, x1, mod, ln_g, ln_b, ys, seq_len, layer):
    N, D = x1.shape
    TT = MOE_TOK
    tiles_per_seq = seq_len // TT
    ln_g3, ln_b3 = ln_g.reshape(-1, 1, D), ln_b.reshape(-1, 1, D)
    lay = lambda t: pl.BlockSpec((None,) + t.shape[1:], lambda i: (layer,) + (0,) * (t.ndim - 1))
    return pl.pallas_call(
        _combine_kernel,
        grid=(N // TT,),
        in_specs=[
            pl.BlockSpec((TOP_K * TT,), lambda i: (i,), memory_space=pltpu.SMEM),
            pl.BlockSpec((TOP_K * TT,), lambda i: (jnp.minimum(i + 1, N // TT - 1),), memory_space=pltpu.SMEM),
            pl.BlockSpec((TT, TOP_K), lambda i: (i, 0)),
            pl.BlockSpec((TT, D), lambda i: (i, 0)),
            pl.BlockSpec((None, 6, D), lambda i: (i // tiles_per_seq, 0, 0)),
            lay(ln_g3), lay(ln_b3),
            pl.BlockSpec(memory_space=pl.ANY),
        ],
        out_specs=pl.BlockSpec((TT, D), lambda i: (i, 0)),
        out_shape=jax.ShapeDtypeStruct((N, D), F32),
        scratch_shapes=[pltpu.VMEM((TOP_K, TT * D // LANES, LANES), F32)] * 2 + [pltpu.SemaphoreType.DMA((2,))],
        compiler_params=_cparams("arbitrary"),
        name="moe_combine_ln2",
    )(dest, dest, gates_t, x1, mod, ln_g3, ln_b3, ys)


def _moe(u2_tiles, logits_t, x1, mod, w1, b1, w2, b2, ln_g, ln_b, seq_len, layer):
    N, D = x1.shape
    BLK = MOE_BLK
    top_e, gates, rank, counts = _topk(logits_t)
    counts = counts[:, 0].astype(jnp.int32)
    padded = (counts + BLK - 1) // BLK * BLK
    pend = jnp.cumsum(padded)
    offs = pend - padded
    n_blocks = -(-(N * TOP_K) // BLK) + N_EXPERTS
    blk_start = jnp.arange(n_blocks, dtype=jnp.int32) * BLK
    blk_e = jnp.sum((pend[None, :] <= blk_start[:, None]).astype(jnp.int32), axis=1)
    blk_e = jnp.minimum(blk_e, N_EXPERTS - 1).astype(jnp.int32)
    n_used = (pend[-1:] // BLK).astype(jnp.int32)
    tail_start = jnp.where(padded > 0, pend - BLK, -1).astype(jnp.int32)
    dest = _dest_rows(offs.astype(jnp.int32), top_e, rank).T.reshape(-1)
    xs = _dispatch(tail_start, n_used, dest, u2_tiles, n_blocks * BLK, D)
    ys = _expert_ffn(blk_e, n_used, xs, w1, b1, w2, b2, layer)
    return _combine(dest, gates.T, x1, mod, ln_g, ln_b, ys, seq_len, layer)


def _split_w_in(w_in):
    AW = N_ATTN_HEADS * HEAD_DIM
    GW = GROUP_WIDTH
    C = HYENA_WIDTH
    w = w_in.astype(BF16)
    w_nat = jnp.concatenate([w[:, 3 * AW + 4 * C:], w[:, 3 * AW:3 * AW + C]], axis=1)
    w_qkv = jnp.stack([
        jnp.concatenate([w[:, t * AW + g * GW:t * AW + (g + 1) * GW] for t in range(3)], axis=1)
        for g in range(len(ATTN_GROUPS))])
    w_hy = jnp.stack([w[:, 3 * AW + (1 + t) * C:3 * AW + (2 + t) * C] for t in range(3)])
    return w_nat, w_qkv, w_hy


def kernel(x, c, w_ada, b_ada, w_in, w_sconv, b_sconv, w_pool, pool_scale, f_w1, f_b1, f_w_inner, f_b_inner, f_w_out, f_freq, hy_bias, p_attn, p_pool, p_hyena, w_out, ln1_g, ln1_b, w_router, b_router, w1, b1, w2, b2, ln2_g, ln2_b):
    B, S, D = x.shape
    depth = w_in.shape[0]
    bias_tables = [jnp.asarray(t) for t in _attn_bias_tables(S)]
    fhalf, ghalf, fa, fb = (jnp.asarray(m) for m in _dft_matrices(S))
    zfeat, decay = _filter_features(S)
    gate_cols = 3 * D
    pool_block = gate_cols // HYENA_WIDTH
    bf = lambda t: t.astype(BF16)
    p_attn_b, p_pool_b, p_hyena_b, w_out_b = map(bf, (p_attn, p_pool, p_hyena, w_out))

    for l in range(depth):
        mod = _ada(c, w_ada, b_ada, l).reshape(B, 6, D)
        w_nat, w_qkv, w_hy = _split_w_in(w_in[l])
        proj, u = _inproj_nat(x, mod, w_nat)
        qkv = _inproj_dil(u, w_qkv, [d for _, d in ATTN_GROUPS])
        hv, hx1, hx2 = _inproj_dil(u, w_hy, [2, 2, 2])
        ya = _attention(qkv, bias_tables)
        yb = _pool(proj, pool_block, w_pool, pool_scale, l)
        filt = _hyena_filters(zfeat, decay, f_w1, f_b1, f_w_inner, f_b_inner, f_w_out, f_freq, l)
        coeffs = _filter_spectra(fa, fb, filt)
        yc = _hyena(hv, hx1, hx2, fhalf, ghalf, coeffs, w_sconv, b_sconv, hy_bias, l)
        x1, u2, logits_t = _merge(ya, yb, yc, proj, x, mod, p_attn_b, p_pool_b, p_hyena_b, w_out_b,
                                  ln1_g, ln1_b, w_router, b_router, l)
        x = _moe(u2, logits_t, x1, mod, w1, b1, w2, b2, ln2_g, ln2_b, S, l).reshape(B, S, D)
    return x
```

```python
import functools
import math

import jax
import jax.numpy as jnp
import numpy as np
from jax import lax
from jax.experimental import pallas as pl
from jax.experimental.pallas import tpu as pltpu

F32 = jnp.float32
BF16 = jnp.bfloat16

HEAD_DIM = 64
HEADS_PER_GROUP = 4
ATTN_GROUPS = ((128, 1), (512, 4), (2048, 16))
N_ATTN_HEADS = HEADS_PER_GROUP * len(ATTN_GROUPS)
GROUP_WIDTH = HEADS_PER_GROUP * HEAD_DIM
HALF_WINDOW = 64
POOL_WINDOWS = (2, 4, 8, 16)
POOL_GROUP = 128
HYENA_WIDTH = 512
FILTER_EMB = 33
FILTER_HIDDEN = 64
N_EXPERTS = 32
TOP_K = 4
SWIGLU_LIMIT = 7.0
SWIGLU_ALPHA = 1.702
LN_EPS = 1e-5
DEPTH = 2
ALPHA_RES = (2 * DEPTH) ** 0.25
DECAY_TARGET = 1e-2
FAST_DECAY_PCT = 0.3
SLOW_DECAY_PCT = 1.5

QBLK = 128
HYENA_FREQ_TILE = 256
HYENA_FREQ_TILES_PER_STEP = 1
HYENA_TIME_TILE = 512
SPECTRA_TILE = 256
INPROJ_COLS = 1792
ATTN_UNROLL = 16
ROW_CHUNK = 256
MERGE_ROWS = 512
TOPK_LANES = 1024
MOE_BLK = 512
FFN_CHUNK = 512
WEIGHT_CAST_ROWS = 128
DMA_PRIORITIES = 2
DMA_ISSUE_UNROLL = 8
COMBINE_CHUNKS = 8
MOE_TOK = 256
PAD_ROWS = 8
NEG_BIG = -1e30
LANES = 128
VMEM_LIMIT = 56 * 1024 * 1024


def _cparams(*sem):
    return pltpu.CompilerParams(dimension_semantics=sem, vmem_limit_bytes=VMEM_LIMIT)


def _dot(a, b):
    return jnp.dot(a, b, preferred_element_type=F32)


def _dot3(a, b):
    a_hi = a.astype(BF16)
    b_hi = b.astype(BF16)
    a_lo = (a - a_hi.astype(F32)).astype(BF16)
    b_lo = (b - b_hi.astype(F32)).astype(BF16)
    return _dot(a_hi, b_hi) + _dot(a_hi, b_lo) + _dot(a_lo, b_hi)


def _layer_norm(r, g, b):
    mu = jnp.mean(r, axis=-1, keepdims=True)
    c = r - mu
    var = jnp.mean(c * c, axis=-1, keepdims=True)
    return c * lax.rsqrt(var + LN_EPS) * g + b


def _store_token_tiles(ref, val):
    T, D = val.shape
    n = D // LANES
    for c in range(n):
        ref[pl.ds(c, T, stride=n), :] = val[:, c * LANES:(c + 1) * LANES]


def _load_token_tiles(ref, T, D, plane=None):
    n = D // LANES
    lead = () if plane is None else tuple(plane)
    idx = lambda c: lead + (pl.ds(c, T, stride=n), slice(None))
    return [ref[idx(c)] for c in range(n)]


def _ada_kernel(c_ref, w_ref, b_ref, o_ref):
    c = c_ref[...]
    cond = c * (1.0 / (1.0 + jnp.exp(-c)))
    o_ref[...] = _dot3(cond, w_ref[...]) + b_ref[...]


def _ada(c, w_ada, b_ada, layer):
    B, D = c.shape
    n_out = w_ada.shape[-1]
    tn = 768
    return pl.pallas_call(
        _ada_kernel,
        grid=(n_out // tn,),
        in_specs=[
            pl.BlockSpec((B, D), lambda j: (0, 0)),
            pl.BlockSpec((None, D, tn), lambda j: (layer, 0, j)),
            pl.BlockSpec((None, 1, tn), lambda j: (layer, 0, j)),
        ],
        out_specs=pl.BlockSpec((B, tn), lambda j: (0, j)),
        out_shape=jax.ShapeDtypeStruct((B, n_out), F32),
        compiler_params=_cparams("arbitrary"),
        name="ada",
    )(c, w_ada, b_ada.reshape(b_ada.shape[0], 1, n_out))


def _inproj_nat_kernel(x_ref, mod_ref, w_ref, o_ref, u_ref):
    S = x_ref.shape[0]

    @pl.when(pl.program_id(1) == 0)
    def _():
        shift = mod_ref[0:1, :]
        scale1p = 1.0 + mod_ref[1:2, :]

        def body(i, carry):
            rows = pl.ds(pl.multiple_of(i * ROW_CHUNK, ROW_CHUNK), ROW_CHUNK)
            u_ref[rows, :] = (x_ref[rows, :] * scale1p + shift).astype(BF16)
            return carry

        lax.fori_loop(0, S // ROW_CHUNK, body, 0)

    o_ref[...] = _dot(u_ref[...], w_ref[...]).astype(BF16)


def _inproj_nat(x, mod, w_nat):
    B, S, D = x.shape
    n_out = w_nat.shape[1]
    tn = INPROJ_COLS
    return pl.pallas_call(
        _inproj_nat_kernel,
        grid=(B, n_out // tn),
        in_specs=[
            pl.BlockSpec((None, S, D), lambda b, j: (b, 0, 0)),
            pl.BlockSpec((None, 6, D), lambda b, j: (b, 0, 0)),
            pl.BlockSpec((D, tn), lambda b, j: (0, j)),
        ],
        out_specs=[
            pl.BlockSpec((None, S, tn), lambda b, j: (b, 0, j)),
            pl.BlockSpec((None, S, D), lambda b, j: (b, 0, 0)),
        ],
        out_shape=[
            jax.ShapeDtypeStruct((B, S, n_out), BF16),
            jax.ShapeDtypeStruct((B, S, D), BF16),
        ],
        compiler_params=_cparams("arbitrary", "arbitrary"),
        name="inproj_nat",
    )(x, mod, w_nat)


def _inproj_dil_kernel(u_ref, w_ref, o0_ref, o1_ref, o2_ref, acc_ref, *, dils):
    S = u_ref.shape[0]
    for g, o_ref in enumerate((o0_ref, o1_ref, o2_ref)):
        d = dils[g]
        L = S // d
        acc = _dot(u_ref[...], w_ref[g])
        if d == 1:
            o_ref[0] = acc.astype(BF16)
            continue
        for c in range(acc_ref.shape[0]):
            acc_ref[c] = acc[:, c * LANES:(c + 1) * LANES]
        for rho in range(d):
            for c in range(acc_ref.shape[0]):
                o_ref[rho, :, c * LANES:(c + 1) * LANES] = (
                    acc_ref[c, pl.ds(rho, L, stride=d), :].astype(BF16))


def _inproj_dil(u, w_groups, dils):
    B, S, D = u.shape
    W = w_groups.shape[-1]
    assert len(dils) == w_groups.shape[0] == 3
    return pl.pallas_call(
        functools.partial(_inproj_dil_kernel, dils=tuple(dils)),
        grid=(B,),
        in_specs=[
            pl.BlockSpec((None, S, D), lambda b: (b, 0, 0)),
            pl.BlockSpec((len(dils), D, W), lambda b: (0, 0, 0)),
        ],
        out_specs=[pl.BlockSpec((None, d, S // d, W), lambda b: (b, 0, 0, 0)) for d in dils],
        out_shape=[jax.ShapeDtypeStruct((B, d, S // d, W), BF16) for d in dils],
        scratch_shapes=[pltpu.VMEM((W // LANES, S, LANES), F32)],
        compiler_params=_cparams("arbitrary"),
        name="inproj_dil",
    )(u, w_groups)


def _window_start(i, L, W):
    return min(max(QBLK * i - HALF_WINDOW, 0), L - W)


def _attn_bias_tables(S):
    h = np.arange(1, N_ATTN_HEADS + 1, dtype=np.float32)
    slopes = (2.0 ** (-8.0 * h / N_ATTN_HEADS)).astype(np.float32)
    tables = []
    for g, (_, d) in enumerate(ATTN_GROUPS):
        L = S // d
        W = min(2 * QBLK, L)
        nb = L // QBLK
        variants = sorted({_window_start(i, L, W) - QBLK * i for i in range(nb)}, reverse=True)
        r = np.arange(QBLK)[:, None]
        c = np.arange(W)[None, :]
        tab = np.empty((len(variants), HEADS_PER_GROUP * QBLK, W), np.float32)
        for vi, delta in enumerate(variants):
            off = np.abs(c - r + delta)
            for hh in range(HEADS_PER_GROUP):
                slope = np.float64(slopes[g * HEADS_PER_GROUP + hh])
                bias = np.where(off <= HALF_WINDOW, -slope * (d * off), NEG_BIG)
                tab[vi, hh * QBLK:(hh + 1) * QBLK] = bias.astype(np.float32)
        tables.append(tab)
    return tables


def _block_attention(q, kwin, vwin, bias, head_of_lane):
    zero = jnp.zeros_like(q)
    q4 = jnp.concatenate([jnp.where(head_of_lane == h, q, zero) for h in range(HEADS_PER_GROUP)], axis=0)
    s = lax.dot_general(q4, kwin, (((1,), (1,)), ((), ())), preferred_element_type=F32)
    s = s * (1.0 / math.sqrt(HEAD_DIM)) + bias
    m = jnp.max(s, axis=-1, keepdims=True)
    p = jnp.exp(s - m)
    l = jnp.sum(p, axis=-1, keepdims=True)
    o4 = _dot(p.astype(BF16), vwin) * (1.0 / l)
    lse4 = m + jnp.log(l)
    out = jnp.zeros((QBLK, GROUP_WIDTH), F32)
    lse = jnp.zeros((QBLK, GROUP_WIDTH), F32)
    for h in range(HEADS_PER_GROUP):
        rows = slice(h * QBLK, (h + 1) * QBLK)
        sel = head_of_lane == h
        out = jnp.where(sel, o4[rows], out)
        lse = jnp.where(sel, lse4[rows], lse)
    return out, lse


def _attn_kernel(q0_ref, q1_ref, q2_ref, b0_ref, b1_ref, b2_ref, o_ref, acc_ref, lse_ref):
    S = o_ref.shape[0]
    GW = GROUP_WIDTH
    head_of_lane = lax.broadcasted_iota(jnp.int32, (QBLK, GW), 1) // HEAD_DIM
    n_planes = GW // LANES

    def put(g, rows, out, lse):
        for c in range(n_planes):
            acc_ref[g, c, rows, :] = out[:, c * LANES:(c + 1) * LANES]
            lse_ref[g, c, rows, :] = lse[:, c * LANES:(c + 1) * LANES]

    L0 = S
    nb0 = L0 // QBLK

    def g0_body(i, carry):
        ws = pl.multiple_of(jnp.clip(QBLK * i - HALF_WINDOW, 0, L0 - 2 * QBLK), HALF_WINDOW)
        var = jnp.where(i == 0, 0, jnp.where(i == nb0 - 1, 2, 1))
        rows = pl.ds(pl.multiple_of(i * QBLK, QBLK), QBLK)
        out, lse = _block_attention(
            q0_ref[0, rows, 0:GW],
            q0_ref[0, pl.ds(ws, 2 * QBLK), GW:2 * GW],
            q0_ref[0, pl.ds(ws, 2 * QBLK), 2 * GW:3 * GW],
            b0_ref[var], head_of_lane)
        put(0, rows, out, lse)
        return carry

    lax.fori_loop(0, nb0, g0_body, 0, unroll=ATTN_UNROLL)

    for g, (q_ref, b_ref) in ((1, (q1_ref, b1_ref)), (2, (q2_ref, b2_ref))):
        d = ATTN_GROUPS[g][1]
        L = S // d
        W = min(2 * QBLK, L)
        nb = L // QBLK
        deltas = sorted({_window_start(i, L, W) - QBLK * i for i in range(nb)}, reverse=True)

        def gd_body(rho, carry, q_ref=q_ref, b_ref=b_ref, d=d, L=L, W=W, nb=nb, deltas=deltas, g=g):
            for i in range(nb):
                ws = _window_start(i, L, W)
                var = deltas.index(ws - QBLK * i)
                out, lse = _block_attention(
                    q_ref[rho, i * QBLK:(i + 1) * QBLK, 0:GW],
                    q_ref[rho, ws:ws + W, GW:2 * GW],
                    q_ref[rho, ws:ws + W, 2 * GW:3 * GW],
                    b_ref[var], head_of_lane)
                dst = pl.ds(rho + i * QBLK * d, QBLK, stride=d)
                put(g, dst, out, lse)
            return carry

        lax.fori_loop(0, d, gd_body, 0, unroll=max(1, ATTN_UNROLL // nb))

    def merge_body(i, carry):
        rows = pl.ds(pl.multiple_of(i * ROW_CHUNK, ROW_CHUNK), ROW_CHUNK)
        for c in range(n_planes):
            l0, l1, l2 = lse_ref[0, c, rows, :], lse_ref[1, c, rows, :], lse_ref[2, c, rows, :]
            mx = jnp.maximum(jnp.maximum(l0, l1), l2)
            w0, w1, w2 = jnp.exp(l0 - mx), jnp.exp(l1 - mx), jnp.exp(l2 - mx)
            num = w0 * acc_ref[0, c, rows, :] + w1 * acc_ref[1, c, rows, :] + w2 * acc_ref[2, c, rows, :]
            o_ref[rows, c * LANES:(c + 1) * LANES] = (num * (1.0 / (w0 + w1 + w2))).astype(BF16)
        return carry

    lax.fori_loop(0, S // ROW_CHUNK, merge_body, 0)


def _attention(qkv, bias_tables):
    q0, q1, q2 = qkv
    B, _, S, W3 = q0.shape
    GW = GROUP_WIDTH
    in_specs = [pl.BlockSpec((None,) + q.shape[1:], lambda b: (b, 0, 0, 0)) for q in qkv]
    in_specs += [pl.BlockSpec(t.shape, lambda b: (0, 0, 0)) for t in bias_tables]
    return pl.pallas_call(
        _attn_kernel,
        grid=(B,),
        in_specs=in_specs,
        out_specs=pl.BlockSpec((None, S, GW), lambda b: (b, 0, 0)),
        out_shape=jax.ShapeDtypeStruct((B, S, GW), BF16),
        scratch_shapes=[pltpu.VMEM((3, GW // LANES, S, LANES), F32)] * 2,
        compiler_params=_cparams("arbitrary"),
        name="attention",
    )(q0, q1, q2, *bias_tables)


def _pool_kernel(y_ref, w_ref, sc_ref, o_ref, pad_ref):
    S = y_ref.shape[0]
    C = y_ref.shape[1]
    P = PAD_ROWS
    pad_ref[0:P, :] = jnp.zeros((P, C), F32)
    pad_ref[S + P:S + 2 * P, :] = jnp.zeros((P, C), F32)
    pad_ref[P:S + P, :] = y_ref[...].astype(F32)
    for c in range(S // ROW_CHUNK):
        r0 = c * ROW_CHUNK
        t = r0 + lax.broadcasted_iota(jnp.int32, (ROW_CHUNK, 1), 0)
        for g, w in enumerate(POOL_WINDOWS):
            cols = slice(g * POOL_GROUP, (g + 1) * POOL_GROUP)
            acc = pad_ref[P + r0 - w // 2:P + r0 - w // 2 + ROW_CHUNK, cols]
            for j in range(-w // 2 + 1, w // 2):
                acc = acc + pad_ref[P + r0 + j:P + r0 + j + ROW_CHUNK, cols]
            cnt = (jnp.minimum(t + w // 2, S) - jnp.maximum(t - w // 2, 0)).astype(F32)
            pooled = acc / cnt - pad_ref[P + r0:P + r0 + ROW_CHUNK, cols]
            mixed = _dot(pooled.astype(BF16), w_ref[g].astype(BF16)) * sc_ref[:, cols]
            o_ref[r0:r0 + ROW_CHUNK, cols] = mixed.astype(BF16)


def _pool(proj, col_block, w_pool, pool_scale, layer):
    B, S, _ = proj.shape
    C = POOL_GROUP * len(POOL_WINDOWS)
    return pl.pallas_call(
        _pool_kernel,
        grid=(B,),
        in_specs=[
            pl.BlockSpec((None, S, C), lambda b: (b, 0, col_block)),
            pl.BlockSpec((None,) + w_pool.shape[1:], lambda b: (layer, 0, 0, 0)),
            pl.BlockSpec((None, 1, C), lambda b: (layer, 0, 0)),
        ],
        out_specs=pl.BlockSpec((None, S, C), lambda b: (b, 0, 0)),
        out_shape=jax.ShapeDtypeStruct((B, S, C), BF16),
        scratch_shapes=[pltpu.VMEM((S + 2 * PAD_ROWS, C), F32)],
        compiler_params=_cparams("arbitrary"),
        name="pool",
    )(proj, w_pool, pool_scale.reshape(pool_scale.shape[0], 1, C))


def _filter_kernel(z_ref, w1_ref, b1_ref, wi_ref, bi_ref, wo_ref, fr_ref, dec_ref, o_ref):
    S = z_ref.shape[0]
    C = HYENA_WIDTH
    freq = fr_ref[...]
    h = jnp.sin(freq * (_dot3(z_ref[...], w1_ref[...]) + b1_ref[...]))
    for i in range(wi_ref.shape[0]):
        h = jnp.sin(freq * (_dot3(h, wi_ref[i]) + bi_ref[i]))
    not_first = lax.broadcasted_iota(jnp.int32, (S, 1), 0) > 0
    for blk in range(wo_ref.shape[1] // C):
        cols = slice(blk * C, (blk + 1) * C)
        f = _dot3(h, wo_ref[:, cols]) * dec_ref[...]
        if blk % 2 == 1:
            f = jnp.where(not_first, f, 0.0)
        o_ref[:, cols] = f.astype(BF16)


def _hyena_filters(zfeat, decay, f_w1, f_b1, f_w_inner, f_b_inner, f_w_out, f_freq, layer):
    S, E = zfeat.shape
    H = FILTER_HIDDEN
    n_out = f_w_out.shape[-1]
    n_inner = f_w_inner.shape[1]
    w1 = jnp.zeros((f_w1.shape[0], E, H), F32).at[:, :f_w1.shape[1]].set(f_w1)
    full = lambda shape: pl.BlockSpec(shape, lambda i: (0,) * len(shape))
    lay = lambda shape: pl.BlockSpec((None,) + shape, lambda i: (layer,) + (0,) * len(shape))
    return pl.pallas_call(
        _filter_kernel,
        grid=(1,),
        in_specs=[
            full((S, E)), lay((E, H)), lay((1, H)), lay((n_inner, H, H)), lay((n_inner, 1, H)),
            lay((H, n_out)), lay((1, H)), full((S, HYENA_WIDTH)),
        ],
        out_specs=full((S, n_out)),
        out_shape=jax.ShapeDtypeStruct((S, n_out), BF16),
        compiler_params=_cparams("arbitrary"),
        name="hyena_filter",
    )(zfeat, w1, f_b1.reshape(-1, 1, H), f_w_inner, f_b_inner.reshape(-1, n_inner, 1, H), f_w_out,
      f_freq.reshape(-1, 1, H), decay)


def _spec_kernel(fa_ref, fb_ref, h_ref, pre_ref, pim_ref, q1re_ref, q1im_ref, q2re_ref, q2im_ref, sp0_ref):
    TK = fa_ref.shape[0] // 2
    C = HYENA_WIDTH
    n_fft = 2 * h_ref.shape[0]
    xa = _dot(fa_ref[...], h_ref[...])
    xb = _dot(fb_ref[...], h_ref[...])
    row = pl.program_id(0) * TK + lax.broadcasted_iota(jnp.int32, (TK, 1), 0)
    theta = row.astype(F32) * (2.0 * math.pi / n_fft)
    cj, sj = jnp.cos(theta), jnp.sin(theta)
    sc = 2.0 / n_fft
    for o in range(pre_ref.shape[0]):
        hf = slice(2 * o * C, (2 * o + 1) * C)
        hb = slice((2 * o + 1) * C, (2 * o + 2) * C)
        ha_re = xa[:TK, hf] + xa[:TK, hb]
        ha_im = xa[TK:, hf] - xa[TK:, hb]
        hm_re = xb[:TK, hf] + xb[:TK, hb]
        hm_im = xb[TK:, hf] - xb[TK:, hb]
        p_re, p_im = (ha_re + hm_re) * sc, (ha_im - hm_im) * sc
        q_re, q_im = (ha_re - hm_re) * sc, (ha_im + hm_im) * sc
        pre_ref[o] = p_re
        pim_ref[o] = p_im
        q1re_ref[o] = q_re * cj + q_im * sj
        q1im_ref[o] = q_im * cj - q_re * sj
        q2re_ref[o] = q_re * cj - q_im * sj
        q2im_ref[o] = q_im * cj + q_re * sj

        @pl.when(pl.program_id(0) == 0)
        def _():
            h0, hs = ha_re[0:1], hm_re[0:1]
            hh_re = xa[TK:TK + 1, hf] + xa[TK:TK + 1, hb]
            hh_im = xb[TK:TK + 1, hf] - xb[TK:TK + 1, hb]
            sp0_ref[o] = jnp.concatenate(
                [(h0 + hs) * (0.5 * sc), (h0 - hs) * (0.5 * sc), hh_re * sc, hh_im * sc,
                 jnp.zeros((4, C), F32)], axis=0)


def _filter_spectra(fa, fb, filt):
    nkt, tk2, S = fa.shape
    TK = tk2 // 2
    C = HYENA_WIDTH
    n_ord = filt.shape[1] // (2 * C)
    out = jax.ShapeDtypeStruct((n_ord, nkt * TK, C), F32)
    ospec = pl.BlockSpec((n_ord, TK, C), lambda k: (0, k, 0))
    fspec = pl.BlockSpec((None, tk2, S), lambda k: (k, 0, 0))
    return pl.pallas_call(
        _spec_kernel,
        grid=(nkt,),
        in_specs=[fspec, fspec, pl.BlockSpec(filt.shape, lambda k: (0, 0))],
        out_specs=[ospec] * 6 + [pl.BlockSpec((n_ord, 8, C), lambda k: (0, 0, 0))],
        out_shape=[out] * 6 + [jax.ShapeDtypeStruct((n_ord, 8, C), F32)],
        compiler_params=_cparams("arbitrary"),
        name="hyena_spectra",
    )(fa, fb, filt)


def _hyena_kernel(v_ref, x1_ref, x2_ref, f_ref, g_ref, pre_ref, pim_ref, q1re_ref, q1im_ref, q2re_ref, q2im_ref,
                  sp0_ref, wsc_ref, bsc_ref, hb_ref, o_ref, pad_ref, z_ref, zb_ref, gate_ref, zcat_ref, il_ref):
    H, C2 = z_ref.shape
    C = C2 // 2
    n_sub, tk2, _ = f_ref.shape
    TK = tk2 // 2
    TT = g_ref.shape[0]
    nf = H // (n_sub * TK)
    R = ROW_CHUNK
    P = PAD_ROWS
    o = pl.program_id(1)
    s = pl.program_id(2)
    even, odd = slice(0, C), slice(C, C2)

    def fill_pad(in_ref):
        for par in range(2):
            pad_ref[par, 0:P, :] = jnp.zeros((P, C), F32)
            pad_ref[par, H + P:H + 2 * P, :] = jnp.zeros((P, C), F32)
            pad_ref[par, P:H + P, :] = in_ref[par].astype(F32)

    def conv_chunk(r0, w, bias):
        ev = pad_ref[0, P + r0:P + r0 + R, :]
        od = pad_ref[1, P + r0:P + r0 + R, :]
        od_prev = pad_ref[1, P + r0 - 1:P + r0 - 1 + R, :]
        ev_next = pad_ref[0, P + r0 + 1:P + r0 + 1 + R, :]
        w0, w1, w2 = w[0:1], w[1:2], w[2:3]
        return w0 * od_prev + w1 * ev + w2 * od + bias, w0 * ev + w1 * od + w2 * ev_next + bias

    @pl.when((o == 0) & (s == 0))
    def _():
        fill_pad(v_ref)
        for c in range(H // R):
            rows = slice(c * R, (c + 1) * R)
            conv_e, conv_o = conv_chunk(c * R, wsc_ref[:, 0:C], bsc_ref[:, 0:C])
            z_ref[rows, even] = conv_e
            z_ref[rows, odd] = conv_o
            zb_ref[rows, even] = conv_e.astype(BF16)
            zb_ref[rows, odd] = conv_o.astype(BF16)

    @pl.when((s == 0) & (o == 0))
    def _():
        fill_pad(x1_ref)

    @pl.when((s == 0) & (o == 1))
    def _():
        fill_pad(x2_ref)

    @pl.when(s < nf)
    def _():
        a, b, c, d = sp0_ref[0:1, :], sp0_ref[1:2, :], sp0_ref[2:3, :], sp0_ref[3:4, :]
        for sub in range(n_sub):
            tile = s * n_sub + sub
            cs = slice(sub * TK, (sub + 1) * TK)
            x = _dot(f_ref[sub], zb_ref[...])
            e_re, o_re, e_im, o_im = x[:TK, even], x[:TK, odd], x[TK:, even], x[TK:, odd]
            p_re, p_im = pre_ref[cs, :], pim_ref[cs, :]
            q1_re, q1_im, q2_re, q2_im = q1re_ref[cs, :], q1im_ref[cs, :], q2re_ref[cs, :], q2im_ref[cs, :]
            ze_re = e_re * p_re - e_im * p_im + o_re * q1_re - o_im * q1_im
            ze_im = e_re * p_im + e_im * p_re + o_re * q1_im + o_im * q1_re
            zo_re = e_re * q2_re - e_im * q2_im + o_re * p_re - o_im * p_im
            zo_im = e_re * q2_im + e_im * q2_re + o_re * p_im + o_im * p_re
            first = (tile * TK + lax.broadcasted_iota(jnp.int32, (TK, 1), 0)) == 0
            ze_re = jnp.where(first, e_re[0:1] * a + o_re[0:1] * b, ze_re)
            zo_re = jnp.where(first, e_re[0:1] * b + o_re[0:1] * a, zo_re)
            ze_im = jnp.where(first, e_im[0:1] * c + o_im[0:1] * d, ze_im)
            zo_im = jnp.where(first, o_im[0:1] * c - e_im[0:1] * d, zo_im)
            base = pl.multiple_of(tile * 2 * TK, 2 * TK)
            zcat_ref[pl.ds(base, TK), even] = ze_re.astype(BF16)
            zcat_ref[pl.ds(base, TK), odd] = zo_re.astype(BF16)
            zcat_ref[pl.ds(base + TK, TK), even] = ze_im.astype(BF16)
            zcat_ref[pl.ds(base + TK, TK), odd] = zo_im.astype(BF16)

    @pl.when(s == nf)
    def _():
        w = jnp.where(o == 0, wsc_ref[:, C:2 * C], wsc_ref[:, 2 * C:3 * C])
        bias = jnp.where(o == 0, bsc_ref[:, C:2 * C], bsc_ref[:, 2 * C:3 * C])
        for c in range(H // R):
            conv_e, conv_o = conv_chunk(c * R, w, bias)
            gate_ref[c * R:(c + 1) * R, even] = conv_e
            gate_ref[c * R:(c + 1) * R, odd] = conv_o

    @pl.when(s >= nf)
    def _():
        tt = s - nf
        rows = pl.ds(pl.multiple_of(tt * TT, TT), TT)
        y = _dot(g_ref[...], zcat_ref[...])
        hrow = jnp.where(o == 0, hb_ref[0:1, :], hb_ref[1:2, :])
        hbias = jnp.concatenate([hrow, hrow], axis=1)
        znew = gate_ref[rows, :] * (y + z_ref[rows, :] * hbias)
        z_ref[rows, :] = znew
        zb_ref[rows, :] = znew.astype(BF16)

        @pl.when(o == pl.num_programs(1) - 1)
        def _():
            for p in range(C // LANES):
                il_ref[p, pl.ds(0, TT, stride=2), :] = znew[:, p * LANES:(p + 1) * LANES]
                il_ref[p, pl.ds(1, TT, stride=2), :] = znew[:, C + p * LANES:C + (p + 1) * LANES]
            out_rows = pl.ds(pl.multiple_of(tt * 2 * TT, 2 * TT), 2 * TT)
            for p in range(C // LANES):
                o_ref[out_rows, p * LANES:(p + 1) * LANES] = il_ref[p].astype(BF16)


def _hyena(hv, hx1, hx2, fhalf, ghalf, coeffs, w_sconv, b_sconv, hy_bias, layer):
    B, _, H, C = hv.shape
    S = 2 * H
    n_tiles, tk2, _ = fhalf.shape
    TK = tk2 // 2
    n_sub = HYENA_FREQ_TILES_PER_STEP
    nf = n_tiles // n_sub
    TT = HYENA_TIME_TILE
    ni = H // TT
    sp0 = coeffs[-1]
    n_ord = sp0.shape[0]
    fwd = lambda s: jnp.minimum(s, nf - 1)
    inv = lambda s: jnp.clip(s - nf, 0, ni - 1)
    inp = pl.BlockSpec((None, 2, H, C), lambda bb, o, s: (bb, 0, 0, 0), pipeline_mode=pl.Buffered(1))
    coef = pl.BlockSpec((None, n_sub * TK, C), lambda bb, o, s: (o, fwd(s), 0))
    return pl.pallas_call(
        _hyena_kernel,
        grid=(B, n_ord, nf + ni),
        in_specs=[
            inp, inp, inp,
            pl.BlockSpec((n_sub, tk2, H), lambda bb, o, s: (fwd(s), 0, 0)),
            pl.BlockSpec((TT, 2 * H), lambda bb, o, s: (inv(s), 0)),
            coef, coef, coef, coef, coef, coef,
            pl.BlockSpec((None, 8, C), lambda bb, o, s: (o, 0, 0)),
            pl.BlockSpec((None, 3, 3 * C), lambda bb, o, s: (layer, 0, 0)),
            pl.BlockSpec((None, 1, 3 * C), lambda bb, o, s: (layer, 0, 0)),
            pl.BlockSpec((None, n_ord, C), lambda bb, o, s: (layer, 0, 0)),
        ],
        out_specs=pl.BlockSpec((None, S, C), lambda bb, o, s: (bb, 0, 0)),
        out_shape=jax.ShapeDtypeStruct((B, S, C), BF16),
        scratch_shapes=[
            pltpu.VMEM((2, H + 2 * PAD_ROWS, C), F32),
            pltpu.VMEM((H, 2 * C), F32),
            pltpu.VMEM((H, 2 * C), BF16),
            pltpu.VMEM((H, 2 * C), F32),
            pltpu.VMEM((2 * H, 2 * C), BF16),
            pltpu.VMEM((C // LANES, 2 * TT, LANES), F32),
        ],
        compiler_params=_cparams("arbitrary", "arbitrary", "arbitrary"),
        name="hyena_conv",
    )(hv, hx1, hx2, fhalf, ghalf, *coeffs, w_sconv, b_sconv.reshape(b_sconv.shape[0], 1, 3 * C), hy_bias)


def _packed_dft(bins, n_samples, n_fft, tile, sin_row0):
    theta = ((bins[:, None] * np.arange(n_samples)[None, :]) % n_fft) * (2.0 * math.pi / n_fft)
    fc = np.cos(theta)
    fs = -np.sin(theta)
    fs[0] = sin_row0
    nt = bins.shape[0] // tile
    packed = np.concatenate([fc.reshape(nt, tile, n_samples), fs.reshape(nt, tile, n_samples)], axis=1)
    return packed.astype(np.float32).astype(BF16)


@functools.lru_cache(maxsize=None)
def _dft_matrices(S):
    H, N = S // 2, 2 * S
    j = np.arange(H)
    m = np.arange(H)
    n = np.arange(S)
    quarter = np.asarray([1.0, 0.0, -1.0, 0.0])
    fhalf = _packed_dft(j, H, S, HYENA_FREQ_TILE, 1.0 - 2.0 * (m & 1))
    ghalf = np.ascontiguousarray(fhalf.reshape(2 * H, H).T)
    fa = _packed_dft(j, S, N, SPECTRA_TILE, quarter[n & 3])
    fb = _packed_dft(S - j, S, N, SPECTRA_TILE, -quarter[(n + 3) & 3])
    return fhalf, ghalf, fa, fb


def _filter_features(S):
    t = jnp.linspace(0.0, 1.0, S, dtype=F32)[:, None]
    bands = (FILTER_EMB - 1) // 2
    w = 2.0 * math.pi * jnp.arange(S, dtype=F32) / S
    f = jnp.linspace(1e-4, bands - 1, bands, dtype=F32)
    ang = w[:, None] * f[None, :]
    z = jnp.concatenate([t, jnp.cos(ang), -jnp.sin(ang)], axis=-1)
    z = jnp.pad(z, ((0, 0), (0, 128 - FILTER_EMB)))
    min_decay = math.log(DECAY_TARGET) / SLOW_DECAY_PCT
    max_decay = math.log(DECAY_TARGET) / FAST_DECAY_PCT
    deltas = jnp.linspace(min_decay, max_decay, HYENA_WIDTH, dtype=F32)
    decay = jnp.exp(-t * jnp.abs(deltas)[None, :])
    return z, decay


def _merge_kernel(ya_ref, yb_ref, yc_ref, gl_ref, x_ref, mod_ref, pa_ref, pp_ref, ph_ref, wo_ref,
                  g_ref, b_ref, wr_ref, br_ref, x1_ref, u2_ref, lt_ref):
    D = x_ref.shape[1]
    sig = lambda v: 1.0 / (1.0 + jnp.exp(-v))
    merged = sig(gl_ref[:, 0:D].astype(F32)) * _dot(ya_ref[...], pa_ref[...])
    merged += sig(gl_ref[:, D:2 * D].astype(F32)) * _dot(yb_ref[...], pp_ref[...])
    merged += sig(gl_ref[:, 2 * D:3 * D].astype(F32)) * _dot(yc_ref[...], ph_ref[...])
    h = _dot(merged.astype(BF16), wo_ref[...])
    gate1, shift2, scale2 = mod_ref[2:3, :], mod_ref[3:4, :], mod_ref[4:5, :]
    x1 = _layer_norm(ALPHA_RES * x_ref[...] + (1.0 + gate1) * h, g_ref[...], b_ref[...])
    x1_ref[...] = x1
    u2 = x1 * (1.0 + scale2) + shift2
    _store_token_tiles(u2_ref, u2)
    TM = u2.shape[0]
    u_hi = u2.astype(BF16)
    u_lo = (u2 - u_hi.astype(F32)).astype(BF16)
    r = _dot(jnp.concatenate([u_hi, u_lo], axis=0), wr_ref[...])
    rt = (r[:TM] + r[TM:] + br_ref[...]).T
    lt_ref[...] = rt[0:N_EXPERTS, :] + rt[N_EXPERTS:2 * N_EXPERTS, :]


def _merge(ya, yb, yc, proj, x, mod, p_attn, p_pool, p_hyena, w_out, ln_g, ln_b, w_router, b_router, layer):
    B, S, D = x.shape
    N = B * S
    TM = MERGE_ROWS
    tiles_per_seq = S // TM
    EP = 128
    flat = lambda t: t.reshape(N, t.shape[-1])
    w_hi = w_router[layer].astype(BF16)
    w_lo = (w_router[layer] - w_hi.astype(F32)).astype(BF16)
    wr = jnp.zeros((D, EP), BF16).at[:, :N_EXPERTS].set(w_hi).at[:, N_EXPERTS:2 * N_EXPERTS].set(w_lo)
    br = jnp.zeros((1, EP), F32).at[:, :N_EXPERTS].set(b_router[layer][None])
    tok = lambda w: pl.BlockSpec((TM, w), lambda i: (i, 0))
    lay = lambda t: pl.BlockSpec((None,) + t.shape[1:], lambda i: (layer,) + (0,) * (t.ndim - 1))
    ln_g3, ln_b3 = ln_g.reshape(-1, 1, D), ln_b.reshape(-1, 1, D)
    return pl.pallas_call(
        _merge_kernel,
        grid=(N // TM,),
        in_specs=[
            tok(ya.shape[-1]), tok(yb.shape[-1]), tok(yc.shape[-1]), tok(3 * D), tok(D),
            pl.BlockSpec((None, 6, D), lambda i: (i // tiles_per_seq, 0, 0)),
            lay(p_attn), lay(p_pool), lay(p_hyena), lay(w_out), lay(ln_g3), lay(ln_b3),
            pl.BlockSpec((D, EP), lambda i: (0, 0)),
            pl.BlockSpec((1, EP), lambda i: (0, 0)),
        ],
        out_specs=[tok(D), pl.BlockSpec((TM * D // LANES, LANES), lambda i: (i, 0)),
                   pl.BlockSpec((N_EXPERTS, TM), lambda i: (0, i))],
        out_shape=[
            jax.ShapeDtypeStruct((N, D), F32),
            jax.ShapeDtypeStruct((N * D // LANES, LANES), F32),
            jax.ShapeDtypeStruct((N_EXPERTS, N), F32),
        ],
        compiler_params=_cparams("arbitrary"),
        name="merge_ln1_router",
    )(flat(ya), flat(yb), flat(yc), flat(proj), flat(x), mod, p_attn, p_pool, p_hyena, w_out,
      ln_g3, ln_b3, wr, br)


def _topk_kernel(l_ref, e_ref, g_ref, r_ref, cnt_ref, carry_ref):
    E, TL = l_ref.shape

    @pl.when(pl.program_id(0) == 0)
    def _():
        carry_ref[...] = jnp.zeros_like(carry_ref)

    l = l_ref[...]
    eio = lax.broadcasted_iota(jnp.int32, (E, TL), 0).astype(F32)
    vals, hots = [], []
    for k in range(TOP_K):
        m = jnp.max(l, axis=0, keepdims=True)
        idx = jnp.min(jnp.where(l == m, eio, float(E)), axis=0, keepdims=True)
        hot = eio == idx
        l = jnp.where(hot, -jnp.inf, l)
        vals.append(m)
        hots.append(hot)
        e_ref[k:k + 1, :] = idx.astype(jnp.int32)
    exps = [jnp.exp(v - vals[0]) for v in vals]
    inv = 1.0 / (exps[0] + exps[1] + exps[2] + exps[3])
    for k in range(TOP_K):
        g_ref[k:k + 1, :] = exps[k] * inv

    memb = jnp.zeros((E, TL), F32)
    for hot in hots:
        memb = memb + hot.astype(F32)
    upper = (lax.broadcasted_iota(jnp.int32, (TL, TL), 0)
             <= lax.broadcasted_iota(jnp.int32, (TL, TL), 1)).astype(BF16)
    incl = _dot(memb.astype(BF16), upper)
    excl = incl - memb + carry_ref[:, 0:1]
    for k in range(TOP_K):
        rank = jnp.sum(jnp.where(hots[k], excl, 0.0), axis=0, keepdims=True)
        r_ref[k:k + 1, :] = rank.astype(jnp.int32)
    carry_ref[...] = carry_ref[...] + jnp.sum(memb, axis=1, keepdims=True)
    cnt_ref[...] = carry_ref[...]


def _topk(logits_t):
    E, N = logits_t.shape
    TL = TOPK_LANES
    tok = pl.BlockSpec((TOP_K, TL), lambda i: (0, i))
    return pl.pallas_call(
        _topk_kernel,
        grid=(N // TL,),
        in_specs=[pl.BlockSpec((E, TL), lambda i: (0, i))],
        out_specs=[tok, tok, tok, pl.BlockSpec((E, 128), lambda i: (0, 0))],
        out_shape=[
            jax.ShapeDtypeStruct((TOP_K, N), jnp.int32),
            jax.ShapeDtypeStruct((TOP_K, N), F32),
            jax.ShapeDtypeStruct((TOP_K, N), jnp.int32),
            jax.ShapeDtypeStruct((E, 128), F32),
        ],
        scratch_shapes=[pltpu.VMEM((E, 128), F32)],
        compiler_params=_cparams("arbitrary"),
        name="topk_rank",
    )(logits_t)


def _dest_kernel(offs_ref, e_ref, r_ref, d_ref):
    e = e_ref[...]
    acc = r_ref[...]
    for ex in range(N_EXPERTS):
        acc = acc + jnp.where(e == ex, offs_ref[ex], 0)
    d_ref[...] = acc


def _dest_rows(offs, top_e, rank):
    K, N = top_e.shape
    TL = TOPK_LANES
    tok = pl.BlockSpec((K, TL), lambda i, offs: (0, i))
    return pl.pallas_call(
        _dest_kernel,
        grid_spec=pltpu.PrefetchScalarGridSpec(
            num_scalar_prefetch=1, grid=(N // TL,), in_specs=[tok, tok], out_specs=tok),
        out_shape=jax.ShapeDtypeStruct((K, N), jnp.int32),
        compiler_params=_cparams("arbitrary"),
        name="dest_rows",
    )(offs, top_e, rank)


def _dispatch_kernel(tail_ref, n_used_ref, dest_ref, u_ref, xs_ref, zero_ref, sem):
    TT = dest_ref.shape[0] // TOP_K
    RT = u_ref.shape[0] // TT
    blk_rows = zero_ref.shape[0]
    n_blocks = xs_ref.shape[0] // blk_rows

    def zero_copy(first_row):
        start = pl.multiple_of(first_row * RT, RT)
        return pltpu.make_async_copy(zero_ref, xs_ref.at[pl.ds(start, blk_rows), :], sem)

    @pl.when(pl.program_id(0) == 0)
    def _():
        zero_ref[...] = jnp.zeros_like(zero_ref)

        def tails(fn):
            def body(e, carry):
                @pl.when(tail_ref[e] >= 0)
                def _():
                    fn(zero_copy(jnp.maximum(tail_ref[e], 0)))
                return carry
            lax.fori_loop(0, N_EXPERTS, body, 0)

        def unused(fn):
            def body(i, carry):
                fn(zero_copy(i * (blk_rows // RT)))
                return carry
            lax.fori_loop(n_used_ref[0], n_blocks, body, 0)

        tails(lambda cp: cp.start())
        unused(lambda cp: cp.start())
        tails(lambda cp: cp.wait())
        unused(lambda cp: cp.wait())

    def row_copy(j, k):
        src = pl.multiple_of(j * RT, RT)
        dst = pl.multiple_of(dest_ref[j * TOP_K + k] * RT, RT)
        return pltpu.make_async_copy(u_ref.at[pl.ds(src, RT), :], xs_ref.at[pl.ds(dst, RT), :], sem)

    def start(j, carry):
        for k in range(TOP_K):
            row_copy(j, k).start(priority=k % DMA_PRIORITIES)
        return carry

    lax.fori_loop(0, TT, start, 0, unroll=DMA_ISSUE_UNROLL)
    for k in range(TOP_K):
        pltpu.make_async_copy(u_ref, xs_ref.at[pl.ds(0, TT * RT), :], sem).wait()


def _dispatch(tail_start, n_used, dest, u2_tiles, n_rows, D):
    RT = D // LANES
    N = u2_tiles.shape[0] // RT
    TT = MOE_TOK
    return pl.pallas_call(
        _dispatch_kernel,
        grid_spec=pltpu.PrefetchScalarGridSpec(
            num_scalar_prefetch=2,
            grid=(N // TT,),
            in_specs=[
                pl.BlockSpec((TOP_K * TT,), lambda i, tail, nu: (i,), memory_space=pltpu.SMEM),
                pl.BlockSpec((TT * RT, LANES), lambda i, tail, nu: (i, 0)),
            ],
            out_specs=pl.BlockSpec(memory_space=pl.ANY),
            scratch_shapes=[pltpu.VMEM((MOE_BLK * RT, LANES), F32), pltpu.SemaphoreType.DMA(())],
        ),
        out_shape=jax.ShapeDtypeStruct((n_rows * RT, LANES), F32),
        compiler_params=_cparams("arbitrary"),
        name="moe_dispatch",
    )(tail_start, n_used, dest, u2_tiles)


def _ffn_kernel(blk_e_ref, n_used_ref, xs_ref, w1_ref, b1_ref, w2_ref, b2_ref, ys_ref,
                w1b_ref, w2b_ref, act_ref):
    F, D = w2_ref.shape
    BLK = xs_ref.shape[0] * LANES // D
    i = pl.program_id(0)
    used = i < n_used_ref[0]
    new_expert = (i == 0) | (blk_e_ref[i] != blk_e_ref[jnp.maximum(i - 1, 0)])

    @pl.when(used & new_expert)
    def _():
        for r in range(0, D, WEIGHT_CAST_ROWS):
            w1b_ref[r:r + WEIGHT_CAST_ROWS, :] = w1_ref[r:r + WEIGHT_CAST_ROWS, :].astype(BF16)
        for r in range(0, F, WEIGHT_CAST_ROWS):
            w2b_ref[r:r + WEIGHT_CAST_ROWS, :] = w2_ref[r:r + WEIGHT_CAST_ROWS, :].astype(BF16)

    @pl.when(used)
    def _():
        x = jnp.concatenate(_load_token_tiles(xs_ref, BLK, D), axis=1).astype(BF16)
        for c in range(0, F, FFN_CHUNK):
            cg = slice(c, c + FFN_CHUNK)
            cl = slice(F + c, F + c + FFN_CHUNK)
            glu = jnp.minimum(_dot(x, w1b_ref[:, cg]) + b1_ref[:, cg], SWIGLU_LIMIT)
            lin = jnp.clip(_dot(x, w1b_ref[:, cl]) + b1_ref[:, cl], -SWIGLU_LIMIT, SWIGLU_LIMIT)
            act = glu * (1.0 / (1.0 + jnp.exp(-SWIGLU_ALPHA * glu))) * (lin + 1.0)
            act_ref[:, cg] = act.astype(BF16)
        _store_token_tiles(ys_ref, _dot(act_ref[...], w2b_ref[...]) + b2_ref[...])

    @pl.when(jnp.logical_not(used))
    def _():
        ys_ref[...] = jnp.zeros_like(ys_ref)


def _expert_ffn(blk_e, n_used, xs_tiles, w1, b1, w2, b2, layer):
    _, E, D, F2 = w1.shape
    F = F2 // 2
    RT = D // LANES
    BLK = MOE_BLK
    rows = pl.BlockSpec((BLK * RT, LANES), lambda i, be, nu: (i, 0))
    per_expert = lambda r, c: pl.BlockSpec((None, None, r, c), lambda i, be, nu: (layer, be[i], 0, 0))
    return pl.pallas_call(
        _ffn_kernel,
        grid_spec=pltpu.PrefetchScalarGridSpec(
            num_scalar_prefetch=2,
            grid=(xs_tiles.shape[0] // (BLK * RT),),
            in_specs=[rows, per_expert(D, F2), per_expert(1, F2), per_expert(F, D), per_expert(1, D)],
            out_specs=rows,
            scratch_shapes=[pltpu.VMEM((D, F2), BF16), pltpu.VMEM((F, D), BF16), pltpu.VMEM((BLK, F), BF16)],
        ),
        out_shape=jax.ShapeDtypeStruct(xs_tiles.shape, F32),
        compiler_params=_cparams("arbitrary"),
        name="moe_ffn",
    )(blk_e, n_used, xs_tiles, w1, b1.reshape(-1, E, 1, F2), w2, b2.reshape(-1, E, 1, D))


def _combine_kernel(dest_ref, dest_next_ref, gt_ref, x1_ref, mod_ref, g_ref, b_ref, ys_ref, o_ref,
                    buf_a, buf_b, sem):
    TT, D = x1_ref.shape
    RT = D // LANES
    RC = TT // COMBINE_CHUNKS
    i = pl.program_id(0)
    last = pl.num_programs(0) - 1

    def row_copy(d_ref, buf, s, j, k):
        src = pl.multiple_of(d_ref[j * TOP_K + k] * RT, RT)
        dst = pl.multiple_of(j * RT, RT)
        return pltpu.make_async_copy(ys_ref.at[pl.ds(src, RT), :], buf.at[k, pl.ds(dst, RT), :], sem.at[s])

    def wait_tile(buf, s):
        for k in range(TOP_K):
            pltpu.make_async_copy(ys_ref.at[pl.ds(0, TT * RT), :], buf.at[k], sem.at[s]).wait()

    def reduce_rows(buf, r0):
        rows = slice(r0, r0 + RC)
        chunks = None
        for k in range(TOP_K):
            gate = gt_ref[rows, k:k + 1]
            tiles = [gate * buf[k, pl.ds(r0 * RT + c, RC, stride=RT), :] for c in range(RT)]
            chunks = tiles if chunks is None else [a + b for a, b in zip(chunks, tiles)]
        h = jnp.concatenate(chunks, axis=1)
        gate2 = mod_ref[5:6, :]
        o_ref[rows, :] = _layer_norm(ALPHA_RES * x1_ref[rows, :] + (1.0 + gate2) * h, g_ref[...], b_ref[...])

    def run(cur, s_cur, nxt, s_nxt):
        wait_tile(cur, s_cur)
        for ch in range(COMBINE_CHUNKS):
            reduce_rows(cur, ch * RC)
            for j in range(ch * RC, (ch + 1) * RC):
                for k in range(TOP_K):
                    row_copy(dest_next_ref, nxt, s_nxt, j, k).start(priority=k % DMA_PRIORITIES)

        @pl.when(i == last)
        def _():
            wait_tile(nxt, s_nxt)

    @pl.when(i == 0)
    def _():
        def body(j, carry):
            for k in range(TOP_K):
                row_copy(dest_ref, buf_a, 0, j, k).start(priority=k % DMA_PRIORITIES)
            return carry
        lax.fori_loop(0, TT, body, 0, unroll=DMA_ISSUE_UNROLL)

    @pl.when(i % 2 == 0)
    def _():
        run(buf_a, 0, buf_b, 1)

    @pl.when(i % 2 == 1)
    def _():
        run(buf_b, 1, buf_a, 0)


def _combine(dest, gates_t, x1, mod, ln_g, ln_b, ys, seq_len, layer):
    N, D = x1.shape
    TT = MOE_TOK
    tiles_per_seq = seq_len // TT
    ln_g3, ln_b3 = ln_g.reshape(-1, 1, D), ln_b.reshape(-1, 1, D)
    lay = lambda t: pl.BlockSpec((None,) + t.shape[1:], lambda i: (layer,) + (0,) * (t.ndim - 1))
    return pl.pallas_call(
        _combine_kernel,
        grid=(N // TT,),
        in_specs=[
            pl.BlockSpec((TOP_K * TT,), lambda i: (i,), memory_space=pltpu.SMEM),
            pl.BlockSpec((TOP_K * TT,), lambda i: (jnp.minimum(i + 1, N // TT - 1),), memory_space=pltpu.SMEM),
            pl.BlockSpec((TT, TOP_K), lambda i: (i, 0)),
            pl.BlockSpec((TT, D), lambda i: (i, 0)),
            pl.BlockSpec((None, 6, D), lambda i: (i // tiles_per_seq, 0, 0)),
            lay(ln_g3), lay(ln_b3),
            pl.BlockSpec(memory_space=pl.ANY),
        ],
        out_specs=pl.BlockSpec((TT, D), lambda i: (i, 0)),
        out_shape=jax.ShapeDtypeStruct((N, D), F32),
        scratch_shapes=[pltpu.VMEM((TOP_K, TT * D // LANES, LANES), F32)] * 2 + [pltpu.SemaphoreType.DMA((2,))],
        compiler_params=_cparams("arbitrary"),
        name="moe_combine_ln2",
    )(dest, dest, gates_t, x1, mod, ln_g3, ln_b3, ys)


def _moe(u2_tiles, logits_t, x1, mod, w1, b1, w2, b2, ln_g, ln_b, seq_len, layer):
    N, D = x1.shape
    BLK = MOE_BLK
    top_e, gates, rank, counts = _topk(logits_t)
    counts = counts[:, 0].astype(jnp.int32)
    padded = (counts + BLK - 1) // BLK * BLK
    pend = jnp.cumsum(padded)
    offs = pend - padded
    n_blocks = -(-(N * TOP_K) // BLK) + N_EXPERTS
    blk_start = jnp.arange(n_blocks, dtype=jnp.int32) * BLK
    blk_e = jnp.sum((pend[None, :] <= blk_start[:, None]).astype(jnp.int32), axis=1)
    blk_e = jnp.minimum(blk_e, N_EXPERTS - 1).astype(jnp.int32)
    n_used = (pend[-1:] // BLK).astype(jnp.int32)
    tail_start = jnp.where(padded > 0, pend - BLK, -1).astype(jnp.int32)
    dest = _dest_rows(offs.astype(jnp.int32), top_e, rank).T.reshape(-1)
    xs = _dispatch(tail_start, n_used, dest, u2_tiles, n_blocks * BLK, D)
    ys = _expert_ffn(blk_e, n_used, xs, w1, b1, w2, b2, layer)
    return _combine(dest, gates.T, x1, mod, ln_g, ln_b, ys, seq_len, layer)


def _split_w_in(w_in):
    AW = N_ATTN_HEADS * HEAD_DIM
    GW = GROUP_WIDTH
    C = HYENA_WIDTH
    w = w_in.astype(BF16)
    w_nat = jnp.concatenate([w[:, 3 * AW + 4 * C:], w[:, 3 * AW:3 * AW + C]], axis=1)
    w_qkv = jnp.stack([
        jnp.concatenate([w[:, t * AW + g * GW:t * AW + (g + 1) * GW] for t in range(3)], axis=1)
        for g in range(len(ATTN_GROUPS))])
    w_hy = jnp.stack([w[:, 3 * AW + (1 + t) * C:3 * AW + (2 + t) * C] for t in range(3)])
    return w_nat, w_qkv, w_hy


def kernel(x, c, w_ada, b_ada, w_in, w_sconv, b_sconv, w_pool, pool_scale, f_w1, f_b1, f_w_inner, f_b_inner, f_w_out, f_freq, hy_bias, p_attn, p_pool, p_hyena, w_out, ln1_g, ln1_b, w_router, b_router, w1, b1, w2, b2, ln2_g, ln2_b):
    B, S, D = x.shape
    depth = w_in.shape[0]
    bias_tables = [jnp.asarray(t) for t in _attn_bias_tables(S)]
    fhalf, ghalf, fa, fb = (jnp.asarray(m) for m in _dft_matrices(S))
    zfeat, decay = _filter_features(S)
    gate_cols = 3 * D
    pool_block = gate_cols // HYENA_WIDTH
    bf = lambda t: t.astype(BF16)
    p_attn_b, p_pool_b, p_hyena_b, w_out_b = map(bf, (p_attn, p_pool, p_hyena, w_out))

    for l in range(depth):
        mod = _ada(c, w_ada, b_ada, l).reshape(B, 6, D)
        w_nat, w_qkv, w_hy = _split_w_in(w_in[l])
        proj, u = _inproj_nat(x, mod, w_nat)
        qkv = _inproj_dil(u, w_qkv, [d for _, d in ATTN_GROUPS])
        hv, hx1, hx2 = _inproj_dil(u, w_hy, [2, 2, 2])
        ya = _attention(qkv, bias_tables)
        yb = _pool(proj, pool_block, w_pool, pool_scale, l)
        filt = _hyena_filters(zfeat, decay, f_w1, f_b1, f_w_inner, f_b_inner, f_w_out, f_freq, l)
        coeffs = _filter_spectra(fa, fb, filt)
        yc = _hyena(hv, hx1, hx2, fhalf, ghalf, coeffs, w_sconv, b_sconv, hy_bias, l)
        x1, u2, logits_t = _merge(ya, yb, yc, proj, x, mod, p_attn_b, p_pool_b, p_hyena_b, w_out_b,
                                  ln1_g, ln1_b, w_router, b_router, l)
        x = _moe(u2, logits_t, x1, mod, w1, b1, w2, b2, ln2_g, ln2_b, S, l).reshape(B, S, D)
    return x
```

```python
import functools
import math

import jax
import jax.numpy as jnp
import numpy as np
from jax import lax
from jax.experimental import pallas as pl
from jax.experimental.pallas import tpu as pltpu

F32 = jnp.float32
BF16 = jnp.bfloat16

HEAD_DIM = 64
HEADS_PER_GROUP = 4
ATTN_GROUPS = ((128, 1), (512, 4), (2048, 16))
N_ATTN_HEADS = HEADS_PER_GROUP * len(ATTN_GROUPS)
GROUP_WIDTH = HEADS_PER_GROUP * HEAD_DIM
HALF_WINDOW = 64
POOL_WINDOWS = (2, 4, 8, 16)
POOL_GROUP = 128
HYENA_WIDTH = 512
FILTER_EMB = 33
FILTER_HIDDEN = 64
N_EXPERTS = 32
TOP_K = 4
SWIGLU_LIMIT = 7.0
SWIGLU_ALPHA = 1.702
LN_EPS = 1e-5
DEPTH = 2
ALPHA_RES = (2 * DEPTH) ** 0.25
DECAY_TARGET = 1e-2
FAST_DECAY_PCT = 0.3
SLOW_DECAY_PCT = 1.5

QBLK = 128
HYENA_FREQ_TILE = 256
HYENA_FREQ_TILES_PER_STEP = 1
HYENA_TIME_TILE = 512
SPECTRA_TILE = 256
INPROJ_COLS = 1792
ATTN_UNROLL = 16
ROW_CHUNK = 256
MERGE_ROWS = 512
TOPK_LANES = 1024
MOE_BLK = 512
FFN_CHUNK = 256
WEIGHT_CAST_ROWS = 128
DMA_PRIORITIES = 2
DMA_ISSUE_UNROLL = 8
COMBINE_CHUNKS = 8
MOE_TOK = 256
PAD_ROWS = 8
NEG_BIG = -1e30
LANES = 128
VMEM_LIMIT = 56 * 1024 * 1024


def _cparams(*sem):
    return pltpu.CompilerParams(dimension_semantics=sem, vmem_limit_bytes=VMEM_LIMIT)


def _dot(a, b):
    return jnp.dot(a, b, preferred_element_type=F32)


def _dot3(a, b):
    a_hi = a.astype(BF16)
    b_hi = b.astype(BF16)
    a_lo = (a - a_hi.astype(F32)).astype(BF16)
    b_lo = (b - b_hi.astype(F32)).astype(BF16)
    return _dot(a_hi, b_hi) + _dot(a_hi, b_lo) + _dot(a_lo, b_hi)


def _layer_norm(r, g, b):
    mu = jnp.mean(r, axis=-1, keepdims=True)
    c = r - mu
    var = jnp.mean(c * c, axis=-1, keepdims=True)
    return c * lax.rsqrt(var + LN_EPS) * g + b


def _store_token_tiles(ref, val):
    T, D = val.shape
    n = D // LANES
    for c in range(n):
        ref[pl.ds(c, T, stride=n), :] = val[:, c * LANES:(c + 1) * LANES]


def _load_token_tiles(ref, T, D, plane=None):
    n = D // LANES
    lead = () if plane is None else tuple(plane)
    idx = lambda c: lead + (pl.ds(c, T, stride=n), slice(None))
    return [ref[idx(c)] for c in range(n)]


def _ada_kernel(c_ref, w_ref, b_ref, o_ref):
    c = c_ref[...]
    cond = c * (1.0 / (1.0 + jnp.exp(-c)))
    o_ref[...] = _dot3(cond, w_ref[...]) + b_ref[...]


def _ada(c, w_ada, b_ada, layer):
    B, D = c.shape
    n_out = w_ada.shape[-1]
    tn = 768
    return pl.pallas_call(
        _ada_kernel,
        grid=(n_out // tn,),
        in_specs=[
            pl.BlockSpec((B, D), lambda j: (0, 0)),
            pl.BlockSpec((None, D, tn), lambda j: (layer, 0, j)),
            pl.BlockSpec((None, 1, tn), lambda j: (layer, 0, j)),
        ],
        out_specs=pl.BlockSpec((B, tn), lambda j: (0, j)),
        out_shape=jax.ShapeDtypeStruct((B, n_out), F32),
        compiler_params=_cparams("arbitrary"),
        name="ada",
    )(c, w_ada, b_ada.reshape(b_ada.shape[0], 1, n_out))


def _inproj_nat_kernel(x_ref, mod_ref, w_ref, o_ref, u_ref):
    S = x_ref.shape[0]

    @pl.when(pl.program_id(1) == 0)
    def _():
        shift = mod_ref[0:1, :]
        scale1p = 1.0 + mod_ref[1:2, :]

        def body(i, carry):
            rows = pl.ds(pl.multiple_of(i * ROW_CHUNK, ROW_CHUNK), ROW_CHUNK)
            u_ref[rows, :] = (x_ref[rows, :] * scale1p + shift).astype(BF16)
            return carry

        lax.fori_loop(0, S // ROW_CHUNK, body, 0)

    o_ref[...] = _dot(u_ref[...], w_ref[...]).astype(BF16)


def _inproj_nat(x, mod, w_nat):
    B, S, D = x.shape
    n_out = w_nat.shape[1]
    tn = INPROJ_COLS
    return pl.pallas_call(
        _inproj_nat_kernel,
        grid=(B, n_out // tn),
        in_specs=[
            pl.BlockSpec((None, S, D), lambda b, j: (b, 0, 0)),
            pl.BlockSpec((None, 6, D), lambda b, j: (b, 0, 0)),
            pl.BlockSpec((D, tn), lambda b, j: (0, j)),
        ],
        out_specs=[
            pl.BlockSpec((None, S, tn), lambda b, j: (b, 0, j)),
            pl.BlockSpec((None, S, D), lambda b, j: (b, 0, 0)),
        ],
        out_shape=[
            jax.ShapeDtypeStruct((B, S, n_out), BF16),
            jax.ShapeDtypeStruct((B, S, D), BF16),
        ],
        compiler_params=_cparams("arbitrary", "arbitrary"),
        name="inproj_nat",
    )(x, mod, w_nat)


def _inproj_dil_kernel(u_ref, w_ref, o0_ref, o1_ref, o2_ref, acc_ref, *, dils):
    S = u_ref.shape[0]
    for g, o_ref in enumerate((o0_ref, o1_ref, o2_ref)):
        d = dils[g]
        L = S // d
        acc = _dot(u_ref[...], w_ref[g])
        if d == 1:
            o_ref[0] = acc.astype(BF16)
            continue
        for c in range(acc_ref.shape[0]):
            acc_ref[c] = acc[:, c * LANES:(c + 1) * LANES]
        for rho in range(d):
            for c in range(acc_ref.shape[0]):
                o_ref[rho, :, c * LANES:(c + 1) * LANES] = (
                    acc_ref[c, pl.ds(rho, L, stride=d), :].astype(BF16))


def _inproj_dil(u, w_groups, dils):
    B, S, D = u.shape
    W = w_groups.shape[-1]
    assert len(dils) == w_groups.shape[0] == 3
    return pl.pallas_call(
        functools.partial(_inproj_dil_kernel, dils=tuple(dils)),
        grid=(B,),
        in_specs=[
            pl.BlockSpec((None, S, D), lambda b: (b, 0, 0)),
            pl.BlockSpec((len(dils), D, W), lambda b: (0, 0, 0)),
        ],
        out_specs=[pl.BlockSpec((None, d, S // d, W), lambda b: (b, 0, 0, 0)) for d in dils],
        out_shape=[jax.ShapeDtypeStruct((B, d, S // d, W), BF16) for d in dils],
        scratch_shapes=[pltpu.VMEM((W // LANES, S, LANES), F32)],
        compiler_params=_cparams("arbitrary"),
        name="inproj_dil",
    )(u, w_groups)


def _window_start(i, L, W):
    return min(max(QBLK * i - HALF_WINDOW, 0), L - W)


def _attn_bias_tables(S):
    h = np.arange(1, N_ATTN_HEADS + 1, dtype=np.float32)
    slopes = (2.0 ** (-8.0 * h / N_ATTN_HEADS)).astype(np.float32)
    tables = []
    for g, (_, d) in enumerate(ATTN_GROUPS):
        L = S // d
        W = min(2 * QBLK, L)
        nb = L // QBLK
        variants = sorted({_window_start(i, L, W) - QBLK * i for i in range(nb)}, reverse=True)
        r = np.arange(QBLK)[:, None]
        c = np.arange(W)[None, :]
        tab = np.empty((len(variants), HEADS_PER_GROUP * QBLK, W), np.float32)
        for vi, delta in enumerate(variants):
            off = np.abs(c - r + delta)
            for hh in range(HEADS_PER_GROUP):
                slope = np.float64(slopes[g * HEADS_PER_GROUP + hh])
                bias = np.where(off <= HALF_WINDOW, -slope * (d * off), NEG_BIG)
                tab[vi, hh * QBLK:(hh + 1) * QBLK] = bias.astype(np.float32)
        tables.append(tab)
    return tables


def _block_attention(q, kwin, vwin, bias, head_of_lane):
    zero = jnp.zeros_like(q)
    q4 = jnp.concatenate([jnp.where(head_of_lane == h, q, zero) for h in range(HEADS_PER_GROUP)], axis=0)
    s = lax.dot_general(q4, kwin, (((1,), (1,)), ((), ())), preferred_element_type=F32)
    s = s + bias
    m = jnp.max(s, axis=-1, keepdims=True)
    p = jnp.exp(s - m)
    l = jnp.sum(p, axis=-1, keepdims=True)
    o4 = _dot(p.astype(BF16), vwin) * (1.0 / l)
    lse4 = m + jnp.log(l)
    out = jnp.zeros((QBLK, GROUP_WIDTH), F32)
    lse = jnp.zeros((QBLK, GROUP_WIDTH), F32)
    for h in range(HEADS_PER_GROUP):
        rows = slice(h * QBLK, (h + 1) * QBLK)
        sel = head_of_lane == h
        out = jnp.where(sel, o4[rows], out)
        lse = jnp.where(sel, lse4[rows], lse)
    return out, lse


def _attn_kernel(q0_ref, q1_ref, q2_ref, b0_ref, b1_ref, b2_ref, o_ref, acc_ref, lse_ref):
    S = o_ref.shape[0]
    GW = GROUP_WIDTH
    head_of_lane = lax.broadcasted_iota(jnp.int32, (QBLK, GW), 1) // HEAD_DIM
    n_planes = GW // LANES

    def put(g, rows, out, lse):
        for c in range(n_planes):
            acc_ref[g, c, rows, :] = out[:, c * LANES:(c + 1) * LANES]
            lse_ref[g, c, rows, :] = lse[:, c * LANES:(c + 1) * LANES]

    L0 = S
    nb0 = L0 // QBLK

    def g0_body(i, carry):
        ws = pl.multiple_of(jnp.clip(QBLK * i - HALF_WINDOW, 0, L0 - 2 * QBLK), HALF_WINDOW)
        var = jnp.where(i == 0, 0, jnp.where(i == nb0 - 1, 2, 1))
        rows = pl.ds(pl.multiple_of(i * QBLK, QBLK), QBLK)
        out, lse = _block_attention(
            q0_ref[0, rows, 0:GW],
            q0_ref[0, pl.ds(ws, 2 * QBLK), GW:2 * GW],
            q0_ref[0, pl.ds(ws, 2 * QBLK), 2 * GW:3 * GW],
            b0_ref[var], head_of_lane)
        put(0, rows, out, lse)
        return carry

    lax.fori_loop(0, nb0, g0_body, 0, unroll=ATTN_UNROLL)

    for g, (q_ref, b_ref) in ((1, (q1_ref, b1_ref)), (2, (q2_ref, b2_ref))):
        d = ATTN_GROUPS[g][1]
        L = S // d
        W = min(2 * QBLK, L)
        nb = L // QBLK
        deltas = sorted({_window_start(i, L, W) - QBLK * i for i in range(nb)}, reverse=True)

        def gd_body(rho, carry, q_ref=q_ref, b_ref=b_ref, d=d, L=L, W=W, nb=nb, deltas=deltas, g=g):
            for i in range(nb):
                ws = _window_start(i, L, W)
                var = deltas.index(ws - QBLK * i)
                out, lse = _block_attention(
                    q_ref[rho, i * QBLK:(i + 1) * QBLK, 0:GW],
                    q_ref[rho, ws:ws + W, GW:2 * GW],
                    q_ref[rho, ws:ws + W, 2 * GW:3 * GW],
                    b_ref[var], head_of_lane)
                dst = pl.ds(rho + i * QBLK * d, QBLK, stride=d)
                put(g, dst, out, lse)
            return carry

        lax.fori_loop(0, d, gd_body, 0, unroll=max(1, ATTN_UNROLL // nb))

    def merge_body(i, carry):
        rows = pl.ds(pl.multiple_of(i * ROW_CHUNK, ROW_CHUNK), ROW_CHUNK)
        for c in range(n_planes):
            l0, l1, l2 = lse_ref[0, c, rows, :], lse_ref[1, c, rows, :], lse_ref[2, c, rows, :]
            mx = jnp.maximum(jnp.maximum(l0, l1), l2)
            w0, w1, w2 = jnp.exp(l0 - mx), jnp.exp(l1 - mx), jnp.exp(l2 - mx)
            num = w0 * acc_ref[0, c, rows, :] + w1 * acc_ref[1, c, rows, :] + w2 * acc_ref[2, c, rows, :]
            o_ref[rows, c * LANES:(c + 1) * LANES] = (num * (1.0 / (w0 + w1 + w2))).astype(BF16)
        return carry

    lax.fori_loop(0, S // ROW_CHUNK, merge_body, 0)


def _attention(qkv, bias_tables):
    q0, q1, q2 = qkv
    B, _, S, W3 = q0.shape
    GW = GROUP_WIDTH
    in_specs = [pl.BlockSpec((None,) + q.shape[1:], lambda b: (b, 0, 0, 0)) for q in qkv]
    in_specs += [pl.BlockSpec(t.shape, lambda b: (0, 0, 0)) for t in bias_tables]
    return pl.pallas_call(
        _attn_kernel,
        grid=(B,),
        in_specs=in_specs,
        out_specs=pl.BlockSpec((None, S, GW), lambda b: (b, 0, 0)),
        out_shape=jax.ShapeDtypeStruct((B, S, GW), BF16),
        scratch_shapes=[pltpu.VMEM((3, GW // LANES, S, LANES), F32)] * 2,
        compiler_params=_cparams("arbitrary"),
        name="attention",
    )(q0, q1, q2, *bias_tables)


def _pool_kernel(y_ref, w_ref, sc_ref, o_ref, pad_ref):
    S = y_ref.shape[0]
    C = y_ref.shape[1]
    P = PAD_ROWS
    pad_ref[0:P, :] = jnp.zeros((P, C), F32)
    pad_ref[S + P:S + 2 * P, :] = jnp.zeros((P, C), F32)
    pad_ref[P:S + P, :] = y_ref[...].astype(F32)
    for c in range(S // ROW_CHUNK):
        r0 = c * ROW_CHUNK
        t = r0 + lax.broadcasted_iota(jnp.int32, (ROW_CHUNK, 1), 0)
        for g, w in enumerate(POOL_WINDOWS):
            cols = slice(g * POOL_GROUP, (g + 1) * POOL_GROUP)
            acc = pad_ref[P + r0 - w // 2:P + r0 - w // 2 + ROW_CHUNK, cols]
            for j in range(-w // 2 + 1, w // 2):
                acc = acc + pad_ref[P + r0 + j:P + r0 + j + ROW_CHUNK, cols]
            cnt = (jnp.minimum(t + w // 2, S) - jnp.maximum(t - w // 2, 0)).astype(F32)
            pooled = acc / cnt - pad_ref[P + r0:P + r0 + ROW_CHUNK, cols]
            mixed = _dot(pooled.astype(BF16), w_ref[g].astype(BF16)) * sc_ref[:, cols]
            o_ref[r0:r0 + ROW_CHUNK, cols] = mixed.astype(BF16)


def _pool(proj, col_block, w_pool, pool_scale, layer):
    B, S, _ = proj.shape
    C = POOL_GROUP * len(POOL_WINDOWS)
    return pl.pallas_call(
        _pool_kernel,
        grid=(B,),
        in_specs=[
            pl.BlockSpec((None, S, C), lambda b: (b, 0, col_block)),
            pl.BlockSpec((None,) + w_pool.shape[1:], lambda b: (layer, 0, 0, 0)),
            pl.BlockSpec((None, 1, C), lambda b: (layer, 0, 0)),
        ],
        out_specs=pl.BlockSpec((None, S, C), lambda b: (b, 0, 0)),
        out_shape=jax.ShapeDtypeStruct((B, S, C), BF16),
        scratch_shapes=[pltpu.VMEM((S + 2 * PAD_ROWS, C), F32)],
        compiler_params=_cparams("arbitrary"),
        name="pool",
    )(proj, w_pool, pool_scale.reshape(pool_scale.shape[0], 1, C))


def _filter_kernel(z_ref, w1_ref, b1_ref, wi_ref, bi_ref, wo_ref, fr_ref, dec_ref, o_ref):
    S = z_ref.shape[0]
    C = HYENA_WIDTH
    freq = fr_ref[...]
    h = jnp.sin(freq * (_dot3(z_ref[...], w1_ref[...]) + b1_ref[...]))
    for i in range(wi_ref.shape[0]):
        h = jnp.sin(freq * (_dot3(h, wi_ref[i]) + bi_ref[i]))
    not_first = lax.broadcasted_iota(jnp.int32, (S, 1), 0) > 0
    for blk in range(wo_ref.shape[1] // C):
        cols = slice(blk * C, (blk + 1) * C)
        f = _dot3(h, wo_ref[:, cols]) * dec_ref[...]
        if blk % 2 == 1:
            f = jnp.where(not_first, f, 0.0)
        o_ref[:, cols] = f.astype(BF16)


def _hyena_filters(zfeat, decay, f_w1, f_b1, f_w_inner, f_b_inner, f_w_out, f_freq, layer):
    S, E = zfeat.shape
    H = FILTER_HIDDEN
    n_out = f_w_out.shape[-1]
    n_inner = f_w_inner.shape[1]
    w1 = jnp.zeros((f_w1.shape[0], E, H), F32).at[:, :f_w1.shape[1]].set(f_w1)
    full = lambda shape: pl.BlockSpec(shape, lambda i: (0,) * len(shape))
    lay = lambda shape: pl.BlockSpec((None,) + shape, lambda i: (layer,) + (0,) * len(shape))
    return pl.pallas_call(
        _filter_kernel,
        grid=(1,),
        in_specs=[
            full((S, E)), lay((E, H)), lay((1, H)), lay((n_inner, H, H)), lay((n_inner, 1, H)),
            lay((H, n_out)), lay((1, H)), full((S, HYENA_WIDTH)),
        ],
        out_specs=full((S, n_out)),
        out_shape=jax.ShapeDtypeStruct((S, n_out), BF16),
        compiler_params=_cparams("arbitrary"),
        name="hyena_filter",
    )(zfeat, w1, f_b1.reshape(-1, 1, H), f_w_inner, f_b_inner.reshape(-1, n_inner, 1, H), f_w_out,
      f_freq.reshape(-1, 1, H), decay)


def _spec_kernel(fa_ref, fb_ref, h_ref, pre_ref, pim_ref, q1re_ref, q1im_ref, q2re_ref, q2im_ref, sp0_ref):
    TK = fa_ref.shape[0] // 2
    C = HYENA_WIDTH
    n_fft = 2 * h_ref.shape[0]
    xa = _dot(fa_ref[...], h_ref[...])
    xb = _dot(fb_ref[...], h_ref[...])
    row = pl.program_id(0) * TK + lax.broadcasted_iota(jnp.int32, (TK, 1), 0)
    theta = row.astype(F32) * (2.0 * math.pi / n_fft)
    cj, sj = jnp.cos(theta), jnp.sin(theta)
    sc = 2.0 / n_fft
    for o in range(pre_ref.shape[0]):
        hf = slice(2 * o * C, (2 * o + 1) * C)
        hb = slice((2 * o + 1) * C, (2 * o + 2) * C)
        ha_re = xa[:TK, hf] + xa[:TK, hb]
        ha_im = xa[TK:, hf] - xa[TK:, hb]
        hm_re = xb[:TK, hf] + xb[:TK, hb]
        hm_im = xb[TK:, hf] - xb[TK:, hb]
        p_re, p_im = (ha_re + hm_re) * sc, (ha_im - hm_im) * sc
        q_re, q_im = (ha_re - hm_re) * sc, (ha_im + hm_im) * sc
        pre_ref[o] = p_re
        pim_ref[o] = p_im
        q1re_ref[o] = q_re * cj + q_im * sj
        q1im_ref[o] = q_im * cj - q_re * sj
        q2re_ref[o] = q_re * cj - q_im * sj
        q2im_ref[o] = q_im * cj + q_re * sj

        @pl.when(pl.program_id(0) == 0)
        def _():
            h0, hs = ha_re[0:1], hm_re[0:1]
            hh_re = xa[TK:TK + 1, hf] + xa[TK:TK + 1, hb]
            hh_im = xb[TK:TK + 1, hf] - xb[TK:TK + 1, hb]
            sp0_ref[o] = jnp.concatenate(
                [(h0 + hs) * (0.5 * sc), (h0 - hs) * (0.5 * sc), hh_re * sc, hh_im * sc,
                 jnp.zeros((4, C), F32)], axis=0)


def _filter_spectra(fa, fb, filt):
    nkt, tk2, S = fa.shape
    TK = tk2 // 2
    C = HYENA_WIDTH
    n_ord = filt.shape[1] // (2 * C)
    out = jax.ShapeDtypeStruct((n_ord, nkt * TK, C), F32)
    ospec = pl.BlockSpec((n_ord, TK, C), lambda k: (0, k, 0))
    fspec = pl.BlockSpec((None, tk2, S), lambda k: (k, 0, 0))
    return pl.pallas_call(
        _spec_kernel,
        grid=(nkt,),
        in_specs=[fspec, fspec, pl.BlockSpec(filt.shape, lambda k: (0, 0))],
        out_specs=[ospec] * 6 + [pl.BlockSpec((n_ord, 8, C), lambda k: (0, 0, 0))],
        out_shape=[out] * 6 + [jax.ShapeDtypeStruct((n_ord, 8, C), F32)],
        compiler_params=_cparams("arbitrary"),
        name="hyena_spectra",
    )(fa, fb, filt)


def _hyena_kernel(v_ref, x1_ref, x2_ref, f_ref, g_ref, pre_ref, pim_ref, q1re_ref, q1im_ref, q2re_ref, q2im_ref,
                  sp0_ref, wsc_ref, bsc_ref, hb_ref, o_ref, pad_ref, z_ref, zb_ref, gate_ref, zcat_ref, il_ref):
    H, C2 = z_ref.shape
    C = C2 // 2
    n_sub, tk2, _ = f_ref.shape
    TK = tk2 // 2
    TT = g_ref.shape[0]
    nf = H // (n_sub * TK)
    R = ROW_CHUNK
    P = PAD_ROWS
    o = pl.program_id(1)
    s = pl.program_id(2)
    even, odd = slice(0, C), slice(C, C2)

    def fill_pad(in_ref):
        for par in range(2):
            pad_ref[par, 0:P, :] = jnp.zeros((P, C), F32)
            pad_ref[par, H + P:H + 2 * P, :] = jnp.zeros((P, C), F32)
            pad_ref[par, P:H + P, :] = in_ref[par].astype(F32)

    def conv_chunk(r0, w, bias):
        ev = pad_ref[0, P + r0:P + r0 + R, :]
        od = pad_ref[1, P + r0:P + r0 + R, :]
        od_prev = pad_ref[1, P + r0 - 1:P + r0 - 1 + R, :]
        ev_next = pad_ref[0, P + r0 + 1:P + r0 + 1 + R, :]
        w0, w1, w2 = w[0:1], w[1:2], w[2:3]
        return w0 * od_prev + w1 * ev + w2 * od + bias, w0 * ev + w1 * od + w2 * ev_next + bias

    @pl.when((o == 0) & (s == 0))
    def _():
        fill_pad(v_ref)
        for c in range(H // R):
            rows = slice(c * R, (c + 1) * R)
            conv_e, conv_o = conv_chunk(c * R, wsc_ref[:, 0:C], bsc_ref[:, 0:C])
            z_ref[rows, even] = conv_e
            z_ref[rows, odd] = conv_o
            zb_ref[rows, even] = conv_e.astype(BF16)
            zb_ref[rows, odd] = conv_o.astype(BF16)

    @pl.when((s == 0) & (o == 0))
    def _():
        fill_pad(x1_ref)

    @pl.when((s == 0) & (o == 1))
    def _():
        fill_pad(x2_ref)

    @pl.when(s < nf)
    def _():
        a, b, c, d = sp0_ref[0:1, :], sp0_ref[1:2, :], sp0_ref[2:3, :], sp0_ref[3:4, :]
        for sub in range(n_sub):
            tile = s * n_sub + sub
            cs = slice(sub * TK, (sub + 1) * TK)
            x = _dot(f_ref[sub], zb_ref[...])
            e_re, o_re, e_im, o_im = x[:TK, even], x[:TK, odd], x[TK:, even], x[TK:, odd]
            p_re, p_im = pre_ref[cs, :], pim_ref[cs, :]
            q1_re, q1_im, q2_re, q2_im = q1re_ref[cs, :], q1im_ref[cs, :], q2re_ref[cs, :], q2im_ref[cs, :]
            ze_re = e_re * p_re - e_im * p_im + o_re * q1_re - o_im * q1_im
            ze_im = e_re * p_im + e_im * p_re + o_re * q1_im + o_im * q1_re
            zo_re = e_re * q2_re - e_im * q2_im + o_re * p_re - o_im * p_im
            zo_im = e_re * q2_im + e_im * q2_re + o_re * p_im + o_im * p_re
            first = (tile * TK + lax.broadcasted_iota(jnp.int32, (TK, 1), 0)) == 0
            ze_re = jnp.where(first, e_re[0:1] * a + o_re[0:1] * b, ze_re)
            zo_re = jnp.where(first, e_re[0:1] * b + o_re[0:1] * a, zo_re)
            ze_im = jnp.where(first, e_im[0:1] * c + o_im[0:1] * d, ze_im)
            zo_im = jnp.where(first, o_im[0:1] * c - e_im[0:1] * d, zo_im)
            base = pl.multiple_of(tile * 2 * TK, 2 * TK)
            zcat_ref[pl.ds(base, TK), even] = ze_re.astype(BF16)
            zcat_ref[pl.ds(base, TK), odd] = zo_re.astype(BF16)
            zcat_ref[pl.ds(base + TK, TK), even] = ze_im.astype(BF16)
            zcat_ref[pl.ds(base + TK, TK), odd] = zo_im.astype(BF16)

    @pl.when(s == nf)
    def _():
        w = jnp.where(o == 0, wsc_ref[:, C:2 * C], wsc_ref[:, 2 * C:3 * C])
        bias = jnp.where(o == 0, bsc_ref[:, C:2 * C], bsc_ref[:, 2 * C:3 * C])
        for c in range(H // R):
            conv_e, conv_o = conv_chunk(c * R, w, bias)
            gate_ref[c * R:(c + 1) * R, even] = conv_e
            gate_ref[c * R:(c + 1) * R, odd] = conv_o

    @pl.when(s >= nf)
    def _():
        tt = s - nf
        rows = pl.ds(pl.multiple_of(tt * TT, TT), TT)
        y = _dot(g_ref[...], zcat_ref[...])
        hrow = jnp.where(o == 0, hb_ref[0:1, :], hb_ref[1:2, :])
        hbias = jnp.concatenate([hrow, hrow], axis=1)
        znew = gate_ref[rows, :] * (y + z_ref[rows, :] * hbias)
        z_ref[rows, :] = znew
        zb_ref[rows, :] = znew.astype(BF16)

        @pl.when(o == pl.num_programs(1) - 1)
        def _():
            for p in range(C // LANES):
                il_ref[p, pl.ds(0, TT, stride=2), :] = znew[:, p * LANES:(p + 1) * LANES]
                il_ref[p, pl.ds(1, TT, stride=2), :] = znew[:, C + p * LANES:C + (p + 1) * LANES]
            out_rows = pl.ds(pl.multiple_of(tt * 2 * TT, 2 * TT), 2 * TT)
            for p in range(C // LANES):
                o_ref[out_rows, p * LANES:(p + 1) * LANES] = il_ref[p].astype(BF16)


def _hyena(hv, hx1, hx2, fhalf, ghalf, coeffs, w_sconv, b_sconv, hy_bias, layer):
    B, _, H, C = hv.shape
    S = 2 * H
    n_tiles, tk2, _ = fhalf.shape
    TK = tk2 // 2
    n_sub = HYENA_FREQ_TILES_PER_STEP
    nf = n_tiles // n_sub
    TT = HYENA_TIME_TILE
    ni = H // TT
    sp0 = coeffs[-1]
    n_ord = sp0.shape[0]
    fwd = lambda s: jnp.minimum(s, nf - 1)
    inv = lambda s: jnp.clip(s - nf, 0, ni - 1)
    inp = pl.BlockSpec((None, 2, H, C), lambda bb, o, s: (bb, 0, 0, 0), pipeline_mode=pl.Buffered(1))
    coef = pl.BlockSpec((None, n_sub * TK, C), lambda bb, o, s: (o, fwd(s), 0))
    return pl.pallas_call(
        _hyena_kernel,
        grid=(B, n_ord, nf + ni),
        in_specs=[
            inp, inp, inp,
            pl.BlockSpec((n_sub, tk2, H), lambda bb, o, s: (fwd(s), 0, 0)),
            pl.BlockSpec((TT, 2 * H), lambda bb, o, s: (inv(s), 0)),
            coef, coef, coef, coef, coef, coef,
            pl.BlockSpec((None, 8, C), lambda bb, o, s: (o, 0, 0)),
            pl.BlockSpec((None, 3, 3 * C), lambda bb, o, s: (layer, 0, 0)),
            pl.BlockSpec((None, 1, 3 * C), lambda bb, o, s: (layer, 0, 0)),
            pl.BlockSpec((None, n_ord, C), lambda bb, o, s: (layer, 0, 0)),
        ],
        out_specs=pl.BlockSpec((None, S, C), lambda bb, o, s: (bb, 0, 0)),
        out_shape=jax.ShapeDtypeStruct((B, S, C), BF16),
        scratch_shapes=[
            pltpu.VMEM((2, H + 2 * PAD_ROWS, C), F32),
            pltpu.VMEM((H, 2 * C), F32),
            pltpu.VMEM((H, 2 * C), BF16),
            pltpu.VMEM((H, 2 * C), F32),
            pltpu.VMEM((2 * H, 2 * C), BF16),
            pltpu.VMEM((C // LANES, 2 * TT, LANES), F32),
        ],
        compiler_params=_cparams("arbitrary", "arbitrary", "arbitrary"),
        name="hyena_conv",
    )(hv, hx1, hx2, fhalf, ghalf, *coeffs, w_sconv, b_sconv.reshape(b_sconv.shape[0], 1, 3 * C), hy_bias)


def _packed_dft(bins, n_samples, n_fft, tile, sin_row0):
    theta = ((bins[:, None] * np.arange(n_samples)[None, :]) % n_fft) * (2.0 * math.pi / n_fft)
    fc = np.cos(theta)
    fs = -np.sin(theta)
    fs[0] = sin_row0
    nt = bins.shape[0] // tile
    packed = np.concatenate([fc.reshape(nt, tile, n_samples), fs.reshape(nt, tile, n_samples)], axis=1)
    return packed.astype(np.float32).astype(BF16)


@functools.lru_cache(maxsize=None)
def _dft_matrices(S):
    H, N = S // 2, 2 * S
    j = np.arange(H)
    m = np.arange(H)
    n = np.arange(S)
    quarter = np.asarray([1.0, 0.0, -1.0, 0.0])
    fhalf = _packed_dft(j, H, S, HYENA_FREQ_TILE, 1.0 - 2.0 * (m & 1))
    ghalf = np.ascontiguousarray(fhalf.reshape(2 * H, H).T)
    fa = _packed_dft(j, S, N, SPECTRA_TILE, quarter[n & 3])
    fb = _packed_dft(S - j, S, N, SPECTRA_TILE, -quarter[(n + 3) & 3])
    return fhalf, ghalf, fa, fb


def _filter_features(S):
    t = jnp.linspace(0.0, 1.0, S, dtype=F32)[:, None]
    bands = (FILTER_EMB - 1) // 2
    w = 2.0 * math.pi * jnp.arange(S, dtype=F32) / S
    f = jnp.linspace(1e-4, bands - 1, bands, dtype=F32)
    ang = w[:, None] * f[None, :]
    z = jnp.concatenate([t, jnp.cos(ang), -jnp.sin(ang)], axis=-1)
    z = jnp.pad(z, ((0, 0), (0, 128 - FILTER_EMB)))
    min_decay = math.log(DECAY_TARGET) / SLOW_DECAY_PCT
    max_decay = math.log(DECAY_TARGET) / FAST_DECAY_PCT
    deltas = jnp.linspace(min_decay, max_decay, HYENA_WIDTH, dtype=F32)
    decay = jnp.exp(-t * jnp.abs(deltas)[None, :])
    return z, decay


def _merge_kernel(ya_ref, yb_ref, yc_ref, gl_ref, x_ref, mod_ref, pa_ref, pp_ref, ph_ref, wo_ref,
                  g_ref, b_ref, wr_ref, br_ref, x1_ref, u2_ref, lt_ref):
    D = x_ref.shape[1]
    sig = lambda v: 1.0 / (1.0 + jnp.exp(-v))
    merged = sig(gl_ref[:, 0:D].astype(F32)) * _dot(ya_ref[...], pa_ref[...])
    merged += sig(gl_ref[:, D:2 * D].astype(F32)) * _dot(yb_ref[...], pp_ref[...])
    merged += sig(gl_ref[:, 2 * D:3 * D].astype(F32)) * _dot(yc_ref[...], ph_ref[...])
    h = _dot(merged.astype(BF16), wo_ref[...])
    gate1, shift2, scale2 = mod_ref[2:3, :], mod_ref[3:4, :], mod_ref[4:5, :]
    x1 = _layer_norm(ALPHA_RES * x_ref[...] + (1.0 + gate1) * h, g_ref[...], b_ref[...])
    x1_ref[...] = x1
    u2 = x1 * (1.0 + scale2) + shift2
    _store_token_tiles(u2_ref, u2)
    TM = u2.shape[0]
    u_hi = u2.astype(BF16)
    u_lo = (u2 - u_hi.astype(F32)).astype(BF16)
    r = _dot(jnp.concatenate([u_hi, u_lo], axis=0), wr_ref[...])
    rt = (r[:TM] + r[TM:] + br_ref[...]).T
    lt_ref[...] = rt[0:N_EXPERTS, :] + rt[N_EXPERTS:2 * N_EXPERTS, :]


def _merge(ya, yb, yc, proj, x, mod, p_attn, p_pool, p_hyena, w_out, ln_g, ln_b, w_router, b_router, layer):
    B, S, D = x.shape
    N = B * S
    TM = MERGE_ROWS
    tiles_per_seq = S // TM
    EP = 128
    flat = lambda t: t.reshape(N, t.shape[-1])
    w_hi = w_router[layer].astype(BF16)
    w_lo = (w_router[layer] - w_hi.astype(F32)).astype(BF16)
    wr = jnp.zeros((D, EP), BF16).at[:, :N_EXPERTS].set(w_hi).at[:, N_EXPERTS:2 * N_EXPERTS].set(w_lo)
    br = jnp.zeros((1, EP), F32).at[:, :N_EXPERTS].set(b_router[layer][None])
    tok = lambda w: pl.BlockSpec((TM, w), lambda i: (i, 0))
    lay = lambda t: pl.BlockSpec((None,) + t.shape[1:], lambda i: (layer,) + (0,) * (t.ndim - 1))
    ln_g3, ln_b3 = ln_g.reshape(-1, 1, D), ln_b.reshape(-1, 1, D)
    return pl.pallas_call(
        _merge_kernel,
        grid=(N // TM,),
        in_specs=[
            tok(ya.shape[-1]), tok(yb.shape[-1]), tok(yc.shape[-1]), tok(3 * D), tok(D),
            pl.BlockSpec((None, 6, D), lambda i: (i // tiles_per_seq, 0, 0)),
            lay(p_attn), lay(p_pool), lay(p_hyena), lay(w_out), lay(ln_g3), lay(ln_b3),
            pl.BlockSpec((D, EP), lambda i: (0, 0)),
            pl.BlockSpec((1, EP), lambda i: (0, 0)),
        ],
        out_specs=[tok(D), pl.BlockSpec((TM * D // LANES, LANES), lambda i: (i, 0)),
                   pl.BlockSpec((N_EXPERTS, TM), lambda i: (0, i))],
        out_shape=[
            jax.ShapeDtypeStruct((N, D), F32),
            jax.ShapeDtypeStruct((N * D // LANES, LANES), F32),
            jax.ShapeDtypeStruct((N_EXPERTS, N), F32),
        ],
        compiler_params=_cparams("arbitrary"),
        name="merge_ln1_router",
    )(flat(ya), flat(yb), flat(yc), flat(proj), flat(x), mod, p_attn, p_pool, p_hyena, w_out,
      ln_g3, ln_b3, wr, br)


def _topk_kernel(l_ref, e_ref, g_ref, r_ref, cnt_ref, carry_ref):
    E, TL = l_ref.shape

    @pl.when(pl.program_id(0) == 0)
    def _():
        carry_ref[...] = jnp.zeros_like(carry_ref)

    l = l_ref[...]
    eio = lax.broadcasted_iota(jnp.int32, (E, TL), 0).astype(F32)
    vals, hots = [], []
    for k in range(TOP_K):
        m = jnp.max(l, axis=0, keepdims=True)
        idx = jnp.min(jnp.where(l == m, eio, float(E)), axis=0, keepdims=True)
        hot = eio == idx
        l = jnp.where(hot, -jnp.inf, l)
        vals.append(m)
        hots.append(hot)
        e_ref[k:k + 1, :] = idx.astype(jnp.int32)
    exps = [jnp.exp(v - vals[0]) for v in vals]
    inv = 1.0 / (exps[0] + exps[1] + exps[2] + exps[3])
    for k in range(TOP_K):
        g_ref[k:k + 1, :] = exps[k] * inv

    memb = jnp.zeros((E, TL), F32)
    for hot in hots:
        memb = memb + hot.astype(F32)
    upper = (lax.broadcasted_iota(jnp.int32, (TL, TL), 0)
             <= lax.broadcasted_iota(jnp.int32, (TL, TL), 1)).astype(BF16)
    incl = _dot(memb.astype(BF16), upper)
    excl = incl - memb + carry_ref[:, 0:1]
    for k in range(TOP_K):
        rank = jnp.sum(jnp.where(hots[k], excl, 0.0), axis=0, keepdims=True)
        r_ref[k:k + 1, :] = rank.astype(jnp.int32)
    carry_ref[...] = carry_ref[...] + jnp.sum(memb, axis=1, keepdims=True)
    cnt_ref[...] = carry_ref[...]


def _topk(logits_t):
    E, N = logits_t.shape
    TL = TOPK_LANES
    tok = pl.BlockSpec((TOP_K, TL), lambda i: (0, i))
    return pl.pallas_call(
        _topk_kernel,
        grid=(N // TL,),
        in_specs=[pl.BlockSpec((E, TL), lambda i: (0, i))],
        out_specs=[tok, tok, tok, pl.BlockSpec((E, 128), lambda i: (0, 0))],
        out_shape=[
            jax.ShapeDtypeStruct((TOP_K, N), jnp.int32),
            jax.ShapeDtypeStruct((TOP_K, N), F32),
            jax.ShapeDtypeStruct((TOP_K, N), jnp.int32),
            jax.ShapeDtypeStruct((E, 128), F32),
        ],
        scratch_shapes=[pltpu.VMEM((E, 128), F32)],
        compiler_params=_cparams("arbitrary"),
        name="topk_rank",
    )(logits_t)


def _dest_kernel(offs_ref, e_ref, r_ref, d_ref):
    e = e_ref[...]
    acc = r_ref[...]
    for ex in range(N_EXPERTS):
        acc = acc + jnp.where(e == ex, offs_ref[ex], 0)
    d_ref[...] = acc


def _dest_rows(offs, top_e, rank):
    K, N = top_e.shape
    TL = TOPK_LANES
    tok = pl.BlockSpec((K, TL), lambda i, offs: (0, i))
    return pl.pallas_call(
        _dest_kernel,
        grid_spec=pltpu.PrefetchScalarGridSpec(
            num_scalar_prefetch=1, grid=(N // TL,), in_specs=[tok, tok], out_specs=tok),
        out_shape=jax.ShapeDtypeStruct((K, N), jnp.int32),
        compiler_params=_cparams("arbitrary"),
        name="dest_rows",
    )(offs, top_e, rank)


def _dispatch_kernel(tail_ref, n_used_ref, dest_ref, u_ref, xs_ref, zero_ref, sem):
    TT = dest_ref.shape[0] // TOP_K
    RT = u_ref.shape[0] // TT
    blk_rows = zero_ref.shape[0]
    n_blocks = xs_ref.shape[0] // blk_rows

    def zero_copy(first_row):
        start = pl.multiple_of(first_row * RT, RT)
        return pltpu.make_async_copy(zero_ref, xs_ref.at[pl.ds(start, blk_rows), :], sem)

    @pl.when(pl.program_id(0) == 0)
    def _():
        zero_ref[...] = jnp.zeros_like(zero_ref)

        def tails(fn):
            def body(e, carry):
                @pl.when(tail_ref[e] >= 0)
                def _():
                    fn(zero_copy(jnp.maximum(tail_ref[e], 0)))
                return carry
            lax.fori_loop(0, N_EXPERTS, body, 0)

        def unused(fn):
            def body(i, carry):
                fn(zero_copy(i * (blk_rows // RT)))
                return carry
            lax.fori_loop(n_used_ref[0], n_blocks, body, 0)

        tails(lambda cp: cp.start())
        unused(lambda cp: cp.start())
        tails(lambda cp: cp.wait())
        unused(lambda cp: cp.wait())

    def row_copy(j, k):
        src = pl.multiple_of(j * RT, RT)
        dst = pl.multiple_of(dest_ref[j * TOP_K + k] * RT, RT)
        return pltpu.make_async_copy(u_ref.at[pl.ds(src, RT), :], xs_ref.at[pl.ds(dst, RT), :], sem)

    def start(j, carry):
        for k in range(TOP_K):
            row_copy(j, k).start(priority=k % DMA_PRIORITIES)
        return carry

    lax.fori_loop(0, TT, start, 0, unroll=DMA_ISSUE_UNROLL)
    for k in range(TOP_K):
        pltpu.make_async_copy(u_ref, xs_ref.at[pl.ds(0, TT * RT), :], sem).wait()


def _dispatch(tail_start, n_used, dest, u2_tiles, n_rows, D):
    RT = D // LANES
    N = u2_tiles.shape[0] // RT
    TT = MOE_TOK
    return pl.pallas_call(
        _dispatch_kernel,
        grid_spec=pltpu.PrefetchScalarGridSpec(
            num_scalar_prefetch=2,
            grid=(N // TT,),
            in_specs=[
                pl.BlockSpec((TOP_K * TT,), lambda i, tail, nu: (i,), memory_space=pltpu.SMEM),
                pl.BlockSpec((TT * RT, LANES), lambda i, tail, nu: (i, 0)),
            ],
            out_specs=pl.BlockSpec(memory_space=pl.ANY),
            scratch_shapes=[pltpu.VMEM((MOE_BLK * RT, LANES), F32), pltpu.SemaphoreType.DMA(())],
        ),
        out_shape=jax.ShapeDtypeStruct((n_rows * RT, LANES), F32),
        compiler_params=_cparams("arbitrary"),
        name="moe_dispatch",
    )(tail_start, n_used, dest, u2_tiles)


def _ffn_kernel(blk_e_ref, n_used_ref, xs_ref, w1_ref, b1_ref, w2_ref, b2_ref, ys_ref,
                w1b_ref, w2b_ref, act_ref):
    F, D = w2_ref.shape
    BLK = xs_ref.shape[0] * LANES // D
    i = pl.program_id(0)
    used = i < n_used_ref[0]
    new_expert = (i == 0) | (blk_e_ref[i] != blk_e_ref[jnp.maximum(i - 1, 0)])

    @pl.when(used & new_expert)
    def _():
        for r in range(0, D, WEIGHT_CAST_ROWS):
            w1b_ref[r:r + WEIGHT_CAST_ROWS, :] = w1_ref[r:r + WEIGHT_CAST_ROWS, :].astype(BF16)
        for r in range(0, F, WEIGHT_CAST_ROWS):
            w2b_ref[r:r + WEIGHT_CAST_ROWS, :] = w2_ref[r:r + WEIGHT_CAST_ROWS, :].astype(BF16)

    @pl.when(used)
    def _():
        x = jnp.concatenate(_load_token_tiles(xs_ref, BLK, D), axis=1).astype(BF16)
        for c in range(0, F, FFN_CHUNK):
            cg = slice(c, c + FFN_CHUNK)
            cl = slice(F + c, F + c + FFN_CHUNK)
            glu = jnp.minimum(_dot(x, w1b_ref[:, cg]) + b1_ref[:, cg], SWIGLU_LIMIT)
            lin = jnp.clip(_dot(x, w1b_ref[:, cl]) + b1_ref[:, cl], -SWIGLU_LIMIT, SWIGLU_LIMIT)
            act = glu * (1.0 / (1.0 + jnp.exp(-SWIGLU_ALPHA * glu))) * (lin + 1.0)
            act_ref[:, cg] = act.astype(BF16)
        _store_token_tiles(ys_ref, _dot(act_ref[...], w2b_ref[...]) + b2_ref[...])

    @pl.when(jnp.logical_not(used))
    def _():
        ys_ref[...] = jnp.zeros_like(ys_ref)


def _expert_ffn(blk_e, n_used, xs_tiles, w1, b1, w2, b2, layer):
    _, E, D, F2 = w1.shape
    F = F2 // 2
    RT = D // LANES
    BLK = MOE_BLK
    rows = pl.BlockSpec((BLK * RT, LANES), lambda i, be, nu: (i, 0))
    per_expert = lambda r, c: pl.BlockSpec((None, None, r, c), lambda i, be, nu: (layer, be[i], 0, 0))
    return pl.pallas_call(
        _ffn_kernel,
        grid_spec=pltpu.PrefetchScalarGridSpec(
            num_scalar_prefetch=2,
            grid=(xs_tiles.shape[0] // (BLK * RT),),
            in_specs=[rows, per_expert(D, F2), per_expert(1, F2), per_expert(F, D), per_expert(1, D)],
            out_specs=rows,
            scratch_shapes=[pltpu.VMEM((D, F2), BF16), pltpu.VMEM((F, D), BF16), pltpu.VMEM((BLK, F), BF16)],
        ),
        out_shape=jax.ShapeDtypeStruct(xs_tiles.shape, F32),
        compiler_params=_cparams("arbitrary"),
        name="moe_ffn",
    )(blk_e, n_used, xs_tiles, w1, b1.reshape(-1, E, 1, F2), w2, b2.reshape(-1, E, 1, D))


def _combine_kernel(dest_ref, dest_next_ref, gt_ref, x1_ref, mod_ref, g_ref, b_ref, ys_ref, o_ref,
                    buf_a, buf_b, sem):
    TT, D = x1_ref.shape
    RT = D // LANES
    RC = TT // COMBINE_CHUNKS
    i = pl.program_id(0)
    last = pl.num_programs(0) - 1

    def row_copy(d_ref, buf, s, j, k):
        src = pl.multiple_of(d_ref[j * TOP_K + k] * RT, RT)
        dst = pl.multiple_of(j * RT, RT)
        return pltpu.make_async_copy(ys_ref.at[pl.ds(src, RT), :], buf.at[k, pl.ds(dst, RT), :], sem.at[s])

    def wait_tile(buf, s):
        for k in range(TOP_K):
            pltpu.make_async_copy(ys_ref.at[pl.ds(0, TT * RT), :], buf.at[k], sem.at[s]).wait()

    def reduce_rows(buf, r0):
        rows = slice(r0, r0 + RC)
        chunks = None
        for k in range(TOP_K):
            gate = gt_ref[rows, k:k + 1]
            tiles = [gate * buf[k, pl.ds(r0 * RT + c, RC, stride=RT), :] for c in range(RT)]
            chunks = tiles if chunks is None else [a + b for a, b in zip(chunks, tiles)]
        h = jnp.concatenate(chunks, axis=1)
        gate2 = mod_ref[5:6, :]
        o_ref[rows, :] = _layer_norm(ALPHA_RES * x1_ref[rows, :] + (1.0 + gate2) * h, g_ref[...], b_ref[...])

    def run(cur, s_cur, nxt, s_nxt):
        wait_tile(cur, s_cur)
        for ch in range(COMBINE_CHUNKS):
            reduce_rows(cur, ch * RC)
            for j in range(ch * RC, (ch + 1) * RC):
                for k in range(TOP_K):
                    row_copy(dest_next_ref, nxt, s_nxt, j, k).start(priority=k % DMA_PRIORITIES)

        @pl.when(i == last)
        def _():
            wait_tile(nxt, s_nxt)

    @pl.when(i == 0)
    def _():
        def body(j, carry):
            for k in range(TOP_K):
                row_copy(dest_ref, buf_a, 0, j, k).start(priority=k % DMA_PRIORITIES)
            return carry
        lax.fori_loop(0, TT, body, 0, unroll=DMA_ISSUE_UNROLL)

    @pl.when(i % 2 == 0)
    def _():
        run(buf_a, 0, buf_b, 1)

    @pl.when(i % 2 == 1)
    def _():
        run(buf_b, 1, buf_a, 0)


def _combine(dest, gates_t, x1, mod, ln_g, ln_b, ys, seq_len, layer):
    N, D = x1.shape
    TT = MOE_TOK
    tiles_per_seq = seq_len // TT
    ln_g3, ln_b3 = ln_g.reshape(-1, 1, D), ln_b.reshape(-1, 1, D)
    lay = lambda t: pl.BlockSpec((None,) + t.shape[1:], lambda i: (layer,) + (0,) * (t.ndim - 1))
    return pl.pallas_call(
        _combine_kernel,
        grid=(N // TT,),
        in_specs=[
            pl.BlockSpec((TOP_K * TT,), lambda i: (i,), memory_space=pltpu.SMEM),
            pl.BlockSpec((TOP_K * TT,), lambda i: (jnp.minimum(i + 1, N // TT - 1),), memory_space=pltpu.SMEM),
            pl.BlockSpec((TT, TOP_K), lambda i: (i, 0)),
            pl.BlockSpec((TT, D), lambda i: (i, 0)),
            pl.BlockSpec((None, 6, D), lambda i: (i // tiles_per_seq, 0, 0)),
            lay(ln_g3), lay(ln_b3),
            pl.BlockSpec(memory_space=pl.ANY),
        ],
        out_specs=pl.BlockSpec((TT, D), lambda i: (i, 0)),
        out_shape=jax.ShapeDtypeStruct((N, D), F32),
        scratch_shapes=[pltpu.VMEM((TOP_K, TT * D // LANES, LANES), F32)] * 2 + [pltpu.SemaphoreType.DMA((2,))],
        compiler_params=_cparams("arbitrary"),
        name="moe_combine_ln2",
    )(dest, dest, gates_t, x1, mod, ln_g3, ln_b3, ys)


def _moe(u2_tiles, logits_t, x1, mod, w1, b1, w2, b2, ln_g, ln_b, seq_len, layer):
    N, D = x1.shape
    BLK = MOE_BLK
    top_e, gates, rank, counts = _topk(logits_t)
    counts = counts[:, 0].astype(jnp.int32)
    padded = (counts + BLK - 1) // BLK * BLK
    pend = jnp.cumsum(padded)
    offs = pend - padded
    n_blocks = -(-(N * TOP_K) // BLK) + N_EXPERTS
    blk_start = jnp.arange(n_blocks, dtype=jnp.int32) * BLK
    blk_e = jnp.sum((pend[None, :] <= blk_start[:, None]).astype(jnp.int32), axis=1)
    blk_e = jnp.minimum(blk_e, N_EXPERTS - 1).astype(jnp.int32)
    n_used = (pend[-1:] // BLK).astype(jnp.int32)
    tail_start = jnp.where(padded > 0, pend - BLK, -1).astype(jnp.int32)
    dest = _dest_rows(offs.astype(jnp.int32), top_e, rank).T.reshape(-1)
    xs = _dispatch(tail_start, n_used, dest, u2_tiles, n_blocks * BLK, D)
    ys = _expert_ffn(blk_e, n_used, xs, w1, b1, w2, b2, layer)
    return _combine(dest, gates.T, x1, mod, ln_g, ln_b, ys, seq_len, layer)


def _split_w_in(w_in):
    AW = N_ATTN_HEADS * HEAD_DIM
    GW = GROUP_WIDTH
    C = HYENA_WIDTH
    w = w_in.astype(BF16)
    w_nat = jnp.concatenate([w[:, 3 * AW + 4 * C:], w[:, 3 * AW:3 * AW + C]], axis=1)
    col_scale = (1.0 / math.sqrt(HEAD_DIM), 1.0, 1.0)
    w_qkv = jnp.stack([
        jnp.concatenate([w[:, t * AW + g * GW:t * AW + (g + 1) * GW] * col_scale[t] for t in range(3)], axis=1)
        for g in range(len(ATTN_GROUPS))]).astype(BF16)
    w_hy = jnp.stack([w[:, 3 * AW + (1 + t) * C:3 * AW + (2 + t) * C] for t in range(3)])
    return w_nat, w_qkv, w_hy


def kernel(x, c, w_ada, b_ada, w_in, w_sconv, b_sconv, w_pool, pool_scale, f_w1, f_b1, f_w_inner, f_b_inner, f_w_out, f_freq, hy_bias, p_attn, p_pool, p_hyena, w_out, ln1_g, ln1_b, w_router, b_router, w1, b1, w2, b2, ln2_g, ln2_b):
    B, S, D = x.shape
    depth = w_in.shape[0]
    bias_tables = [jnp.asarray(t) for t in _attn_bias_tables(S)]
    fhalf, ghalf, fa, fb = (jnp.asarray(m) for m in _dft_matrices(S))
    zfeat, decay = _filter_features(S)
    gate_cols = 3 * D
    pool_block = gate_cols // HYENA_WIDTH
    bf = lambda t: t.astype(BF16)
    p_attn_b, p_pool_b, p_hyena_b, w_out_b = map(bf, (p_attn, p_pool, p_hyena, w_out))

    for l in range(depth):
        mod = _ada(c, w_ada, b_ada, l).reshape(B, 6, D)
        w_nat, w_qkv, w_hy = _split_w_in(w_in[l])
        proj, u = _inproj_nat(x, mod, w_nat)
        qkv = _inproj_dil(u, w_qkv, [d for _, d in ATTN_GROUPS])
        hv, hx1, hx2 = _inproj_dil(u, w_hy, [2, 2, 2])
        ya = _attention(qkv, bias_tables)
        yb = _pool(proj, pool_block, w_pool, pool_scale, l)
        filt = _hyena_filters(zfeat, decay, f_w1, f_b1, f_w_inner, f_b_inner, f_w_out, f_freq, l)
        coeffs = _filter_spectra(fa, fb, filt)
        yc = _hyena(hv, hx1, hx2, fhalf, ghalf, coeffs, w_sconv, b_sconv, hy_bias, l)
        x1, u2, logits_t = _merge(ya, yb, yc, proj, x, mod, p_attn_b, p_pool_b, p_hyena_b, w_out_b,
                                  ln1_g, ln1_b, w_router, b_router, l)
        x = _moe(u2, logits_t, x1, mod, w1, b1, w2, b2, ln2_g, ln2_b, S, l).reshape(B, S, D)
    return x
```

```python
import functools
import math

import jax
import jax.numpy as jnp
import numpy as np
from jax import lax
from jax.experimental import pallas as pl
from jax.experimental.pallas import tpu as pltpu

F32 = jnp.float32
BF16 = jnp.bfloat16

HEAD_DIM = 64
HEADS_PER_GROUP = 4
ATTN_GROUPS = ((128, 1), (512, 4), (2048, 16))
N_ATTN_HEADS = HEADS_PER_GROUP * len(ATTN_GROUPS)
GROUP_WIDTH = HEADS_PER_GROUP * HEAD_DIM
HALF_WINDOW = 64
POOL_WINDOWS = (2, 4, 8, 16)
POOL_GROUP = 128
HYENA_WIDTH = 512
FILTER_EMB = 33
FILTER_HIDDEN = 64
N_EXPERTS = 32
TOP_K = 4
SWIGLU_LIMIT = 7.0
SWIGLU_ALPHA = 1.702
LN_EPS = 1e-5
DEPTH = 2
ALPHA_RES = (2 * DEPTH) ** 0.25
DECAY_TARGET = 1e-2
FAST_DECAY_PCT = 0.3
SLOW_DECAY_PCT = 1.5

QBLK = 128
HYENA_FREQ_TILE = 256
HYENA_FREQ_TILES_PER_STEP = 1
HYENA_TIME_TILE = 512
SPECTRA_TILE = 256
INPROJ_COLS = 1792
ATTN_UNROLL = 16
ROW_CHUNK = 256
MERGE_ROWS = 512
TOPK_LANES = 1024
MOE_BLK = 512
FFN_CHUNK = 1024
WEIGHT_CAST_ROWS = 128
DMA_PRIORITIES = 2
DMA_ISSUE_UNROLL = 8
COMBINE_CHUNKS = 8
MOE_TOK = 256
PAD_ROWS = 8
NEG_BIG = -1e30
LANES = 128
VMEM_LIMIT = 56 * 1024 * 1024


def _cparams(*sem):
    return pltpu.CompilerParams(dimension_semantics=sem, vmem_limit_bytes=VMEM_LIMIT)


def _dot(a, b):
    return jnp.dot(a, b, preferred_element_type=F32)


def _dot3(a, b):
    a_hi = a.astype(BF16)
    b_hi = b.astype(BF16)
    a_lo = (a - a_hi.astype(F32)).astype(BF16)
    b_lo = (b - b_hi.astype(F32)).astype(BF16)
    return _dot(a_hi, b_hi) + _dot(a_hi, b_lo) + _dot(a_lo, b_hi)


def _layer_norm(r, g, b):
    mu = jnp.mean(r, axis=-1, keepdims=True)
    c = r - mu
    var = jnp.mean(c * c, axis=-1, keepdims=True)
    return c * lax.rsqrt(var + LN_EPS) * g + b


def _store_token_tiles(ref, val):
    T, D = val.shape
    n = D // LANES
    for c in range(n):
        ref[pl.ds(c, T, stride=n), :] = val[:, c * LANES:(c + 1) * LANES]


def _load_token_tiles(ref, T, D, plane=None):
    n = D // LANES
    lead = () if plane is None else tuple(plane)
    idx = lambda c: lead + (pl.ds(c, T, stride=n), slice(None))
    return [ref[idx(c)] for c in range(n)]


def _ada_kernel(c_ref, w_ref, b_ref, o_ref):
    c = c_ref[...]
    cond = c * (1.0 / (1.0 + jnp.exp(-c)))
    o_ref[...] = _dot3(cond, w_ref[...]) + b_ref[...]


def _ada(c, w_ada, b_ada, layer):
    B, D = c.shape
    n_out = w_ada.shape[-1]
    tn = 768
    return pl.pallas_call(
        _ada_kernel,
        grid=(n_out // tn,),
        in_specs=[
            pl.BlockSpec((B, D), lambda j: (0, 0)),
            pl.BlockSpec((None, D, tn), lambda j: (layer, 0, j)),
            pl.BlockSpec((None, 1, tn), lambda j: (layer, 0, j)),
        ],
        out_specs=pl.BlockSpec((B, tn), lambda j: (0, j)),
        out_shape=jax.ShapeDtypeStruct((B, n_out), F32),
        compiler_params=_cparams("arbitrary"),
        name="ada",
    )(c, w_ada, b_ada.reshape(b_ada.shape[0], 1, n_out))


def _inproj_nat_kernel(x_ref, mod_ref, w_ref, o_ref, u_ref):
    S = x_ref.shape[0]

    @pl.when(pl.program_id(1) == 0)
    def _():
        shift = mod_ref[0:1, :]
        scale1p = 1.0 + mod_ref[1:2, :]

        def body(i, carry):
            rows = pl.ds(pl.multiple_of(i * ROW_CHUNK, ROW_CHUNK), ROW_CHUNK)
            u_ref[rows, :] = (x_ref[rows, :] * scale1p + shift).astype(BF16)
            return carry

        lax.fori_loop(0, S // ROW_CHUNK, body, 0)

    o_ref[...] = _dot(u_ref[...], w_ref[...]).astype(BF16)


def _inproj_nat(x, mod, w_nat):
    B, S, D = x.shape
    n_out = w_nat.shape[1]
    tn = INPROJ_COLS
    return pl.pallas_call(
        _inproj_nat_kernel,
        grid=(B, n_out // tn),
        in_specs=[
            pl.BlockSpec((None, S, D), lambda b, j: (b, 0, 0)),
            pl.BlockSpec((None, 6, D), lambda b, j: (b, 0, 0)),
            pl.BlockSpec((D, tn), lambda b, j: (0, j)),
        ],
        out_specs=[
            pl.BlockSpec((None, S, tn), lambda b, j: (b, 0, j)),
            pl.BlockSpec((None, S, D), lambda b, j: (b, 0, 0)),
        ],
        out_shape=[
            jax.ShapeDtypeStruct((B, S, n_out), BF16),
            jax.ShapeDtypeStruct((B, S, D), BF16),
        ],
        compiler_params=_cparams("arbitrary", "arbitrary"),
        name="inproj_nat",
    )(x, mod, w_nat)


def _inproj_dil_kernel(u_ref, w_ref, o0_ref, o1_ref, o2_ref, acc_ref, *, dils):
    S = u_ref.shape[0]
    for g, o_ref in enumerate((o0_ref, o1_ref, o2_ref)):
        d = dils[g]
        L = S // d
        acc = _dot(u_ref[...], w_ref[g])
        if d == 1:
            o_ref[0] = acc.astype(BF16)
            continue
        for c in range(acc_ref.shape[0]):
            acc_ref[c] = acc[:, c * LANES:(c + 1) * LANES]
        for rho in range(d):
            for c in range(acc_ref.shape[0]):
                o_ref[rho, :, c * LANES:(c + 1) * LANES] = (
                    acc_ref[c, pl.ds(rho, L, stride=d), :].astype(BF16))


def _inproj_dil(u, w_groups, dils):
    B, S, D = u.shape
    W = w_groups.shape[-1]
    assert len(dils) == w_groups.shape[0] == 3
    return pl.pallas_call(
        functools.partial(_inproj_dil_kernel, dils=tuple(dils)),
        grid=(B,),
        in_specs=[
            pl.BlockSpec((None, S, D), lambda b: (b, 0, 0)),
            pl.BlockSpec((len(dils), D, W), lambda b: (0, 0, 0)),
        ],
        out_specs=[pl.BlockSpec((None, d, S // d, W), lambda b: (b, 0, 0, 0)) for d in dils],
        out_shape=[jax.ShapeDtypeStruct((B, d, S // d, W), BF16) for d in dils],
        scratch_shapes=[pltpu.VMEM((W // LANES, S, LANES), F32)],
        compiler_params=_cparams("arbitrary"),
        name="inproj_dil",
    )(u, w_groups)


def _window_start(i, L, W):
    return min(max(QBLK * i - HALF_WINDOW, 0), L - W)


def _attn_bias_tables(S):
    h = np.arange(1, N_ATTN_HEADS + 1, dtype=np.float32)
    slopes = (2.0 ** (-8.0 * h / N_ATTN_HEADS)).astype(np.float32)
    tables = []
    for g, (_, d) in enumerate(ATTN_GROUPS):
        L = S // d
        W = min(2 * QBLK, L)
        nb = L // QBLK
        variants = sorted({_window_start(i, L, W) - QBLK * i for i in range(nb)}, reverse=True)
        r = np.arange(QBLK)[:, None]
        c = np.arange(W)[None, :]
        tab = np.empty((len(variants), HEADS_PER_GROUP * QBLK, W), np.float32)
        for vi, delta in enumerate(variants):
            off = np.abs(c - r + delta)
            for hh in range(HEADS_PER_GROUP):
                slope = np.float64(slopes[g * HEADS_PER_GROUP + hh])
                bias = np.where(off <= HALF_WINDOW, -slope * (d * off), NEG_BIG)
                tab[vi, hh * QBLK:(hh + 1) * QBLK] = bias.astype(np.float32)
        tables.append(tab)
    return tables


def _block_attention(q, kwin, vwin, bias, head_of_lane):
    zero = jnp.zeros_like(q)
    q4 = jnp.concatenate([jnp.where(head_of_lane == h, q, zero) for h in range(HEADS_PER_GROUP)], axis=0)
    s = lax.dot_general(q4, kwin, (((1,), (1,)), ((), ())), preferred_element_type=F32)
    s = s + bias
    m = jnp.max(s, axis=-1, keepdims=True)
    p = jnp.exp(s - m)
    l = jnp.sum(p, axis=-1, keepdims=True)
    o4 = _dot(p.astype(BF16), vwin) * (1.0 / l)
    lse4 = m + jnp.log(l)
    out = jnp.zeros((QBLK, GROUP_WIDTH), F32)
    lse = jnp.zeros((QBLK, GROUP_WIDTH), F32)
    for h in range(HEADS_PER_GROUP):
        rows = slice(h * QBLK, (h + 1) * QBLK)
        sel = head_of_lane == h
        out = jnp.where(sel, o4[rows], out)
        lse = jnp.where(sel, lse4[rows], lse)
    return out, lse


def _attn_kernel(q0_ref, q1_ref, q2_ref, b0_ref, b1_ref, b2_ref, o_ref, acc_ref, lse_ref):
    S = o_ref.shape[0]
    GW = GROUP_WIDTH
    head_of_lane = lax.broadcasted_iota(jnp.int32, (QBLK, GW), 1) // HEAD_DIM
    n_planes = GW // LANES

    def put(g, rows, out, lse):
        for c in range(n_planes):
            acc_ref[g, c, rows, :] = out[:, c * LANES:(c + 1) * LANES]
            lse_ref[g, c, rows, :] = lse[:, c * LANES:(c + 1) * LANES]

    L0 = S
    nb0 = L0 // QBLK

    def g0_body(i, carry):
        ws = pl.multiple_of(jnp.clip(QBLK * i - HALF_WINDOW, 0, L0 - 2 * QBLK), HALF_WINDOW)
        var = jnp.where(i == 0, 0, jnp.where(i == nb0 - 1, 2, 1))
        rows = pl.ds(pl.multiple_of(i * QBLK, QBLK), QBLK)
        out, lse = _block_attention(
            q0_ref[0, rows, 0:GW],
            q0_ref[0, pl.ds(ws, 2 * QBLK), GW:2 * GW],
            q0_ref[0, pl.ds(ws, 2 * QBLK), 2 * GW:3 * GW],
            b0_ref[var], head_of_lane)
        put(0, rows, out, lse)
        return carry

    lax.fori_loop(0, nb0, g0_body, 0, unroll=ATTN_UNROLL)

    for g, (q_ref, b_ref) in ((1, (q1_ref, b1_ref)), (2, (q2_ref, b2_ref))):
        d = ATTN_GROUPS[g][1]
        L = S // d
        W = min(2 * QBLK, L)
        nb = L // QBLK
        deltas = sorted({_window_start(i, L, W) - QBLK * i for i in range(nb)}, reverse=True)

        def gd_body(rho, carry, q_ref=q_ref, b_ref=b_ref, d=d, L=L, W=W, nb=nb, deltas=deltas, g=g):
            for i in range(nb):
                ws = _window_start(i, L, W)
                var = deltas.index(ws - QBLK * i)
                out, lse = _block_attention(
                    q_ref[rho, i * QBLK:(i + 1) * QBLK, 0:GW],
                    q_ref[rho, ws:ws + W, GW:2 * GW],
                    q_ref[rho, ws:ws + W, 2 * GW:3 * GW],
                    b_ref[var], head_of_lane)
                dst = pl.ds(rho + i * QBLK * d, QBLK, stride=d)
                put(g, dst, out, lse)
            return carry

        lax.fori_loop(0, d, gd_body, 0, unroll=max(1, ATTN_UNROLL // nb))

    def merge_body(i, carry):
        rows = pl.ds(pl.multiple_of(i * ROW_CHUNK, ROW_CHUNK), ROW_CHUNK)
        for c in range(n_planes):
            l0, l1, l2 = lse_ref[0, c, rows, :], lse_ref[1, c, rows, :], lse_ref[2, c, rows, :]
            mx = jnp.maximum(jnp.maximum(l0, l1), l2)
            w0, w1, w2 = jnp.exp(l0 - mx), jnp.exp(l1 - mx), jnp.exp(l2 - mx)
            num = w0 * acc_ref[0, c, rows, :] + w1 * acc_ref[1, c, rows, :] + w2 * acc_ref[2, c, rows, :]
            o_ref[rows, c * LANES:(c + 1) * LANES] = (num * (1.0 / (w0 + w1 + w2))).astype(BF16)
        return carry

    lax.fori_loop(0, S // ROW_CHUNK, merge_body, 0)


def _attention(qkv, bias_tables):
    q0, q1, q2 = qkv
    B, _, S, W3 = q0.shape
    GW = GROUP_WIDTH
    in_specs = [pl.BlockSpec((None,) + q.shape[1:], lambda b: (b, 0, 0, 0)) for q in qkv]
    in_specs += [pl.BlockSpec(t.shape, lambda b: (0, 0, 0)) for t in bias_tables]
    return pl.pallas_call(
        _attn_kernel,
        grid=(B,),
        in_specs=in_specs,
        out_specs=pl.BlockSpec((None, S, GW), lambda b: (b, 0, 0)),
        out_shape=jax.ShapeDtypeStruct((B, S, GW), BF16),
        scratch_shapes=[pltpu.VMEM((3, GW // LANES, S, LANES), F32)] * 2,
        compiler_params=_cparams("arbitrary"),
        name="attention",
    )(q0, q1, q2, *bias_tables)


def _pool_kernel(y_ref, w_ref, sc_ref, o_ref, pad_ref):
    S = y_ref.shape[0]
    C = y_ref.shape[1]
    P = PAD_ROWS
    pad_ref[0:P, :] = jnp.zeros((P, C), F32)
    pad_ref[S + P:S + 2 * P, :] = jnp.zeros((P, C), F32)
    pad_ref[P:S + P, :] = y_ref[...].astype(F32)
    for c in range(S // ROW_CHUNK):
        r0 = c * ROW_CHUNK
        t = r0 + lax.broadcasted_iota(jnp.int32, (ROW_CHUNK, 1), 0)
        for g, w in enumerate(POOL_WINDOWS):
            cols = slice(g * POOL_GROUP, (g + 1) * POOL_GROUP)
            acc = pad_ref[P + r0 - w // 2:P + r0 - w // 2 + ROW_CHUNK, cols]
            for j in range(-w // 2 + 1, w // 2):
                acc = acc + pad_ref[P + r0 + j:P + r0 + j + ROW_CHUNK, cols]
            cnt = (jnp.minimum(t + w // 2, S) - jnp.maximum(t - w // 2, 0)).astype(F32)
            pooled = acc / cnt - pad_ref[P + r0:P + r0 + ROW_CHUNK, cols]
            mixed = _dot(pooled.astype(BF16), w_ref[g].astype(BF16)) * sc_ref[:, cols]
            o_ref[r0:r0 + ROW_CHUNK, cols] = mixed.astype(BF16)


def _pool(proj, col_block, w_pool, pool_scale, layer):
    B, S, _ = proj.shape
    C = POOL_GROUP * len(POOL_WINDOWS)
    return pl.pallas_call(
        _pool_kernel,
        grid=(B,),
        in_specs=[
            pl.BlockSpec((None, S, C), lambda b: (b, 0, col_block)),
            pl.BlockSpec((None,) + w_pool.shape[1:], lambda b: (layer, 0, 0, 0)),
            pl.BlockSpec((None, 1, C), lambda b: (layer, 0, 0)),
        ],
        out_specs=pl.BlockSpec((None, S, C), lambda b: (b, 0, 0)),
        out_shape=jax.ShapeDtypeStruct((B, S, C), BF16),
        scratch_shapes=[pltpu.VMEM((S + 2 * PAD_ROWS, C), F32)],
        compiler_params=_cparams("arbitrary"),
        name="pool",
    )(proj, w_pool, pool_scale.reshape(pool_scale.shape[0], 1, C))


def _filter_kernel(z_ref, w1_ref, b1_ref, wi_ref, bi_ref, wo_ref, fr_ref, dec_ref, o_ref):
    S = z_ref.shape[0]
    C = HYENA_WIDTH
    freq = fr_ref[...]
    h = jnp.sin(freq * (_dot3(z_ref[...], w1_ref[...]) + b1_ref[...]))
    for i in range(wi_ref.shape[0]):
        h = jnp.sin(freq * (_dot3(h, wi_ref[i]) + bi_ref[i]))
    not_first = lax.broadcasted_iota(jnp.int32, (S, 1), 0) > 0
    for blk in range(wo_ref.shape[1] // C):
        cols = slice(blk * C, (blk + 1) * C)
        f = _dot3(h, wo_ref[:, cols]) * dec_ref[...]
        if blk % 2 == 1:
            f = jnp.where(not_first, f, 0.0)
        o_ref[:, cols] = f.astype(BF16)


def _hyena_filters(zfeat, decay, f_w1, f_b1, f_w_inner, f_b_inner, f_w_out, f_freq, layer):
    S, E = zfeat.shape
    H = FILTER_HIDDEN
    n_out = f_w_out.shape[-1]
    n_inner = f_w_inner.shape[1]
    w1 = jnp.zeros((f_w1.shape[0], E, H), F32).at[:, :f_w1.shape[1]].set(f_w1)
    full = lambda shape: pl.BlockSpec(shape, lambda i: (0,) * len(shape))
    lay = lambda shape: pl.BlockSpec((None,) + shape, lambda i: (layer,) + (0,) * len(shape))
    return pl.pallas_call(
        _filter_kernel,
        grid=(1,),
        in_specs=[
            full((S, E)), lay((E, H)), lay((1, H)), lay((n_inner, H, H)), lay((n_inner, 1, H)),
            lay((H, n_out)), lay((1, H)), full((S, HYENA_WIDTH)),
        ],
        out_specs=full((S, n_out)),
        out_shape=jax.ShapeDtypeStruct((S, n_out), BF16),
        compiler_params=_cparams("arbitrary"),
        name="hyena_filter",
    )(zfeat, w1, f_b1.reshape(-1, 1, H), f_w_inner, f_b_inner.reshape(-1, n_inner, 1, H), f_w_out,
      f_freq.reshape(-1, 1, H), decay)


def _spec_kernel(fa_ref, fb_ref, h_ref, pre_ref, pim_ref, q1re_ref, q1im_ref, q2re_ref, q2im_ref, sp0_ref):
    TK = fa_ref.shape[0] // 2
    C = HYENA_WIDTH
    n_fft = 2 * h_ref.shape[0]
    xa = _dot(fa_ref[...], h_ref[...])
    xb = _dot(fb_ref[...], h_ref[...])
    row = pl.program_id(0) * TK + lax.broadcasted_iota(jnp.int32, (TK, 1), 0)
    theta = row.astype(F32) * (2.0 * math.pi / n_fft)
    cj, sj = jnp.cos(theta), jnp.sin(theta)
    sc = 2.0 / n_fft
    for o in range(pre_ref.shape[0]):
        hf = slice(2 * o * C, (2 * o + 1) * C)
        hb = slice((2 * o + 1) * C, (2 * o + 2) * C)
        ha_re = xa[:TK, hf] + xa[:TK, hb]
        ha_im = xa[TK:, hf] - xa[TK:, hb]
        hm_re = xb[:TK, hf] + xb[:TK, hb]
        hm_im = xb[TK:, hf] - xb[TK:, hb]
        p_re, p_im = (ha_re + hm_re) * sc, (ha_im - hm_im) * sc
        q_re, q_im = (ha_re - hm_re) * sc, (ha_im + hm_im) * sc
        pre_ref[o] = p_re
        pim_ref[o] = p_im
        q1re_ref[o] = q_re * cj + q_im * sj
        q1im_ref[o] = q_im * cj - q_re * sj
        q2re_ref[o] = q_re * cj - q_im * sj
        q2im_ref[o] = q_im * cj + q_re * sj

        @pl.when(pl.program_id(0) == 0)
        def _():
            h0, hs = ha_re[0:1], hm_re[0:1]
            hh_re = xa[TK:TK + 1, hf] + xa[TK:TK + 1, hb]
            hh_im = xb[TK:TK + 1, hf] - xb[TK:TK + 1, hb]
            sp0_ref[o] = jnp.concatenate(
                [(h0 + hs) * (0.5 * sc), (h0 - hs) * (0.5 * sc), hh_re * sc, hh_im * sc,
                 jnp.zeros((4, C), F32)], axis=0)


def _filter_spectra(fa, fb, filt):
    nkt, tk2, S = fa.shape
    TK = tk2 // 2
    C = HYENA_WIDTH
    n_ord = filt.shape[1] // (2 * C)
    out = jax.ShapeDtypeStruct((n_ord, nkt * TK, C), F32)
    ospec = pl.BlockSpec((n_ord, TK, C), lambda k: (0, k, 0))
    fspec = pl.BlockSpec((None, tk2, S), lambda k: (k, 0, 0))
    return pl.pallas_call(
        _spec_kernel,
        grid=(nkt,),
        in_specs=[fspec, fspec, pl.BlockSpec(filt.shape, lambda k: (0, 0))],
        out_specs=[ospec] * 6 + [pl.BlockSpec((n_ord, 8, C), lambda k: (0, 0, 0))],
        out_shape=[out] * 6 + [jax.ShapeDtypeStruct((n_ord, 8, C), F32)],
        compiler_params=_cparams("arbitrary"),
        name="hyena_spectra",
    )(fa, fb, filt)


def _hyena_kernel(v_ref, x1_ref, x2_ref, f_ref, g_ref, pre_ref, pim_ref, q1re_ref, q1im_ref, q2re_ref, q2im_ref,
                  sp0_ref, wsc_ref, bsc_ref, hb_ref, o_ref, pad_ref, z_ref, zb_ref, gate_ref, zcat_ref, il_ref):
    H, C2 = z_ref.shape
    C = C2 // 2
    n_sub, tk2, _ = f_ref.shape
    TK = tk2 // 2
    TT = g_ref.shape[0]
    nf = H // (n_sub * TK)
    R = ROW_CHUNK
    P = PAD_ROWS
    o = pl.program_id(1)
    s = pl.program_id(2)
    even, odd = slice(0, C), slice(C, C2)

    def fill_pad(in_ref):
        for par in range(2):
            pad_ref[par, 0:P, :] = jnp.zeros((P, C), F32)
            pad_ref[par, H + P:H + 2 * P, :] = jnp.zeros((P, C), F32)
            pad_ref[par, P:H + P, :] = in_ref[par].astype(F32)

    def conv_chunk(r0, w, bias):
        ev = pad_ref[0, P + r0:P + r0 + R, :]
        od = pad_ref[1, P + r0:P + r0 + R, :]
        od_prev = pad_ref[1, P + r0 - 1:P + r0 - 1 + R, :]
        ev_next = pad_ref[0, P + r0 + 1:P + r0 + 1 + R, :]
        w0, w1, w2 = w[0:1], w[1:2], w[2:3]
        return w0 * od_prev + w1 * ev + w2 * od + bias, w0 * ev + w1 * od + w2 * ev_next + bias

    @pl.when((o == 0) & (s == 0))
    def _():
        fill_pad(v_ref)
        for c in range(H // R):
            rows = slice(c * R, (c + 1) * R)
            conv_e, conv_o = conv_chunk(c * R, wsc_ref[:, 0:C], bsc_ref[:, 0:C])
            z_ref[rows, even] = conv_e
            z_ref[rows, odd] = conv_o
            zb_ref[rows, even] = conv_e.astype(BF16)
            zb_ref[rows, odd] = conv_o.astype(BF16)

    @pl.when((s == 0) & (o == 0))
    def _():
        fill_pad(x1_ref)

    @pl.when((s == 0) & (o == 1))
    def _():
        fill_pad(x2_ref)

    @pl.when(s < nf)
    def _():
        a, b, c, d = sp0_ref[0:1, :], sp0_ref[1:2, :], sp0_ref[2:3, :], sp0_ref[3:4, :]
        for sub in range(n_sub):
            tile = s * n_sub + sub
            cs = slice(sub * TK, (sub + 1) * TK)
            x = _dot(f_ref[sub], zb_ref[...])
            e_re, o_re, e_im, o_im = x[:TK, even], x[:TK, odd], x[TK:, even], x[TK:, odd]
            p_re, p_im = pre_ref[cs, :], pim_ref[cs, :]
            q1_re, q1_im, q2_re, q2_im = q1re_ref[cs, :], q1im_ref[cs, :], q2re_ref[cs, :], q2im_ref[cs, :]
            ze_re = e_re * p_re - e_im * p_im + o_re * q1_re - o_im * q1_im
            ze_im = e_re * p_im + e_im * p_re + o_re * q1_im + o_im * q1_re
            zo_re = e_re * q2_re - e_im * q2_im + o_re * p_re - o_im * p_im
            zo_im = e_re * q2_im + e_im * q2_re + o_re * p_im + o_im * p_re
            first = (tile * TK + lax.broadcasted_iota(jnp.int32, (TK, 1), 0)) == 0
            ze_re = jnp.where(first, e_re[0:1] * a + o_re[0:1] * b, ze_re)
            zo_re = jnp.where(first, e_re[0:1] * b + o_re[0:1] * a, zo_re)
            ze_im = jnp.where(first, e_im[0:1] * c + o_im[0:1] * d, ze_im)
            zo_im = jnp.where(first, o_im[0:1] * c - e_im[0:1] * d, zo_im)
            base = pl.multiple_of(tile * 2 * TK, 2 * TK)
            zcat_ref[pl.ds(base, TK), even] = ze_re.astype(BF16)
            zcat_ref[pl.ds(base, TK), odd] = zo_re.astype(BF16)
            zcat_ref[pl.ds(base + TK, TK), even] = ze_im.astype(BF16)
            zcat_ref[pl.ds(base + TK, TK), odd] = zo_im.astype(BF16)

    @pl.when(s == nf)
    def _():
        w = jnp.where(o == 0, wsc_ref[:, C:2 * C], wsc_ref[:, 2 * C:3 * C])
        bias = jnp.where(o == 0, bsc_ref[:, C:2 * C], bsc_ref[:, 2 * C:3 * C])
        for c in range(H // R):
            conv_e, conv_o = conv_chunk(c * R, w, bias)
            gate_ref[c * R:(c + 1) * R, even] = conv_e
            gate_ref[c * R:(c + 1) * R, odd] = conv_o

    @pl.when(s >= nf)
    def _():
        tt = s - nf
        rows = pl.ds(pl.multiple_of(tt * TT, TT), TT)
        y = _dot(g_ref[...], zcat_ref[...])
        hrow = jnp.where(o == 0, hb_ref[0:1, :], hb_ref[1:2, :])
        hbias = jnp.concatenate([hrow, hrow], axis=1)
        znew = gate_ref[rows, :] * (y + z_ref[rows, :] * hbias)
        z_ref[rows, :] = znew
        zb_ref[rows, :] = znew.astype(BF16)

        @pl.when(o == pl.num_programs(1) - 1)
        def _():
            for p in range(C // LANES):
                il_ref[p, pl.ds(0, TT, stride=2), :] = znew[:, p * LANES:(p + 1) * LANES]
                il_ref[p, pl.ds(1, TT, stride=2), :] = znew[:, C + p * LANES:C + (p + 1) * LANES]
            out_rows = pl.ds(pl.multiple_of(tt * 2 * TT, 2 * TT), 2 * TT)
            for p in range(C // LANES):
                o_ref[out_rows, p * LANES:(p + 1) * LANES] = il_ref[p].astype(BF16)


def _hyena(hv, hx1, hx2, fhalf, ghalf, coeffs, w_sconv, b_sconv, hy_bias, layer):
    B, _, H, C = hv.shape
    S = 2 * H
    n_tiles, tk2, _ = fhalf.shape
    TK = tk2 // 2
    n_sub = HYENA_FREQ_TILES_PER_STEP
    nf = n_tiles // n_sub
    TT = HYENA_TIME_TILE
    ni = H // TT
    sp0 = coeffs[-1]
    n_ord = sp0.shape[0]
    fwd = lambda s: jnp.minimum(s, nf - 1)
    inv = lambda s: jnp.clip(s - nf, 0, ni - 1)
    inp = pl.BlockSpec((None, 2, H, C), lambda bb, o, s: (bb, 0, 0, 0), pipeline_mode=pl.Buffered(1))
    coef = pl.BlockSpec((None, n_sub * TK, C), lambda bb, o, s: (o, fwd(s), 0))
    return pl.pallas_call(
        _hyena_kernel,
        grid=(B, n_ord, nf + ni),
        in_specs=[
            inp, inp, inp,
            pl.BlockSpec((n_sub, tk2, H), lambda bb, o, s: (fwd(s), 0, 0)),
            pl.BlockSpec((TT, 2 * H), lambda bb, o, s: (inv(s), 0)),
            coef, coef, coef, coef, coef, coef,
            pl.BlockSpec((None, 8, C), lambda bb, o, s: (o, 0, 0)),
            pl.BlockSpec((None, 3, 3 * C), lambda bb, o, s: (layer, 0, 0)),
            pl.BlockSpec((None, 1, 3 * C), lambda bb, o, s: (layer, 0, 0)),
            pl.BlockSpec((None, n_ord, C), lambda bb, o, s: (layer, 0, 0)),
        ],
        out_specs=pl.BlockSpec((None, S, C), lambda bb, o, s: (bb, 0, 0)),
        out_shape=jax.ShapeDtypeStruct((B, S, C), BF16),
        scratch_shapes=[
            pltpu.VMEM((2, H + 2 * PAD_ROWS, C), F32),
            pltpu.VMEM((H, 2 * C), F32),
            pltpu.VMEM((H, 2 * C), BF16),
            pltpu.VMEM((H, 2 * C), F32),
            pltpu.VMEM((2 * H, 2 * C), BF16),
            pltpu.VMEM((C // LANES, 2 * TT, LANES), F32),
        ],
        compiler_params=_cparams("arbitrary", "arbitrary", "arbitrary"),
        name="hyena_conv",
    )(hv, hx1, hx2, fhalf, ghalf, *coeffs, w_sconv, b_sconv.reshape(b_sconv.shape[0], 1, 3 * C), hy_bias)


def _packed_dft(bins, n_samples, n_fft, tile, sin_row0):
    theta = ((bins[:, None] * np.arange(n_samples)[None, :]) % n_fft) * (2.0 * math.pi / n_fft)
    fc = np.cos(theta)
    fs = -np.sin(theta)
    fs[0] = sin_row0
    nt = bins.shape[0] // tile
    packed = np.concatenate([fc.reshape(nt, tile, n_samples), fs.reshape(nt, tile, n_samples)], axis=1)
    return packed.astype(np.float32).astype(BF16)


@functools.lru_cache(maxsize=None)
def _dft_matrices(S):
    H, N = S // 2, 2 * S
    j = np.arange(H)
    m = np.arange(H)
    n = np.arange(S)
    quarter = np.asarray([1.0, 0.0, -1.0, 0.0])
    fhalf = _packed_dft(j, H, S, HYENA_FREQ_TILE, 1.0 - 2.0 * (m & 1))
    ghalf = np.ascontiguousarray(fhalf.reshape(2 * H, H).T)
    fa = _packed_dft(j, S, N, SPECTRA_TILE, quarter[n & 3])
    fb = _packed_dft(S - j, S, N, SPECTRA_TILE, -quarter[(n + 3) & 3])
    return fhalf, ghalf, fa, fb


def _filter_features(S):
    t = jnp.linspace(0.0, 1.0, S, dtype=F32)[:, None]
    bands = (FILTER_EMB - 1) // 2
    w = 2.0 * math.pi * jnp.arange(S, dtype=F32) / S
    f = jnp.linspace(1e-4, bands - 1, bands, dtype=F32)
    ang = w[:, None] * f[None, :]
    z = jnp.concatenate([t, jnp.cos(ang), -jnp.sin(ang)], axis=-1)
    z = jnp.pad(z, ((0, 0), (0, 128 - FILTER_EMB)))
    min_decay = math.log(DECAY_TARGET) / SLOW_DECAY_PCT
    max_decay = math.log(DECAY_TARGET) / FAST_DECAY_PCT
    deltas = jnp.linspace(min_decay, max_decay, HYENA_WIDTH, dtype=F32)
    decay = jnp.exp(-t * jnp.abs(deltas)[None, :])
    return z, decay


def _merge_kernel(ya_ref, yb_ref, yc_ref, gl_ref, x_ref, mod_ref, pa_ref, pp_ref, ph_ref, wo_ref,
                  g_ref, b_ref, wr_ref, br_ref, x1_ref, u2_ref, lt_ref):
    D = x_ref.shape[1]
    sig = lambda v: 1.0 / (1.0 + jnp.exp(-v))
    merged = sig(gl_ref[:, 0:D].astype(F32)) * _dot(ya_ref[...], pa_ref[...])
    merged += sig(gl_ref[:, D:2 * D].astype(F32)) * _dot(yb_ref[...], pp_ref[...])
    merged += sig(gl_ref[:, 2 * D:3 * D].astype(F32)) * _dot(yc_ref[...], ph_ref[...])
    h = _dot(merged.astype(BF16), wo_ref[...])
    gate1, shift2, scale2 = mod_ref[2:3, :], mod_ref[3:4, :], mod_ref[4:5, :]
    x1 = _layer_norm(ALPHA_RES * x_ref[...] + (1.0 + gate1) * h, g_ref[...], b_ref[...])
    x1_ref[...] = x1
    u2 = x1 * (1.0 + scale2) + shift2
    _store_token_tiles(u2_ref, u2)
    TM = u2.shape[0]
    u_hi = u2.astype(BF16)
    u_lo = (u2 - u_hi.astype(F32)).astype(BF16)
    r = _dot(jnp.concatenate([u_hi, u_lo], axis=0), wr_ref[...])
    rt = (r[:TM] + r[TM:] + br_ref[...]).T
    lt_ref[...] = rt[0:N_EXPERTS, :] + rt[N_EXPERTS:2 * N_EXPERTS, :]


def _merge(ya, yb, yc, proj, x, mod, p_attn, p_pool, p_hyena, w_out, ln_g, ln_b, w_router, b_router, layer):
    B, S, D = x.shape
    N = B * S
    TM = MERGE_ROWS
    tiles_per_seq = S // TM
    EP = 128
    flat = lambda t: t.reshape(N, t.shape[-1])
    w_hi = w_router[layer].astype(BF16)
    w_lo = (w_router[layer] - w_hi.astype(F32)).astype(BF16)
    wr = jnp.zeros((D, EP), BF16).at[:, :N_EXPERTS].set(w_hi).at[:, N_EXPERTS:2 * N_EXPERTS].set(w_lo)
    br = jnp.zeros((1, EP), F32).at[:, :N_EXPERTS].set(b_router[layer][None])
    tok = lambda w: pl.BlockSpec((TM, w), lambda i: (i, 0))
    lay = lambda t: pl.BlockSpec((None,) + t.shape[1:], lambda i: (layer,) + (0,) * (t.ndim - 1))
    ln_g3, ln_b3 = ln_g.reshape(-1, 1, D), ln_b.reshape(-1, 1, D)
    return pl.pallas_call(
        _merge_kernel,
        grid=(N // TM,),
        in_specs=[
            tok(ya.shape[-1]), tok(yb.shape[-1]), tok(yc.shape[-1]), tok(3 * D), tok(D),
            pl.BlockSpec((None, 6, D), lambda i: (i // tiles_per_seq, 0, 0)),
            lay(p_attn), lay(p_pool), lay(p_hyena), lay(w_out), lay(ln_g3), lay(ln_b3),
            pl.BlockSpec((D, EP), lambda i: (0, 0)),
            pl.BlockSpec((1, EP), lambda i: (0, 0)),
        ],
        out_specs=[tok(D), pl.BlockSpec((TM * D // LANES, LANES), lambda i: (i, 0)),
                   pl.BlockSpec((N_EXPERTS, TM), lambda i: (0, i))],
        out_shape=[
            jax.ShapeDtypeStruct((N, D), F32),
            jax.ShapeDtypeStruct((N * D // LANES, LANES), F32),
            jax.ShapeDtypeStruct((N_EXPERTS, N), F32),
        ],
        compiler_params=_cparams("arbitrary"),
        name="merge_ln1_router",
    )(flat(ya), flat(yb), flat(yc), flat(proj), flat(x), mod, p_attn, p_pool, p_hyena, w_out,
      ln_g3, ln_b3, wr, br)


def _topk_kernel(l_ref, e_ref, g_ref, r_ref, cnt_ref, carry_ref):
    E, TL = l_ref.shape

    @pl.when(pl.program_id(0) == 0)
    def _():
        carry_ref[...] = jnp.zeros_like(carry_ref)

    l = l_ref[...]
    eio = lax.broadcasted_iota(jnp.int32, (E, TL), 0).astype(F32)
    vals, hots = [], []
    for k in range(TOP_K):
        m = jnp.max(l, axis=0, keepdims=True)
        idx = jnp.min(jnp.where(l == m, eio, float(E)), axis=0, keepdims=True)
        hot = eio == idx
        l = jnp.where(hot, -jnp.inf, l)
        vals.append(m)
        hots.append(hot)
        e_ref[k:k + 1, :] = idx.astype(jnp.int32)
    exps = [jnp.exp(v - vals[0]) for v in vals]
    inv = 1.0 / (exps[0] + exps[1] + exps[2] + exps[3])
    for k in range(TOP_K):
        g_ref[k:k + 1, :] = exps[k] * inv

    memb = jnp.zeros((E, TL), F32)
    for hot in hots:
        memb = memb + hot.astype(F32)
    upper = (lax.broadcasted_iota(jnp.int32, (TL, TL), 0)
             <= lax.broadcasted_iota(jnp.int32, (TL, TL), 1)).astype(BF16)
    incl = _dot(memb.astype(BF16), upper)
    excl = incl - memb + carry_ref[:, 0:1]
    for k in range(TOP_K):
        rank = jnp.sum(jnp.where(hots[k], excl, 0.0), axis=0, keepdims=True)
        r_ref[k:k + 1, :] = rank.astype(jnp.int32)
    carry_ref[...] = carry_ref[...] + jnp.sum(memb, axis=1, keepdims=True)
    cnt_ref[...] = carry_ref[...]


def _topk(logits_t):
    E, N = logits_t.shape
    TL = TOPK_LANES
    tok = pl.BlockSpec((TOP_K, TL), lambda i: (0, i))
    return pl.pallas_call(
        _topk_kernel,
        grid=(N // TL,),
        in_specs=[pl.BlockSpec((E, TL), lambda i: (0, i))],
        out_specs=[tok, tok, tok, pl.BlockSpec((E, 128), lambda i: (0, 0))],
        out_shape=[
            jax.ShapeDtypeStruct((TOP_K, N), jnp.int32),
            jax.ShapeDtypeStruct((TOP_K, N), F32),
            jax.ShapeDtypeStruct((TOP_K, N), jnp.int32),
            jax.ShapeDtypeStruct((E, 128), F32),
        ],
        scratch_shapes=[pltpu.VMEM((E, 128), F32)],
        compiler_params=_cparams("arbitrary"),
        name="topk_rank",
    )(logits_t)


def _dest_kernel(offs_ref, e_ref, r_ref, d_ref):
    e = e_ref[...]
    acc = r_ref[...]
    for ex in range(N_EXPERTS):
        acc = acc + jnp.where(e == ex, offs_ref[ex], 0)
    d_ref[...] = acc


def _dest_rows(offs, top_e, rank):
    K, N = top_e.shape
    TL = TOPK_LANES
    tok = pl.BlockSpec((K, TL), lambda i, offs: (0, i))
    return pl.pallas_call(
        _dest_kernel,
        grid_spec=pltpu.PrefetchScalarGridSpec(
            num_scalar_prefetch=1, grid=(N // TL,), in_specs=[tok, tok], out_specs=tok),
        out_shape=jax.ShapeDtypeStruct((K, N), jnp.int32),
        compiler_params=_cparams("arbitrary"),
        name="dest_rows",
    )(offs, top_e, rank)


def _dispatch_kernel(tail_ref, n_used_ref, dest_ref, u_ref, xs_ref, zero_ref, sem):
    TT = dest_ref.shape[0] // TOP_K
    RT = u_ref.shape[0] // TT
    blk_rows = zero_ref.shape[0]
    n_blocks = xs_ref.shape[0] // blk_rows

    def zero_copy(first_row):
        start = pl.multiple_of(first_row * RT, RT)
        return pltpu.make_async_copy(zero_ref, xs_ref.at[pl.ds(start, blk_rows), :], sem)

    @pl.when(pl.program_id(0) == 0)
    def _():
        zero_ref[...] = jnp.zeros_like(zero_ref)

        def tails(fn):
            def body(e, carry):
                @pl.when(tail_ref[e] >= 0)
                def _():
                    fn(zero_copy(jnp.maximum(tail_ref[e], 0)))
                return carry
            lax.fori_loop(0, N_EXPERTS, body, 0)

        def unused(fn):
            def body(i, carry):
                fn(zero_copy(i * (blk_rows // RT)))
                return carry
            lax.fori_loop(n_used_ref[0], n_blocks, body, 0)

        tails(lambda cp: cp.start())
        unused(lambda cp: cp.start())
        tails(lambda cp: cp.wait())
        unused(lambda cp: cp.wait())

    def row_copy(j, k):
        src = pl.multiple_of(j * RT, RT)
        dst = pl.multiple_of(dest_ref[j * TOP_K + k] * RT, RT)
        return pltpu.make_async_copy(u_ref.at[pl.ds(src, RT), :], xs_ref.at[pl.ds(dst, RT), :], sem)

    def start(j, carry):
        for k in range(TOP_K):
            row_copy(j, k).start(priority=k % DMA_PRIORITIES)
        return carry

    lax.fori_loop(0, TT, start, 0, unroll=DMA_ISSUE_UNROLL)
    for k in range(TOP_K):
        pltpu.make_async_copy(u_ref, xs_ref.at[pl.ds(0, TT * RT), :], sem).wait()


def _dispatch(tail_start, n_used, dest, u2_tiles, n_rows, D):
    RT = D // LANES
    N = u2_tiles.shape[0] // RT
    TT = MOE_TOK
    return pl.pallas_call(
        _dispatch_kernel,
        grid_spec=pltpu.PrefetchScalarGridSpec(
            num_scalar_prefetch=2,
            grid=(N // TT,),
            in_specs=[
                pl.BlockSpec((TOP_K * TT,), lambda i, tail, nu: (i,), memory_space=pltpu.SMEM),
                pl.BlockSpec((TT * RT, LANES), lambda i, tail, nu: (i, 0)),
            ],
            out_specs=pl.BlockSpec(memory_space=pl.ANY),
            scratch_shapes=[pltpu.VMEM((MOE_BLK * RT, LANES), F32), pltpu.SemaphoreType.DMA(())],
        ),
        out_shape=jax.ShapeDtypeStruct((n_rows * RT, LANES), F32),
        compiler_params=_cparams("arbitrary"),
        name="moe_dispatch",
    )(tail_start, n_used, dest, u2_tiles)


def _ffn_kernel(blk_e_ref, n_used_ref, xs_ref, w1_ref, b1_ref, w2_ref, b2_ref, ys_ref,
                w1b_ref, w2b_ref, act_ref):
    F, D = w2_ref.shape
    BLK = xs_ref.shape[0] * LANES // D
    i = pl.program_id(0)
    used = i < n_used_ref[0]
    new_expert = (i == 0) | (blk_e_ref[i] != blk_e_ref[jnp.maximum(i - 1, 0)])

    @pl.when(used & new_expert)
    def _():
        for r in range(0, D, WEIGHT_CAST_ROWS):
            w1b_ref[r:r + WEIGHT_CAST_ROWS, :] = w1_ref[r:r + WEIGHT_CAST_ROWS, :].astype(BF16)
        for r in range(0, F, WEIGHT_CAST_ROWS):
            w2b_ref[r:r + WEIGHT_CAST_ROWS, :] = w2_ref[r:r + WEIGHT_CAST_ROWS, :].astype(BF16)

    @pl.when(used)
    def _():
        x = jnp.concatenate(_load_token_tiles(xs_ref, BLK, D), axis=1).astype(BF16)
        for c in range(0, F, FFN_CHUNK):
            cg = slice(c, c + FFN_CHUNK)
            cl = slice(F + c, F + c + FFN_CHUNK)
            glu = jnp.minimum(_dot(x, w1b_ref[:, cg]) + b1_ref[:, cg], SWIGLU_LIMIT)
            lin = jnp.clip(_dot(x, w1b_ref[:, cl]) + b1_ref[:, cl], -SWIGLU_LIMIT, SWIGLU_LIMIT)
            act = glu * (1.0 / (1.0 + jnp.exp(-SWIGLU_ALPHA * glu))) * (lin + 1.0)
            act_ref[:, cg] = act.astype(BF16)
        _store_token_tiles(ys_ref, _dot(act_ref[...], w2b_ref[...]) + b2_ref[...])

    @pl.when(jnp.logical_not(used))
    def _():
        ys_ref[...] = jnp.zeros_like(ys_ref)


def _expert_ffn(blk_e, n_used, xs_tiles, w1, b1, w2, b2, layer):
    _, E, D, F2 = w1.shape
    F = F2 // 2
    RT = D // LANES
    BLK = MOE_BLK
    rows = pl.BlockSpec((BLK * RT, LANES), lambda i, be, nu: (i, 0))
    per_expert = lambda r, c: pl.BlockSpec((None, None, r, c), lambda i, be, nu: (layer, be[i], 0, 0))
    return pl.pallas_call(
        _ffn_kernel,
        grid_spec=pltpu.PrefetchScalarGridSpec(
            num_scalar_prefetch=2,
            grid=(xs_tiles.shape[0] // (BLK * RT),),
            in_specs=[rows, per_expert(D, F2), per_expert(1, F2), per_expert(F, D), per_expert(1, D)],
            out_specs=rows,
            scratch_shapes=[pltpu.VMEM((D, F2), BF16), pltpu.VMEM((F, D), BF16), pltpu.VMEM((BLK, F), BF16)],
        ),
        out_shape=jax.ShapeDtypeStruct(xs_tiles.shape, F32),
        compiler_params=_cparams("arbitrary"),
        name="moe_ffn",
    )(blk_e, n_used, xs_tiles, w1, b1.reshape(-1, E, 1, F2), w2, b2.reshape(-1, E, 1, D))


def _combine_kernel(dest_ref, dest_next_ref, gt_ref, x1_ref, mod_ref, g_ref, b_ref, ys_ref, o_ref,
                    buf_a, buf_b, sem):
    TT, D = x1_ref.shape
    RT = D // LANES
    RC = TT // COMBINE_CHUNKS
    i = pl.program_id(0)
    last = pl.num_programs(0) - 1

    def row_copy(d_ref, buf, s, j, k):
        src = pl.multiple_of(d_ref[j * TOP_K + k] * RT, RT)
        dst = pl.multiple_of(j * RT, RT)
        return pltpu.make_async_copy(ys_ref.at[pl.ds(src, RT), :], buf.at[k, pl.ds(dst, RT), :], sem.at[s])

    def wait_tile(buf, s):
        for k in range(TOP_K):
            pltpu.make_async_copy(ys_ref.at[pl.ds(0, TT * RT), :], buf.at[k], sem.at[s]).wait()

    def reduce_rows(buf, r0):
        rows = slice(r0, r0 + RC)
        chunks = None
        for k in range(TOP_K):
            gate = gt_ref[rows, k:k + 1]
            tiles = [gate * buf[k, pl.ds(r0 * RT + c, RC, stride=RT), :] for c in range(RT)]
            chunks = tiles if chunks is None else [a + b for a, b in zip(chunks, tiles)]
        h = jnp.concatenate(chunks, axis=1)
        gate2 = mod_ref[5:6, :]
        o_ref[rows, :] = _layer_norm(ALPHA_RES * x1_ref[rows, :] + (1.0 + gate2) * h, g_ref[...], b_ref[...])

    def run(cur, s_cur, nxt, s_nxt):
        wait_tile(cur, s_cur)
        for ch in range(COMBINE_CHUNKS):
            reduce_rows(cur, ch * RC)
            for j in range(ch * RC, (ch + 1) * RC):
                for k in range(TOP_K):
                    row_copy(dest_next_ref, nxt, s_nxt, j, k).start(priority=k % DMA_PRIORITIES)

        @pl.when(i == last)
        def _():
            wait_tile(nxt, s_nxt)

    @pl.when(i == 0)
    def _():
        def body(j, carry):
            for k in range(TOP_K):
                row_copy(dest_ref, buf_a, 0, j, k).start(priority=k % DMA_PRIORITIES)
            return carry
        lax.fori_loop(0, TT, body, 0, unroll=DMA_ISSUE_UNROLL)

    @pl.when(i % 2 == 0)
    def _():
        run(buf_a, 0, buf_b, 1)

    @pl.when(i % 2 == 1)
    def _():
        run(buf_b, 1, buf_a, 0)


def _combine(dest, gates_t, x1, mod, ln_g, ln_b, ys, seq_len, layer):
    N, D = x1.shape
    TT = MOE_TOK
    tiles_per_seq = seq_len // TT
    ln_g3, ln_b3 = ln_g.reshape(-1, 1, D), ln_b.reshape(-1, 1, D)
    lay = lambda t: pl.BlockSpec((None,) + t.shape[1:], lambda i: (layer,) + (0,) * (t.ndim - 1))
    return pl.pallas_call(
        _combine_kernel,
        grid=(N // TT,),
        in_specs=[
            pl.BlockSpec((TOP_K * TT,), lambda i: (i,), memory_space=pltpu.SMEM),
            pl.BlockSpec((TOP_K * TT,), lambda i: (jnp.minimum(i + 1, N // TT - 1),), memory_space=pltpu.SMEM),
            pl.BlockSpec((TT, TOP_K), lambda i: (i, 0)),
            pl.BlockSpec((TT, D), lambda i: (i, 0)),
            pl.BlockSpec((None, 6, D), lambda i: (i // tiles_per_seq, 0, 0)),
            lay(ln_g3), lay(ln_b3),
            pl.BlockSpec(memory_space=pl.ANY),
        ],
        out_specs=pl.BlockSpec((TT, D), lambda i: (i, 0)),
        out_shape=jax.ShapeDtypeStruct((N, D), F32),
        scratch_shapes=[pltpu.VMEM((TOP_K, TT * D // LANES, LANES), F32)] * 2 + [pltpu.SemaphoreType.DMA((2,))],
        compiler_params=_cparams("arbitrary"),
        name="moe_combine_ln2",
    )(dest, dest, gates_t, x1, mod, ln_g3, ln_b3, ys)


def _moe(u2_tiles, logits_t, x1, mod, w1, b1, w2, b2, ln_g, ln_b, seq_len, layer):
    N, D = x1.shape
    BLK = MOE_BLK
    top_e, gates, rank, counts = _topk(logits_t)
    counts = counts[:, 0].astype(jnp.int32)
    padded = (counts + BLK - 1) // BLK * BLK
    pend = jnp.cumsum(padded)
    offs = pend - padded
    n_blocks = -(-(N * TOP_K) // BLK) + N_EXPERTS
    blk_start = jnp.arange(n_blocks, dtype=jnp.int32) * BLK
    blk_e = jnp.sum((pend[None, :] <= blk_start[:, None]).astype(jnp.int32), axis=1)
    blk_e = jnp.minimum(blk_e, N_EXPERTS - 1).astype(jnp.int32)
    n_used = (pend[-1:] // BLK).astype(jnp.int32)
    tail_start = jnp.where(padded > 0, pend - BLK, -1).astype(jnp.int32)
    dest = _dest_rows(offs.astype(jnp.int32), top_e, rank).T.reshape(-1)
    xs = _dispatch(tail_start, n_used, dest, u2_tiles, n_blocks * BLK, D)
    ys = _expert_ffn(blk_e, n_used, xs, w1, b1, w2, b2, layer)
    return _combine(dest, gates.T, x1, mod, ln_g, ln_b, ys, seq_len, layer)


def _split_w_in(w_in):
    AW = N_ATTN_HEADS * HEAD_DIM
    GW = GROUP_WIDTH
    C = HYENA_WIDTH
    w = w_in.astype(BF16)
    w_nat = jnp.concatenate([w[:, 3 * AW + 4 * C:], w[:, 3 * AW:3 * AW + C]], axis=1)
    col_scale = (1.0 / math.sqrt(HEAD_DIM), 1.0, 1.0)
    w_qkv = jnp.stack([
        jnp.concatenate([w[:, t * AW + g * GW:t * AW + (g + 1) * GW] * col_scale[t] for t in range(3)], axis=1)
        for g in range(len(ATTN_GROUPS))]).astype(BF16)
    w_hy = jnp.stack([w[:, 3 * AW + (1 + t) * C:3 * AW + (2 + t) * C] for t in range(3)])
    return w_nat, w_qkv, w_hy


def kernel(x, c, w_ada, b_ada, w_in, w_sconv, b_sconv, w_pool, pool_scale, f_w1, f_b1, f_w_inner, f_b_inner, f_w_out, f_freq, hy_bias, p_attn, p_pool, p_hyena, w_out, ln1_g, ln1_b, w_router, b_router, w1, b1, w2, b2, ln2_g, ln2_b):
    B, S, D = x.shape
    depth = w_in.shape[0]
    bias_tables = [jnp.asarray(t) for t in _attn_bias_tables(S)]
    fhalf, ghalf, fa, fb = (jnp.asarray(m) for m in _dft_matrices(S))
    zfeat, decay = _filter_features(S)
    gate_cols = 3 * D
    pool_block = gate_cols // HYENA_WIDTH
    bf = lambda t: t.astype(BF16)
    p_attn_b, p_pool_b, p_hyena_b, w_out_b = map(bf, (p_attn, p_pool, p_hyena, w_out))

    for l in range(depth):
        mod = _ada(c, w_ada, b_ada, l).reshape(B, 6, D)
        w_nat, w_qkv, w_hy = _split_w_in(w_in[l])
        proj, u = _inproj_nat(x, mod, w_nat)
        qkv = _inproj_dil(u, w_qkv, [d for _, d in ATTN_GROUPS])
        hv, hx1, hx2 = _inproj_dil(u, w_hy, [2, 2, 2])
        ya = _attention(qkv, bias_tables)
        yb = _pool(proj, pool_block, w_pool, pool_scale, l)
        filt = _hyena_filters(zfeat, decay, f_w1, f_b1, f_w_inner, f_b_inner, f_w_out, f_freq, l)
        coeffs = _filter_spectra(fa, fb, filt)
        yc = _hyena(hv, hx1, hx2, fhalf, ghalf, coeffs, w_sconv, b_sconv, hy_bias, l)
        x1, u2, logits_t = _merge(ya, yb, yc, proj, x, mod, p_attn_b, p_pool_b, p_hyena_b, w_out_b,
                                  ln1_g, ln1_b, w_router, b_router, l)
        x = _moe(u2, logits_t, x1, mod, w1, b1, w2, b2, ln2_g, ln2_b, S, l).reshape(B, S, D)
    return x
```

```python
import functools
import math

import jax
import jax.numpy as jnp
import numpy as np
from jax import lax
from jax.experimental import pallas as pl
from jax.experimental.pallas import tpu as pltpu

F32 = jnp.float32
BF16 = jnp.bfloat16

HEAD_DIM = 64
HEADS_PER_GROUP = 4
ATTN_GROUPS = ((128, 1), (512, 4), (2048, 16))
N_ATTN_HEADS = HEADS_PER_GROUP * len(ATTN_GROUPS)
GROUP_WIDTH = HEADS_PER_GROUP * HEAD_DIM
HALF_WINDOW = 64
POOL_WINDOWS = (2, 4, 8, 16)
POOL_GROUP = 128
HYENA_WIDTH = 512
FILTER_EMB = 33
FILTER_HIDDEN = 64
N_EXPERTS = 32
TOP_K = 4
SWIGLU_LIMIT = 7.0
SWIGLU_ALPHA = 1.702
LN_EPS = 1e-5
DEPTH = 2
ALPHA_RES = (2 * DEPTH) ** 0.25
DECAY_TARGET = 1e-2
FAST_DECAY_PCT = 0.3
SLOW_DECAY_PCT = 1.5

QBLK = 128
HYENA_FREQ_TILE = 256
HYENA_FREQ_TILES_PER_STEP = 1
HYENA_TIME_TILE = 512
SPECTRA_TILE = 256
INPROJ_COLS = 1792
ATTN_UNROLL = 16
ROW_CHUNK = 256
MERGE_ROWS = 512
TOPK_LANES = 1024
MOE_BLK = 512
FFN_CHUNK = 1024
WEIGHT_CAST_ROWS = 128
DMA_PRIORITIES = 2
DMA_ISSUE_UNROLL = 8
COMBINE_CHUNKS = 8
MOE_TOK = 256
PAD_ROWS = 8
NEG_BIG = -1e30
LANES = 128
V7X_VMEM_BYTES = 64 * 1024 * 1024
VMEM_LIMIT = V7X_VMEM_BYTES * 7 // 8


def _cparams(*sem):
    return pltpu.CompilerParams(dimension_semantics=sem, vmem_limit_bytes=VMEM_LIMIT)


def _dot(a, b):
    return jnp.dot(a, b, preferred_element_type=F32)


def _dot3(a, b):
    a_hi = a.astype(BF16)
    b_hi = b.astype(BF16)
    a_lo = (a - a_hi.astype(F32)).astype(BF16)
    b_lo = (b - b_hi.astype(F32)).astype(BF16)
    return _dot(a_hi, b_hi) + _dot(a_hi, b_lo) + _dot(a_lo, b_hi)


def _layer_norm(r, g, b):
    mu = jnp.mean(r, axis=-1, keepdims=True)
    c = r - mu
    var = jnp.mean(c * c, axis=-1, keepdims=True)
    return c * lax.rsqrt(var + LN_EPS) * g + b


def _store_token_tiles(ref, val):
    T, D = val.shape
    n = D // LANES
    for c in range(n):
        ref[pl.ds(c, T, stride=n), :] = val[:, c * LANES:(c + 1) * LANES]


def _load_token_tiles(ref, T, D, plane=None):
    n = D // LANES
    lead = () if plane is None else tuple(plane)
    idx = lambda c: lead + (pl.ds(c, T, stride=n), slice(None))
    return [ref[idx(c)] for c in range(n)]


def _ada_kernel(c_ref, w_ref, b_ref, o_ref):
    c = c_ref[...]
    cond = c * (1.0 / (1.0 + jnp.exp(-c)))
    o_ref[...] = _dot3(cond, w_ref[...]) + b_ref[...]


def _ada(c, w_ada, b_ada, layer):
    B, D = c.shape
    n_out = w_ada.shape[-1]
    tn = 768
    return pl.pallas_call(
        _ada_kernel,
        grid=(n_out // tn,),
        in_specs=[
            pl.BlockSpec((B, D), lambda j: (0, 0)),
            pl.BlockSpec((None, D, tn), lambda j: (layer, 0, j)),
            pl.BlockSpec((None, 1, tn), lambda j: (layer, 0, j)),
        ],
        out_specs=pl.BlockSpec((B, tn), lambda j: (0, j)),
        out_shape=jax.ShapeDtypeStruct((B, n_out), F32),
        compiler_params=_cparams("arbitrary"),
        name="ada",
    )(c, w_ada, b_ada.reshape(b_ada.shape[0], 1, n_out))


def _inproj_nat_kernel(x_ref, mod_ref, w_ref, o_ref, u_ref):
    S = x_ref.shape[0]

    @pl.when(pl.program_id(1) == 0)
    def _():
        shift = mod_ref[0:1, :]
        scale1p = 1.0 + mod_ref[1:2, :]

        def body(i, carry):
            rows = pl.ds(pl.multiple_of(i * ROW_CHUNK, ROW_CHUNK), ROW_CHUNK)
            u_ref[rows, :] = (x_ref[rows, :] * scale1p + shift).astype(BF16)
            return carry

        lax.fori_loop(0, S // ROW_CHUNK, body, 0)

    o_ref[...] = _dot(u_ref[...], w_ref[...]).astype(BF16)


def _inproj_nat(x, mod, w_nat):
    B, S, D = x.shape
    n_out = w_nat.shape[1]
    tn = INPROJ_COLS
    return pl.pallas_call(
        _inproj_nat_kernel,
        grid=(B, n_out // tn),
        in_specs=[
            pl.BlockSpec((None, S, D), lambda b, j: (b, 0, 0)),
            pl.BlockSpec((None, 6, D), lambda b, j: (b, 0, 0)),
            pl.BlockSpec((D, tn), lambda b, j: (0, j)),
        ],
        out_specs=[
            pl.BlockSpec((None, S, tn), lambda b, j: (b, 0, j)),
            pl.BlockSpec((None, S, D), lambda b, j: (b, 0, 0)),
        ],
        out_shape=[
            jax.ShapeDtypeStruct((B, S, n_out), BF16),
            jax.ShapeDtypeStruct((B, S, D), BF16),
        ],
        compiler_params=_cparams("arbitrary", "arbitrary"),
        name="inproj_nat",
    )(x, mod, w_nat)


def _inproj_dil_kernel(u_ref, w_ref, o0_ref, o1_ref, o2_ref, acc_ref, *, dils):
    S = u_ref.shape[0]
    for g, o_ref in enumerate((o0_ref, o1_ref, o2_ref)):
        d = dils[g]
        L = S // d
        acc = _dot(u_ref[...], w_ref[g])
        if d == 1:
            o_ref[0] = acc.astype(BF16)
            continue
        for c in range(acc_ref.shape[0]):
            acc_ref[c] = acc[:, c * LANES:(c + 1) * LANES]
        for rho in range(d):
            for c in range(acc_ref.shape[0]):
                o_ref[rho, :, c * LANES:(c + 1) * LANES] = (
                    acc_ref[c, pl.ds(rho, L, stride=d), :].astype(BF16))


def _inproj_dil(u, w_groups, dils):
    B, S, D = u.shape
    W = w_groups.shape[-1]
    assert len(dils) == w_groups.shape[0] == 3
    return pl.pallas_call(
        functools.partial(_inproj_dil_kernel, dils=tuple(dils)),
        grid=(B,),
        in_specs=[
            pl.BlockSpec((None, S, D), lambda b: (b, 0, 0)),
            pl.BlockSpec((len(dils), D, W), lambda b: (0, 0, 0)),
        ],
        out_specs=[pl.BlockSpec((None, d, S // d, W), lambda b: (b, 0, 0, 0)) for d in dils],
        out_shape=[jax.ShapeDtypeStruct((B, d, S // d, W), BF16) for d in dils],
        scratch_shapes=[pltpu.VMEM((W // LANES, S, LANES), F32)],
        compiler_params=_cparams("arbitrary"),
        name="inproj_dil",
    )(u, w_groups)


def _window_start(i, L, W):
    return min(max(QBLK * i - HALF_WINDOW, 0), L - W)


def _attn_bias_tables(S):
    h = np.arange(1, N_ATTN_HEADS + 1, dtype=np.float32)
    slopes = (2.0 ** (-8.0 * h / N_ATTN_HEADS)).astype(np.float32)
    tables = []
    for g, (_, d) in enumerate(ATTN_GROUPS):
        L = S // d
        W = min(2 * QBLK, L)
        nb = L // QBLK
        variants = sorted({_window_start(i, L, W) - QBLK * i for i in range(nb)}, reverse=True)
        r = np.arange(QBLK)[:, None]
        c = np.arange(W)[None, :]
        tab = np.empty((len(variants), HEADS_PER_GROUP * QBLK, W), np.float32)
        for vi, delta in enumerate(variants):
            off = np.abs(c - r + delta)
            for hh in range(HEADS_PER_GROUP):
                slope = np.float64(slopes[g * HEADS_PER_GROUP + hh])
                bias = np.where(off <= HALF_WINDOW, -slope * (d * off), NEG_BIG)
                tab[vi, hh * QBLK:(hh + 1) * QBLK] = bias.astype(np.float32)
        tables.append(tab)
    return tables


def _block_attention(q, kwin, vwin, bias, head_of_lane):
    zero = jnp.zeros_like(q)
    q4 = jnp.concatenate([jnp.where(head_of_lane == h, q, zero) for h in range(HEADS_PER_GROUP)], axis=0)
    s = lax.dot_general(q4, kwin, (((1,), (1,)), ((), ())), preferred_element_type=F32)
    s = s + bias
    m = jnp.max(s, axis=-1, keepdims=True)
    p = jnp.exp(s - m)
    l = jnp.sum(p, axis=-1, keepdims=True)
    o4 = _dot(p.astype(BF16), vwin) * (1.0 / l)
    lse4 = m + jnp.log(l)
    out = jnp.zeros((QBLK, GROUP_WIDTH), F32)
    lse = jnp.zeros((QBLK, GROUP_WIDTH), F32)
    for h in range(HEADS_PER_GROUP):
        rows = slice(h * QBLK, (h + 1) * QBLK)
        sel = head_of_lane == h
        out = jnp.where(sel, o4[rows], out)
        lse = jnp.where(sel, lse4[rows], lse)
    return out, lse


def _attn_kernel(q0_ref, q1_ref, q2_ref, b0_ref, b1_ref, b2_ref, o_ref, acc_ref, lse_ref):
    S = o_ref.shape[0]
    GW = GROUP_WIDTH
    head_of_lane = lax.broadcasted_iota(jnp.int32, (QBLK, GW), 1) // HEAD_DIM
    n_planes = GW // LANES

    def put(g, rows, out, lse):
        for c in range(n_planes):
            acc_ref[g, c, rows, :] = out[:, c * LANES:(c + 1) * LANES]
            lse_ref[g, c, rows, :] = lse[:, c * LANES:(c + 1) * LANES]

    L0 = S
    nb0 = L0 // QBLK

    def g0_body(i, carry):
        ws = pl.multiple_of(jnp.clip(QBLK * i - HALF_WINDOW, 0, L0 - 2 * QBLK), HALF_WINDOW)
        var = jnp.where(i == 0, 0, jnp.where(i == nb0 - 1, 2, 1))
        rows = pl.ds(pl.multiple_of(i * QBLK, QBLK), QBLK)
        out, lse = _block_attention(
            q0_ref[0, rows, 0:GW],
            q0_ref[0, pl.ds(ws, 2 * QBLK), GW:2 * GW],
            q0_ref[0, pl.ds(ws, 2 * QBLK), 2 * GW:3 * GW],
            b0_ref[var], head_of_lane)
        put(0, rows, out, lse)
        return carry

    lax.fori_loop(0, nb0, g0_body, 0, unroll=ATTN_UNROLL)

    for g, (q_ref, b_ref) in ((1, (q1_ref, b1_ref)), (2, (q2_ref, b2_ref))):
        d = ATTN_GROUPS[g][1]
        L = S // d
        W = min(2 * QBLK, L)
        nb = L // QBLK
        deltas = sorted({_window_start(i, L, W) - QBLK * i for i in range(nb)}, reverse=True)

        def gd_body(rho, carry, q_ref=q_ref, b_ref=b_ref, d=d, L=L, W=W, nb=nb, deltas=deltas, g=g):
            for i in range(nb):
                ws = _window_start(i, L, W)
                var = deltas.index(ws - QBLK * i)
                out, lse = _block_attention(
                    q_ref[rho, i * QBLK:(i + 1) * QBLK, 0:GW],
                    q_ref[rho, ws:ws + W, GW:2 * GW],
                    q_ref[rho, ws:ws + W, 2 * GW:3 * GW],
                    b_ref[var], head_of_lane)
                dst = pl.ds(rho + i * QBLK * d, QBLK, stride=d)
                put(g, dst, out, lse)
            return carry

        lax.fori_loop(0, d, gd_body, 0, unroll=max(1, ATTN_UNROLL // nb))

    def merge_body(i, carry):
        rows = pl.ds(pl.multiple_of(i * ROW_CHUNK, ROW_CHUNK), ROW_CHUNK)
        for c in range(n_planes):
            l0, l1, l2 = lse_ref[0, c, rows, :], lse_ref[1, c, rows, :], lse_ref[2, c, rows, :]
            mx = jnp.maximum(jnp.maximum(l0, l1), l2)
            w0, w1, w2 = jnp.exp(l0 - mx), jnp.exp(l1 - mx), jnp.exp(l2 - mx)
            num = w0 * acc_ref[0, c, rows, :] + w1 * acc_ref[1, c, rows, :] + w2 * acc_ref[2, c, rows, :]
            o_ref[rows, c * LANES:(c + 1) * LANES] = (num * (1.0 / (w0 + w1 + w2))).astype(BF16)
        return carry

    lax.fori_loop(0, S // ROW_CHUNK, merge_body, 0)


def _attention(qkv, bias_tables):
    q0, q1, q2 = qkv
    B, _, S, W3 = q0.shape
    GW = GROUP_WIDTH
    in_specs = [pl.BlockSpec((None,) + q.shape[1:], lambda b: (b, 0, 0, 0)) for q in qkv]
    in_specs += [pl.BlockSpec(t.shape, lambda b: (0, 0, 0)) for t in bias_tables]
    return pl.pallas_call(
        _attn_kernel,
        grid=(B,),
        in_specs=in_specs,
        out_specs=pl.BlockSpec((None, S, GW), lambda b: (b, 0, 0)),
        out_shape=jax.ShapeDtypeStruct((B, S, GW), BF16),
        scratch_shapes=[pltpu.VMEM((3, GW // LANES, S, LANES), F32)] * 2,
        compiler_params=_cparams("arbitrary"),
        name="attention",
    )(q0, q1, q2, *bias_tables)


def _pool_kernel(y_ref, w_ref, sc_ref, o_ref, pad_ref):
    S = y_ref.shape[0]
    C = y_ref.shape[1]
    P = PAD_ROWS
    pad_ref[0:P, :] = jnp.zeros((P, C), F32)
    pad_ref[S + P:S + 2 * P, :] = jnp.zeros((P, C), F32)
    pad_ref[P:S + P, :] = y_ref[...].astype(F32)
    for c in range(S // ROW_CHUNK):
        r0 = c * ROW_CHUNK
        t = r0 + lax.broadcasted_iota(jnp.int32, (ROW_CHUNK, 1), 0)
        for g, w in enumerate(POOL_WINDOWS):
            cols = slice(g * POOL_GROUP, (g + 1) * POOL_GROUP)
            acc = pad_ref[P + r0 - w // 2:P + r0 - w // 2 + ROW_CHUNK, cols]
            for j in range(-w // 2 + 1, w // 2):
                acc = acc + pad_ref[P + r0 + j:P + r0 + j + ROW_CHUNK, cols]
            cnt = (jnp.minimum(t + w // 2, S) - jnp.maximum(t - w // 2, 0)).astype(F32)
            pooled = acc / cnt - pad_ref[P + r0:P + r0 + ROW_CHUNK, cols]
            mixed = _dot(pooled.astype(BF16), w_ref[g].astype(BF16)) * sc_ref[:, cols]
            o_ref[r0:r0 + ROW_CHUNK, cols] = mixed.astype(BF16)


def _pool(proj, col_block, w_pool, pool_scale, layer):
    B, S, _ = proj.shape
    C = POOL_GROUP * len(POOL_WINDOWS)
    return pl.pallas_call(
        _pool_kernel,
        grid=(B,),
        in_specs=[
            pl.BlockSpec((None, S, C), lambda b: (b, 0, col_block)),
            pl.BlockSpec((None,) + w_pool.shape[1:], lambda b: (layer, 0, 0, 0)),
            pl.BlockSpec((None, 1, C), lambda b: (layer, 0, 0)),
        ],
        out_specs=pl.BlockSpec((None, S, C), lambda b: (b, 0, 0)),
        out_shape=jax.ShapeDtypeStruct((B, S, C), BF16),
        scratch_shapes=[pltpu.VMEM((S + 2 * PAD_ROWS, C), F32)],
        compiler_params=_cparams("arbitrary"),
        name="pool",
    )(proj, w_pool, pool_scale.reshape(pool_scale.shape[0], 1, C))


def _filter_kernel(z_ref, w1_ref, b1_ref, wi_ref, bi_ref, wo_ref, fr_ref, dec_ref, o_ref):
    S = z_ref.shape[0]
    C = HYENA_WIDTH
    freq = fr_ref[...]
    h = jnp.sin(freq * (_dot3(z_ref[...], w1_ref[...]) + b1_ref[...]))
    for i in range(wi_ref.shape[0]):
        h = jnp.sin(freq * (_dot3(h, wi_ref[i]) + bi_ref[i]))
    not_first = lax.broadcasted_iota(jnp.int32, (S, 1), 0) > 0
    for blk in range(wo_ref.shape[1] // C):
        cols = slice(blk * C, (blk + 1) * C)
        f = _dot3(h, wo_ref[:, cols]) * dec_ref[...]
        if blk % 2 == 1:
            f = jnp.where(not_first, f, 0.0)
        o_ref[:, cols] = f.astype(BF16)


def _hyena_filters(zfeat, decay, f_w1, f_b1, f_w_inner, f_b_inner, f_w_out, f_freq, layer):
    S, E = zfeat.shape
    H = FILTER_HIDDEN
    n_out = f_w_out.shape[-1]
    n_inner = f_w_inner.shape[1]
    w1 = jnp.zeros((f_w1.shape[0], E, H), F32).at[:, :f_w1.shape[1]].set(f_w1)
    full = lambda shape: pl.BlockSpec(shape, lambda i: (0,) * len(shape))
    lay = lambda shape: pl.BlockSpec((None,) + shape, lambda i: (layer,) + (0,) * len(shape))
    return pl.pallas_call(
        _filter_kernel,
        grid=(1,),
        in_specs=[
            full((S, E)), lay((E, H)), lay((1, H)), lay((n_inner, H, H)), lay((n_inner, 1, H)),
            lay((H, n_out)), lay((1, H)), full((S, HYENA_WIDTH)),
        ],
        out_specs=full((S, n_out)),
        out_shape=jax.ShapeDtypeStruct((S, n_out), BF16),
        compiler_params=_cparams("arbitrary"),
        name="hyena_filter",
    )(zfeat, w1, f_b1.reshape(-1, 1, H), f_w_inner, f_b_inner.reshape(-1, n_inner, 1, H), f_w_out,
      f_freq.reshape(-1, 1, H), decay)


def _spec_kernel(fa_ref, fb_ref, h_ref, pre_ref, pim_ref, q1re_ref, q1im_ref, q2re_ref, q2im_ref, sp0_ref):
    TK = fa_ref.shape[0] // 2
    C = HYENA_WIDTH
    n_fft = 2 * h_ref.shape[0]
    xa = _dot(fa_ref[...], h_ref[...])
    xb = _dot(fb_ref[...], h_ref[...])
    row = pl.program_id(0) * TK + lax.broadcasted_iota(jnp.int32, (TK, 1), 0)
    theta = row.astype(F32) * (2.0 * math.pi / n_fft)
    cj, sj = jnp.cos(theta), jnp.sin(theta)
    sc = 2.0 / n_fft
    for o in range(pre_ref.shape[0]):
        hf = slice(2 * o * C, (2 * o + 1) * C)
        hb = slice((2 * o + 1) * C, (2 * o + 2) * C)
        ha_re = xa[:TK, hf] + xa[:TK, hb]
        ha_im = xa[TK:, hf] - xa[TK:, hb]
        hm_re = xb[:TK, hf] + xb[:TK, hb]
        hm_im = xb[TK:, hf] - xb[TK:, hb]
        p_re, p_im = (ha_re + hm_re) * sc, (ha_im - hm_im) * sc
        q_re, q_im = (ha_re - hm_re) * sc, (ha_im + hm_im) * sc
        pre_ref[o] = p_re
        pim_ref[o] = p_im
        q1re_ref[o] = q_re * cj + q_im * sj
        q1im_ref[o] = q_im * cj - q_re * sj
        q2re_ref[o] = q_re * cj - q_im * sj
        q2im_ref[o] = q_im * cj + q_re * sj

        @pl.when(pl.program_id(0) == 0)
        def _():
            h0, hs = ha_re[0:1], hm_re[0:1]
            hh_re = xa[TK:TK + 1, hf] + xa[TK:TK + 1, hb]
            hh_im = xb[TK:TK + 1, hf] - xb[TK:TK + 1, hb]
            sp0_ref[o] = jnp.concatenate(
                [(h0 + hs) * (0.5 * sc), (h0 - hs) * (0.5 * sc), hh_re * sc, hh_im * sc,
                 jnp.zeros((4, C), F32)], axis=0)


def _filter_spectra(fa, fb, filt):
    nkt, tk2, S = fa.shape
    TK = tk2 // 2
    C = HYENA_WIDTH
    n_ord = filt.shape[1] // (2 * C)
    out = jax.ShapeDtypeStruct((n_ord, nkt * TK, C), F32)
    ospec = pl.BlockSpec((n_ord, TK, C), lambda k: (0, k, 0))
    fspec = pl.BlockSpec((None, tk2, S), lambda k: (k, 0, 0))
    return pl.pallas_call(
        _spec_kernel,
        grid=(nkt,),
        in_specs=[fspec, fspec, pl.BlockSpec(filt.shape, lambda k: (0, 0))],
        out_specs=[ospec] * 6 + [pl.BlockSpec((n_ord, 8, C), lambda k: (0, 0, 0))],
        out_shape=[out] * 6 + [jax.ShapeDtypeStruct((n_ord, 8, C), F32)],
        compiler_params=_cparams("arbitrary"),
        name="hyena_spectra",
    )(fa, fb, filt)


def _hyena_kernel(v_ref, x1_ref, x2_ref, f_ref, g_ref, pre_ref, pim_ref, q1re_ref, q1im_ref, q2re_ref, q2im_ref,
                  sp0_ref, wsc_ref, bsc_ref, hb_ref, o_ref, pad_ref, z_ref, zb_ref, gate_ref, zcat_ref, il_ref):
    H, C2 = z_ref.shape
    C = C2 // 2
    n_sub, tk2, _ = f_ref.shape
    TK = tk2 // 2
    TT = g_ref.shape[0]
    nf = H // (n_sub * TK)
    R = ROW_CHUNK
    P = PAD_ROWS
    o = pl.program_id(1)
    s = pl.program_id(2)
    even, odd = slice(0, C), slice(C, C2)

    def fill_pad(in_ref):
        for par in range(2):
            pad_ref[par, 0:P, :] = jnp.zeros((P, C), F32)
            pad_ref[par, H + P:H + 2 * P, :] = jnp.zeros((P, C), F32)
            pad_ref[par, P:H + P, :] = in_ref[par].astype(F32)

    def conv_chunk(r0, w, bias):
        ev = pad_ref[0, P + r0:P + r0 + R, :]
        od = pad_ref[1, P + r0:P + r0 + R, :]
        od_prev = pad_ref[1, P + r0 - 1:P + r0 - 1 + R, :]
        ev_next = pad_ref[0, P + r0 + 1:P + r0 + 1 + R, :]
        w0, w1, w2 = w[0:1], w[1:2], w[2:3]
        return w0 * od_prev + w1 * ev + w2 * od + bias, w0 * ev + w1 * od + w2 * ev_next + bias

    @pl.when((o == 0) & (s == 0))
    def _():
        fill_pad(v_ref)
        for c in range(H // R):
            rows = slice(c * R, (c + 1) * R)
            conv_e, conv_o = conv_chunk(c * R, wsc_ref[:, 0:C], bsc_ref[:, 0:C])
            z_ref[rows, even] = conv_e
            z_ref[rows, odd] = conv_o
            zb_ref[rows, even] = conv_e.astype(BF16)
            zb_ref[rows, odd] = conv_o.astype(BF16)

    @pl.when((s == 0) & (o == 0))
    def _():
        fill_pad(x1_ref)

    @pl.when((s == 0) & (o == 1))
    def _():
        fill_pad(x2_ref)

    @pl.when(s < nf)
    def _():
        a, b, c, d = sp0_ref[0:1, :], sp0_ref[1:2, :], sp0_ref[2:3, :], sp0_ref[3:4, :]
        for sub in range(n_sub):
            tile = s * n_sub + sub
            cs = slice(sub * TK, (sub + 1) * TK)
            x = _dot(f_ref[sub], zb_ref[...])
            e_re, o_re, e_im, o_im = x[:TK, even], x[:TK, odd], x[TK:, even], x[TK:, odd]
            p_re, p_im = pre_ref[cs, :], pim_ref[cs, :]
            q1_re, q1_im, q2_re, q2_im = q1re_ref[cs, :], q1im_ref[cs, :], q2re_ref[cs, :], q2im_ref[cs, :]
            ze_re = e_re * p_re - e_im * p_im + o_re * q1_re - o_im * q1_im
            ze_im = e_re * p_im + e_im * p_re + o_re * q1_im + o_im * q1_re
            zo_re = e_re * q2_re - e_im * q2_im + o_re * p_re - o_im * p_im
            zo_im = e_re * q2_im + e_im * q2_re + o_re * p_im + o_im * p_re
            first = (tile * TK + lax.broadcasted_iota(jnp.int32, (TK, 1), 0)) == 0
            ze_re = jnp.where(first, e_re[0:1] * a + o_re[0:1] * b, ze_re)
            zo_re = jnp.where(first, e_re[0:1] * b + o_re[0:1] * a, zo_re)
            ze_im = jnp.where(first, e_im[0:1] * c + o_im[0:1] * d, ze_im)
            zo_im = jnp.where(first, o_im[0:1] * c - e_im[0:1] * d, zo_im)
            base = pl.multiple_of(tile * 2 * TK, 2 * TK)
            zcat_ref[pl.ds(base, TK), even] = ze_re.astype(BF16)
            zcat_ref[pl.ds(base, TK), odd] = zo_re.astype(BF16)
            zcat_ref[pl.ds(base + TK, TK), even] = ze_im.astype(BF16)
            zcat_ref[pl.ds(base + TK, TK), odd] = zo_im.astype(BF16)

    @pl.when(s == nf)
    def _():
        w = jnp.where(o == 0, wsc_ref[:, C:2 * C], wsc_ref[:, 2 * C:3 * C])
        bias = jnp.where(o == 0, bsc_ref[:, C:2 * C], bsc_ref[:, 2 * C:3 * C])
        for c in range(H // R):
            conv_e, conv_o = conv_chunk(c * R, w, bias)
            gate_ref[c * R:(c + 1) * R, even] = conv_e
            gate_ref[c * R:(c + 1) * R, odd] = conv_o

    @pl.when(s >= nf)
    def _():
        tt = s - nf
        rows = pl.ds(pl.multiple_of(tt * TT, TT), TT)
        y = _dot(g_ref[...], zcat_ref[...])
        hrow = jnp.where(o == 0, hb_ref[0:1, :], hb_ref[1:2, :])
        hbias = jnp.concatenate([hrow, hrow], axis=1)
        znew = gate_ref[rows, :] * (y + z_ref[rows, :] * hbias)
        z_ref[rows, :] = znew
        zb_ref[rows, :] = znew.astype(BF16)

        @pl.when(o == pl.num_programs(1) - 1)
        def _():
            for p in range(C // LANES):
                il_ref[p, pl.ds(0, TT, stride=2), :] = znew[:, p * LANES:(p + 1) * LANES]
                il_ref[p, pl.ds(1, TT, stride=2), :] = znew[:, C + p * LANES:C + (p + 1) * LANES]
            out_rows = pl.ds(pl.multiple_of(tt * 2 * TT, 2 * TT), 2 * TT)
            for p in range(C // LANES):
                o_ref[out_rows, p * LANES:(p + 1) * LANES] = il_ref[p].astype(BF16)


def _hyena(hv, hx1, hx2, fhalf, ghalf, coeffs, w_sconv, b_sconv, hy_bias, layer):
    B, _, H, C = hv.shape
    S = 2 * H
    n_tiles, tk2, _ = fhalf.shape
    TK = tk2 // 2
    n_sub = HYENA_FREQ_TILES_PER_STEP
    nf = n_tiles // n_sub
    TT = HYENA_TIME_TILE
    ni = H // TT
    sp0 = coeffs[-1]
    n_ord = sp0.shape[0]
    fwd = lambda s: jnp.minimum(s, nf - 1)
    inv = lambda s: jnp.clip(s - nf, 0, ni - 1)
    inp = pl.BlockSpec((None, 2, H, C), lambda bb, o, s: (bb, 0, 0, 0), pipeline_mode=pl.Buffered(1))
    coef = pl.BlockSpec((None, n_sub * TK, C), lambda bb, o, s: (o, fwd(s), 0))
    return pl.pallas_call(
        _hyena_kernel,
        grid=(B, n_ord, nf + ni),
        in_specs=[
            inp, inp, inp,
            pl.BlockSpec((n_sub, tk2, H), lambda bb, o, s: (fwd(s), 0, 0)),
            pl.BlockSpec((TT, 2 * H), lambda bb, o, s: (inv(s), 0)),
            coef, coef, coef, coef, coef, coef,
            pl.BlockSpec((None, 8, C), lambda bb, o, s: (o, 0, 0)),
            pl.BlockSpec((None, 3, 3 * C), lambda bb, o, s: (layer, 0, 0)),
            pl.BlockSpec((None, 1, 3 * C), lambda bb, o, s: (layer, 0, 0)),
            pl.BlockSpec((None, n_ord, C), lambda bb, o, s: (layer, 0, 0)),
        ],
        out_specs=pl.BlockSpec((None, S, C), lambda bb, o, s: (bb, 0, 0)),
        out_shape=jax.ShapeDtypeStruct((B, S, C), BF16),
        scratch_shapes=[
            pltpu.VMEM((2, H + 2 * PAD_ROWS, C), F32),
            pltpu.VMEM((H, 2 * C), F32),
            pltpu.VMEM((H, 2 * C), BF16),
            pltpu.VMEM((H, 2 * C), F32),
            pltpu.VMEM((2 * H, 2 * C), BF16),
            pltpu.VMEM((C // LANES, 2 * TT, LANES), F32),
        ],
        compiler_params=_cparams("arbitrary", "arbitrary", "arbitrary"),
        name="hyena_conv",
    )(hv, hx1, hx2, fhalf, ghalf, *coeffs, w_sconv, b_sconv.reshape(b_sconv.shape[0], 1, 3 * C), hy_bias)


def _packed_dft(bins, n_samples, n_fft, tile, sin_row0):
    theta = ((bins[:, None] * np.arange(n_samples)[None, :]) % n_fft) * (2.0 * math.pi / n_fft)
    fc = np.cos(theta)
    fs = -np.sin(theta)
    fs[0] = sin_row0
    nt = bins.shape[0] // tile
    packed = np.concatenate([fc.reshape(nt, tile, n_samples), fs.reshape(nt, tile, n_samples)], axis=1)
    return packed.astype(np.float32).astype(BF16)


@functools.lru_cache(maxsize=None)
def _dft_matrices(S):
    H, N = S // 2, 2 * S
    j = np.arange(H)
    m = np.arange(H)
    n = np.arange(S)
    quarter = np.asarray([1.0, 0.0, -1.0, 0.0])
    fhalf = _packed_dft(j, H, S, HYENA_FREQ_TILE, 1.0 - 2.0 * (m & 1))
    ghalf = np.ascontiguousarray(fhalf.reshape(2 * H, H).T)
    fa = _packed_dft(j, S, N, SPECTRA_TILE, quarter[n & 3])
    fb = _packed_dft(S - j, S, N, SPECTRA_TILE, -quarter[(n + 3) & 3])
    return fhalf, ghalf, fa, fb


def _filter_features(S):
    t = jnp.linspace(0.0, 1.0, S, dtype=F32)[:, None]
    bands = (FILTER_EMB - 1) // 2
    w = 2.0 * math.pi * jnp.arange(S, dtype=F32) / S
    f = jnp.linspace(1e-4, bands - 1, bands, dtype=F32)
    ang = w[:, None] * f[None, :]
    z = jnp.concatenate([t, jnp.cos(ang), -jnp.sin(ang)], axis=-1)
    z = jnp.pad(z, ((0, 0), (0, 128 - FILTER_EMB)))
    min_decay = math.log(DECAY_TARGET) / SLOW_DECAY_PCT
    max_decay = math.log(DECAY_TARGET) / FAST_DECAY_PCT
    deltas = jnp.linspace(min_decay, max_decay, HYENA_WIDTH, dtype=F32)
    decay = jnp.exp(-t * jnp.abs(deltas)[None, :])
    return z, decay


def _merge_kernel(ya_ref, yb_ref, yc_ref, gl_ref, x_ref, mod_ref, pa_ref, pp_ref, ph_ref, wo_ref,
                  g_ref, b_ref, wr_ref, br_ref, x1_ref, u2_ref, lt_ref):
    D = x_ref.shape[1]
    sig = lambda v: 1.0 / (1.0 + jnp.exp(-v))
    merged = sig(gl_ref[:, 0:D].astype(F32)) * _dot(ya_ref[...], pa_ref[...])
    merged += sig(gl_ref[:, D:2 * D].astype(F32)) * _dot(yb_ref[...], pp_ref[...])
    merged += sig(gl_ref[:, 2 * D:3 * D].astype(F32)) * _dot(yc_ref[...], ph_ref[...])
    h = _dot(merged.astype(BF16), wo_ref[...])
    gate1, shift2, scale2 = mod_ref[2:3, :], mod_ref[3:4, :], mod_ref[4:5, :]
    x1 = _layer_norm(ALPHA_RES * x_ref[...] + (1.0 + gate1) * h, g_ref[...], b_ref[...])
    x1_ref[...] = x1
    u2 = x1 * (1.0 + scale2) + shift2
    _store_token_tiles(u2_ref, u2)
    TM = u2.shape[0]
    u_hi = u2.astype(BF16)
    u_lo = (u2 - u_hi.astype(F32)).astype(BF16)
    r = _dot(jnp.concatenate([u_hi, u_lo], axis=0), wr_ref[...])
    rt = (r[:TM] + r[TM:] + br_ref[...]).T
    lt_ref[...] = rt[0:N_EXPERTS, :] + rt[N_EXPERTS:2 * N_EXPERTS, :]


def _merge(ya, yb, yc, proj, x, mod, p_attn, p_pool, p_hyena, w_out, ln_g, ln_b, w_router, b_router, layer):
    B, S, D = x.shape
    N = B * S
    TM = MERGE_ROWS
    tiles_per_seq = S // TM
    EP = 128
    flat = lambda t: t.reshape(N, t.shape[-1])
    w_hi = w_router[layer].astype(BF16)
    w_lo = (w_router[layer] - w_hi.astype(F32)).astype(BF16)
    wr = jnp.zeros((D, EP), BF16).at[:, :N_EXPERTS].set(w_hi).at[:, N_EXPERTS:2 * N_EXPERTS].set(w_lo)
    br = jnp.zeros((1, EP), F32).at[:, :N_EXPERTS].set(b_router[layer][None])
    tok = lambda w: pl.BlockSpec((TM, w), lambda i: (i, 0))
    lay = lambda t: pl.BlockSpec((None,) + t.shape[1:], lambda i: (layer,) + (0,) * (t.ndim - 1))
    ln_g3, ln_b3 = ln_g.reshape(-1, 1, D), ln_b.reshape(-1, 1, D)
    return pl.pallas_call(
        _merge_kernel,
        grid=(N // TM,),
        in_specs=[
            tok(ya.shape[-1]), tok(yb.shape[-1]), tok(yc.shape[-1]), tok(3 * D), tok(D),
            pl.BlockSpec((None, 6, D), lambda i: (i // tiles_per_seq, 0, 0)),
            lay(p_attn), lay(p_pool), lay(p_hyena), lay(w_out), lay(ln_g3), lay(ln_b3),
            pl.BlockSpec((D, EP), lambda i: (0, 0)),
            pl.BlockSpec((1, EP), lambda i: (0, 0)),
        ],
        out_specs=[tok(D), pl.BlockSpec((TM * D // LANES, LANES), lambda i: (i, 0)),
                   pl.BlockSpec((N_EXPERTS, TM), lambda i: (0, i))],
        out_shape=[
            jax.ShapeDtypeStruct((N, D), F32),
            jax.ShapeDtypeStruct((N * D // LANES, LANES), F32),
            jax.ShapeDtypeStruct((N_EXPERTS, N), F32),
        ],
        compiler_params=_cparams("arbitrary"),
        name="merge_ln1_router",
    )(flat(ya), flat(yb), flat(yc), flat(proj), flat(x), mod, p_attn, p_pool, p_hyena, w_out,
      ln_g3, ln_b3, wr, br)


def _topk_kernel(l_ref, e_ref, g_ref, r_ref, cnt_ref, carry_ref):
    E, TL = l_ref.shape

    @pl.when(pl.program_id(0) == 0)
    def _():
        carry_ref[...] = jnp.zeros_like(carry_ref)

    l = l_ref[...]
    eio = lax.broadcasted_iota(jnp.int32, (E, TL), 0).astype(F32)
    vals, hots = [], []
    for k in range(TOP_K):
        m = jnp.max(l, axis=0, keepdims=True)
        idx = jnp.min(jnp.where(l == m, eio, float(E)), axis=0, keepdims=True)
        hot = eio == idx
        l = jnp.where(hot, -jnp.inf, l)
        vals.append(m)
        hots.append(hot)
        e_ref[k:k + 1, :] = idx.astype(jnp.int32)
    exps = [jnp.exp(v - vals[0]) for v in vals]
    inv = 1.0 / (exps[0] + exps[1] + exps[2] + exps[3])
    for k in range(TOP_K):
        g_ref[k:k + 1, :] = exps[k] * inv

    memb = jnp.zeros((E, TL), F32)
    for hot in hots:
        memb = memb + hot.astype(F32)
    upper = (lax.broadcasted_iota(jnp.int32, (TL, TL), 0)
             <= lax.broadcasted_iota(jnp.int32, (TL, TL), 1)).astype(BF16)
    incl = _dot(memb.astype(BF16), upper)
    excl = incl - memb + carry_ref[:, 0:1]
    for k in range(TOP_K):
        rank = jnp.sum(jnp.where(hots[k], excl, 0.0), axis=0, keepdims=True)
        r_ref[k:k + 1, :] = rank.astype(jnp.int32)
    carry_ref[...] = carry_ref[...] + jnp.sum(memb, axis=1, keepdims=True)
    cnt_ref[...] = carry_ref[...]


def _topk(logits_t):
    E, N = logits_t.shape
    TL = TOPK_LANES
    tok = pl.BlockSpec((TOP_K, TL), lambda i: (0, i))
    return pl.pallas_call(
        _topk_kernel,
        grid=(N // TL,),
        in_specs=[pl.BlockSpec((E, TL), lambda i: (0, i))],
        out_specs=[tok, tok, tok, pl.BlockSpec((E, 128), lambda i: (0, 0))],
        out_shape=[
            jax.ShapeDtypeStruct((TOP_K, N), jnp.int32),
            jax.ShapeDtypeStruct((TOP_K, N), F32),
            jax.ShapeDtypeStruct((TOP_K, N), jnp.int32),
            jax.ShapeDtypeStruct((E, 128), F32),
        ],
        scratch_shapes=[pltpu.VMEM((E, 128), F32)],
        compiler_params=_cparams("arbitrary"),
        name="topk_rank",
    )(logits_t)


def _dest_kernel(offs_ref, e_ref, r_ref, d_ref):
    e = e_ref[...]
    acc = r_ref[...]
    for ex in range(N_EXPERTS):
        acc = acc + jnp.where(e == ex, offs_ref[ex], 0)
    d_ref[...] = acc


def _dest_rows(offs, top_e, rank):
    K, N = top_e.shape
    TL = TOPK_LANES
    tok = pl.BlockSpec((K, TL), lambda i, offs: (0, i))
    return pl.pallas_call(
        _dest_kernel,
        grid_spec=pltpu.PrefetchScalarGridSpec(
            num_scalar_prefetch=1, grid=(N // TL,), in_specs=[tok, tok], out_specs=tok),
        out_shape=jax.ShapeDtypeStruct((K, N), jnp.int32),
        compiler_params=_cparams("arbitrary"),
        name="dest_rows",
    )(offs, top_e, rank)


def _dispatch_kernel(tail_ref, n_used_ref, dest_ref, u_ref, xs_ref, zero_ref, sem):
    TT = dest_ref.shape[0] // TOP_K
    RT = u_ref.shape[0] // TT
    blk_rows = zero_ref.shape[0]
    n_blocks = xs_ref.shape[0] // blk_rows

    def zero_copy(first_row):
        start = pl.multiple_of(first_row * RT, RT)
        return pltpu.make_async_copy(zero_ref, xs_ref.at[pl.ds(start, blk_rows), :], sem)

    @pl.when(pl.program_id(0) == 0)
    def _():
        zero_ref[...] = jnp.zeros_like(zero_ref)

        def tails(fn):
            def body(e, carry):
                @pl.when(tail_ref[e] >= 0)
                def _():
                    fn(zero_copy(jnp.maximum(tail_ref[e], 0)))
                return carry
            lax.fori_loop(0, N_EXPERTS, body, 0)

        def unused(fn):
            def body(i, carry):
                fn(zero_copy(i * (blk_rows // RT)))
                return carry
            lax.fori_loop(n_used_ref[0], n_blocks, body, 0)

        tails(lambda cp: cp.start())
        unused(lambda cp: cp.start())
        tails(lambda cp: cp.wait())
        unused(lambda cp: cp.wait())

    def row_copy(j, k):
        src = pl.multiple_of(j * RT, RT)
        dst = pl.multiple_of(dest_ref[j * TOP_K + k] * RT, RT)
        return pltpu.make_async_copy(u_ref.at[pl.ds(src, RT), :], xs_ref.at[pl.ds(dst, RT), :], sem)

    def start(j, carry):
        for k in range(TOP_K):
            row_copy(j, k).start(priority=k % DMA_PRIORITIES)
        return carry

    lax.fori_loop(0, TT, start, 0, unroll=DMA_ISSUE_UNROLL)
    for k in range(TOP_K):
        pltpu.make_async_copy(u_ref, xs_ref.at[pl.ds(0, TT * RT), :], sem).wait()


def _dispatch(tail_start, n_used, dest, u2_tiles, n_rows, D):
    RT = D // LANES
    N = u2_tiles.shape[0] // RT
    TT = MOE_TOK
    return pl.pallas_call(
        _dispatch_kernel,
        grid_spec=pltpu.PrefetchScalarGridSpec(
            num_scalar_prefetch=2,
            grid=(N // TT,),
            in_specs=[
                pl.BlockSpec((TOP_K * TT,), lambda i, tail, nu: (i,), memory_space=pltpu.SMEM),
                pl.BlockSpec((TT * RT, LANES), lambda i, tail, nu: (i, 0)),
            ],
            out_specs=pl.BlockSpec(memory_space=pl.ANY),
            scratch_shapes=[pltpu.VMEM((MOE_BLK * RT, LANES), F32), pltpu.SemaphoreType.DMA(())],
        ),
        out_shape=jax.ShapeDtypeStruct((n_rows * RT, LANES), F32),
        compiler_params=_cparams("arbitrary"),
        name="moe_dispatch",
    )(tail_start, n_used, dest, u2_tiles)


def _ffn_kernel(blk_e_ref, n_used_ref, xs_ref, w1_ref, b1_ref, w2_ref, b2_ref, ys_ref,
                w1b_ref, w2b_ref, act_ref):
    F, D = w2_ref.shape
    BLK = xs_ref.shape[0] * LANES // D
    i = pl.program_id(0)
    used = i < n_used_ref[0]
    new_expert = (i == 0) | (blk_e_ref[i] != blk_e_ref[jnp.maximum(i - 1, 0)])

    @pl.when(used & new_expert)
    def _():
        for r in range(0, D, WEIGHT_CAST_ROWS):
            w1b_ref[r:r + WEIGHT_CAST_ROWS, :] = w1_ref[r:r + WEIGHT_CAST_ROWS, :].astype(BF16)
        for r in range(0, F, WEIGHT_CAST_ROWS):
            w2b_ref[r:r + WEIGHT_CAST_ROWS, :] = w2_ref[r:r + WEIGHT_CAST_ROWS, :].astype(BF16)

    @pl.when(used)
    def _():
        x = jnp.concatenate(_load_token_tiles(xs_ref, BLK, D), axis=1).astype(BF16)
        for c in range(0, F, FFN_CHUNK):
            cg = slice(c, c + FFN_CHUNK)
            cl = slice(F + c, F + c + FFN_CHUNK)
            glu = jnp.minimum(_dot(x, w1b_ref[:, cg]) + b1_ref[:, cg], SWIGLU_LIMIT)
            lin = jnp.clip(_dot(x, w1b_ref[:, cl]) + b1_ref[:, cl], -SWIGLU_LIMIT, SWIGLU_LIMIT)
            act = glu * (1.0 / (1.0 + jnp.exp(-SWIGLU_ALPHA * glu))) * (lin + 1.0)
            act_ref[:, cg] = act.astype(BF16)
        _store_token_tiles(ys_ref, _dot(act_ref[...], w2b_ref[...]) + b2_ref[...])

    @pl.when(jnp.logical_not(used))
    def _():
        ys_ref[...] = jnp.zeros_like(ys_ref)


def _expert_ffn(blk_e, n_used, xs_tiles, w1, b1, w2, b2, layer):
    _, E, D, F2 = w1.shape
    F = F2 // 2
    RT = D // LANES
    BLK = MOE_BLK
    rows = pl.BlockSpec((BLK * RT, LANES), lambda i, be, nu: (i, 0))
    per_expert = lambda r, c: pl.BlockSpec((None, None, r, c), lambda i, be, nu: (layer, be[i], 0, 0))
    return pl.pallas_call(
        _ffn_kernel,
        grid_spec=pltpu.PrefetchScalarGridSpec(
            num_scalar_prefetch=2,
            grid=(xs_tiles.shape[0] // (BLK * RT),),
            in_specs=[rows, per_expert(D, F2), per_expert(1, F2), per_expert(F, D), per_expert(1, D)],
            out_specs=rows,
            scratch_shapes=[pltpu.VMEM((D, F2), BF16), pltpu.VMEM((F, D), BF16), pltpu.VMEM((BLK, F), BF16)],
        ),
        out_shape=jax.ShapeDtypeStruct(xs_tiles.shape, F32),
        compiler_params=_cparams("arbitrary"),
        name="moe_ffn",
    )(blk_e, n_used, xs_tiles, w1, b1.reshape(-1, E, 1, F2), w2, b2.reshape(-1, E, 1, D))


def _combine_kernel(dest_ref, dest_next_ref, gt_ref, x1_ref, mod_ref, g_ref, b_ref, ys_ref, o_ref,
                    buf_a, buf_b, sem):
    TT, D = x1_ref.shape
    RT = D // LANES
    RC = TT // COMBINE_CHUNKS
    i = pl.program_id(0)
    last = pl.num_programs(0) - 1

    def row_copy(d_ref, buf, s, j, k):
        src = pl.multiple_of(d_ref[j * TOP_K + k] * RT, RT)
        dst = pl.multiple_of(j * RT, RT)
        return pltpu.make_async_copy(ys_ref.at[pl.ds(src, RT), :], buf.at[k, pl.ds(dst, RT), :], sem.at[s])

    def wait_tile(buf, s):
        for k in range(TOP_K):
            pltpu.make_async_copy(ys_ref.at[pl.ds(0, TT * RT), :], buf.at[k], sem.at[s]).wait()

    def reduce_rows(buf, r0):
        rows = slice(r0, r0 + RC)
        chunks = None
        for k in range(TOP_K):
            gate = gt_ref[rows, k:k + 1]
            tiles = [gate * buf[k, pl.ds(r0 * RT + c, RC, stride=RT), :] for c in range(RT)]
            chunks = tiles if chunks is None else [a + b for a, b in zip(chunks, tiles)]
        h = jnp.concatenate(chunks, axis=1)
        gate2 = mod_ref[5:6, :]
        o_ref[rows, :] = _layer_norm(ALPHA_RES * x1_ref[rows, :] + (1.0 + gate2) * h, g_ref[...], b_ref[...])

    def run(cur, s_cur, nxt, s_nxt):
        wait_tile(cur, s_cur)
        for ch in range(COMBINE_CHUNKS):
            reduce_rows(cur, ch * RC)
            for j in range(ch * RC, (ch + 1) * RC):
                for k in range(TOP_K):
                    row_copy(dest_next_ref, nxt, s_nxt, j, k).start(priority=k % DMA_PRIORITIES)

        @pl.when(i == last)
        def _():
            wait_tile(nxt, s_nxt)

    @pl.when(i == 0)
    def _():
        def body(j, carry):
            for k in range(TOP_K):
                row_copy(dest_ref, buf_a, 0, j, k).start(priority=k % DMA_PRIORITIES)
            return carry
        lax.fori_loop(0, TT, body, 0, unroll=DMA_ISSUE_UNROLL)

    @pl.when(i % 2 == 0)
    def _():
        run(buf_a, 0, buf_b, 1)

    @pl.when(i % 2 == 1)
    def _():
        run(buf_b, 1, buf_a, 0)


def _combine(dest, gates_t, x1, mod, ln_g, ln_b, ys, seq_len, layer):
    N, D = x1.shape
    TT = MOE_TOK
    tiles_per_seq = seq_len // TT
    ln_g3, ln_b3 = ln_g.reshape(-1, 1, D), ln_b.reshape(-1, 1, D)
    lay = lambda t: pl.BlockSpec((None,) + t.shape[1:], lambda i: (layer,) + (0,) * (t.ndim - 1))
    return pl.pallas_call(
        _combine_kernel,
        grid=(N // TT,),
        in_specs=[
            pl.BlockSpec((TOP_K * TT,), lambda i: (i,), memory_space=pltpu.SMEM),
            pl.BlockSpec((TOP_K * TT,), lambda i: (jnp.minimum(i + 1, N // TT - 1),), memory_space=pltpu.SMEM),
            pl.BlockSpec((TT, TOP_K), lambda i: (i, 0)),
            pl.BlockSpec((TT, D), lambda i: (i, 0)),
            pl.BlockSpec((None, 6, D), lambda i: (i // tiles_per_seq, 0, 0)),
            lay(ln_g3), lay(ln_b3),
            pl.BlockSpec(memory_space=pl.ANY),
        ],
        out_specs=pl.BlockSpec((TT, D), lambda i: (i, 0)),
        out_shape=jax.ShapeDtypeStruct((N, D), F32),
        scratch_shapes=[pltpu.VMEM((TOP_K, TT * D // LANES, LANES), F32)] * 2 + [pltpu.SemaphoreType.DMA((2,))],
        compiler_params=_cparams("arbitrary"),
        name="moe_combine_ln2",
    )(dest, dest, gates_t, x1, mod, ln_g3, ln_b3, ys)


def _moe(u2_tiles, logits_t, x1, mod, w1, b1, w2, b2, ln_g, ln_b, seq_len, layer):
    N, D = x1.shape
    BLK = MOE_BLK
    top_e, gates, rank, counts = _topk(logits_t)
    counts = counts[:, 0].astype(jnp.int32)
    padded = (counts + BLK - 1) // BLK * BLK
    pend = jnp.cumsum(padded)
    offs = pend - padded
    n_blocks = -(-(N * TOP_K) // BLK) + N_EXPERTS
    blk_start = jnp.arange(n_blocks, dtype=jnp.int32) * BLK
    blk_e = jnp.sum((pend[None, :] <= blk_start[:, None]).astype(jnp.int32), axis=1)
    blk_e = jnp.minimum(blk_e, N_EXPERTS - 1).astype(jnp.int32)
    n_used = (pend[-1:] // BLK).astype(jnp.int32)
    tail_start = jnp.where(padded > 0, pend - BLK, -1).astype(jnp.int32)
    dest = _dest_rows(offs.astype(jnp.int32), top_e, rank).T.reshape(-1)
    xs = _dispatch(tail_start, n_used, dest, u2_tiles, n_blocks * BLK, D)
    ys = _expert_ffn(blk_e, n_used, xs, w1, b1, w2, b2, layer)
    return _combine(dest, gates.T, x1, mod, ln_g, ln_b, ys, seq_len, layer)


def _split_w_in(w_in):
    AW = N_ATTN_HEADS * HEAD_DIM
    GW = GROUP_WIDTH
    C = HYENA_WIDTH
    w = w_in.astype(BF16)
    w_nat = jnp.concatenate([w[:, 3 * AW + 4 * C:], w[:, 3 * AW:3 * AW + C]], axis=1)
    col_scale = (1.0 / math.sqrt(HEAD_DIM), 1.0, 1.0)
    w_qkv = jnp.stack([
        jnp.concatenate([w[:, t * AW + g * GW:t * AW + (g + 1) * GW] * col_scale[t] for t in range(3)], axis=1)
        for g in range(len(ATTN_GROUPS))]).astype(BF16)
    w_hy = jnp.stack([w[:, 3 * AW + (1 + t) * C:3 * AW + (2 + t) * C] for t in range(3)])
    return w_nat, w_qkv, w_hy


def kernel(x, c, w_ada, b_ada, w_in, w_sconv, b_sconv, w_pool, pool_scale, f_w1, f_b1, f_w_inner, f_b_inner, f_w_out, f_freq, hy_bias, p_attn, p_pool, p_hyena, w_out, ln1_g, ln1_b, w_router, b_router, w1, b1, w2, b2, ln2_g, ln2_b):
    B, S, D = x.shape
    depth = w_in.shape[0]
    bias_tables = [jnp.asarray(t) for t in _attn_bias_tables(S)]
    fhalf, ghalf, fa, fb = (jnp.asarray(m) for m in _dft_matrices(S))
    zfeat, decay = _filter_features(S)
    gate_cols = 3 * D
    pool_block = gate_cols // HYENA_WIDTH
    bf = lambda t: t.astype(BF16)
    p_attn_b, p_pool_b, p_hyena_b, w_out_b = map(bf, (p_attn, p_pool, p_hyena, w_out))

    for l in range(depth):
        mod = _ada(c, w_ada, b_ada, l).reshape(B, 6, D)
        w_nat, w_qkv, w_hy = _split_w_in(w_in[l])
        proj, u = _inproj_nat(x, mod, w_nat)
        qkv = _inproj_dil(u, w_qkv, [d for _, d in ATTN_GROUPS])
        hv, hx1, hx2 = _inproj_dil(u, w_hy, [2, 2, 2])
        ya = _attention(qkv, bias_tables)
        yb = _pool(proj, pool_block, w_pool, pool_scale, l)
        filt = _hyena_filters(zfeat, decay, f_w1, f_b1, f_w_inner, f_b_inner, f_w_out, f_freq, l)
        coeffs = _filter_spectra(fa, fb, filt)
        yc = _hyena(hv, hx1, hx2, fhalf, ghalf, coeffs, w_sconv, b_sconv, hy_bias, l)
        x1, u2, logits_t = _merge(ya, yb, yc, proj, x, mod, p_attn_b, p_pool_b, p_hyena_b, w_out_b,
                                  ln1_g, ln1_b, w_router, b_router, l)
        x = _moe(u2, logits_t, x1, mod, w1, b1, w2, b2, ln2_g, ln2_b, S, l).reshape(B, S, D)
    return x
```

```python
import functools
import math

import jax
import jax.numpy as jnp
import numpy as np
from jax import lax
from jax.experimental import pallas as pl
from jax.experimental.pallas import tpu as pltpu

F32 = jnp.float32
BF16 = jnp.bfloat16

HEAD_DIM = 64
HEADS_PER_GROUP = 4
ATTN_GROUPS = ((128, 1), (512, 4), (2048, 16))
N_ATTN_HEADS = HEADS_PER_GROUP * len(ATTN_GROUPS)
GROUP_WIDTH = HEADS_PER_GROUP * HEAD_DIM
HALF_WINDOW = 64
POOL_WINDOWS = (2, 4, 8, 16)
POOL_GROUP = 128
HYENA_WIDTH = 512
FILTER_EMB = 33
FILTER_HIDDEN = 64
N_EXPERTS = 32
TOP_K = 4
SWIGLU_LIMIT = 7.0
SWIGLU_ALPHA = 1.702
LN_EPS = 1e-5
DEPTH = 2
ALPHA_RES = (2 * DEPTH) ** 0.25
DECAY_TARGET = 1e-2
FAST_DECAY_PCT = 0.3
SLOW_DECAY_PCT = 1.5

QBLK = 128
HYENA_FREQ_TILE = 256
HYENA_FREQ_TILES_PER_STEP = 1
HYENA_TIME_TILE = 512
SPECTRA_TILE = 256
INPROJ_COLS = 1792
ATTN_UNROLL = 16
ROW_CHUNK = 256
MERGE_ROWS = 512
TOPK_LANES = 1024
MOE_BLK = 512
FFN_CHUNK = 1024
WEIGHT_CAST_ROWS = 128
DMA_PRIORITIES = 2
DMA_ISSUE_UNROLL = 8
COMBINE_CHUNKS = 8
MOE_TOK = 256
DISPATCH_TOK = 512
PAD_ROWS = 8
NEG_BIG = -1e30
LANES = 128
V7X_VMEM_BYTES = 64 * 1024 * 1024
VMEM_LIMIT = V7X_VMEM_BYTES * 7 // 8


def _cparams(*sem):
    return pltpu.CompilerParams(dimension_semantics=sem, vmem_limit_bytes=VMEM_LIMIT)


def _dot(a, b):
    return jnp.dot(a, b, preferred_element_type=F32)


def _dot3(a, b):
    a_hi = a.astype(BF16)
    b_hi = b.astype(BF16)
    a_lo = (a - a_hi.astype(F32)).astype(BF16)
    b_lo = (b - b_hi.astype(F32)).astype(BF16)
    return _dot(a_hi, b_hi) + _dot(a_hi, b_lo) + _dot(a_lo, b_hi)


def _layer_norm(r, g, b):
    mu = jnp.mean(r, axis=-1, keepdims=True)
    c = r - mu
    var = jnp.mean(c * c, axis=-1, keepdims=True)
    return c * lax.rsqrt(var + LN_EPS) * g + b


def _store_token_tiles(ref, val):
    T, D = val.shape
    n = D // LANES
    for c in range(n):
        ref[pl.ds(c, T, stride=n), :] = val[:, c * LANES:(c + 1) * LANES]


def _load_token_tiles(ref, T, D, plane=None):
    n = D // LANES
    lead = () if plane is None else tuple(plane)
    idx = lambda c: lead + (pl.ds(c, T, stride=n), slice(None))
    return [ref[idx(c)] for c in range(n)]


def _ada_kernel(c_ref, w_ref, b_ref, o_ref):
    c = c_ref[...]
    cond = c * (1.0 / (1.0 + jnp.exp(-c)))
    o_ref[...] = _dot3(cond, w_ref[...]) + b_ref[...]


def _ada(c, w_ada, b_ada, layer):
    B, D = c.shape
    n_out = w_ada.shape[-1]
    tn = 768
    return pl.pallas_call(
        _ada_kernel,
        grid=(n_out // tn,),
        in_specs=[
            pl.BlockSpec((B, D), lambda j: (0, 0)),
            pl.BlockSpec((None, D, tn), lambda j: (layer, 0, j)),
            pl.BlockSpec((None, 1, tn), lambda j: (layer, 0, j)),
        ],
        out_specs=pl.BlockSpec((B, tn), lambda j: (0, j)),
        out_shape=jax.ShapeDtypeStruct((B, n_out), F32),
        compiler_params=_cparams("arbitrary"),
        name="ada",
    )(c, w_ada, b_ada.reshape(b_ada.shape[0], 1, n_out))


def _inproj_nat_kernel(x_ref, mod_ref, w_ref, o_ref, u_ref):
    S = x_ref.shape[0]

    @pl.when(pl.program_id(1) == 0)
    def _():
        shift = mod_ref[0:1, :]
        scale1p = 1.0 + mod_ref[1:2, :]

        def body(i, carry):
            rows = pl.ds(pl.multiple_of(i * ROW_CHUNK, ROW_CHUNK), ROW_CHUNK)
            u_ref[rows, :] = (x_ref[rows, :] * scale1p + shift).astype(BF16)
            return carry

        lax.fori_loop(0, S // ROW_CHUNK, body, 0)

    o_ref[...] = _dot(u_ref[...], w_ref[...]).astype(BF16)


def _inproj_nat(x, mod, w_nat):
    B, S, D = x.shape
    n_out = w_nat.shape[1]
    tn = INPROJ_COLS
    return pl.pallas_call(
        _inproj_nat_kernel,
        grid=(B, n_out // tn),
        in_specs=[
            pl.BlockSpec((None, S, D), lambda b, j: (b, 0, 0)),
            pl.BlockSpec((None, 6, D), lambda b, j: (b, 0, 0)),
            pl.BlockSpec((D, tn), lambda b, j: (0, j)),
        ],
        out_specs=[
            pl.BlockSpec((None, S, tn), lambda b, j: (b, 0, j)),
            pl.BlockSpec((None, S, D), lambda b, j: (b, 0, 0)),
        ],
        out_shape=[
            jax.ShapeDtypeStruct((B, S, n_out), BF16),
            jax.ShapeDtypeStruct((B, S, D), BF16),
        ],
        compiler_params=_cparams("arbitrary", "arbitrary"),
        name="inproj_nat",
    )(x, mod, w_nat)


def _inproj_dil_kernel(u_ref, w_ref, o0_ref, o1_ref, o2_ref, acc_ref, *, dils):
    S = u_ref.shape[0]
    for g, o_ref in enumerate((o0_ref, o1_ref, o2_ref)):
        d = dils[g]
        L = S // d
        acc = _dot(u_ref[...], w_ref[g])
        if d == 1:
            o_ref[0] = acc.astype(BF16)
            continue
        for c in range(acc_ref.shape[0]):
            acc_ref[c] = acc[:, c * LANES:(c + 1) * LANES]
        for rho in range(d):
            for c in range(acc_ref.shape[0]):
                o_ref[rho, :, c * LANES:(c + 1) * LANES] = (
                    acc_ref[c, pl.ds(rho, L, stride=d), :].astype(BF16))


def _inproj_dil(u, w_groups, dils):
    B, S, D = u.shape
    W = w_groups.shape[-1]
    assert len(dils) == w_groups.shape[0] == 3
    return pl.pallas_call(
        functools.partial(_inproj_dil_kernel, dils=tuple(dils)),
        grid=(B,),
        in_specs=[
            pl.BlockSpec((None, S, D), lambda b: (b, 0, 0)),
            pl.BlockSpec((len(dils), D, W), lambda b: (0, 0, 0)),
        ],
        out_specs=[pl.BlockSpec((None, d, S // d, W), lambda b: (b, 0, 0, 0)) for d in dils],
        out_shape=[jax.ShapeDtypeStruct((B, d, S // d, W), BF16) for d in dils],
        scratch_shapes=[pltpu.VMEM((W // LANES, S, LANES), F32)],
        compiler_params=_cparams("arbitrary"),
        name="inproj_dil",
    )(u, w_groups)


def _window_start(i, L, W):
    return min(max(QBLK * i - HALF_WINDOW, 0), L - W)


def _attn_bias_tables(S):
    h = np.arange(1, N_ATTN_HEADS + 1, dtype=np.float32)
    slopes = (2.0 ** (-8.0 * h / N_ATTN_HEADS)).astype(np.float32)
    tables = []
    for g, (_, d) in enumerate(ATTN_GROUPS):
        L = S // d
        W = min(2 * QBLK, L)
        nb = L // QBLK
        variants = sorted({_window_start(i, L, W) - QBLK * i for i in range(nb)}, reverse=True)
        r = np.arange(QBLK)[:, None]
        c = np.arange(W)[None, :]
        tab = np.empty((len(variants), HEADS_PER_GROUP * QBLK, W), np.float32)
        for vi, delta in enumerate(variants):
            off = np.abs(c - r + delta)
            for hh in range(HEADS_PER_GROUP):
                slope = np.float64(slopes[g * HEADS_PER_GROUP + hh])
                bias = np.where(off <= HALF_WINDOW, -slope * (d * off), NEG_BIG)
                tab[vi, hh * QBLK:(hh + 1) * QBLK] = bias.astype(np.float32)
        tables.append(tab)
    return tables


def _block_attention(q, kwin, vwin, bias, head_of_lane):
    zero = jnp.zeros_like(q)
    q4 = jnp.concatenate([jnp.where(head_of_lane == h, q, zero) for h in range(HEADS_PER_GROUP)], axis=0)
    s = lax.dot_general(q4, kwin, (((1,), (1,)), ((), ())), preferred_element_type=F32)
    s = s + bias
    m = jnp.max(s, axis=-1, keepdims=True)
    p = jnp.exp(s - m)
    l = jnp.sum(p, axis=-1, keepdims=True)
    o4 = _dot(p.astype(BF16), vwin) * (1.0 / l)
    lse4 = m + jnp.log(l)
    out = jnp.zeros((QBLK, GROUP_WIDTH), F32)
    lse = jnp.zeros((QBLK, GROUP_WIDTH), F32)
    for h in range(HEADS_PER_GROUP):
        rows = slice(h * QBLK, (h + 1) * QBLK)
        sel = head_of_lane == h
        out = jnp.where(sel, o4[rows], out)
        lse = jnp.where(sel, lse4[rows], lse)
    return out, lse


def _attn_kernel(q0_ref, q1_ref, q2_ref, b0_ref, b1_ref, b2_ref, o_ref, acc_ref, lse_ref):
    S = o_ref.shape[0]
    GW = GROUP_WIDTH
    head_of_lane = lax.broadcasted_iota(jnp.int32, (QBLK, GW), 1) // HEAD_DIM
    n_planes = GW // LANES

    def put(g, rows, out, lse):
        for c in range(n_planes):
            acc_ref[g, c, rows, :] = out[:, c * LANES:(c + 1) * LANES]
            lse_ref[g, c, rows, :] = lse[:, c * LANES:(c + 1) * LANES]

    L0 = S
    nb0 = L0 // QBLK

    def g0_body(i, carry):
        ws = pl.multiple_of(jnp.clip(QBLK * i - HALF_WINDOW, 0, L0 - 2 * QBLK), HALF_WINDOW)
        var = jnp.where(i == 0, 0, jnp.where(i == nb0 - 1, 2, 1))
        rows = pl.ds(pl.multiple_of(i * QBLK, QBLK), QBLK)
        out, lse = _block_attention(
            q0_ref[0, rows, 0:GW],
            q0_ref[0, pl.ds(ws, 2 * QBLK), GW:2 * GW],
            q0_ref[0, pl.ds(ws, 2 * QBLK), 2 * GW:3 * GW],
            b0_ref[var], head_of_lane)
        put(0, rows, out, lse)
        return carry

    lax.fori_loop(0, nb0, g0_body, 0, unroll=ATTN_UNROLL)

    for g, (q_ref, b_ref) in ((1, (q1_ref, b1_ref)), (2, (q2_ref, b2_ref))):
        d = ATTN_GROUPS[g][1]
        L = S // d
        W = min(2 * QBLK, L)
        nb = L // QBLK
        deltas = sorted({_window_start(i, L, W) - QBLK * i for i in range(nb)}, reverse=True)

        def gd_body(rho, carry, q_ref=q_ref, b_ref=b_ref, d=d, L=L, W=W, nb=nb, deltas=deltas, g=g):
            for i in range(nb):
                ws = _window_start(i, L, W)
                var = deltas.index(ws - QBLK * i)
                out, lse = _block_attention(
                    q_ref[rho, i * QBLK:(i + 1) * QBLK, 0:GW],
                    q_ref[rho, ws:ws + W, GW:2 * GW],
                    q_ref[rho, ws:ws + W, 2 * GW:3 * GW],
                    b_ref[var], head_of_lane)
                dst = pl.ds(rho + i * QBLK * d, QBLK, stride=d)
                put(g, dst, out, lse)
            return carry

        lax.fori_loop(0, d, gd_body, 0, unroll=max(1, ATTN_UNROLL // nb))

    def merge_body(i, carry):
        rows = pl.ds(pl.multiple_of(i * ROW_CHUNK, ROW_CHUNK), ROW_CHUNK)
        for c in range(n_planes):
            l0, l1, l2 = lse_ref[0, c, rows, :], lse_ref[1, c, rows, :], lse_ref[2, c, rows, :]
            mx = jnp.maximum(jnp.maximum(l0, l1), l2)
            w0, w1, w2 = jnp.exp(l0 - mx), jnp.exp(l1 - mx), jnp.exp(l2 - mx)
            num = w0 * acc_ref[0, c, rows, :] + w1 * acc_ref[1, c, rows, :] + w2 * acc_ref[2, c, rows, :]
            o_ref[rows, c * LANES:(c + 1) * LANES] = (num * (1.0 / (w0 + w1 + w2))).astype(BF16)
        return carry

    lax.fori_loop(0, S // ROW_CHUNK, merge_body, 0)


def _attention(qkv, bias_tables):
    q0, q1, q2 = qkv
    B, _, S, W3 = q0.shape
    GW = GROUP_WIDTH
    in_specs = [pl.BlockSpec((None,) + q.shape[1:], lambda b: (b, 0, 0, 0)) for q in qkv]
    in_specs += [pl.BlockSpec(t.shape, lambda b: (0, 0, 0)) for t in bias_tables]
    return pl.pallas_call(
        _attn_kernel,
        grid=(B,),
        in_specs=in_specs,
        out_specs=pl.BlockSpec((None, S, GW), lambda b: (b, 0, 0)),
        out_shape=jax.ShapeDtypeStruct((B, S, GW), BF16),
        scratch_shapes=[pltpu.VMEM((3, GW // LANES, S, LANES), F32)] * 2,
        compiler_params=_cparams("arbitrary"),
        name="attention",
    )(q0, q1, q2, *bias_tables)


def _pool_kernel(y_ref, w_ref, sc_ref, o_ref, pad_ref):
    S = y_ref.shape[0]
    C = y_ref.shape[1]
    P = PAD_ROWS
    pad_ref[0:P, :] = jnp.zeros((P, C), F32)
    pad_ref[S + P:S + 2 * P, :] = jnp.zeros((P, C), F32)
    pad_ref[P:S + P, :] = y_ref[...].astype(F32)
    for c in range(S // ROW_CHUNK):
        r0 = c * ROW_CHUNK
        t = r0 + lax.broadcasted_iota(jnp.int32, (ROW_CHUNK, 1), 0)
        for g, w in enumerate(POOL_WINDOWS):
            cols = slice(g * POOL_GROUP, (g + 1) * POOL_GROUP)
            acc = pad_ref[P + r0 - w // 2:P + r0 - w // 2 + ROW_CHUNK, cols]
            for j in range(-w // 2 + 1, w // 2):
                acc = acc + pad_ref[P + r0 + j:P + r0 + j + ROW_CHUNK, cols]
            cnt = (jnp.minimum(t + w // 2, S) - jnp.maximum(t - w // 2, 0)).astype(F32)
            pooled = acc / cnt - pad_ref[P + r0:P + r0 + ROW_CHUNK, cols]
            mixed = _dot(pooled.astype(BF16), w_ref[g].astype(BF16)) * sc_ref[:, cols]
            o_ref[r0:r0 + ROW_CHUNK, cols] = mixed.astype(BF16)


def _pool(proj, col_block, w_pool, pool_scale, layer):
    B, S, _ = proj.shape
    C = POOL_GROUP * len(POOL_WINDOWS)
    return pl.pallas_call(
        _pool_kernel,
        grid=(B,),
        in_specs=[
            pl.BlockSpec((None, S, C), lambda b: (b, 0, col_block)),
            pl.BlockSpec((None,) + w_pool.shape[1:], lambda b: (layer, 0, 0, 0)),
            pl.BlockSpec((None, 1, C), lambda b: (layer, 0, 0)),
        ],
        out_specs=pl.BlockSpec((None, S, C), lambda b: (b, 0, 0)),
        out_shape=jax.ShapeDtypeStruct((B, S, C), BF16),
        scratch_shapes=[pltpu.VMEM((S + 2 * PAD_ROWS, C), F32)],
        compiler_params=_cparams("arbitrary"),
        name="pool",
    )(proj, w_pool, pool_scale.reshape(pool_scale.shape[0], 1, C))


def _filter_kernel(z_ref, w1_ref, b1_ref, wi_ref, bi_ref, wo_ref, fr_ref, dec_ref, o_ref):
    S = z_ref.shape[0]
    C = HYENA_WIDTH
    freq = fr_ref[...]
    h = jnp.sin(freq * (_dot3(z_ref[...], w1_ref[...]) + b1_ref[...]))
    for i in range(wi_ref.shape[0]):
        h = jnp.sin(freq * (_dot3(h, wi_ref[i]) + bi_ref[i]))
    not_first = lax.broadcasted_iota(jnp.int32, (S, 1), 0) > 0
    for blk in range(wo_ref.shape[1] // C):
        cols = slice(blk * C, (blk + 1) * C)
        f = _dot3(h, wo_ref[:, cols]) * dec_ref[...]
        if blk % 2 == 1:
            f = jnp.where(not_first, f, 0.0)
        o_ref[:, cols] = f.astype(BF16)


def _hyena_filters(zfeat, decay, f_w1, f_b1, f_w_inner, f_b_inner, f_w_out, f_freq, layer):
    S, E = zfeat.shape
    H = FILTER_HIDDEN
    n_out = f_w_out.shape[-1]
    n_inner = f_w_inner.shape[1]
    w1 = jnp.zeros((f_w1.shape[0], E, H), F32).at[:, :f_w1.shape[1]].set(f_w1)
    full = lambda shape: pl.BlockSpec(shape, lambda i: (0,) * len(shape))
    lay = lambda shape: pl.BlockSpec((None,) + shape, lambda i: (layer,) + (0,) * len(shape))
    return pl.pallas_call(
        _filter_kernel,
        grid=(1,),
        in_specs=[
            full((S, E)), lay((E, H)), lay((1, H)), lay((n_inner, H, H)), lay((n_inner, 1, H)),
            lay((H, n_out)), lay((1, H)), full((S, HYENA_WIDTH)),
        ],
        out_specs=full((S, n_out)),
        out_shape=jax.ShapeDtypeStruct((S, n_out), BF16),
        compiler_params=_cparams("arbitrary"),
        name="hyena_filter",
    )(zfeat, w1, f_b1.reshape(-1, 1, H), f_w_inner, f_b_inner.reshape(-1, n_inner, 1, H), f_w_out,
      f_freq.reshape(-1, 1, H), decay)


def _spec_kernel(fa_ref, fb_ref, h_ref, pre_ref, pim_ref, q1re_ref, q1im_ref, q2re_ref, q2im_ref, sp0_ref):
    TK = fa_ref.shape[0] // 2
    C = HYENA_WIDTH
    n_fft = 2 * h_ref.shape[0]
    xa = _dot(fa_ref[...], h_ref[...])
    xb = _dot(fb_ref[...], h_ref[...])
    row = pl.program_id(0) * TK + lax.broadcasted_iota(jnp.int32, (TK, 1), 0)
    theta = row.astype(F32) * (2.0 * math.pi / n_fft)
    cj, sj = jnp.cos(theta), jnp.sin(theta)
    sc = 2.0 / n_fft
    for o in range(pre_ref.shape[0]):
        hf = slice(2 * o * C, (2 * o + 1) * C)
        hb = slice((2 * o + 1) * C, (2 * o + 2) * C)
        ha_re = xa[:TK, hf] + xa[:TK, hb]
        ha_im = xa[TK:, hf] - xa[TK:, hb]
        hm_re = xb[:TK, hf] + xb[:TK, hb]
        hm_im = xb[TK:, hf] - xb[TK:, hb]
        p_re, p_im = (ha_re + hm_re) * sc, (ha_im - hm_im) * sc
        q_re, q_im = (ha_re - hm_re) * sc, (ha_im + hm_im) * sc
        pre_ref[o] = p_re
        pim_ref[o] = p_im
        q1re_ref[o] = q_re * cj + q_im * sj
        q1im_ref[o] = q_im * cj - q_re * sj
        q2re_ref[o] = q_re * cj - q_im * sj
        q2im_ref[o] = q_im * cj + q_re * sj

        @pl.when(pl.program_id(0) == 0)
        def _():
            h0, hs = ha_re[0:1], hm_re[0:1]
            hh_re = xa[TK:TK + 1, hf] + xa[TK:TK + 1, hb]
            hh_im = xb[TK:TK + 1, hf] - xb[TK:TK + 1, hb]
            sp0_ref[o] = jnp.concatenate(
                [(h0 + hs) * (0.5 * sc), (h0 - hs) * (0.5 * sc), hh_re * sc, hh_im * sc,
                 jnp.zeros((4, C), F32)], axis=0)


def _filter_spectra(fa, fb, filt):
    nkt, tk2, S = fa.shape
    TK = tk2 // 2
    C = HYENA_WIDTH
    n_ord = filt.shape[1] // (2 * C)
    out = jax.ShapeDtypeStruct((n_ord, nkt * TK, C), F32)
    ospec = pl.BlockSpec((n_ord, TK, C), lambda k: (0, k, 0))
    fspec = pl.BlockSpec((None, tk2, S), lambda k: (k, 0, 0))
    return pl.pallas_call(
        _spec_kernel,
        grid=(nkt,),
        in_specs=[fspec, fspec, pl.BlockSpec(filt.shape, lambda k: (0, 0))],
        out_specs=[ospec] * 6 + [pl.BlockSpec((n_ord, 8, C), lambda k: (0, 0, 0))],
        out_shape=[out] * 6 + [jax.ShapeDtypeStruct((n_ord, 8, C), F32)],
        compiler_params=_cparams("arbitrary"),
        name="hyena_spectra",
    )(fa, fb, filt)


def _hyena_kernel(v_ref, x1_ref, x2_ref, f_ref, g_ref, pre_ref, pim_ref, q1re_ref, q1im_ref, q2re_ref, q2im_ref,
                  sp0_ref, wsc_ref, bsc_ref, hb_ref, o_ref, pad_ref, z_ref, zb_ref, gate_ref, zcat_ref, il_ref):
    H, C2 = z_ref.shape
    C = C2 // 2
    n_sub, tk2, _ = f_ref.shape
    TK = tk2 // 2
    TT = g_ref.shape[0]
    nf = H // (n_sub * TK)
    R = ROW_CHUNK
    P = PAD_ROWS
    o = pl.program_id(1)
    s = pl.program_id(2)
    even, odd = slice(0, C), slice(C, C2)

    def fill_pad(in_ref):
        for par in range(2):
            pad_ref[par, 0:P, :] = jnp.zeros((P, C), F32)
            pad_ref[par, H + P:H + 2 * P, :] = jnp.zeros((P, C), F32)
            pad_ref[par, P:H + P, :] = in_ref[par].astype(F32)

    def conv_chunk(r0, w, bias):
        ev = pad_ref[0, P + r0:P + r0 + R, :]
        od = pad_ref[1, P + r0:P + r0 + R, :]
        od_prev = pad_ref[1, P + r0 - 1:P + r0 - 1 + R, :]
        ev_next = pad_ref[0, P + r0 + 1:P + r0 + 1 + R, :]
        w0, w1, w2 = w[0:1], w[1:2], w[2:3]
        return w0 * od_prev + w1 * ev + w2 * od + bias, w0 * ev + w1 * od + w2 * ev_next + bias

    @pl.when((o == 0) & (s == 0))
    def _():
        fill_pad(v_ref)
        for c in range(H // R):
            rows = slice(c * R, (c + 1) * R)
            conv_e, conv_o = conv_chunk(c * R, wsc_ref[:, 0:C], bsc_ref[:, 0:C])
            z_ref[rows, even] = conv_e
            z_ref[rows, odd] = conv_o
            zb_ref[rows, even] = conv_e.astype(BF16)
            zb_ref[rows, odd] = conv_o.astype(BF16)

    @pl.when((s == 0) & (o == 0))
    def _():
        fill_pad(x1_ref)

    @pl.when((s == 0) & (o == 1))
    def _():
        fill_pad(x2_ref)

    @pl.when(s < nf)
    def _():
        a, b, c, d = sp0_ref[0:1, :], sp0_ref[1:2, :], sp0_ref[2:3, :], sp0_ref[3:4, :]
        for sub in range(n_sub):
            tile = s * n_sub + sub
            cs = slice(sub * TK, (sub + 1) * TK)
            x = _dot(f_ref[sub], zb_ref[...])
            e_re, o_re, e_im, o_im = x[:TK, even], x[:TK, odd], x[TK:, even], x[TK:, odd]
            p_re, p_im = pre_ref[cs, :], pim_ref[cs, :]
            q1_re, q1_im, q2_re, q2_im = q1re_ref[cs, :], q1im_ref[cs, :], q2re_ref[cs, :], q2im_ref[cs, :]
            ze_re = e_re * p_re - e_im * p_im + o_re * q1_re - o_im * q1_im
            ze_im = e_re * p_im + e_im * p_re + o_re * q1_im + o_im * q1_re
            zo_re = e_re * q2_re - e_im * q2_im + o_re * p_re - o_im * p_im
            zo_im = e_re * q2_im + e_im * q2_re + o_re * p_im + o_im * p_re
            first = (tile * TK + lax.broadcasted_iota(jnp.int32, (TK, 1), 0)) == 0
            ze_re = jnp.where(first, e_re[0:1] * a + o_re[0:1] * b, ze_re)
            zo_re = jnp.where(first, e_re[0:1] * b + o_re[0:1] * a, zo_re)
            ze_im = jnp.where(first, e_im[0:1] * c + o_im[0:1] * d, ze_im)
            zo_im = jnp.where(first, o_im[0:1] * c - e_im[0:1] * d, zo_im)
            base = pl.multiple_of(tile * 2 * TK, 2 * TK)
            zcat_ref[pl.ds(base, TK), even] = ze_re.astype(BF16)
            zcat_ref[pl.ds(base, TK), odd] = zo_re.astype(BF16)
            zcat_ref[pl.ds(base + TK, TK), even] = ze_im.astype(BF16)
            zcat_ref[pl.ds(base + TK, TK), odd] = zo_im.astype(BF16)

    @pl.when(s == nf)
    def _():
        w = jnp.where(o == 0, wsc_ref[:, C:2 * C], wsc_ref[:, 2 * C:3 * C])
        bias = jnp.where(o == 0, bsc_ref[:, C:2 * C], bsc_ref[:, 2 * C:3 * C])
        for c in range(H // R):
            conv_e, conv_o = conv_chunk(c * R, w, bias)
            gate_ref[c * R:(c + 1) * R, even] = conv_e
            gate_ref[c * R:(c + 1) * R, odd] = conv_o

    @pl.when(s >= nf)
    def _():
        tt = s - nf
        rows = pl.ds(pl.multiple_of(tt * TT, TT), TT)
        y = _dot(g_ref[...], zcat_ref[...])
        hrow = jnp.where(o == 0, hb_ref[0:1, :], hb_ref[1:2, :])
        hbias = jnp.concatenate([hrow, hrow], axis=1)
        znew = gate_ref[rows, :] * (y + z_ref[rows, :] * hbias)
        z_ref[rows, :] = znew
        zb_ref[rows, :] = znew.astype(BF16)

        @pl.when(o == pl.num_programs(1) - 1)
        def _():
            for p in range(C // LANES):
                il_ref[p, pl.ds(0, TT, stride=2), :] = znew[:, p * LANES:(p + 1) * LANES]
                il_ref[p, pl.ds(1, TT, stride=2), :] = znew[:, C + p * LANES:C + (p + 1) * LANES]
            out_rows = pl.ds(pl.multiple_of(tt * 2 * TT, 2 * TT), 2 * TT)
            for p in range(C // LANES):
                o_ref[out_rows, p * LANES:(p + 1) * LANES] = il_ref[p].astype(BF16)


def _hyena(hv, hx1, hx2, fhalf, ghalf, coeffs, w_sconv, b_sconv, hy_bias, layer):
    B, _, H, C = hv.shape
    S = 2 * H
    n_tiles, tk2, _ = fhalf.shape
    TK = tk2 // 2
    n_sub = HYENA_FREQ_TILES_PER_STEP
    nf = n_tiles // n_sub
    TT = HYENA_TIME_TILE
    ni = H // TT
    sp0 = coeffs[-1]
    n_ord = sp0.shape[0]
    fwd = lambda s: jnp.minimum(s, nf - 1)
    inv = lambda s: jnp.clip(s - nf, 0, ni - 1)
    inp = pl.BlockSpec((None, 2, H, C), lambda bb, o, s: (bb, 0, 0, 0), pipeline_mode=pl.Buffered(1))
    coef = pl.BlockSpec((None, n_sub * TK, C), lambda bb, o, s: (o, fwd(s), 0))
    return pl.pallas_call(
        _hyena_kernel,
        grid=(B, n_ord, nf + ni),
        in_specs=[
            inp, inp, inp,
            pl.BlockSpec((n_sub, tk2, H), lambda bb, o, s: (fwd(s), 0, 0)),
            pl.BlockSpec((TT, 2 * H), lambda bb, o, s: (inv(s), 0)),
            coef, coef, coef, coef, coef, coef,
            pl.BlockSpec((None, 8, C), lambda bb, o, s: (o, 0, 0)),
            pl.BlockSpec((None, 3, 3 * C), lambda bb, o, s: (layer, 0, 0)),
            pl.BlockSpec((None, 1, 3 * C), lambda bb, o, s: (layer, 0, 0)),
            pl.BlockSpec((None, n_ord, C), lambda bb, o, s: (layer, 0, 0)),
        ],
        out_specs=pl.BlockSpec((None, S, C), lambda bb, o, s: (bb, 0, 0)),
        out_shape=jax.ShapeDtypeStruct((B, S, C), BF16),
        scratch_shapes=[
            pltpu.VMEM((2, H + 2 * PAD_ROWS, C), F32),
            pltpu.VMEM((H, 2 * C), F32),
            pltpu.VMEM((H, 2 * C), BF16),
            pltpu.VMEM((H, 2 * C), F32),
            pltpu.VMEM((2 * H, 2 * C), BF16),
            pltpu.VMEM((C // LANES, 2 * TT, LANES), F32),
        ],
        compiler_params=_cparams("arbitrary", "arbitrary", "arbitrary"),
        name="hyena_conv",
    )(hv, hx1, hx2, fhalf, ghalf, *coeffs, w_sconv, b_sconv.reshape(b_sconv.shape[0], 1, 3 * C), hy_bias)


def _packed_dft(bins, n_samples, n_fft, tile, sin_row0):
    theta = ((bins[:, None] * np.arange(n_samples)[None, :]) % n_fft) * (2.0 * math.pi / n_fft)
    fc = np.cos(theta)
    fs = -np.sin(theta)
    fs[0] = sin_row0
    nt = bins.shape[0] // tile
    packed = np.concatenate([fc.reshape(nt, tile, n_samples), fs.reshape(nt, tile, n_samples)], axis=1)
    return packed.astype(np.float32).astype(BF16)


@functools.lru_cache(maxsize=None)
def _dft_matrices(S):
    H, N = S // 2, 2 * S
    j = np.arange(H)
    m = np.arange(H)
    n = np.arange(S)
    quarter = np.asarray([1.0, 0.0, -1.0, 0.0])
    fhalf = _packed_dft(j, H, S, HYENA_FREQ_TILE, 1.0 - 2.0 * (m & 1))
    ghalf = np.ascontiguousarray(fhalf.reshape(2 * H, H).T)
    fa = _packed_dft(j, S, N, SPECTRA_TILE, quarter[n & 3])
    fb = _packed_dft(S - j, S, N, SPECTRA_TILE, -quarter[(n + 3) & 3])
    return fhalf, ghalf, fa, fb


def _filter_features(S):
    t = jnp.linspace(0.0, 1.0, S, dtype=F32)[:, None]
    bands = (FILTER_EMB - 1) // 2
    w = 2.0 * math.pi * jnp.arange(S, dtype=F32) / S
    f = jnp.linspace(1e-4, bands - 1, bands, dtype=F32)
    ang = w[:, None] * f[None, :]
    z = jnp.concatenate([t, jnp.cos(ang), -jnp.sin(ang)], axis=-1)
    z = jnp.pad(z, ((0, 0), (0, 128 - FILTER_EMB)))
    min_decay = math.log(DECAY_TARGET) / SLOW_DECAY_PCT
    max_decay = math.log(DECAY_TARGET) / FAST_DECAY_PCT
    deltas = jnp.linspace(min_decay, max_decay, HYENA_WIDTH, dtype=F32)
    decay = jnp.exp(-t * jnp.abs(deltas)[None, :])
    return z, decay


def _merge_kernel(ya_ref, yb_ref, yc_ref, gl_ref, x_ref, mod_ref, pa_ref, pp_ref, ph_ref, wo_ref,
                  g_ref, b_ref, wr_ref, br_ref, x1_ref, u2_ref, lt_ref):
    D = x_ref.shape[1]
    sig = lambda v: 1.0 / (1.0 + jnp.exp(-v))
    merged = sig(gl_ref[:, 0:D].astype(F32)) * _dot(ya_ref[...], pa_ref[...])
    merged += sig(gl_ref[:, D:2 * D].astype(F32)) * _dot(yb_ref[...], pp_ref[...])
    merged += sig(gl_ref[:, 2 * D:3 * D].astype(F32)) * _dot(yc_ref[...], ph_ref[...])
    h = _dot(merged.astype(BF16), wo_ref[...])
    gate1, shift2, scale2 = mod_ref[2:3, :], mod_ref[3:4, :], mod_ref[4:5, :]
    x1 = _layer_norm(ALPHA_RES * x_ref[...] + (1.0 + gate1) * h, g_ref[...], b_ref[...])
    x1_ref[...] = x1
    u2 = x1 * (1.0 + scale2) + shift2
    _store_token_tiles(u2_ref, u2)
    TM = u2.shape[0]
    u_hi = u2.astype(BF16)
    u_lo = (u2 - u_hi.astype(F32)).astype(BF16)
    r = _dot(jnp.concatenate([u_hi, u_lo], axis=0), wr_ref[...])
    rt = (r[:TM] + r[TM:] + br_ref[...]).T
    lt_ref[...] = rt[0:N_EXPERTS, :] + rt[N_EXPERTS:2 * N_EXPERTS, :]


def _merge(ya, yb, yc, proj, x, mod, p_attn, p_pool, p_hyena, w_out, ln_g, ln_b, w_router, b_router, layer):
    B, S, D = x.shape
    N = B * S
    TM = MERGE_ROWS
    tiles_per_seq = S // TM
    EP = 128
    flat = lambda t: t.reshape(N, t.shape[-1])
    w_hi = w_router[layer].astype(BF16)
    w_lo = (w_router[layer] - w_hi.astype(F32)).astype(BF16)
    wr = jnp.zeros((D, EP), BF16).at[:, :N_EXPERTS].set(w_hi).at[:, N_EXPERTS:2 * N_EXPERTS].set(w_lo)
    br = jnp.zeros((1, EP), F32).at[:, :N_EXPERTS].set(b_router[layer][None])
    tok = lambda w: pl.BlockSpec((TM, w), lambda i: (i, 0))
    lay = lambda t: pl.BlockSpec((None,) + t.shape[1:], lambda i: (layer,) + (0,) * (t.ndim - 1))
    ln_g3, ln_b3 = ln_g.reshape(-1, 1, D), ln_b.reshape(-1, 1, D)
    return pl.pallas_call(
        _merge_kernel,
        grid=(N // TM,),
        in_specs=[
            tok(ya.shape[-1]), tok(yb.shape[-1]), tok(yc.shape[-1]), tok(3 * D), tok(D),
            pl.BlockSpec((None, 6, D), lambda i: (i // tiles_per_seq, 0, 0)),
            lay(p_attn), lay(p_pool), lay(p_hyena), lay(w_out), lay(ln_g3), lay(ln_b3),
            pl.BlockSpec((D, EP), lambda i: (0, 0)),
            pl.BlockSpec((1, EP), lambda i: (0, 0)),
        ],
        out_specs=[tok(D), pl.BlockSpec((TM * D // LANES, LANES), lambda i: (i, 0)),
                   pl.BlockSpec((N_EXPERTS, TM), lambda i: (0, i))],
        out_shape=[
            jax.ShapeDtypeStruct((N, D), F32),
            jax.ShapeDtypeStruct((N * D // LANES, LANES), F32),
            jax.ShapeDtypeStruct((N_EXPERTS, N), F32),
        ],
        compiler_params=_cparams("arbitrary"),
        name="merge_ln1_router",
    )(flat(ya), flat(yb), flat(yc), flat(proj), flat(x), mod, p_attn, p_pool, p_hyena, w_out,
      ln_g3, ln_b3, wr, br)


def _topk_kernel(l_ref, e_ref, g_ref, r_ref, cnt_ref, carry_ref):
    E, TL = l_ref.shape

    @pl.when(pl.program_id(0) == 0)
    def _():
        carry_ref[...] = jnp.zeros_like(carry_ref)

    l = l_ref[...]
    eio = lax.broadcasted_iota(jnp.int32, (E, TL), 0).astype(F32)
    vals, hots = [], []
    for k in range(TOP_K):
        m = jnp.max(l, axis=0, keepdims=True)
        idx = jnp.min(jnp.where(l == m, eio, float(E)), axis=0, keepdims=True)
        hot = eio == idx
        l = jnp.where(hot, -jnp.inf, l)
        vals.append(m)
        hots.append(hot)
        e_ref[k:k + 1, :] = idx.astype(jnp.int32)
    exps = [jnp.exp(v - vals[0]) for v in vals]
    inv = 1.0 / (exps[0] + exps[1] + exps[2] + exps[3])
    for k in range(TOP_K):
        g_ref[k:k + 1, :] = exps[k] * inv

    memb = jnp.zeros((E, TL), F32)
    for hot in hots:
        memb = memb + hot.astype(F32)
    upper = (lax.broadcasted_iota(jnp.int32, (TL, TL), 0)
             <= lax.broadcasted_iota(jnp.int32, (TL, TL), 1)).astype(BF16)
    incl = _dot(memb.astype(BF16), upper)
    excl = incl - memb + carry_ref[:, 0:1]
    for k in range(TOP_K):
        rank = jnp.sum(jnp.where(hots[k], excl, 0.0), axis=0, keepdims=True)
        r_ref[k:k + 1, :] = rank.astype(jnp.int32)
    carry_ref[...] = carry_ref[...] + jnp.sum(memb, axis=1, keepdims=True)
    cnt_ref[...] = carry_ref[...]


def _topk(logits_t):
    E, N = logits_t.shape
    TL = TOPK_LANES
    tok = pl.BlockSpec((TOP_K, TL), lambda i: (0, i))
    return pl.pallas_call(
        _topk_kernel,
        grid=(N // TL,),
        in_specs=[pl.BlockSpec((E, TL), lambda i: (0, i))],
        out_specs=[tok, tok, tok, pl.BlockSpec((E, 128), lambda i: (0, 0))],
        out_shape=[
            jax.ShapeDtypeStruct((TOP_K, N), jnp.int32),
            jax.ShapeDtypeStruct((TOP_K, N), F32),
            jax.ShapeDtypeStruct((TOP_K, N), jnp.int32),
            jax.ShapeDtypeStruct((E, 128), F32),
        ],
        scratch_shapes=[pltpu.VMEM((E, 128), F32)],
        compiler_params=_cparams("arbitrary"),
        name="topk_rank",
    )(logits_t)


def _dest_kernel(offs_ref, e_ref, r_ref, d_ref):
    e = e_ref[...]
    acc = r_ref[...]
    for ex in range(N_EXPERTS):
        acc = acc + jnp.where(e == ex, offs_ref[ex], 0)
    d_ref[...] = acc


def _dest_rows(offs, top_e, rank):
    K, N = top_e.shape
    TL = TOPK_LANES
    tok = pl.BlockSpec((K, TL), lambda i, offs: (0, i))
    return pl.pallas_call(
        _dest_kernel,
        grid_spec=pltpu.PrefetchScalarGridSpec(
            num_scalar_prefetch=1, grid=(N // TL,), in_specs=[tok, tok], out_specs=tok),
        out_shape=jax.ShapeDtypeStruct((K, N), jnp.int32),
        compiler_params=_cparams("arbitrary"),
        name="dest_rows",
    )(offs, top_e, rank)


def _dispatch_kernel(tail_ref, n_used_ref, dest_ref, u_ref, xs_ref, zero_ref, sem):
    TT = dest_ref.shape[0] // TOP_K
    RT = u_ref.shape[0] // TT
    blk_rows = zero_ref.shape[0]
    n_blocks = xs_ref.shape[0] // blk_rows

    def zero_copy(first_row):
        start = pl.multiple_of(first_row * RT, RT)
        return pltpu.make_async_copy(zero_ref, xs_ref.at[pl.ds(start, blk_rows), :], sem)

    @pl.when(pl.program_id(0) == 0)
    def _():
        zero_ref[...] = jnp.zeros_like(zero_ref)

        def tails(fn):
            def body(e, carry):
                @pl.when(tail_ref[e] >= 0)
                def _():
                    fn(zero_copy(jnp.maximum(tail_ref[e], 0)))
                return carry
            lax.fori_loop(0, N_EXPERTS, body, 0)

        def unused(fn):
            def body(i, carry):
                fn(zero_copy(i * (blk_rows // RT)))
                return carry
            lax.fori_loop(n_used_ref[0], n_blocks, body, 0)

        tails(lambda cp: cp.start())
        unused(lambda cp: cp.start())
        tails(lambda cp: cp.wait())
        unused(lambda cp: cp.wait())

    def row_copy(j, k):
        src = pl.multiple_of(j * RT, RT)
        dst = pl.multiple_of(dest_ref[j * TOP_K + k] * RT, RT)
        return pltpu.make_async_copy(u_ref.at[pl.ds(src, RT), :], xs_ref.at[pl.ds(dst, RT), :], sem)

    def start(j, carry):
        for k in range(TOP_K):
            row_copy(j, k).start(priority=k % DMA_PRIORITIES)
        return carry

    lax.fori_loop(0, TT, start, 0, unroll=DMA_ISSUE_UNROLL)
    for k in range(TOP_K):
        pltpu.make_async_copy(u_ref, xs_ref.at[pl.ds(0, TT * RT), :], sem).wait()


def _dispatch(tail_start, n_used, dest, u2_tiles, n_rows, D):
    RT = D // LANES
    N = u2_tiles.shape[0] // RT
    TT = DISPATCH_TOK
    return pl.pallas_call(
        _dispatch_kernel,
        grid_spec=pltpu.PrefetchScalarGridSpec(
            num_scalar_prefetch=2,
            grid=(N // TT,),
            in_specs=[
                pl.BlockSpec((TOP_K * TT,), lambda i, tail, nu: (i,), memory_space=pltpu.SMEM),
                pl.BlockSpec((TT * RT, LANES), lambda i, tail, nu: (i, 0)),
            ],
            out_specs=pl.BlockSpec(memory_space=pl.ANY),
            scratch_shapes=[pltpu.VMEM((MOE_BLK * RT, LANES), F32), pltpu.SemaphoreType.DMA(())],
        ),
        out_shape=jax.ShapeDtypeStruct((n_rows * RT, LANES), F32),
        compiler_params=_cparams("arbitrary"),
        name="moe_dispatch",
    )(tail_start, n_used, dest, u2_tiles)


def _ffn_kernel(blk_e_ref, n_used_ref, xs_ref, w1_ref, b1_ref, w2_ref, b2_ref, ys_ref,
                w1b_ref, w2b_ref, act_ref):
    F, D = w2_ref.shape
    BLK = xs_ref.shape[0] * LANES // D
    i = pl.program_id(0)
    used = i < n_used_ref[0]
    new_expert = (i == 0) | (blk_e_ref[i] != blk_e_ref[jnp.maximum(i - 1, 0)])

    @pl.when(used & new_expert)
    def _():
        for r in range(0, D, WEIGHT_CAST_ROWS):
            w1b_ref[r:r + WEIGHT_CAST_ROWS, :] = w1_ref[r:r + WEIGHT_CAST_ROWS, :].astype(BF16)
        for r in range(0, F, WEIGHT_CAST_ROWS):
            w2b_ref[r:r + WEIGHT_CAST_ROWS, :] = w2_ref[r:r + WEIGHT_CAST_ROWS, :].astype(BF16)

    @pl.when(used)
    def _():
        x = jnp.concatenate(_load_token_tiles(xs_ref, BLK, D), axis=1).astype(BF16)
        for c in range(0, F, FFN_CHUNK):
            cg = slice(c, c + FFN_CHUNK)
            cl = slice(F + c, F + c + FFN_CHUNK)
            glu = jnp.minimum(_dot(x, w1b_ref[:, cg]) + b1_ref[:, cg], SWIGLU_LIMIT)
            lin = jnp.clip(_dot(x, w1b_ref[:, cl]) + b1_ref[:, cl], -SWIGLU_LIMIT, SWIGLU_LIMIT)
            act = glu * (1.0 / (1.0 + jnp.exp(-SWIGLU_ALPHA * glu))) * (lin + 1.0)
            act_ref[:, cg] = act.astype(BF16)
        _store_token_tiles(ys_ref, _dot(act_ref[...], w2b_ref[...]) + b2_ref[...])

    @pl.when(jnp.logical_not(used))
    def _():
        ys_ref[...] = jnp.zeros_like(ys_ref)


def _expert_ffn(blk_e, n_used, xs_tiles, w1, b1, w2, b2, layer):
    _, E, D, F2 = w1.shape
    F = F2 // 2
    RT = D // LANES
    BLK = MOE_BLK
    rows = pl.BlockSpec((BLK * RT, LANES), lambda i, be, nu: (i, 0))
    rows_in = pl.BlockSpec((BLK * RT, LANES), lambda i, be, nu: (jnp.minimum(i, jnp.maximum(nu[0] - 1, 0)), 0))
    per_expert = lambda r, c: pl.BlockSpec((None, None, r, c), lambda i, be, nu: (layer, be[i], 0, 0))
    return pl.pallas_call(
        _ffn_kernel,
        grid_spec=pltpu.PrefetchScalarGridSpec(
            num_scalar_prefetch=2,
            grid=(xs_tiles.shape[0] // (BLK * RT),),
            in_specs=[rows_in, per_expert(D, F2), per_expert(1, F2), per_expert(F, D), per_expert(1, D)],
            out_specs=rows,
            scratch_shapes=[pltpu.VMEM((D, F2), BF16), pltpu.VMEM((F, D), BF16), pltpu.VMEM((BLK, F), BF16)],
        ),
        out_shape=jax.ShapeDtypeStruct(xs_tiles.shape, F32),
        compiler_params=_cparams("arbitrary"),
        name="moe_ffn",
    )(blk_e, n_used, xs_tiles, w1, b1.reshape(-1, E, 1, F2), w2, b2.reshape(-1, E, 1, D))


def _combine_kernel(dest_ref, dest_next_ref, gt_ref, x1_ref, mod_ref, g_ref, b_ref, ys_ref, o_ref,
                    buf_a, buf_b, sem):
    TT, D = x1_ref.shape
    RT = D // LANES
    RC = TT // COMBINE_CHUNKS
    i = pl.program_id(0)
    last = pl.num_programs(0) - 1

    def row_copy(d_ref, buf, s, j, k):
        src = pl.multiple_of(d_ref[j * TOP_K + k] * RT, RT)
        dst = pl.multiple_of(j * RT, RT)
        return pltpu.make_async_copy(ys_ref.at[pl.ds(src, RT), :], buf.at[k, pl.ds(dst, RT), :], sem.at[s])

    def wait_tile(buf, s):
        for k in range(TOP_K):
            pltpu.make_async_copy(ys_ref.at[pl.ds(0, TT * RT), :], buf.at[k], sem.at[s]).wait()

    def reduce_rows(buf, r0):
        rows = slice(r0, r0 + RC)
        chunks = None
        for k in range(TOP_K):
            gate = gt_ref[rows, k:k + 1]
            tiles = [gate * buf[k, pl.ds(r0 * RT + c, RC, stride=RT), :] for c in range(RT)]
            chunks = tiles if chunks is None else [a + b for a, b in zip(chunks, tiles)]
        h = jnp.concatenate(chunks, axis=1)
        gate2 = mod_ref[5:6, :]
        o_ref[rows, :] = _layer_norm(ALPHA_RES * x1_ref[rows, :] + (1.0 + gate2) * h, g_ref[...], b_ref[...])

    def run(cur, s_cur, nxt, s_nxt):
        wait_tile(cur, s_cur)
        for ch in range(COMBINE_CHUNKS):
            reduce_rows(cur, ch * RC)
            for j in range(ch * RC, (ch + 1) * RC):
                for k in range(TOP_K):
                    row_copy(dest_next_ref, nxt, s_nxt, j, k).start(priority=k % DMA_PRIORITIES)

        @pl.when(i == last)
        def _():
            wait_tile(nxt, s_nxt)

    @pl.when(i == 0)
    def _():
        def body(j, carry):
            for k in range(TOP_K):
                row_copy(dest_ref, buf_a, 0, j, k).start(priority=k % DMA_PRIORITIES)
            return carry
        lax.fori_loop(0, TT, body, 0, unroll=DMA_ISSUE_UNROLL)

    @pl.when(i % 2 == 0)
    def _():
        run(buf_a, 0, buf_b, 1)

    @pl.when(i % 2 == 1)
    def _():
        run(buf_b, 1, buf_a, 0)


def _combine(dest, gates_t, x1, mod, ln_g, ln_b, ys, seq_len, layer):
    N, D = x1.shape
    TT = MOE_TOK
    tiles_per_seq = seq_len // TT
    ln_g3, ln_b3 = ln_g.reshape(-1, 1, D), ln_b.reshape(-1, 1, D)
    lay = lambda t: pl.BlockSpec((None,) + t.shape[1:], lambda i: (layer,) + (0,) * (t.ndim - 1))
    return pl.pallas_call(
        _combine_kernel,
        grid=(N // TT,),
        in_specs=[
            pl.BlockSpec((TOP_K * TT,), lambda i: (i,), memory_space=pltpu.SMEM),
            pl.BlockSpec((TOP_K * TT,), lambda i: (jnp.minimum(i + 1, N // TT - 1),), memory_space=pltpu.SMEM),
            pl.BlockSpec((TT, TOP_K), lambda i: (i, 0)),
            pl.BlockSpec((TT, D), lambda i: (i, 0)),
            pl.BlockSpec((None, 6, D), lambda i: (i // tiles_per_seq, 0, 0)),
            lay(ln_g3), lay(ln_b3),
            pl.BlockSpec(memory_space=pl.ANY),
        ],
        out_specs=pl.BlockSpec((TT, D), lambda i: (i, 0)),
        out_shape=jax.ShapeDtypeStruct((N, D), F32),
        scratch_shapes=[pltpu.VMEM((TOP_K, TT * D // LANES, LANES), F32)] * 2 + [pltpu.SemaphoreType.DMA((2,))],
        compiler_params=_cparams("arbitrary"),
        name="moe_combine_ln2",
    )(dest, dest, gates_t, x1, mod, ln_g3, ln_b3, ys)


def _moe(u2_tiles, logits_t, x1, mod, w1, b1, w2, b2, ln_g, ln_b, seq_len, layer):
    N, D = x1.shape
    BLK = MOE_BLK
    top_e, gates, rank, counts = _topk(logits_t)
    counts = counts[:, 0].astype(jnp.int32)
    padded = (counts + BLK - 1) // BLK * BLK
    pend = jnp.cumsum(padded)
    offs = pend - padded
    n_blocks = -(-(N * TOP_K) // BLK) + N_EXPERTS
    blk_start = jnp.arange(n_blocks, dtype=jnp.int32) * BLK
    blk_e = jnp.sum((pend[None, :] <= blk_start[:, None]).astype(jnp.int32), axis=1)
    blk_e = jnp.minimum(blk_e, N_EXPERTS - 1).astype(jnp.int32)
    n_used = (pend[-1:] // BLK).astype(jnp.int32)
    tail_start = jnp.where(padded > 0, pend - BLK, -1).astype(jnp.int32)
    dest = _dest_rows(offs.astype(jnp.int32), top_e, rank).T.reshape(-1)
    xs = _dispatch(tail_start, n_used, dest, u2_tiles, n_blocks * BLK, D)
    ys = _expert_ffn(blk_e, n_used, xs, w1, b1, w2, b2, layer)
    return _combine(dest, gates.T, x1, mod, ln_g, ln_b, ys, seq_len, layer)


def _split_w_in(w_in):
    AW = N_ATTN_HEADS * HEAD_DIM
    GW = GROUP_WIDTH
    C = HYENA_WIDTH
    w = w_in.astype(BF16)
    w_nat = jnp.concatenate([w[:, 3 * AW + 4 * C:], w[:, 3 * AW:3 * AW + C]], axis=1)
    col_scale = (1.0 / math.sqrt(HEAD_DIM), 1.0, 1.0)
    w_qkv = jnp.stack([
        jnp.concatenate([w[:, t * AW + g * GW:t * AW + (g + 1) * GW] * col_scale[t] for t in range(3)], axis=1)
        for g in range(len(ATTN_GROUPS))]).astype(BF16)
    w_hy = jnp.stack([w[:, 3 * AW + (1 + t) * C:3 * AW + (2 + t) * C] for t in range(3)])
    return w_nat, w_qkv, w_hy


def kernel(x, c, w_ada, b_ada, w_in, w_sconv, b_sconv, w_pool, pool_scale, f_w1, f_b1, f_w_inner, f_b_inner, f_w_out, f_freq, hy_bias, p_attn, p_pool, p_hyena, w_out, ln1_g, ln1_b, w_router, b_router, w1, b1, w2, b2, ln2_g, ln2_b):
    B, S, D = x.shape
    depth = w_in.shape[0]
    bias_tables = [jnp.asarray(t) for t in _attn_bias_tables(S)]
    fhalf, ghalf, fa, fb = (jnp.asarray(m) for m in _dft_matrices(S))
    zfeat, decay = _filter_features(S)
    gate_cols = 3 * D
    pool_block = gate_cols // HYENA_WIDTH
    bf = lambda t: t.astype(BF16)
    p_attn_b, p_pool_b, p_hyena_b, w_out_b = map(bf, (p_attn, p_pool, p_hyena, w_out))

    for l in range(depth):
        mod = _ada(c, w_ada, b_ada, l).reshape(B, 6, D)
        w_nat, w_qkv, w_hy = _split_w_in(w_in[l])
        proj, u = _inproj_nat(x, mod, w_nat)
        qkv = _inproj_dil(u, w_qkv, [d for _, d in ATTN_GROUPS])
        hv, hx1, hx2 = _inproj_dil(u, w_hy, [2, 2, 2])
        ya = _attention(qkv, bias_tables)
        yb = _pool(proj, pool_block, w_pool, pool_scale, l)
        filt = _hyena_filters(zfeat, decay, f_w1, f_b1, f_w_inner, f_b_inner, f_w_out, f_freq, l)
        coeffs = _filter_spectra(fa, fb, filt)
        yc = _hyena(hv, hx1, hx2, fhalf, ghalf, coeffs, w_sconv, b_sconv, hy_bias, l)
        x1, u2, logits_t = _merge(ya, yb, yc, proj, x, mod, p_attn_b, p_pool_b, p_hyena_b, w_out_b,
                                  ln1_g, ln1_b, w_router, b_router, l)
        x = _moe(u2, logits_t, x1, mod, w1, b1, w2, b2, ln2_g, ln2_b, S, l).reshape(B, S, D)
    return x
```
